```python
import functools
import jax, jax.numpy as jnp
from jax import lax
import numpy as np

D_MODEL = 1024
BATCH = 32
SEQ = 2048
DEPTH = 1
DEC_BATCH = 16
DEC_SEQ = 16
PAST_LEN = 1024

CHUNK = 64
BAND_CHUNKS = 8
N_HEADS = 8
HEAD_DIM = 64
ATTN_DIM = N_HEADS * HEAD_DIM
MAX_REL = 128
CONV_DIM = 512
CONV_WIDTH = 31
N_EXPERTS = 64
TOP_K = 8
N_GROUPS = 8
TOPK_GROUPS = 4
EXPERT_FF = 256
SHARED_FF = 256
ROUTE_SCALE = 2.5
EPS = 1e-6
N_IN = 3 * ATTN_DIM + 2 * CONV_DIM + 2 * D_MODEL

kernel_name = 'streaming_hybrid_band_attn_conv_moe_adaln'


def rmsnorm(x, g):
    xf = x.astype(jnp.float32)
    y = xf * lax.rsqrt(jnp.mean(xf * xf, axis=-1, keepdims=True) + EPS)
    return (y * g.astype(jnp.float32)).astype(x.dtype)


def layernorm(x, g, b):
    xf = x.astype(jnp.float32)
    mu = jnp.mean(xf, axis=-1, keepdims=True)
    xc = xf - mu
    y = xc * lax.rsqrt(jnp.mean(xc * xc, axis=-1, keepdims=True) + EPS)
    return (y * g.astype(jnp.float32) + b.astype(jnp.float32)).astype(x.dtype)


def modulate(xn, shift, scale):
    return xn * (1 + scale[:, None, :]) + shift[:, None, :]


def band_attend(q, k, v, q_pos, k_pos, table):
    qc = q_pos // CHUNK
    kc = k_pos // CHUNK
    allowed = (kc[None, :] <= qc[:, None]) & (kc[None, :] >= qc[:, None] - BAND_CHUNKS) & (k_pos[None, :] >= 0)
    rel = jnp.clip(q_pos[:, None] - k_pos[None, :], -MAX_REL, MAX_REL) + MAX_REL
    bias = table[:, rel].astype(jnp.float32)
    s = jnp.einsum('bqhd,bkhd->bhqk', q, k).astype(jnp.float32) * (HEAD_DIM ** -0.5) + bias[None]
    s = jnp.where(allowed[None, None], s, jnp.float32(-1e30))
    p = jax.nn.softmax(s, axis=-1).astype(v.dtype)
    return jnp.einsum('bhqk,bkhd->bqhd', p, v)


def prompt_attention(q, k, v, table):
    B, T, H, Dh = q.shape
    nc = T // CHUNK
    left = BAND_CHUNKS * CHUNK
    band = left + CHUNK
    kp = jnp.pad(k, ((0, 0), (left, 0), (0, 0), (0, 0)))
    vp = jnp.pad(v, ((0, 0), (left, 0), (0, 0), (0, 0)))
    qs = q.reshape(B, nc, CHUNK, H, Dh).transpose(1, 0, 2, 3, 4)

    def one_chunk(args):
        n, qn = args
        start = n * CHUNK
        kb = lax.dynamic_slice_in_dim(kp, start, band, axis=1)
        vb = lax.dynamic_slice_in_dim(vp, start, band, axis=1)
        q_pos = start + jnp.arange(CHUNK)
        k_pos = start - left + jnp.arange(band)
        return band_attend(qn, kb, vb, q_pos, k_pos, table)

    out = lax.map(one_chunk, (jnp.arange(nc), qs))
    return out.transpose(1, 0, 2, 3, 4).reshape(B, T, H, Dh)


def sample_attention(q, k_new, v_new, k_cache, v_cache, table):
    keep = k_cache.shape[1]
    ts = q.shape[1]
    k = jnp.concatenate([k_cache.astype(k_new.dtype), k_new], axis=1)
    v = jnp.concatenate([v_cache.astype(v_new.dtype), v_new], axis=1)
    q_pos = PAST_LEN + jnp.arange(ts)
    k_pos = jnp.concatenate([PAST_LEN - keep + jnp.arange(keep), q_pos])
    return band_attend(q, k, v, q_pos, k_pos, table)


def causal_dwconv(u_ext, w, b):
    c = u_ext.shape[-1]
    y = lax.conv_general_dilated(u_ext, w[:, None, :].astype(u_ext.dtype), window_strides=(1,), padding='VALID',
                                 dimension_numbers=('NWC', 'WIO', 'NWC'), feature_group_count=c)
    return y + b


def token_mixer(h, lp, attn_fn, conv_left):
    B, T, _ = h.shape
    z = h @ lp['w_in']
    cuts = [ATTN_DIM, 2 * ATTN_DIM, 3 * ATTN_DIM, 3 * ATTN_DIM + CONV_DIM, 3 * ATTN_DIM + 2 * CONV_DIM]
    q, k, v, a_in, b_in, g_in = jnp.split(z, cuts, axis=-1)
    q = rmsnorm(q.reshape(B, T, N_HEADS, HEAD_DIM), lp['q_norm_g'])
    k = rmsnorm(k.reshape(B, T, N_HEADS, HEAD_DIM), lp['k_norm_g'])
    v = v.reshape(B, T, N_HEADS, HEAD_DIM)
    o = attn_fn(q, k, v).reshape(B, T, ATTN_DIM)
    y_attn = o @ lp['w_attn_o']
    u = a_in * jax.nn.sigmoid(b_in)
    u_ext = jnp.concatenate([conv_left.astype(u.dtype), u], axis=1)
    cv = causal_dwconv(u_ext, lp['conv_w'], lp['conv_b'])
    cv = jax.nn.silu(layernorm(cv, lp['conv_ln_g'], lp['conv_ln_b']))
    y_conv = cv @ lp['w_conv_o'] + lp['b_conv_o']
    g_attn, g_conv = jnp.split(jax.nn.sigmoid(g_in + lp['b_gate']), 2, axis=-1)
    m = g_attn * y_attn + g_conv * y_conv
    return m @ lp['w_out'], k, v, u_ext[:, -(CONV_WIDTH - 1):]


def route(hf, w_router, r_bias):
    t = hf.shape[0]
    s = jax.nn.sigmoid(hf.astype(jnp.float32) @ w_router.astype(jnp.float32))
    choice = s + r_bias.astype(jnp.float32)
    grp = choice.reshape(t, N_GROUPS, N_EXPERTS // N_GROUPS)
    gscore = jnp.sum(lax.top_k(grp, 2)[0], axis=-1)
    _, gidx = lax.top_k(gscore, TOPK_GROUPS)
    rows = jnp.arange(t)[:, None]
    gmask = jnp.zeros((t, N_GROUPS), bool).at[rows, gidx].set(True)
    emask = jnp.repeat(gmask, N_EXPERTS // N_GROUPS, axis=1)
    _, eidx = lax.top_k(jnp.where(emask, choice, -jnp.inf), TOP_K)
    w = jnp.take_along_axis(s, eidx, axis=-1)
    w = w / jnp.sum(w, axis=-1, keepdims=True) * ROUTE_SCALE
    return jnp.zeros((t, N_EXPERTS), jnp.float32).at[rows, eidx].set(w)


def swiglu(h, wg, wu, wd):
    return (jax.nn.silu(h @ wg) * (h @ wu)) @ wd


def moe(h, lp):
    B, T, D = h.shape
    hf = h.reshape(B * T, D)
    gates = route(hf, lp['w_router'], lp['router_bias']).astype(h.dtype)
    out = swiglu(hf, lp['w_s_gate'], lp['w_s_up'], lp['w_s_down'])
    for e in range(N_EXPERTS):
        out = out + gates[:, e:e + 1] * swiglu(hf, lp['w_e_gate'][e], lp['w_e_up'][e], lp['w_e_down'][e])
    return out.reshape(B, T, D)


def encoder_layer(x, c, lp, attn_fn, conv_left):
    mod = jax.nn.silu(c) @ lp['w_mod'] + lp['b_mod']
    sh1, sc1, g1, sh2, sc2, g2 = jnp.split(mod, 6, axis=-1)
    h = modulate(rmsnorm(x, lp['norm1_g']), sh1, sc1)
    y, k, v, conv_state = token_mixer(h, lp, attn_fn, conv_left)
    x = x + g1[:, None, :] * y
    h2 = modulate(rmsnorm(x, lp['norm2_g']), sh2, sc2)
    x = x + g2[:, None, :] * moe(h2, lp)
    return x, k, v, conv_state


def setup_inputs(seed: int = 0) -> dict:
    key = jax.random.key(seed)
    ks = jax.random.split(key, 32)
    keep = min(BAND_CHUNKS * CHUNK, PAST_LEN)
    L = DEPTH

    def nrm(k, shape, scale):
        return jax.random.normal(k, shape, jnp.float32) * scale

    return {
        'x_prompt': nrm(ks[0], (BATCH, SEQ, D_MODEL), 1.0),
        'x_sample': nrm(ks[1], (DEC_BATCH, DEC_SEQ, D_MODEL), 1.0),
        'c_prompt': nrm(ks[2], (BATCH, D_MODEL), 1.0),
        'c_sample': nrm(ks[3], (DEC_BATCH, D_MODEL), 1.0),
        'cache_k': nrm(ks[4], (L, DEC_BATCH, keep, N_HEADS, HEAD_DIM), 1.0),
        'cache_v': nrm(ks[5], (L, DEC_BATCH, keep, N_HEADS, HEAD_DIM), 1.0),
        'state_conv': nrm(ks[6], (L, DEC_BATCH, CONV_WIDTH - 1, CONV_DIM), 0.7),
        'w_mod': nrm(ks[7], (L, D_MODEL, 6 * D_MODEL), 0.5 * D_MODEL ** -0.5),
        'b_mod': nrm(ks[8], (L, 6 * D_MODEL), 0.02),
        'norm1_g': 1.0 + nrm(ks[9], (L, D_MODEL), 0.05),
        'w_in': nrm(ks[10], (L, D_MODEL, N_IN), D_MODEL ** -0.5),
        'q_norm_g': 1.0 + nrm(ks[11], (L, HEAD_DIM), 0.05),
        'k_norm_g': 1.0 + nrm(ks[12], (L, HEAD_DIM), 0.05),
        'rel_bias': nrm(ks[13], (L, N_HEADS, 2 * MAX_REL + 1), 0.5),
        'w_attn_o': nrm(ks[14], (L, ATTN_DIM, D_MODEL), ATTN_DIM ** -0.5),
        'conv_w': nrm(ks[15], (L, CONV_WIDTH, CONV_DIM), CONV_WIDTH ** -0.5),
        'conv_b': nrm(ks[16], (L, CONV_DIM), 0.02),
        'conv_ln_g': 1.0 + nrm(ks[17], (L, CONV_DIM), 0.05),
        'conv_ln_b': nrm(ks[18], (L, CONV_DIM), 0.02),
        'w_conv_o': nrm(ks[19], (L, CONV_DIM, D_MODEL), CONV_DIM ** -0.5),
        'b_conv_o': nrm(ks[20], (L, D_MODEL), 0.02),
        'b_gate': nrm(ks[21], (L, 2 * D_MODEL), 0.02),
        'w_out': nrm(ks[22], (L, D_MODEL, D_MODEL), D_MODEL ** -0.5),
        'norm2_g': 1.0 + nrm(ks[23], (L, D_MODEL), 0.05),
        'w_router': nrm(ks[24], (L, D_MODEL, N_EXPERTS), D_MODEL ** -0.5),
        'router_bias': nrm(ks[25], (L, N_EXPERTS), 0.01),
        'w_e_gate': nrm(ks[26], (L, N_EXPERTS, D_MODEL, EXPERT_FF), D_MODEL ** -0.5),
        'w_e_up': nrm(ks[27], (L, N_EXPERTS, D_MODEL, EXPERT_FF), D_MODEL ** -0.5),
        'w_e_down': nrm(ks[28], (L, N_EXPERTS, EXPERT_FF, D_MODEL), EXPERT_FF ** -0.5),
        'w_s_gate': nrm(ks[29], (L, D_MODEL, SHARED_FF), D_MODEL ** -0.5),
        'w_s_up': nrm(ks[30], (L, D_MODEL, SHARED_FF), D_MODEL ** -0.5),
        'w_s_down': nrm(ks[31], (L, SHARED_FF, D_MODEL), SHARED_FF ** -0.5),
    }


def reference(x_prompt, x_sample, c_prompt, c_sample, cache_k, cache_v, state_conv,
              w_mod, b_mod, norm1_g, w_in, q_norm_g, k_norm_g, rel_bias, w_attn_o,
              conv_w, conv_b, conv_ln_g, conv_ln_b, w_conv_o, b_conv_o, b_gate, w_out,
              norm2_g, w_router, router_bias, w_e_gate, w_e_up, w_e_down,
              w_s_gate, w_s_up, w_s_down):
    yp, ys = x_prompt, x_sample
    nkp, nvp, ncp, nks, nvs, ncs = [], [], [], [], [], []
    for l in range(DEPTH):
        lp = {
            'w_mod': w_mod[l], 'b_mod': b_mod[l], 'norm1_g': norm1_g[l], 'w_in': w_in[l],
            'q_norm_g': q_norm_g[l], 'k_norm_g': k_norm_g[l], 'rel_bias': rel_bias[l],
            'w_attn_o': w_attn_o[l], 'conv_w': conv_w[l], 'conv_b': conv_b[l],
            'conv_ln_g': conv_ln_g[l], 'conv_ln_b': conv_ln_b[l], 'w_conv_o': w_conv_o[l],
            'b_conv_o': b_conv_o[l], 'b_gate': b_gate[l], 'w_out': w_out[l], 'norm2_g': norm2_g[l],
            'w_router': w_router[l], 'router_bias': router_bias[l], 'w_e_gate': w_e_gate[l],
            'w_e_up': w_e_up[l], 'w_e_down': w_e_down[l], 'w_s_gate': w_s_gate[l],
            'w_s_up': w_s_up[l], 'w_s_down': w_s_down[l],
        }
        conv_left_p = jnp.zeros((yp.shape[0], CONV_WIDTH - 1, CONV_DIM), yp.dtype)
        attn_p = functools.partial(prompt_attention, table=lp['rel_bias'])
        yp, kp, vp, cp = encoder_layer(yp, c_prompt, lp, attn_p, conv_left_p)
        keep_p = min(BAND_CHUNKS * CHUNK, yp.shape[1])
        nkp.append(kp[:, -keep_p:])
        nvp.append(vp[:, -keep_p:])
        ncp.append(cp)
        attn_s = functools.partial(sample_attention, k_cache=cache_k[l], v_cache=cache_v[l], table=lp['rel_bias'])
        ys, ks_new, vs_new, cs_new = encoder_layer(ys, c_sample, lp, attn_s, state_conv[l])
        nks.append(ks_new)
        nvs.append(vs_new)
        ncs.append(cs_new)
    return (yp, ys, jnp.stack(nkp), jnp.stack(nvp), jnp.stack(ncp), jnp.stack(nks), jnp.stack(nvs), jnp.stack(ncs))
```

```python
import dataclasses
import functools

import jax
import jax.numpy as jnp
from jax import lax
from jax.experimental import pallas as pl
from jax.experimental.pallas import tpu as pltpu

F32 = jnp.float32
BF16 = jnp.bfloat16

CHUNK = 64
BAND_CHUNKS = 8
PREV = BAND_CHUNKS * CHUNK
PAST_LEN = 1024
N_HEADS = 8
HEAD_DIM = 64
ATTN_DIM = N_HEADS * HEAD_DIM
MAX_REL = 128
CONV_WIDTH = 31
CONV_PAD = 32
N_EXPERTS = 64
N_GROUPS = 8
GROUP_SIZE = N_EXPERTS // N_GROUPS
TOPK_GROUPS = 4
TOP_K = 8
ROUTE_SCALE = 2.5
EPS = 1e-6
GATE_LANES = 128
NEG = -1e30

VMEM_LIMIT_BYTES = 56 * 1024 * 1024


def _sigmoid(x):
    return 1.0 / (1.0 + jnp.exp(-x))


def _split_bf16(x):
    hi = x.astype(BF16)
    lo = (x - hi.astype(F32)).astype(BF16)
    return hi, lo


def _dot(a, b):
    return jnp.dot(a, b, preferred_element_type=F32)


def _dot3(a, b):
    a_hi, a_lo = _split_bf16(a)
    b_hi, b_lo = _split_bf16(b)
    return _dot(a_hi, b_hi) + _dot(a_hi, b_lo) + _dot(a_lo, b_hi)


def _mod_kernel(c_ref, w_ref, b_ref, o_ref):
    c = c_ref[...]
    o_ref[...] = _dot3(c * _sigmoid(c), w_ref[...]) + b_ref[...]


def _mod_call(c, w_mod, b_mod):
    n, d = c.shape
    dout = w_mod.shape[1]
    bn = 1536
    return pl.pallas_call(
        _mod_kernel,
        out_shape=jax.ShapeDtypeStruct((n, dout), F32),
        grid=(dout // bn,),
        in_specs=[pl.BlockSpec((n, d), lambda j: (0, 0)),
                  pl.BlockSpec((d, bn), lambda j: (0, j)),
                  pl.BlockSpec((1, bn), lambda j: (0, j))],
        out_specs=pl.BlockSpec((n, bn), lambda j: (0, j)),
        compiler_params=pltpu.CompilerParams(dimension_semantics=("arbitrary",),
                                             vmem_limit_bytes=VMEM_LIMIT_BYTES),
        name="mod",
    )(c, w_mod, b_mod.reshape(1, dout))


@dataclasses.dataclass(frozen=True)
class MixerCfg:
    tt: int
    qb: int
    kb: int
    n_qblk: int
    kstride: int
    has_cache: bool


def _rms(x, g):
    ms = jnp.mean(x * x, axis=-1, keepdims=True)
    return x * lax.rsqrt(ms + EPS) * g


def _head_rms(z, hsum, g):
    hi, lo = _split_bf16(z * z)
    ss = _dot(hi, hsum) + _dot(lo, hsum)
    return z * lax.rsqrt(ss * (1.0 / HEAD_DIM) + EPS) * g


def _route(choice, s):
    t = choice.shape[1]
    sub = lax.broadcasted_iota(jnp.int32, (GROUP_SIZE, t), 0).astype(F32)
    slabs = [choice[g * GROUP_SIZE:(g + 1) * GROUP_SIZE, :] for g in range(N_GROUPS)]
    gscore = []
    for c in slabs:
        m1 = jnp.max(c, axis=0, keepdims=True)
        first = jnp.min(jnp.where(c == m1, sub, float(GROUP_SIZE)), axis=0, keepdims=True)
        m2 = jnp.max(jnp.where(sub == first, -jnp.inf, c), axis=0, keepdims=True)
        gscore.append(m1 + m2)
    gsel = [jnp.zeros((1, t), F32) for _ in range(N_GROUPS)]
    for _ in range(TOPK_GROUPS):
        m = functools.reduce(jnp.maximum, gscore)
        first = functools.reduce(jnp.minimum,
                                 [jnp.where(gscore[g] == m, float(g), float(N_GROUPS)) for g in range(N_GROUPS)])
        for g in range(N_GROUPS):
            hit = first == float(g)
            gsel[g] = jnp.where(hit, 1.0, gsel[g])
            gscore[g] = jnp.where(hit, -jnp.inf, gscore[g])
    masked = [jnp.where(gsel[g] > 0.5, slabs[g], -jnp.inf) for g in range(N_GROUPS)]
    idx = [sub + float(g * GROUP_SIZE) for g in range(N_GROUPS)]
    w = [jnp.zeros((GROUP_SIZE, t), F32) for _ in range(N_GROUPS)]
    for _ in range(TOP_K):
        m = jnp.max(functools.reduce(jnp.maximum, masked), axis=0, keepdims=True)
        cand = [jnp.where(masked[g] == m, idx[g], float(N_EXPERTS)) for g in range(N_GROUPS)]
        first = jnp.min(functools.reduce(jnp.minimum, cand), axis=0, keepdims=True)
        for g in range(N_GROUPS):
            hit = idx[g] == first
            w[g] = jnp.where(hit, s[g * GROUP_SIZE:(g + 1) * GROUP_SIZE, :], w[g])
            masked[g] = jnp.where(hit, -jnp.inf, masked[g])
    wsum = jnp.sum(functools.reduce(jnp.add, w), axis=0, keepdims=True)
    return jnp.concatenate([wg / wsum * ROUTE_SCALE for wg in w], axis=0)


def _mixer_kernel(cfg, *refs):
    (x_ref, mod_ref, n1g_ref, n2g_ref, win_ref, qg_ref, kg_ref, hsum_ref, bias_ref, wao_ref,
     cw_ref, cb_ref, lng_ref, lnb_ref, wco_ref, bco_ref, bg_ref, wout_ref,
     wrh_ref, wrl_ref, rb_ref) = refs[:21]
    refs = refs[21:]
    if cfg.has_cache:
        kc_ref, vc_ref, cs_ref = refs[:3]
        refs = refs[3:]
    (x1_ref, h2_ref, gates_ref, kout_ref, vout_ref, cout_ref,
     qbuf, kbuf, vbuf, obuf, uext, cvbuf) = refs

    tt = cfg.tt
    t = pl.program_id(1)

    if cfg.has_cache:
        kbuf[0:PREV, :] = kc_ref[0].astype(BF16)
        vbuf[0:PREV, :] = vc_ref[0].astype(BF16)
        uext[CONV_PAD - (CONV_WIDTH - 1):CONV_PAD, :] = cs_ref[0]
    else:
        @pl.when(t == 0)
        def _():
            kbuf[0:PREV, :] = jnp.zeros((PREV, ATTN_DIM), BF16)
            vbuf[0:PREV, :] = jnp.zeros((PREV, ATTN_DIM), BF16)
            uext[0:CONV_PAD, :] = jnp.zeros((CONV_PAD, uext.shape[1]), F32)

    mod = mod_ref[0]
    sh1, sc1, g1, sh2, sc2 = (mod[i:i + 1, :] for i in range(5))

    x = x_ref[0]
    hb = (_rms(x, n1g_ref[...]) * (1.0 + sc1) + sh1).astype(BF16)

    a0, a1, a2, a3, a4, a5 = (0, ATTN_DIM, 2 * ATTN_DIM, 3 * ATTN_DIM,
                              3 * ATTN_DIM + cw_ref.shape[1], 3 * ATTN_DIM + 2 * cw_ref.shape[1])
    d_model = x.shape[1]

    q = _head_rms(_dot(hb, win_ref[:, a0:a1]), hsum_ref[...], qg_ref[...])
    qbuf[...] = (q * (HEAD_DIM ** -0.5)).astype(BF16)
    k = _head_rms(_dot(hb, win_ref[:, a1:a2]), hsum_ref[...], kg_ref[...])
    kout_ref[0] = k
    kbuf[PREV:PREV + tt, :] = k.astype(BF16)
    v = _dot(hb, win_ref[:, a2:a3])
    vout_ref[0] = v
    vbuf[PREV:PREV + tt, :] = v.astype(BF16)

    for j in range(cfg.n_qblk):
        r0, k0 = j * cfg.qb, j * cfg.kstride
        if not cfg.has_cache:
            col = lax.broadcasted_iota(jnp.int32, (1, cfg.kb), 1)
            valid = jnp.logical_or(col >= PREV - k0, t > 0)
        for h in range(N_HEADS):
            c0, c1 = h * HEAD_DIM, (h + 1) * HEAD_DIM
            s = lax.dot_general(qbuf[r0:r0 + cfg.qb, c0:c1], kbuf[k0:k0 + cfg.kb, c0:c1],
                                (((1,), (1,)), ((), ())), preferred_element_type=F32)
            s = s + bias_ref[h]
            if not cfg.has_cache:
                s = jnp.where(valid, s, NEG)
            p = jnp.exp(s - jnp.max(s, axis=-1, keepdims=True))
            l = jnp.sum(p, axis=-1, keepdims=True)
            o = _dot(p.astype(BF16), vbuf[k0:k0 + cfg.kb, c0:c1]) * (1.0 / l)
            obuf[r0:r0 + cfg.qb, c0:c1] = o.astype(BF16)
    y_attn = _dot(obuf[...], wao_ref[...])

    u = _dot(hb, win_ref[:, a3:a4]) * _sigmoid(_dot(hb, win_ref[:, a4:a5]))
    uext[CONV_PAD:CONV_PAD + tt, :] = u
    cout_ref[0] = uext[CONV_PAD + tt - (CONV_WIDTH - 1):CONV_PAD + tt, :]
    rc = min(tt, 64)
    for r in range(0, tt, rc):
        acc = jnp.broadcast_to(cb_ref[...], (rc, u.shape[1]))
        for j in range(CONV_WIDTH):
            off = CONV_PAD - (CONV_WIDTH - 1) + r + j
            acc = acc + cw_ref[j:j + 1, :] * uext[off:off + rc, :]
        cvbuf[r:r + rc, :] = acc
    if not cfg.has_cache:
        uext[0:CONV_PAD, :] = uext[tt:tt + CONV_PAD, :]
    cv = cvbuf[...]
    mu = jnp.mean(cv, axis=-1, keepdims=True)
    xc = cv - mu
    var = jnp.mean(xc * xc, axis=-1, keepdims=True)
    cv = xc * lax.rsqrt(var + EPS) * lng_ref[...] + lnb_ref[...]
    cv = cv * _sigmoid(cv)
    y_conv = _dot(cv.astype(BF16), wco_ref[...]) + bco_ref[...]

    g_attn = _sigmoid(_dot(hb, win_ref[:, a5:a5 + d_model]) + bg_ref[:, 0:d_model])
    g_conv = _sigmoid(_dot(hb, win_ref[:, a5 + d_model:a5 + 2 * d_model]) + bg_ref[:, d_model:2 * d_model])
    m = g_attn * y_attn + g_conv * y_conv
    x1 = x + g1 * _dot(m.astype(BF16), wout_ref[...])
    x1_ref[0] = x1

    h2 = _rms(x1, n2g_ref[...]) * (1.0 + sc2) + sh2
    h2_hi, h2_lo = _split_bf16(h2)
    h2_ref[0] = h2_hi
    nt = (((1,), (1,)), ((), ()))
    logits = (lax.dot_general(wrh_ref[...], h2_hi, nt, preferred_element_type=F32)
              + lax.dot_general(wrl_ref[...], h2_hi, nt, preferred_element_type=F32)
              + lax.dot_general(wrh_ref[...], h2_lo, nt, preferred_element_type=F32))
    s = _sigmoid(logits)
    gates_t = _route(s + rb_ref[...], s)
    gates_t = jnp.concatenate([gates_t, jnp.zeros((GATE_LANES - N_EXPERTS, tt), F32)], axis=0)
    gates_ref[0] = gates_t.T


def _rel_bias_blocks(table, cfg, q_pos0, k_pos0):
    q_pos = q_pos0 + jnp.arange(cfg.qb)
    k_pos = k_pos0 + jnp.arange(cfg.kb)
    qc = q_pos // CHUNK
    kc = k_pos // CHUNK
    allowed = (kc[None, :] <= qc[:, None]) & (kc[None, :] >= qc[:, None] - BAND_CHUNKS)
    rel = jnp.clip(q_pos[:, None] - k_pos[None, :], -MAX_REL, MAX_REL) + MAX_REL
    return jnp.where(allowed[None], table[:, rel].astype(F32), NEG)


def _mixer_call(cfg, x, mod, p, cache=None):
    b, t_total, d = x.shape
    nt = t_total // cfg.tt
    conv_dim = p["conv_w"].shape[1]
    n_in = p["w_in"].shape[1]

    def const(shape):
        return pl.BlockSpec(shape, lambda i, j: (0,) * len(shape), pipeline_mode=pl.Buffered(1))

    in_specs = [
        pl.BlockSpec((1, cfg.tt, d), lambda i, j: (i, j, 0)),
        pl.BlockSpec((1, 6, d), lambda i, j: (i, 0, 0)),
        const((1, d)), const((1, d)), const((d, n_in)),
        const((1, ATTN_DIM)), const((1, ATTN_DIM)), const((ATTN_DIM, ATTN_DIM)),
        const((N_HEADS, cfg.qb, cfg.kb)), const((ATTN_DIM, d)),
        const((CONV_WIDTH, conv_dim)), const((1, conv_dim)), const((1, conv_dim)), const((1, conv_dim)),
        const((conv_dim, d)), const((1, d)), const((1, 2 * d)), const((d, d)),
        const((N_EXPERTS, d)), const((N_EXPERTS, d)), const((N_EXPERTS, 1)),
    ]
    args = [x, mod, p["norm1_g"], p["norm2_g"], p["w_in"], p["q_norm_g"], p["k_norm_g"], p["hsum"],
            p["bias"], p["w_attn_o"], p["conv_w"], p["conv_b"], p["conv_ln_g"], p["conv_ln_b"],
            p["w_conv_o"], p["b_conv_o"], p["b_gate"], p["w_out"], p["wr_hi"], p["wr_lo"], p["router_bias"]]
    if cfg.has_cache:
        in_specs += [pl.BlockSpec((1, PREV, ATTN_DIM), lambda i, j: (i, 0, 0)),
                     pl.BlockSpec((1, PREV, ATTN_DIM), lambda i, j: (i, 0, 0)),
                     pl.BlockSpec((1, CONV_WIDTH - 1, conv_dim), lambda i, j: (i, 0, 0))]
        args += list(cache)
    keep = min(PREV, t_total)
    out_shape = [jax.ShapeDtypeStruct((b, t_total, d), F32),
                 jax.ShapeDtypeStruct((b, t_total, d), BF16),
                 jax.ShapeDtypeStruct((b, t_total, GATE_LANES), F32),
                 jax.ShapeDtypeStruct((b, keep, ATTN_DIM), F32),
                 jax.ShapeDtypeStruct((b, keep, ATTN_DIM), F32),
                 jax.ShapeDtypeStruct((b, CONV_WIDTH - 1, conv_dim), F32)]
    assert keep == cfg.tt
    out_specs = [pl.BlockSpec((1, cfg.tt, d), lambda i, j: (i, j, 0)),
                 pl.BlockSpec((1, cfg.tt, d), lambda i, j: (i, j, 0)),
                 pl.BlockSpec((1, cfg.tt, GATE_LANES), lambda i, j: (i, j, 0)),
                 pl.BlockSpec((1, keep, ATTN_DIM), lambda i, j: (i, 0, 0)),
                 pl.BlockSpec((1, keep, ATTN_DIM), lambda i, j: (i, 0, 0)),
                 pl.BlockSpec((1, CONV_WIDTH - 1, conv_dim), lambda i, j: (i, 0, 0))]
    scratch = [pltpu.VMEM((cfg.tt, ATTN_DIM), BF16),
               pltpu.VMEM((PREV + cfg.tt, ATTN_DIM), BF16),
               pltpu.VMEM((PREV + cfg.tt, ATTN_DIM), BF16),
               pltpu.VMEM((cfg.tt, ATTN_DIM), BF16),
               pltpu.VMEM((CONV_PAD + cfg.tt, conv_dim), F32),
               pltpu.VMEM((cfg.tt, conv_dim), F32)]
    return pl.pallas_call(
        functools.partial(_mixer_kernel, cfg),
        out_shape=out_shape,
        grid=(b, nt),
        in_specs=in_specs,
        out_specs=out_specs,
        scratch_shapes=scratch,
        compiler_params=pltpu.CompilerParams(dimension_semantics=("arbitrary", "arbitrary"),
                                             vmem_limit_bytes=VMEM_LIMIT_BYTES),
        name="mixer_sample" if cfg.has_cache else "mixer_prompt",
    )(*args)


def _swiglu_hidden(h, wg, wu):
    a = _dot(h, wg)
    return a * _sigmoid(a) * _dot(h, wu)


def _moe_kernel(h_ref, gates_ref, x1_ref, g2_ref, wsg_ref, wsu_ref, wsd_ref, wg_ref, wu_ref, wd_ref,
                o_ref, acc_ref):
    e = pl.program_id(1)
    h = h_ref[...]

    @pl.when(e == 0)
    def _():
        acc_ref[...] = _dot(_swiglu_hidden(h, wsg_ref[...], wsu_ref[...]).astype(BF16), wsd_ref[...])

    gates = gates_ref[...]
    lane = lax.broadcasted_iota(jnp.int32, gates.shape, 1)
    gate = jnp.sum(jnp.where(lane == e, gates, 0.0), axis=1, keepdims=True)
    act = _swiglu_hidden(h, wg_ref[0], wu_ref[0]) * gate
    acc_ref[...] += _dot(act.astype(BF16), wd_ref[0])

    @pl.when(e == pl.num_programs(1) - 1)
    def _():
        o_ref[...] = x1_ref[...] + g2_ref[...] * acc_ref[...]


def _moe_call(h2, gates, x1, g2, g2_rows, p, tm):
    n, d = h2.shape
    ff = p["w_e_gate"].shape[2]
    sff = p["w_s_gate"].shape[1]
    if g2_rows == 1:
        g2 = g2.reshape(g2.shape[0], 1, d)
        g2_spec = pl.BlockSpec((None, 1, d), lambda i, e: (i * tm // (n // g2.shape[0]), 0, 0))
    else:
        g2_spec = pl.BlockSpec((tm, d), lambda i, e: (i, 0))

    def const(shape):
        return pl.BlockSpec(shape, lambda i, e: (0,) * len(shape), pipeline_mode=pl.Buffered(1))

    return pl.pallas_call(
        _moe_kernel,
        out_shape=jax.ShapeDtypeStruct((n, d), F32),
        grid=(n // tm, N_EXPERTS),
        in_specs=[pl.BlockSpec((tm, d), lambda i, e: (i, 0)),
                  pl.BlockSpec((tm, GATE_LANES), lambda i, e: (i, 0)),
                  pl.BlockSpec((tm, d), lambda i, e: (i, 0)),
                  g2_spec,
                  const((d, sff)), const((d, sff)), const((sff, d)),
                  pl.BlockSpec((1, d, ff), lambda i, e: (e, 0, 0)),
                  pl.BlockSpec((1, d, ff), lambda i, e: (e, 0, 0)),
                  pl.BlockSpec((1, ff, d), lambda i, e: (e, 0, 0))],
        out_specs=pl.BlockSpec((tm, d), lambda i, e: (i, 0)),
        scratch_shapes=[pltpu.VMEM((tm, d), F32)],
        compiler_params=pltpu.CompilerParams(dimension_semantics=("arbitrary", "arbitrary"),
                                             vmem_limit_bytes=VMEM_LIMIT_BYTES),
        name="moe",
    )(h2, gates, x1, g2, p["w_s_gate"], p["w_s_up"], p["w_s_down"], p["w_e_gate"], p["w_e_up"], p["w_e_down"])


PROMPT_CFG = MixerCfg(tt=512, qb=128, kb=640, n_qblk=4, kstride=128, has_cache=False)
MOE_TILE = 1024


def _layer_params(l, w_in, q_norm_g, k_norm_g, w_attn_o, conv_w, conv_b, conv_ln_g, conv_ln_b, w_conv_o,
                  b_conv_o, b_gate, w_out, norm1_g, norm2_g, w_router, router_bias,
                  w_e_gate, w_e_up, w_e_down, w_s_gate, w_s_up, w_s_down):
    row = lambda a: a[l].reshape(1, -1)
    wr_t = w_router[l].T
    wr_hi = wr_t.astype(BF16)
    head = jnp.arange(ATTN_DIM) // HEAD_DIM
    return {
        "norm1_g": row(norm1_g), "norm2_g": row(norm2_g), "w_in": w_in[l].astype(BF16),
        "q_norm_g": jnp.tile(q_norm_g[l], N_HEADS).reshape(1, -1),
        "k_norm_g": jnp.tile(k_norm_g[l], N_HEADS).reshape(1, -1),
        "hsum": (head[:, None] == head[None, :]).astype(BF16),
        "w_attn_o": w_attn_o[l].astype(BF16), "conv_w": conv_w[l], "conv_b": row(conv_b),
        "conv_ln_g": row(conv_ln_g), "conv_ln_b": row(conv_ln_b), "w_conv_o": w_conv_o[l].astype(BF16),
        "b_conv_o": row(b_conv_o), "b_gate": row(b_gate), "w_out": w_out[l].astype(BF16),
        "wr_hi": wr_hi, "wr_lo": (wr_t - wr_hi.astype(F32)).astype(BF16),
        "router_bias": router_bias[l].reshape(-1, 1),
        "w_e_gate": w_e_gate[l].astype(BF16), "w_e_up": w_e_up[l].astype(BF16),
        "w_e_down": w_e_down[l].astype(BF16), "w_s_gate": w_s_gate[l].astype(BF16),
        "w_s_up": w_s_up[l].astype(BF16), "w_s_down": w_s_down[l].astype(BF16),
    }


def kernel(x_prompt, x_sample, c_prompt, c_sample, cache_k, cache_v, state_conv, w_mod, b_mod, norm1_g, w_in, q_norm_g, k_norm_g, rel_bias, w_attn_o, conv_w, conv_b, conv_ln_g, conv_ln_b, w_conv_o, b_conv_o, b_gate, w_out, norm2_g, w_router, router_bias, w_e_gate, w_e_up, w_e_down, w_s_gate, w_s_up, w_s_down):
    depth = w_mod.shape[0]
    bp, tp, d = x_prompt.shape
    bs, ts, _ = x_sample.shape
    assert cache_k.shape[2] == min(PREV, PAST_LEN) == PREV
    sample_cfg = MixerCfg(tt=ts, qb=ts, kb=PREV + ts, n_qblk=1, kstride=0, has_cache=True)

    yp, ys = x_prompt, x_sample
    outs = [[] for _ in range(6)]
    for l in range(depth):
        p = _layer_params(l, w_in, q_norm_g, k_norm_g, w_attn_o, conv_w, conv_b, conv_ln_g, conv_ln_b,
                          w_conv_o, b_conv_o, b_gate, w_out, norm1_g, norm2_g, w_router, router_bias,
                          w_e_gate, w_e_up, w_e_down, w_s_gate, w_s_up, w_s_down)
        mod = _mod_call(jnp.concatenate([c_prompt, c_sample], axis=0), w_mod[l], b_mod[l])
        mod = mod.reshape(bp + bs, 6, d)
        mod_p, mod_s = mod[:bp], mod[bp:]

        p["bias"] = _rel_bias_blocks(rel_bias[l], PROMPT_CFG, PREV, 0)
        x1, h2, gates, kp, vp, cp = _mixer_call(PROMPT_CFG, yp, mod_p, p)
        yp = _moe_call(h2.reshape(bp * tp, d), gates.reshape(bp * tp, GATE_LANES), x1.reshape(bp * tp, d),
                       mod_p[:, 5, :], 1, p, MOE_TILE).reshape(bp, tp, d)
        outs[0].append(kp.reshape(bp, -1, N_HEADS, HEAD_DIM))
        outs[1].append(vp.reshape(bp, -1, N_HEADS, HEAD_DIM))
        outs[2].append(cp)

        p["bias"] = _rel_bias_blocks(rel_bias[l], sample_cfg, PAST_LEN, PAST_LEN - PREV)
        cache = (cache_k[l].reshape(bs, PREV, ATTN_DIM), cache_v[l].reshape(bs, PREV, ATTN_DIM), state_conv[l])
        x1, h2, gates, ks, vs, cs = _mixer_call(sample_cfg, ys, mod_s, p, cache)
        g2_rows = jnp.repeat(mod_s[:, 5, :], ts, axis=0)
        ys = _moe_call(h2.reshape(bs * ts, d), gates.reshape(bs * ts, GATE_LANES), x1.reshape(bs * ts, d),
                       g2_rows, bs * ts, p, bs * ts).reshape(bs, ts, d)
        outs[3].append(ks.reshape(bs, ts, N_HEADS, HEAD_DIM))
        outs[4].append(vs.reshape(bs, ts, N_HEADS, HEAD_DIM))
        outs[5].append(cs)
    return (yp, ys) + tuple(jnp.stack(o) for o in outs)
```

```python
import dataclasses
import functools

import jax
import jax.numpy as jnp
from jax import lax
from jax.experimental import pallas as pl
from jax.experimental.pallas import tpu as pltpu
from jax.experimental.pallas import tpu_sc as plsc

F32 = jnp.float32
BF16 = jnp.bfloat16
I32 = jnp.int32

CHUNK = 64
BAND_CHUNKS = 8
PREV = BAND_CHUNKS * CHUNK
PAST_LEN = 1024
N_HEADS = 8
HEAD_DIM = 64
ATTN_DIM = N_HEADS * HEAD_DIM
MAX_REL = 128
CONV_WIDTH = 31
CONV_PAD = 32
N_EXPERTS = 64
N_GROUPS = 8
GROUP_SIZE = N_EXPERTS // N_GROUPS
TOPK_GROUPS = 4
TOP_K = 8
ROUTE_SCALE = 2.5
EPS = 1e-6
LANES = 128
NEG = -1e30

VMEM_LIMIT_BYTES = 56 * 1024 * 1024

SC_CORES = 2
SC_SUBCORES = 16
SC_WORKERS = SC_CORES * SC_SUBCORES
SC_ROWS = 64

MAX_TILES = LANES
ROW_BLOCK = 1024


def _sigmoid(x):
    return 1.0 / (1.0 + jnp.exp(-x))


def _split_bf16(x):
    hi = x.astype(BF16)
    lo = (x - hi.astype(F32)).astype(BF16)
    return hi, lo


def _dot(a, b):
    return jnp.dot(a, b, preferred_element_type=F32)


def _dot3(a, b):
    a_hi, a_lo = _split_bf16(a)
    b_hi, b_lo = _split_bf16(b)
    return _dot(a_hi, b_hi) + _dot(a_hi, b_lo) + _dot(a_lo, b_hi)


def _pack(a, b):
    ia = lax.bitcast_convert_type(a.astype(BF16).astype(F32), I32)
    ib = lax.bitcast_convert_type(b.astype(BF16).astype(F32), I32)
    return ia | lax.shift_right_logical(ib, 16)


def _unpack(p):
    a = lax.bitcast_convert_type(p & jnp.int32(-65536), F32)
    b = lax.bitcast_convert_type(lax.shift_left(p, 16), F32)
    return a, b


def _dot_halves(a, b, w_ref):
    half = a.shape[1]
    return _dot(a, w_ref[0:half, :]) + _dot(b, w_ref[half:2 * half, :])


def _mod_kernel(c_ref, w_ref, b_ref, o_ref):
    c = c_ref[...]
    o_ref[...] = _dot3(c * _sigmoid(c), w_ref[...]) + b_ref[...]


def _mod_call(c, w_mod, b_mod):
    n, d = c.shape
    dout = w_mod.shape[1]
    bn = 1536
    return pl.pallas_call(
        _mod_kernel,
        out_shape=jax.ShapeDtypeStruct((n, dout), F32),
        grid=(dout // bn,),
        in_specs=[pl.BlockSpec((n, d), lambda j: (0, 0)),
                  pl.BlockSpec((d, bn), lambda j: (0, j)),
                  pl.BlockSpec((1, bn), lambda j: (0, j))],
        out_specs=pl.BlockSpec((n, bn), lambda j: (0, j)),
        compiler_params=pltpu.CompilerParams(dimension_semantics=("arbitrary",),
                                             vmem_limit_bytes=VMEM_LIMIT_BYTES),
        name="mod",
    )(c, w_mod, b_mod.reshape(1, dout))


@dataclasses.dataclass(frozen=True)
class MixerCfg:
    tt: int
    qb: int
    kb: int
    n_qblk: int
    kstride: int
    has_cache: bool
    sparse: bool


def _rms(x, g):
    ms = jnp.mean(x * x, axis=-1, keepdims=True)
    return x * lax.rsqrt(ms + EPS) * g


def _head_rms(z, hsum, g):
    hi, lo = _split_bf16(z * z)
    ss = _dot(hi, hsum) + _dot(lo, hsum)
    return z * lax.rsqrt(ss * (1.0 / HEAD_DIM) + EPS) * g


def _expert_ids(t):
    sub = lax.broadcasted_iota(I32, (GROUP_SIZE, t), 0).astype(F32)
    return [sub + float(g * GROUP_SIZE) for g in range(N_GROUPS)]


def _pick(ids, eid, slabs):
    acc = functools.reduce(jnp.add, [jnp.where(ids[g] == eid, slabs[g], 0.0) for g in range(N_GROUPS)])
    return jnp.sum(acc, axis=0, keepdims=True)


def _route(choice, s):
    t = choice.shape[1]
    sub = lax.broadcasted_iota(I32, (GROUP_SIZE, t), 0).astype(F32)
    slabs = [choice[g * GROUP_SIZE:(g + 1) * GROUP_SIZE, :] for g in range(N_GROUPS)]
    s_slabs = [s[g * GROUP_SIZE:(g + 1) * GROUP_SIZE, :] for g in range(N_GROUPS)]
    gscore = []
    for c in slabs:
        m1 = jnp.max(c, axis=0, keepdims=True)
        first = jnp.min(jnp.where(c == m1, sub, float(GROUP_SIZE)), axis=0, keepdims=True)
        m2 = jnp.max(jnp.where(sub == first, -jnp.inf, c), axis=0, keepdims=True)
        gscore.append(m1 + m2)
    gsel = [jnp.zeros((1, t), F32) for _ in range(N_GROUPS)]
    for _ in range(TOPK_GROUPS):
        m = functools.reduce(jnp.maximum, gscore)
        first = functools.reduce(jnp.minimum,
                                 [jnp.where(gscore[g] == m, float(g), float(N_GROUPS)) for g in range(N_GROUPS)])
        for g in range(N_GROUPS):
            hit = first == float(g)
            gsel[g] = jnp.where(hit, 1.0, gsel[g])
            gscore[g] = jnp.where(hit, -jnp.inf, gscore[g])
    masked = [jnp.where(gsel[g] > 0.5, slabs[g], -jnp.inf) for g in range(N_GROUPS)]
    ids = _expert_ids(t)
    eids, raw = [], []
    for _ in range(TOP_K):
        m = jnp.max(functools.reduce(jnp.maximum, masked), axis=0, keepdims=True)
        cand = [jnp.where(masked[g] == m, ids[g], float(N_EXPERTS)) for g in range(N_GROUPS)]
        first = jnp.min(functools.reduce(jnp.minimum, cand), axis=0, keepdims=True)
        eids.append(first)
        raw.append(_pick(ids, first, s_slabs))
        masked = [jnp.where(ids[g] == first, -jnp.inf, masked[g]) for g in range(N_GROUPS)]
    wsum = functools.reduce(jnp.add, raw)
    return eids, [r / wsum * ROUTE_SCALE for r in raw]


def _lane_dense_rows(rows, t):
    pad = jnp.zeros((LANES - len(rows), t), F32)
    return jnp.concatenate(rows + [pad], axis=0).T


def _mixer_kernel(cfg, *refs):
    (x_ref, mod_ref, n1g_ref, n2g_ref, win_ref, qg_ref, kg_ref, hsum_ref, bias_ref, wao_ref,
     cw_ref, cb_ref, lng_ref, lnb_ref, wco_ref, bco_ref, bg_ref, wout_ref,
     wrh_ref, wrl_ref, rb_ref) = refs[:21]
    refs = refs[21:]
    if cfg.has_cache:
        kc_ref, vc_ref, cs_ref = refs[:3]
        refs = refs[3:]
    if cfg.sparse:
        tri_ref = refs[0]
        refs = refs[1:]
    x1_ref, h2_ref, tok_ref, kout_ref, vout_ref, cout_ref = refs[:6]
    refs = refs[6:]
    if cfg.sparse:
        eid_ref, rank_ref, cnt_ref = refs[:3]
        refs = refs[3:]
    qbuf, kbuf, vbuf, obuf, uext, cvbuf = refs

    tt = cfg.tt
    t = pl.program_id(1)

    if cfg.has_cache:
        kbuf[0:PREV, :] = kc_ref[0].astype(BF16)
        vbuf[0:PREV, :] = vc_ref[0].astype(BF16)
        uext[CONV_PAD - (CONV_WIDTH - 1):CONV_PAD, :] = cs_ref[0]
    else:
        @pl.when(t == 0)
        def _():
            kbuf[0:PREV, :] = jnp.zeros((PREV, ATTN_DIM), BF16)
            vbuf[0:PREV, :] = jnp.zeros((PREV, ATTN_DIM), BF16)
            uext[0:CONV_PAD, :] = jnp.zeros((CONV_PAD, uext.shape[1]), F32)

    mod = mod_ref[0]
    sh1, sc1, g1, sh2, sc2 = (mod[i:i + 1, :] for i in range(5))

    x = x_ref[0]
    hb = (_rms(x, n1g_ref[...]) * (1.0 + sc1) + sh1).astype(BF16)

    a0, a1, a2, a3, a4, a5 = (0, ATTN_DIM, 2 * ATTN_DIM, 3 * ATTN_DIM,
                              3 * ATTN_DIM + cw_ref.shape[1], 3 * ATTN_DIM + 2 * cw_ref.shape[1])
    d_model = x.shape[1]

    q = _head_rms(_dot(hb, win_ref[:, a0:a1]), hsum_ref[...], qg_ref[...])
    qbuf[...] = (q * (HEAD_DIM ** -0.5)).astype(BF16)
    k = _head_rms(_dot(hb, win_ref[:, a1:a2]), hsum_ref[...], kg_ref[...])
    kout_ref[0] = k
    kbuf[PREV:PREV + tt, :] = k.astype(BF16)
    v = _dot(hb, win_ref[:, a2:a3])
    vout_ref[0] = v
    vbuf[PREV:PREV + tt, :] = v.astype(BF16)

    for j in range(cfg.n_qblk):
        r0, k0 = j * cfg.qb, j * cfg.kstride
        if not cfg.has_cache:
            col = lax.broadcasted_iota(I32, (1, cfg.kb), 1)
            valid = jnp.logical_or(col >= PREV - k0, t > 0)
        for h in range(N_HEADS):
            c0, c1 = h * HEAD_DIM, (h + 1) * HEAD_DIM
            s = lax.dot_general(qbuf[r0:r0 + cfg.qb, c0:c1], kbuf[k0:k0 + cfg.kb, c0:c1],
                                (((1,), (1,)), ((), ())), preferred_element_type=F32)
            s = s + bias_ref[h]
            if not cfg.has_cache:
                s = jnp.where(valid, s, NEG)
            p = jnp.exp(s - jnp.max(s, axis=-1, keepdims=True))
            l = jnp.sum(p, axis=-1, keepdims=True)
            o = _dot(p.astype(BF16), vbuf[k0:k0 + cfg.kb, c0:c1]) * (1.0 / l)
            obuf[r0:r0 + cfg.qb, c0:c1] = o.astype(BF16)
    y_attn = _dot(obuf[...], wao_ref[...])

    u = _dot(hb, win_ref[:, a3:a4]) * _sigmoid(_dot(hb, win_ref[:, a4:a5]))
    uext[CONV_PAD:CONV_PAD + tt, :] = u
    cout_ref[0] = uext[CONV_PAD + tt - (CONV_WIDTH - 1):CONV_PAD + tt, :]
    rc = min(tt, 64)
    for r in range(0, tt, rc):
        acc = jnp.broadcast_to(cb_ref[...], (rc, u.shape[1]))
        for j in range(CONV_WIDTH):
            off = CONV_PAD - (CONV_WIDTH - 1) + r + j
            acc = acc + cw_ref[j:j + 1, :] * uext[off:off + rc, :]
        cvbuf[r:r + rc, :] = acc
    if not cfg.has_cache:
        uext[0:CONV_PAD, :] = uext[tt:tt + CONV_PAD, :]
    cv = cvbuf[...]
    mu = jnp.mean(cv, axis=-1, keepdims=True)
    xc = cv - mu
    var = jnp.mean(xc * xc, axis=-1, keepdims=True)
    cv = xc * lax.rsqrt(var + EPS) * lng_ref[...] + lnb_ref[...]
    cv = cv * _sigmoid(cv)
    y_conv = _dot(cv.astype(BF16), wco_ref[...]) + bco_ref[...]

    g_attn = _sigmoid(_dot(hb, win_ref[:, a5:a5 + d_model]) + bg_ref[:, 0:d_model])
    g_conv = _sigmoid(_dot(hb, win_ref[:, a5 + d_model:a5 + 2 * d_model]) + bg_ref[:, d_model:2 * d_model])
    m = g_attn * y_attn + g_conv * y_conv
    x1 = x + g1 * _dot(m.astype(BF16), wout_ref[...])
    x1_ref[0] = x1

    h2 = _rms(x1, n2g_ref[...]) * (1.0 + sc2) + sh2
    h2_ref[0] = _pack(h2[:, 0:d_model // 2], h2[:, d_model // 2:d_model])
    h2_hi, h2_lo = _split_bf16(h2)
    nt_dims = (((1,), (1,)), ((), ()))
    logits = (lax.dot_general(wrh_ref[...], h2_hi, nt_dims, preferred_element_type=F32)
              + lax.dot_general(wrl_ref[...], h2_hi, nt_dims, preferred_element_type=F32)
              + lax.dot_general(wrh_ref[...], h2_lo, nt_dims, preferred_element_type=F32))
    s = _sigmoid(logits)
    eids, weights = _route(s + rb_ref[...], s)
    ids = _expert_ids(tt)
    if cfg.sparse:
        sel = [functools.reduce(jnp.add, [jnp.where(ids[g] == e, 1.0, 0.0) for e in eids]) for g in range(N_GROUPS)]
        sel = jnp.concatenate(sel, axis=0)
        rank = _dot(sel.astype(BF16), tri_ref[...])
        rank_slabs = [rank[g * GROUP_SIZE:(g + 1) * GROUP_SIZE, :] for g in range(N_GROUPS)]
        tok_ref[0] = _lane_dense_rows(weights, tt)
        eid_ref[...] = jnp.concatenate(eids, axis=0)
        rank_ref[...] = jnp.concatenate([_pick(ids, e, rank_slabs) for e in eids], axis=0)
        step = pl.program_id(0) * pl.num_programs(1) + t

        @pl.when(step == 0)
        def _():
            cnt_ref[...] = jnp.zeros(cnt_ref.shape, F32)

        lane = lax.broadcasted_iota(I32, cnt_ref.shape, 1)
        cnt_ref[...] = jnp.where(lane == step, jnp.sum(sel, axis=1, keepdims=True), cnt_ref[...])
    else:
        gates = [functools.reduce(jnp.add, [jnp.where(ids[g] == e, w, 0.0) for e, w in zip(eids, weights)])
                 for g in range(N_GROUPS)]
        gates = jnp.concatenate(gates + [jnp.zeros((LANES - N_EXPERTS, tt), F32)], axis=0)
        tok_ref[0] = gates.T


def _toeplitz(v, rows, cols):
    w = rows + cols
    flat = jnp.tile(v, (1, rows))[:, :rows * (w - 1)]
    return flat.reshape(v.shape[0], rows, w - 1)[:, :, :cols]


def _rel_bias_blocks(table, cfg, q_pos0, k_pos0):
    q_pos = q_pos0 + jnp.arange(cfg.qb)
    k_pos = k_pos0 + jnp.arange(cfg.kb)
    qc = q_pos // CHUNK
    kc = k_pos // CHUNK
    allowed = (kc[None, :] <= qc[:, None]) & (kc[None, :] >= qc[:, None] - BAND_CHUNKS)
    w = cfg.qb + cfg.kb
    dj = jnp.arange(w)
    dj = jnp.where(dj < cfg.kb, dj, dj - w)
    rel = jnp.clip(q_pos0 - k_pos0 - dj, -MAX_REL, MAX_REL) + MAX_REL
    bias = _toeplitz(table[:, rel].astype(F32), cfg.qb, cfg.kb)
    return jnp.where(allowed[None], bias, NEG)


def _mixer_call(cfg, x, mod, p, cache=None):
    b, t_total, d = x.shape
    nt = t_total // cfg.tt
    conv_dim = p["conv_w"].shape[1]
    n_in = p["w_in"].shape[1]

    def const(shape):
        return pl.BlockSpec(shape, lambda i, j: (0,) * len(shape), pipeline_mode=pl.Buffered(1))

    in_specs = [
        pl.BlockSpec((1, cfg.tt, d), lambda i, j: (i, j, 0)),
        pl.BlockSpec((1, 6, d), lambda i, j: (i, 0, 0)),
        const((1, d)), const((1, d)), const((d, n_in)),
        const((1, ATTN_DIM)), const((1, ATTN_DIM)), const((ATTN_DIM, ATTN_DIM)),
        const((N_HEADS, cfg.qb, cfg.kb)), const((ATTN_DIM, d)),
        const((CONV_WIDTH, conv_dim)), const((1, conv_dim)), const((1, conv_dim)), const((1, conv_dim)),
        const((conv_dim, d)), const((1, d)), const((1, 2 * d)), const((d, d)),
        const((N_EXPERTS, d)), const((N_EXPERTS, d)), const((N_EXPERTS, 1)),
    ]
    args = [x, mod, p["norm1_g"], p["norm2_g"], p["w_in"], p["q_norm_g"], p["k_norm_g"], p["hsum"],
            p["bias"], p["w_attn_o"], p["conv_w"], p["conv_b"], p["conv_ln_g"], p["conv_ln_b"],
            p["w_conv_o"], p["b_conv_o"], p["b_gate"], p["w_out"], p["wr_hi"], p["wr_lo"], p["router_bias"]]
    if cfg.has_cache:
        in_specs += [pl.BlockSpec((1, PREV, ATTN_DIM), lambda i, j: (i, 0, 0)),
                     pl.BlockSpec((1, PREV, ATTN_DIM), lambda i, j: (i, 0, 0)),
                     pl.BlockSpec((1, CONV_WIDTH - 1, conv_dim), lambda i, j: (i, 0, 0))]
        args += list(cache)
    if cfg.sparse:
        assert b * nt <= MAX_TILES
        in_specs += [const((cfg.tt, cfg.tt))]
        tok = jnp.arange(cfg.tt)
        args += [(tok[:, None] < tok[None, :]).astype(BF16)]
    keep = min(PREV, t_total)
    assert keep == cfg.tt
    out_shape = [jax.ShapeDtypeStruct((b, t_total, d), F32),
                 jax.ShapeDtypeStruct((b, t_total, d // 2), I32),
                 jax.ShapeDtypeStruct((b, t_total, LANES), F32),
                 jax.ShapeDtypeStruct((b, keep, ATTN_DIM), F32),
                 jax.ShapeDtypeStruct((b, keep, ATTN_DIM), F32),
                 jax.ShapeDtypeStruct((b, CONV_WIDTH - 1, conv_dim), F32)]
    out_specs = [pl.BlockSpec((1, cfg.tt, d), lambda i, j: (i, j, 0)),
                 pl.BlockSpec((1, cfg.tt, d // 2), lambda i, j: (i, j, 0)),
                 pl.BlockSpec((1, cfg.tt, LANES), lambda i, j: (i, j, 0)),
                 pl.BlockSpec((1, keep, ATTN_DIM), lambda i, j: (i, 0, 0)),
                 pl.BlockSpec((1, keep, ATTN_DIM), lambda i, j: (i, 0, 0)),
                 pl.BlockSpec((1, CONV_WIDTH - 1, conv_dim), lambda i, j: (i, 0, 0))]
    if cfg.sparse:
        out_shape += [jax.ShapeDtypeStruct((TOP_K, b * t_total), F32),
                      jax.ShapeDtypeStruct((TOP_K, b * t_total), F32),
                      jax.ShapeDtypeStruct((N_EXPERTS, MAX_TILES), F32)]
        out_specs += [pl.BlockSpec((TOP_K, cfg.tt), lambda i, j: (0, i * nt + j)),
                      pl.BlockSpec((TOP_K, cfg.tt), lambda i, j: (0, i * nt + j)),
                      pl.BlockSpec((N_EXPERTS, MAX_TILES), lambda i, j: (0, 0))]
    scratch = [pltpu.VMEM((cfg.tt, ATTN_DIM), BF16),
               pltpu.VMEM((PREV + cfg.tt, ATTN_DIM), BF16),
               pltpu.VMEM((PREV + cfg.tt, ATTN_DIM), BF16),
               pltpu.VMEM((cfg.tt, ATTN_DIM), BF16),
               pltpu.VMEM((CONV_PAD + cfg.tt, conv_dim), F32),
               pltpu.VMEM((cfg.tt, conv_dim), F32)]
    return pl.pallas_call(
        functools.partial(_mixer_kernel, cfg),
        out_shape=out_shape,
        grid=(b, nt),
        in_specs=in_specs,
        out_specs=out_specs,
        scratch_shapes=scratch,
        compiler_params=pltpu.CompilerParams(dimension_semantics=("arbitrary", "arbitrary"),
                                             vmem_limit_bytes=VMEM_LIMIT_BYTES),
        name="mixer_sample" if cfg.has_cache else "mixer_prompt",
    )(*args)


def _exact_parts(x, n):
    parts = []
    for _ in range(n):
        part = x.astype(BF16)
        parts.append(part)
        x = x - part.astype(F32)
    return parts


def _plan_kernel(tt, cnt_ref, eid_ref, rank_ref, tri_e_ref, tri_t_ref, pos_ref, blk_ref, nblk_ref, base_ref):
    step = pl.program_id(0)

    @pl.when(step == 0)
    def _():
        cnt = cnt_ref[...]
        total = jnp.sum(cnt, axis=1, keepdims=True)
        padded = jnp.floor((total + float(ROW_BLOCK - 1)) * (1.0 / ROW_BLOCK)) * float(ROW_BLOCK)
        padded = jnp.broadcast_to(padded, cnt.shape)
        start = functools.reduce(jnp.add, [_dot(tri_e_ref[...], part) for part in _exact_parts(padded, 3)])
        before = functools.reduce(jnp.add, [_dot(part, tri_t_ref[...]) for part in _exact_parts(cnt, 2)])
        base_ref[...] = start + before
        end = (start + padded)[:, 0:1]
        first_row = lax.broadcasted_iota(I32, blk_ref.shape, 1).astype(F32) * float(ROW_BLOCK)
        owner = jnp.sum(jnp.where(end <= first_row, 1.0, 0.0), axis=0, keepdims=True)
        blk_ref[...] = jnp.minimum(owner, float(N_EXPERTS - 1)).astype(I32)
        nblk = jnp.max(end, axis=0, keepdims=True) * (1.0 / ROW_BLOCK)
        nblk_ref[...] = jnp.broadcast_to(nblk, nblk_ref.shape).astype(I32)

    lane = lax.broadcasted_iota(I32, base_ref.shape, 1)
    col = jnp.sum(jnp.where(lane == step, base_ref[...], 0.0), axis=1, keepdims=True)
    ids = _expert_ids(tt)
    col_slabs = [jnp.broadcast_to(col[g * GROUP_SIZE:(g + 1) * GROUP_SIZE, :], (GROUP_SIZE, tt))
                 for g in range(N_GROUPS)]
    eid = eid_ref[...]
    rows = [_pick(ids, eid[k:k + 1, :], col_slabs) for k in range(TOP_K)]
    pos_ref[...] = (jnp.concatenate(rows, axis=0) + rank_ref[...]).astype(I32)


def _plan_call(cnt, eid, rank, tt, n_blocks):
    n = eid.shape[1]
    e = jnp.arange(N_EXPERTS)
    s = jnp.arange(MAX_TILES)
    nb_pad = -(-n_blocks // LANES) * LANES
    return pl.pallas_call(
        functools.partial(_plan_kernel, tt),
        out_shape=[jax.ShapeDtypeStruct((TOP_K, n), I32),
                   jax.ShapeDtypeStruct((1, nb_pad), I32),
                   jax.ShapeDtypeStruct((1, LANES), I32)],
        grid=(n // tt,),
        in_specs=[pl.BlockSpec((N_EXPERTS, MAX_TILES), lambda i: (0, 0)),
                  pl.BlockSpec((TOP_K, tt), lambda i: (0, i)),
                  pl.BlockSpec((TOP_K, tt), lambda i: (0, i)),
                  pl.BlockSpec((N_EXPERTS, N_EXPERTS), lambda i: (0, 0)),
                  pl.BlockSpec((MAX_TILES, MAX_TILES), lambda i: (0, 0))],
        out_specs=[pl.BlockSpec((TOP_K, tt), lambda i: (0, i)),
                   pl.BlockSpec((1, nb_pad), lambda i: (0, 0)),
                   pl.BlockSpec((1, LANES), lambda i: (0, 0))],
        scratch_shapes=[pltpu.VMEM((N_EXPERTS, MAX_TILES), F32)],
        compiler_params=pltpu.CompilerParams(dimension_semantics=("arbitrary",),
                                             vmem_limit_bytes=VMEM_LIMIT_BYTES),
        name="plan",
    )(cnt, eid, rank, (e[None, :] < e[:, None]).astype(BF16), (s[:, None] < s[None, :]).astype(BF16))


def _sc_mesh():
    return plsc.VectorSubcoreMesh(core_axis_name="c", subcore_axis_name="s")


def _dispatch_call(rows, pos, n_sorted):
    n, w = rows.shape
    per_worker = n // SC_WORKERS
    nsteps = per_worker // SC_ROWS
    assert per_worker * SC_WORKERS == n and nsteps * SC_ROWS == per_worker and nsteps % 2 == 0

    @functools.partial(
        pl.kernel, mesh=_sc_mesh(),
        out_type=jax.ShapeDtypeStruct((n_sorted, w), rows.dtype),
        scratch_types=[pltpu.VMEM((2, TOP_K, SC_ROWS), I32),
                       pltpu.VMEM((2, SC_ROWS, w), rows.dtype),
                       pltpu.SemaphoreType.DMA((2,)),
                       pltpu.SemaphoreType.DMA((2,))],
    )
    def dispatch(x_hbm, pos_hbm, out_hbm, idx_v, rows_v, load_sem, scat_sem):
        base = (lax.axis_index("s") * SC_CORES + lax.axis_index("c")) * per_worker

        def load(i, b):
            return pltpu.make_async_copy(x_hbm.at[pl.ds(base + i * SC_ROWS, SC_ROWS)], rows_v.at[b], load_sem.at[b])

        def scatter(b, k):
            return pltpu.make_async_copy(rows_v.at[b], out_hbm.at[idx_v.at[b, k]], scat_sem.at[b])

        def load_start(i, b):
            for k in range(TOP_K):
                pltpu.sync_copy(pos_hbm.at[pl.ds(k * n + base + i * SC_ROWS, SC_ROWS)], idx_v.at[b, k])
            load(i, b).start()

        load_start(0, 0)

        @pl.loop(0, nsteps, step=2)
        def _(i):
            for b in range(2):
                ii = i + b

                @pl.when(ii >= 1)
                def _():
                    for k in range(TOP_K):
                        scatter(1 - b, k).wait()

                @pl.when(ii + 1 < nsteps)
                def _():
                    load_start(ii + 1, 1 - b)

                load(ii, b).wait()
                for k in range(TOP_K):
                    scatter(b, k).start()

        for k in range(TOP_K):
            scatter((nsteps - 1) % 2, k).wait()

    return dispatch(rows, pos)


def _collect_call(table, idx):
    n = idx.shape[0]
    w = table.shape[1]
    per_worker = n // SC_WORKERS
    nsteps = per_worker // SC_ROWS
    assert per_worker * SC_WORKERS == n and nsteps * SC_ROWS == per_worker and nsteps % 2 == 0

    @functools.partial(
        pl.kernel, mesh=_sc_mesh(),
        out_type=jax.ShapeDtypeStruct((n, w), table.dtype),
        scratch_types=[pltpu.VMEM((2, SC_ROWS), I32),
                       pltpu.VMEM((2, SC_ROWS, w), table.dtype),
                       pltpu.SemaphoreType.DMA((2,)),
                       pltpu.SemaphoreType.DMA((2,))],
    )
    def collect(table_hbm, idx_hbm, out_hbm, idx_v, rows_v, gather_sem, write_sem):
        base = (lax.axis_index("s") * SC_CORES + lax.axis_index("c")) * per_worker

        def gather(b):
            return pltpu.make_async_copy(table_hbm.at[idx_v.at[b]], rows_v.at[b], gather_sem.at[b])

        def write(i, b):
            return pltpu.make_async_copy(rows_v.at[b], out_hbm.at[pl.ds(base + i * SC_ROWS, SC_ROWS)], write_sem.at[b])

        def gather_start(i, b):
            pltpu.sync_copy(idx_hbm.at[pl.ds(base + i * SC_ROWS, SC_ROWS)], idx_v.at[b])
            gather(b).start()

        gather_start(0, 0)

        @pl.loop(0, nsteps, step=2)
        def _(i):
            for b in range(2):
                ii = i + b

                @pl.when(ii >= 1)
                def _():
                    write(ii - 1, 1 - b).wait()

                @pl.when(ii + 1 < nsteps)
                def _():
                    gather_start(ii + 1, 1 - b)

                gather(b).wait()
                write(ii, b).start()

        write(nsteps - 1, (nsteps - 1) % 2).wait()

    return collect(table, idx)


def _expert_ffn_kernel(blk_ref, nblk_ref, x_ref, wg_ref, wu_ref, wd_ref, o_ref):
    @pl.when(pl.program_id(0) < nblk_ref[0])
    def _():
        a, b = _unpack(x_ref[...])
        a, b = a.astype(BF16), b.astype(BF16)
        hg = _dot_halves(a, b, wg_ref.at[0])
        act = hg * _sigmoid(hg) * _dot_halves(a, b, wu_ref.at[0])
        y = _dot(act.astype(BF16), wd_ref[0])
        half = y.shape[1] // 2
        o_ref[...] = _pack(y[:, 0:half], y[:, half:2 * half])


def _expert_ffn_call(xs, blk, nblk, p):
    n_sorted, w = xs.shape
    d = 2 * w
    ff = p["w_e_gate"].shape[2]

    def rows(j, blk, nblk):
        return (jnp.minimum(j, nblk[0] - 1), 0)

    def expert(j, blk, nblk):
        return (blk[j], 0, 0)

    return pl.pallas_call(
        _expert_ffn_kernel,
        out_shape=jax.ShapeDtypeStruct((n_sorted, w), I32),
        grid_spec=pltpu.PrefetchScalarGridSpec(
            num_scalar_prefetch=2,
            grid=(n_sorted // ROW_BLOCK,),
            in_specs=[pl.BlockSpec((ROW_BLOCK, w), rows),
                      pl.BlockSpec((1, d, ff), expert),
                      pl.BlockSpec((1, d, ff), expert),
                      pl.BlockSpec((1, ff, d), expert)],
            out_specs=pl.BlockSpec((ROW_BLOCK, w), rows)),
        compiler_params=pltpu.CompilerParams(dimension_semantics=("arbitrary",),
                                             vmem_limit_bytes=VMEM_LIMIT_BYTES),
        name="expert_ffn",
    )(blk, nblk, xs, p["w_e_gate"], p["w_e_up"], p["w_e_down"])


def _swiglu_halves(a, b, wg_ref, wu_ref):
    hg = _dot_halves(a, b, wg_ref)
    return hg * _sigmoid(hg) * _dot_halves(a, b, wu_ref)


def _moe_out_kernel(x1_ref, h_ref, y_ref, w_ref, g2_ref, wsg_ref, wsu_ref, wsd_ref, o_ref):
    a, b = _unpack(h_ref[...])
    shared = _dot(_swiglu_halves(a.astype(BF16), b.astype(BF16), wsg_ref, wsu_ref).astype(BF16), wsd_ref[...])
    half = h_ref.shape[1]
    w = w_ref[...]
    acc_a, acc_b = shared[:, 0:half], shared[:, half:2 * half]
    for k in range(TOP_K):
        ya, yb = _unpack(y_ref[k])
        acc_a = acc_a + w[:, k:k + 1] * ya
        acc_b = acc_b + w[:, k:k + 1] * yb
    g2 = g2_ref[...]
    o_ref[:, 0:half] = x1_ref[:, 0:half] + g2[:, 0:half] * acc_a
    o_ref[:, half:2 * half] = x1_ref[:, half:2 * half] + g2[:, half:2 * half] * acc_b


def _moe_out_call(x1, h2, yg, w_tok, g2, p, tm):
    n, d = x1.shape
    sff = p["w_s_gate"].shape[1]
    per_row = n // g2.shape[0]
    g2 = g2.reshape(g2.shape[0], 1, d)

    def const(shape):
        return pl.BlockSpec(shape, lambda i: (0,) * len(shape), pipeline_mode=pl.Buffered(1))

    return pl.pallas_call(
        _moe_out_kernel,
        out_shape=jax.ShapeDtypeStruct((n, d), F32),
        grid=(n // tm,),
        in_specs=[pl.BlockSpec((tm, d), lambda i: (i, 0)),
                  pl.BlockSpec((tm, d // 2), lambda i: (i, 0)),
                  pl.BlockSpec((TOP_K, tm, d // 2), lambda i: (0, i, 0)),
                  pl.BlockSpec((tm, LANES), lambda i: (i, 0)),
                  pl.BlockSpec((None, 1, d), lambda i: (i * tm // per_row, 0, 0)),
                  const((d, sff)), const((d, sff)), const((sff, d))],
        out_specs=pl.BlockSpec((tm, d), lambda i: (i, 0)),
        compiler_params=pltpu.CompilerParams(dimension_semantics=("arbitrary",),
                                             vmem_limit_bytes=VMEM_LIMIT_BYTES),
        name="moe_out",
    )(x1, h2, yg, w_tok, g2, p["w_s_gate"], p["w_s_up"], p["w_s_down"])


def _moe_dense_kernel(h_ref, gates_ref, x1_ref, g2_ref, wsg_ref, wsu_ref, wsd_ref, wg_ref, wu_ref, wd_ref,
                      o_ref, acc_ref):
    e = pl.program_id(1)
    a, b = _unpack(h_ref[...])
    a, b = a.astype(BF16), b.astype(BF16)

    @pl.when(e == 0)
    def _():
        acc_ref[...] = _dot(_swiglu_halves(a, b, wsg_ref, wsu_ref).astype(BF16), wsd_ref[...])

    gates = gates_ref[...]
    lane = lax.broadcasted_iota(I32, gates.shape, 1)
    gate = jnp.sum(jnp.where(lane == e, gates, 0.0), axis=1, keepdims=True)
    act = _swiglu_halves(a, b, wg_ref.at[0], wu_ref.at[0]) * gate
    acc_ref[...] += _dot(act.astype(BF16), wd_ref[0])

    @pl.when(e == pl.num_programs(1) - 1)
    def _():
        o_ref[...] = x1_ref[...] + g2_ref[...] * acc_ref[...]


def _moe_dense_call(h2, gates, x1, g2_rows, p):
    n, d = x1.shape
    ff = p["w_e_gate"].shape[2]
    sff = p["w_s_gate"].shape[1]

    def const(shape):
        return pl.BlockSpec(shape, lambda i, e: (0,) * len(shape), pipeline_mode=pl.Buffered(1))

    return pl.pallas_call(
        _moe_dense_kernel,
        out_shape=jax.ShapeDtypeStruct((n, d), F32),
        grid=(1, N_EXPERTS),
        in_specs=[const((n, d // 2)), const((n, LANES)), const((n, d)), const((n, d)),
                  const((d, sff)), const((d, sff)), const((sff, d)),
                  pl.BlockSpec((1, d, ff), lambda i, e: (e, 0, 0)),
                  pl.BlockSpec((1, d, ff), lambda i, e: (e, 0, 0)),
                  pl.BlockSpec((1, ff, d), lambda i, e: (e, 0, 0))],
        out_specs=pl.BlockSpec((n, d), lambda i, e: (0, 0)),
        scratch_shapes=[pltpu.VMEM((n, d), F32)],
        compiler_params=pltpu.CompilerParams(dimension_semantics=("arbitrary", "arbitrary"),
                                             vmem_limit_bytes=VMEM_LIMIT_BYTES),
        name="moe_dense",
    )(h2, gates, x1, g2_rows, p["w_s_gate"], p["w_s_up"], p["w_s_down"], p["w_e_gate"], p["w_e_up"], p["w_e_down"])


PROMPT_CFG = MixerCfg(tt=512, qb=128, kb=640, n_qblk=4, kstride=128, has_cache=False, sparse=True)
MOE_OUT_TILE = 512


def _layer_params(l, w_in, q_norm_g, k_norm_g, w_attn_o, conv_w, conv_b, conv_ln_g, conv_ln_b, w_conv_o,
                  b_conv_o, b_gate, w_out, norm1_g, norm2_g, w_router, router_bias,
                  w_e_gate, w_e_up, w_e_down, w_s_gate, w_s_up, w_s_down):
    row = lambda a: a[l].reshape(1, -1)
    wr_t = w_router[l].T
    wr_hi = wr_t.astype(BF16)
    head = jnp.arange(ATTN_DIM) // HEAD_DIM
    return {
        "norm1_g": row(norm1_g), "norm2_g": row(norm2_g), "w_in": w_in[l].astype(BF16),
        "q_norm_g": jnp.tile(q_norm_g[l], N_HEADS).reshape(1, -1),
        "k_norm_g": jnp.tile(k_norm_g[l], N_HEADS).reshape(1, -1),
        "hsum": (head[:, None] == head[None, :]).astype(BF16),
        "w_attn_o": w_attn_o[l].astype(BF16), "conv_w": conv_w[l], "conv_b": row(conv_b),
        "conv_ln_g": row(conv_ln_g), "conv_ln_b": row(conv_ln_b), "w_conv_o": w_conv_o[l].astype(BF16),
        "b_conv_o": row(b_conv_o), "b_gate": row(b_gate), "w_out": w_out[l].astype(BF16),
        "wr_hi": wr_hi, "wr_lo": (wr_t - wr_hi.astype(F32)).astype(BF16),
        "router_bias": router_bias[l].reshape(-1, 1),
        "w_e_gate": w_e_gate[l].astype(BF16), "w_e_up": w_e_up[l].astype(BF16),
        "w_e_down": w_e_down[l].astype(BF16), "w_s_gate": w_s_gate[l].astype(BF16),
        "w_s_up": w_s_up[l].astype(BF16), "w_s_down": w_s_down[l].astype(BF16),
    }


def _sparse_moe(x1, h2, w_tok, eid, rank, cnt, g2, p, tt):
    n, d = x1.shape
    n_sorted = n * TOP_K + N_EXPERTS * ROW_BLOCK
    pos, blk, nblk = _plan_call(cnt, eid, rank, tt, n_sorted // ROW_BLOCK)
    pos = pos.reshape(TOP_K * n)
    xs = _dispatch_call(h2, pos, n_sorted)
    ys = _expert_ffn_call(xs, blk.reshape(-1), nblk.reshape(-1), p)
    yg = _collect_call(ys, pos).reshape(TOP_K, n, d // 2)
    return _moe_out_call(x1, h2, yg, w_tok, g2, p, MOE_OUT_TILE)


def kernel(x_prompt, x_sample, c_prompt, c_sample, cache_k, cache_v, state_conv, w_mod, b_mod, norm1_g, w_in, q_norm_g, k_norm_g, rel_bias, w_attn_o, conv_w, conv_b, conv_ln_g, conv_ln_b, w_conv_o, b_conv_o, b_gate, w_out, norm2_g, w_router, router_bias, w_e_gate, w_e_up, w_e_down, w_s_gate, w_s_up, w_s_down):
    depth = w_mod.shape[0]
    bp, tp, d = x_prompt.shape
    bs, ts, _ = x_sample.shape
    assert cache_k.shape[2] == min(PREV, PAST_LEN) == PREV
    sample_cfg = MixerCfg(tt=ts, qb=ts, kb=PREV + ts, n_qblk=1, kstride=0, has_cache=True, sparse=False)

    yp, ys = x_prompt, x_sample
    outs = [[] for _ in range(6)]
    for l in range(depth):
        p = _layer_params(l, w_in, q_norm_g, k_norm_g, w_attn_o, conv_w, conv_b, conv_ln_g, conv_ln_b,
                          w_conv_o, b_conv_o, b_gate, w_out, norm1_g, norm2_g, w_router, router_bias,
                          w_e_gate, w_e_up, w_e_down, w_s_gate, w_s_up, w_s_down)
        mod = _mod_call(jnp.concatenate([c_prompt, c_sample], axis=0), w_mod[l], b_mod[l])
        mod = mod.reshape(bp + bs, 6, d)
        mod_p, mod_s = mod[:bp], mod[bp:]

        p["bias"] = _rel_bias_blocks(rel_bias[l], PROMPT_CFG, PREV, 0)
        x1, h2, w_tok, kp, vp, cp, eid, rank, cnt = _mixer_call(PROMPT_CFG, yp, mod_p, p)
        yp = _sparse_moe(x1.reshape(bp * tp, d), h2.reshape(bp * tp, d // 2), w_tok.reshape(bp * tp, LANES),
                         eid, rank, cnt, mod_p[:, 5, :], p, PROMPT_CFG.tt).reshape(bp, tp, d)
        outs[0].append(kp.reshape(bp, -1, N_HEADS, HEAD_DIM))
        outs[1].append(vp.reshape(bp, -1, N_HEADS, HEAD_DIM))
        outs[2].append(cp)

        p["bias"] = _rel_bias_blocks(rel_bias[l], sample_cfg, PAST_LEN, PAST_LEN - PREV)
        cache = (cache_k[l].reshape(bs, PREV, ATTN_DIM), cache_v[l].reshape(bs, PREV, ATTN_DIM), state_conv[l])
        x1, h2, gates, ks, vs, cs = _mixer_call(sample_cfg, ys, mod_s, p, cache)
        ys = _moe_dense_call(h2.reshape(bs * ts, d // 2), gates.reshape(bs * ts, LANES), x1.reshape(bs * ts, d),
                             jnp.repeat(mod_s[:, 5, :], ts, axis=0), p).reshape(bs, ts, d)
        outs[3].append(ks.reshape(bs, ts, N_HEADS, HEAD_DIM))
        outs[4].append(vs.reshape(bs, ts, N_HEADS, HEAD_DIM))
        outs[5].append(cs)
    return (yp, ys) + tuple(jnp.stack(o) for o in outs)
```

```python
import dataclasses
import functools

import jax
import jax.numpy as jnp
from jax import lax
from jax.experimental import pallas as pl
from jax.experimental.pallas import tpu as pltpu
from jax.experimental.pallas import tpu_sc as plsc

F32 = jnp.float32
BF16 = jnp.bfloat16
I32 = jnp.int32

CHUNK = 64
BAND_CHUNKS = 8
PREV = BAND_CHUNKS * CHUNK
PAST_LEN = 1024
N_HEADS = 8
HEAD_DIM = 64
ATTN_DIM = N_HEADS * HEAD_DIM
MAX_REL = 128
CONV_WIDTH = 31
CONV_PAD = 32
N_EXPERTS = 64
N_GROUPS = 8
GROUP_SIZE = N_EXPERTS // N_GROUPS
TOPK_GROUPS = 4
TOP_K = 8
ROUTE_SCALE = 2.5
EPS = 1e-6
LANES = 128
SUBLANES = 8
NEG = -1e30

VMEM_LIMIT_BYTES = 60 * 1024 * 1024

SC_CORES = 2
SC_SUBCORES = 16
SC_WORKERS = SC_CORES * SC_SUBCORES
SC_ROWS = 64

MAX_TILES = LANES
ROW_BLOCK = 1024


def _sigmoid(x):
    return 1.0 / (1.0 + jnp.exp(-x))


def _sigmoid_t(x):
    return 0.5 * jnp.tanh(0.5 * x) + 0.5


def _split_bf16(x):
    hi = x.astype(BF16)
    lo = (x - hi.astype(F32)).astype(BF16)
    return hi, lo


def _dot(a, b):
    return jnp.dot(a, b, preferred_element_type=F32)


def _dot3(a, b):
    a_hi, a_lo = _split_bf16(a)
    b_hi, b_lo = _split_bf16(b)
    return _dot(a_hi, b_hi) + _dot(a_hi, b_lo) + _dot(a_lo, b_hi)


def _pack(a, b):
    ia = lax.bitcast_convert_type(a.astype(BF16).astype(F32), I32)
    ib = lax.bitcast_convert_type(b.astype(BF16).astype(F32), I32)
    return ia | lax.shift_right_logical(ib, 16)


def _unpack(p):
    a = lax.bitcast_convert_type(p & jnp.int32(-65536), F32)
    b = lax.bitcast_convert_type(lax.shift_left(p, 16), F32)
    return a, b


def _dot_halves(a, b, w_ref):
    half = a.shape[1]
    return _dot(a, w_ref[0:half, :]) + _dot(b, w_ref[half:2 * half, :])


def _mod_kernel(c_ref, w_ref, b_ref, o_ref):
    c = c_ref[...]
    o_ref[...] = _dot3(c * _sigmoid(c), w_ref[...]) + b_ref[...]


def _mod_call(c, w_mod, b_mod):
    n, d = c.shape
    dout = w_mod.shape[1]
    bn = 1536
    return pl.pallas_call(
        _mod_kernel,
        out_shape=jax.ShapeDtypeStruct((n, dout), F32),
        grid=(dout // bn,),
        in_specs=[pl.BlockSpec((n, d), lambda j: (0, 0)),
                  pl.BlockSpec((d, bn), lambda j: (0, j)),
                  pl.BlockSpec((1, bn), lambda j: (0, j))],
        out_specs=pl.BlockSpec((n, bn), lambda j: (0, j)),
        compiler_params=pltpu.CompilerParams(dimension_semantics=("arbitrary",),
                                             vmem_limit_bytes=VMEM_LIMIT_BYTES),
        name="mod",
    )(c, w_mod, b_mod.reshape(1, dout))


@dataclasses.dataclass(frozen=True)
class MixerCfg:
    tt: int
    qb: int
    kb: int
    n_qblk: int
    kstride: int
    has_cache: bool
    sparse: bool


def _rms(x, g):
    ms = jnp.mean(x * x, axis=-1, keepdims=True)
    return x * lax.rsqrt(ms + EPS) * g


def _head_rms(z, hsum, g):
    ss = _dot((z * z).astype(BF16), hsum)
    return z * lax.rsqrt(ss * (1.0 / HEAD_DIM) + EPS) * g


def _expert_ids(t):
    sub = lax.broadcasted_iota(I32, (GROUP_SIZE, t), 0).astype(F32)
    return [sub + float(g * GROUP_SIZE) for g in range(N_GROUPS)]


def _pick(ids, eid, slabs):
    acc = functools.reduce(jnp.add, [jnp.where(ids[g] == eid, slabs[g], 0.0) for g in range(N_GROUPS)])
    return jnp.sum(acc, axis=0, keepdims=True)


def _route(choice, s):
    t = choice.shape[1]
    sub = lax.broadcasted_iota(I32, (GROUP_SIZE, t), 0).astype(F32)
    slabs = [choice[g * GROUP_SIZE:(g + 1) * GROUP_SIZE, :] for g in range(N_GROUPS)]
    s_slabs = [s[g * GROUP_SIZE:(g + 1) * GROUP_SIZE, :] for g in range(N_GROUPS)]
    gscore = []
    for c in slabs:
        m1 = jnp.max(c, axis=0, keepdims=True)
        first = jnp.min(jnp.where(c == m1, sub, float(GROUP_SIZE)), axis=0, keepdims=True)
        m2 = jnp.max(jnp.where(sub == first, -jnp.inf, c), axis=0, keepdims=True)
        gscore.append(m1 + m2)
    gsel = [jnp.zeros((1, t), F32) for _ in range(N_GROUPS)]
    for _ in range(TOPK_GROUPS):
        m = functools.reduce(jnp.maximum, gscore)
        first = functools.reduce(jnp.minimum,
                                 [jnp.where(gscore[g] == m, float(g), float(N_GROUPS)) for g in range(N_GROUPS)])
        for g in range(N_GROUPS):
            hit = first == float(g)
            gsel[g] = jnp.where(hit, 1.0, gsel[g])
            gscore[g] = jnp.where(hit, -jnp.inf, gscore[g])
    masked = [jnp.where(gsel[g] > 0.5, slabs[g], -jnp.inf) for g in range(N_GROUPS)]
    ids = _expert_ids(t)
    eids, raw = [], []
    for _ in range(TOP_K):
        m = jnp.max(functools.reduce(jnp.maximum, masked), axis=0, keepdims=True)
        cand = [jnp.where(masked[g] == m, ids[g], float(N_EXPERTS)) for g in range(N_GROUPS)]
        first = jnp.min(functools.reduce(jnp.minimum, cand), axis=0, keepdims=True)
        eids.append(first)
        raw.append(_pick(ids, first, s_slabs))
        masked = [jnp.where(ids[g] == first, -jnp.inf, masked[g]) for g in range(N_GROUPS)]
    wsum = functools.reduce(jnp.add, raw)
    return eids, [r / wsum * ROUTE_SCALE for r in raw]


def _lane_dense_rows(rows, t):
    pad = jnp.zeros((LANES - len(rows), t), F32)
    return jnp.concatenate(rows + [pad], axis=0).T


def _mixer_kernel(cfg, *refs):
    (x_ref, mod_ref, n1g_ref, n2g_ref, win_ref, qg_ref, kg_ref, hsum_ref, bias_ref, wao_ref,
     cw_ref, cb_ref, lng_ref, lnb_ref, wco_ref, bco_ref, bg_ref, wout_ref,
     wrh_ref, wrl_ref, rb_ref) = refs[:21]
    refs = refs[21:]
    if cfg.has_cache:
        kc_ref, vc_ref, cs_ref = refs[:3]
        refs = refs[3:]
    if cfg.sparse:
        tri_ref = refs[0]
        refs = refs[1:]
    x1_ref, h2_ref, tok_ref, kout_ref, vout_ref, cout_ref = refs[:6]
    refs = refs[6:]
    if cfg.sparse:
        eid_ref, rank_ref, cnt_ref = refs[:3]
        refs = refs[3:]
    qbuf, kbuf, vbuf, obuf, uext, cvbuf = refs

    tt = cfg.tt
    t = pl.program_id(1)

    if cfg.has_cache:
        kbuf[0:PREV, :] = kc_ref[0].astype(BF16)
        vbuf[0:PREV, :] = vc_ref[0].astype(BF16)
        uext[CONV_PAD - (CONV_WIDTH - 1):CONV_PAD, :] = cs_ref[0]
    else:
        @pl.when(t == 0)
        def _():
            kbuf[0:PREV, :] = jnp.zeros((PREV, ATTN_DIM), BF16)
            vbuf[0:PREV, :] = jnp.zeros((PREV, ATTN_DIM), BF16)
            uext[0:CONV_PAD, :] = jnp.zeros((CONV_PAD, uext.shape[1]), F32)

    mod = mod_ref[0]
    sh1, sc1, g1, sh2, sc2 = (mod[i:i + 1, :] for i in range(5))

    x = x_ref[0]
    hb = (_rms(x, n1g_ref[...]) * (1.0 + sc1) + sh1).astype(BF16)

    a0, a1, a2, a3, a4, a5 = (0, ATTN_DIM, 2 * ATTN_DIM, 3 * ATTN_DIM,
                              3 * ATTN_DIM + cw_ref.shape[1], 3 * ATTN_DIM + 2 * cw_ref.shape[1])
    d_model = x.shape[1]

    q = _head_rms(_dot(hb, win_ref[:, a0:a1]), hsum_ref[...], qg_ref[...])
    qbuf[...] = (q * (HEAD_DIM ** -0.5)).astype(BF16)
    k = _head_rms(_dot(hb, win_ref[:, a1:a2]), hsum_ref[...], kg_ref[...])
    kout_ref[0] = k
    kbuf[PREV:PREV + tt, :] = k.astype(BF16)
    v = _dot(hb, win_ref[:, a2:a3])
    vout_ref[0] = v
    vbuf[PREV:PREV + tt, :] = v.astype(BF16)

    u = _dot(hb, win_ref[:, a3:a4]) * _sigmoid_t(_dot(hb, win_ref[:, a4:a5]))
    uext[CONV_PAD:CONV_PAD + tt, :] = u
    cout_ref[0] = uext[CONV_PAD + tt - (CONV_WIDTH - 1):CONV_PAD + tt, :]

    def attend(r0, k0):
        if not cfg.has_cache:
            col = lax.broadcasted_iota(I32, (1, cfg.kb), 1)
            valid = jnp.logical_or(col >= PREV - k0, t > 0)
        heads = []
        for h in range(N_HEADS):
            c0, c1 = h * HEAD_DIM, (h + 1) * HEAD_DIM
            s = lax.dot_general(qbuf[pl.ds(r0, cfg.qb), c0:c1], kbuf[pl.ds(k0, cfg.kb), c0:c1],
                                (((1,), (1,)), ((), ())), preferred_element_type=F32)
            s = s + bias_ref[h]
            if not cfg.has_cache:
                s = jnp.where(valid, s, NEG)
            p = jnp.exp(s - jnp.max(s, axis=-1, keepdims=True))
            l = jnp.sum(p, axis=-1, keepdims=True)
            heads.append(_dot(p.astype(BF16), vbuf[pl.ds(k0, cfg.kb), c0:c1]) * (1.0 / l))
        obuf[pl.ds(r0, cfg.qb), :] = jnp.concatenate(heads, axis=1).astype(BF16)

    rc = min(cfg.qb, 64)
    cc = min(u.shape[1], 256)
    shifts = [j + CONV_PAD - (CONV_WIDTH - 1) for j in range(CONV_WIDTH)]

    def conv_rows(r0):
        for c0 in range(0, u.shape[1], cc):
            acc = jnp.broadcast_to(cb_ref[:, c0:c0 + cc], (rc, cc))
            for res in range(SUBLANES):
                group = [s for s in shifts if s % SUBLANES == res]
                if not group:
                    continue
                lo, hi = min(group) - res, max(group) - res
                rows = hi - lo + rc + (SUBLANES if res else 0)
                slab = uext[pl.ds(r0 + lo, rows), c0:c0 + cc]
                if res:
                    slab = pltpu.roll(slab, rows - res, 0)
                for s in group:
                    j = s - shifts[0]
                    a = s - res - lo
                    acc = acc + cw_ref[j:j + 1, c0:c0 + cc] * slab[a:a + rc, :]
            cvbuf[pl.ds(r0, rc), c0:c0 + cc] = acc

    def mix_block(r0, k0):
        attend(r0, k0)
        for i in range(cfg.qb // rc):
            conv_rows(r0 + i * rc)

    if cfg.n_qblk == 1:
        mix_block(0, 0)
    else:
        def mix_block_step(j, carry):
            mix_block(pl.multiple_of(j * cfg.qb, cfg.qb), pl.multiple_of(j * cfg.kstride, cfg.kstride))
            return carry
        lax.fori_loop(0, cfg.n_qblk, mix_block_step, 0)
    y_attn = _dot(obuf[...], wao_ref[...])

    if not cfg.has_cache:
        uext[0:CONV_PAD, :] = uext[tt:tt + CONV_PAD, :]
    cv = cvbuf[...]
    mu = jnp.mean(cv, axis=-1, keepdims=True)
    xc = cv - mu
    var = jnp.mean(xc * xc, axis=-1, keepdims=True)
    cv = xc * lax.rsqrt(var + EPS) * lng_ref[...] + lnb_ref[...]
    cv = cv * _sigmoid_t(cv)
    y_conv = _dot(cv.astype(BF16), wco_ref[...]) + bco_ref[...]

    g_attn = _sigmoid_t(_dot(hb, win_ref[:, a5:a5 + d_model]) + bg_ref[:, 0:d_model])
    g_conv = _sigmoid_t(_dot(hb, win_ref[:, a5 + d_model:a5 + 2 * d_model]) + bg_ref[:, d_model:2 * d_model])
    m = g_attn * y_attn + g_conv * y_conv
    x1 = x + g1 * _dot(m.astype(BF16), wout_ref[...])
    x1_ref[0] = x1

    h2 = _rms(x1, n2g_ref[...]) * (1.0 + sc2) + sh2
    h2_ref[0] = _pack(h2[:, 0:d_model // 2], h2[:, d_model // 2:d_model])
    h2_hi, h2_lo = _split_bf16(h2)
    nt_dims = (((1,), (1,)), ((), ()))
    logits = (lax.dot_general(wrh_ref[...], h2_hi, nt_dims, preferred_element_type=F32)
              + lax.dot_general(wrl_ref[...], h2_hi, nt_dims, preferred_element_type=F32)
              + lax.dot_general(wrh_ref[...], h2_lo, nt_dims, preferred_element_type=F32))
    s = _sigmoid(logits)
    eids, weights = _route(s + rb_ref[...], s)
    ids = _expert_ids(tt)
    if cfg.sparse:
        sel = [functools.reduce(jnp.add, [jnp.where(ids[g] == e, 1.0, 0.0) for e in eids]) for g in range(N_GROUPS)]
        sel = jnp.concatenate(sel, axis=0)
        rank = _dot(sel.astype(BF16), tri_ref[...])
        rank_slabs = [rank[g * GROUP_SIZE:(g + 1) * GROUP_SIZE, :] for g in range(N_GROUPS)]
        tok_ref[0] = _lane_dense_rows(weights, tt)
        eid_ref[...] = jnp.concatenate(eids, axis=0)
        rank_ref[...] = jnp.concatenate([_pick(ids, e, rank_slabs) for e in eids], axis=0)
        step = pl.program_id(0) * pl.num_programs(1) + t

        @pl.when(step == 0)
        def _():
            cnt_ref[...] = jnp.zeros(cnt_ref.shape, F32)

        lane = lax.broadcasted_iota(I32, cnt_ref.shape, 1)
        cnt_ref[...] = jnp.where(lane == step, jnp.sum(sel, axis=1, keepdims=True), cnt_ref[...])
    else:
        gates = [functools.reduce(jnp.add, [jnp.where(ids[g] == e, w, 0.0) for e, w in zip(eids, weights)])
                 for g in range(N_GROUPS)]
        gates = jnp.concatenate(gates + [jnp.zeros((LANES - N_EXPERTS, tt), F32)], axis=0)
        tok_ref[0] = gates.T


def _toeplitz(v, rows, cols):
    w = rows + cols
    flat = jnp.tile(v, (1, rows))[:, :rows * (w - 1)]
    return flat.reshape(v.shape[0], rows, w - 1)[:, :, :cols]


def _rel_bias_blocks(table, cfg, q_pos0, k_pos0):
    q_pos = q_pos0 + jnp.arange(cfg.qb)
    k_pos = k_pos0 + jnp.arange(cfg.kb)
    qc = q_pos // CHUNK
    kc = k_pos // CHUNK
    allowed = (kc[None, :] <= qc[:, None]) & (kc[None, :] >= qc[:, None] - BAND_CHUNKS)
    w = cfg.qb + cfg.kb
    dj = jnp.arange(w)
    dj = jnp.where(dj < cfg.kb, dj, dj - w)
    rel = jnp.clip(q_pos0 - k_pos0 - dj, -MAX_REL, MAX_REL) + MAX_REL
    bias = _toeplitz(table[:, rel].astype(F32), cfg.qb, cfg.kb)
    return jnp.where(allowed[None], bias, NEG)


def _mixer_call(cfg, x, mod, p, cache=None):
    b, t_total, d = x.shape
    nt = t_total // cfg.tt
    conv_dim = p["conv_w"].shape[1]
    n_in = p["w_in"].shape[1]

    def const(shape):
        return pl.BlockSpec(shape, lambda i, j: (0,) * len(shape), pipeline_mode=pl.Buffered(1))

    in_specs = [
        pl.BlockSpec((1, cfg.tt, d), lambda i, j: (i, j, 0)),
        pl.BlockSpec((1, 6, d), lambda i, j: (i, 0, 0)),
        const((1, d)), const((1, d)), const((d, n_in)),
        const((1, ATTN_DIM)), const((1, ATTN_DIM)), const((ATTN_DIM, ATTN_DIM)),
        const((N_HEADS, cfg.qb, cfg.kb)), const((ATTN_DIM, d)),
        const((CONV_WIDTH, conv_dim)), const((1, conv_dim)), const((1, conv_dim)), const((1, conv_dim)),
        const((conv_dim, d)), const((1, d)), const((1, 2 * d)), const((d, d)),
        const((N_EXPERTS, d)), const((N_EXPERTS, d)), const((N_EXPERTS, 1)),
    ]
    args = [x, mod, p["norm1_g"], p["norm2_g"], p["w_in"], p["q_norm_g"], p["k_norm_g"], p["hsum"],
            p["bias"], p["w_attn_o"], p["conv_w"], p["conv_b"], p["conv_ln_g"], p["conv_ln_b"],
            p["w_conv_o"], p["b_conv_o"], p["b_gate"], p["w_out"], p["wr_hi"], p["wr_lo"], p["router_bias"]]
    if cfg.has_cache:
        in_specs += [pl.BlockSpec((1, PREV, ATTN_DIM), lambda i, j: (i, 0, 0)),
                     pl.BlockSpec((1, PREV, ATTN_DIM), lambda i, j: (i, 0, 0)),
                     pl.BlockSpec((1, CONV_WIDTH - 1, conv_dim), lambda i, j: (i, 0, 0))]
        args += list(cache)
    if cfg.sparse:
        assert b * nt <= MAX_TILES
        in_specs += [const((cfg.tt, cfg.tt))]
        tok = jnp.arange(cfg.tt)
        args += [(tok[:, None] < tok[None, :]).astype(BF16)]
    keep = min(PREV, t_total)
    assert keep == cfg.tt
    out_shape = [jax.ShapeDtypeStruct((b, t_total, d), F32),
                 jax.ShapeDtypeStruct((b, t_total, d // 2), I32),
                 jax.ShapeDtypeStruct((b, t_total, LANES), F32),
                 jax.ShapeDtypeStruct((b, keep, ATTN_DIM), F32),
                 jax.ShapeDtypeStruct((b, keep, ATTN_DIM), F32),
                 jax.ShapeDtypeStruct((b, CONV_WIDTH - 1, conv_dim), F32)]
    out_specs = [pl.BlockSpec((1, cfg.tt, d), lambda i, j: (i, j, 0)),
                 pl.BlockSpec((1, cfg.tt, d // 2), lambda i, j: (i, j, 0)),
                 pl.BlockSpec((1, cfg.tt, LANES), lambda i, j: (i, j, 0)),
                 pl.BlockSpec((1, keep, ATTN_DIM), lambda i, j: (i, 0, 0)),
                 pl.BlockSpec((1, keep, ATTN_DIM), lambda i, j: (i, 0, 0)),
                 pl.BlockSpec((1, CONV_WIDTH - 1, conv_dim), lambda i, j: (i, 0, 0))]
    if cfg.sparse:
        out_shape += [jax.ShapeDtypeStruct((TOP_K, b * t_total), F32),
                      jax.ShapeDtypeStruct((TOP_K, b * t_total), F32),
                      jax.ShapeDtypeStruct((N_EXPERTS, MAX_TILES), F32)]
        out_specs += [pl.BlockSpec((TOP_K, cfg.tt), lambda i, j: (0, i * nt + j)),
                      pl.BlockSpec((TOP_K, cfg.tt), lambda i, j: (0, i * nt + j)),
                      pl.BlockSpec((N_EXPERTS, MAX_TILES), lambda i, j: (0, 0))]
    scratch = [pltpu.VMEM((cfg.tt, ATTN_DIM), BF16),
               pltpu.VMEM((PREV + cfg.tt, ATTN_DIM), BF16),
               pltpu.VMEM((PREV + cfg.tt, ATTN_DIM), BF16),
               pltpu.VMEM((cfg.tt, ATTN_DIM), BF16),
               pltpu.VMEM((CONV_PAD + cfg.tt, conv_dim), F32),
               pltpu.VMEM((cfg.tt, conv_dim), F32)]
    return pl.pallas_call(
        functools.partial(_mixer_kernel, cfg),
        out_shape=out_shape,
        grid=(b, nt),
        in_specs=in_specs,
        out_specs=out_specs,
        scratch_shapes=scratch,
        compiler_params=pltpu.CompilerParams(dimension_semantics=("arbitrary", "arbitrary"),
                                             vmem_limit_bytes=VMEM_LIMIT_BYTES),
        name="mixer_sample" if cfg.has_cache else "mixer_prompt",
    )(*args)


def _exact_parts(x, n):
    parts = []
    for _ in range(n):
        part = x.astype(BF16)
        parts.append(part)
        x = x - part.astype(F32)
    return parts


def _plan_kernel(tt, cnt_ref, eid_ref, rank_ref, tri_e_ref, tri_t_ref, pos_ref, blk_ref, nblk_ref, base_ref):
    step = pl.program_id(0)

    @pl.when(step == 0)
    def _():
        cnt = cnt_ref[...]
        total = jnp.sum(cnt, axis=1, keepdims=True)
        padded = jnp.floor((total + float(ROW_BLOCK - 1)) * (1.0 / ROW_BLOCK)) * float(ROW_BLOCK)
        padded = jnp.broadcast_to(padded, cnt.shape)
        start = functools.reduce(jnp.add, [_dot(tri_e_ref[...], part) for part in _exact_parts(padded, 3)])
        before = functools.reduce(jnp.add, [_dot(part, tri_t_ref[...]) for part in _exact_parts(cnt, 2)])
        base_ref[...] = start + before
        end = (start + padded)[:, 0:1]
        first_row = lax.broadcasted_iota(I32, blk_ref.shape, 1).astype(F32) * float(ROW_BLOCK)
        owner = jnp.sum(jnp.where(end <= first_row, 1.0, 0.0), axis=0, keepdims=True)
        blk_ref[...] = jnp.minimum(owner, float(N_EXPERTS - 1)).astype(I32)
        nblk = jnp.max(end, axis=0, keepdims=True) * (1.0 / ROW_BLOCK)
        nblk_ref[...] = jnp.broadcast_to(nblk, nblk_ref.shape).astype(I32)

    lane = lax.broadcasted_iota(I32, base_ref.shape, 1)
    col = jnp.sum(jnp.where(lane == step, base_ref[...], 0.0), axis=1, keepdims=True)
    ids = _expert_ids(tt)
    col_slabs = [jnp.broadcast_to(col[g * GROUP_SIZE:(g + 1) * GROUP_SIZE, :], (GROUP_SIZE, tt))
                 for g in range(N_GROUPS)]
    eid = eid_ref[...]
    rows = [_pick(ids, eid[k:k + 1, :], col_slabs) for k in range(TOP_K)]
    pos_ref[...] = (jnp.concatenate(rows, axis=0) + rank_ref[...]).astype(I32)


def _plan_call(cnt, eid, rank, tt, n_blocks):
    n = eid.shape[1]
    e = jnp.arange(N_EXPERTS)
    s = jnp.arange(MAX_TILES)
    nb_pad = -(-n_blocks // LANES) * LANES
    return pl.pallas_call(
        functools.partial(_plan_kernel, tt),
        out_shape=[jax.ShapeDtypeStruct((TOP_K, n), I32),
                   jax.ShapeDtypeStruct((1, nb_pad), I32),
                   jax.ShapeDtypeStruct((1, LANES), I32)],
        grid=(n // tt,),
        in_specs=[pl.BlockSpec((N_EXPERTS, MAX_TILES), lambda i: (0, 0)),
                  pl.BlockSpec((TOP_K, tt), lambda i: (0, i)),
                  pl.BlockSpec((TOP_K, tt), lambda i: (0, i)),
                  pl.BlockSpec((N_EXPERTS, N_EXPERTS), lambda i: (0, 0)),
                  pl.BlockSpec((MAX_TILES, MAX_TILES), lambda i: (0, 0))],
        out_specs=[pl.BlockSpec((TOP_K, tt), lambda i: (0, i)),
                   pl.BlockSpec((1, nb_pad), lambda i: (0, 0)),
                   pl.BlockSpec((1, LANES), lambda i: (0, 0))],
        scratch_shapes=[pltpu.VMEM((N_EXPERTS, MAX_TILES), F32)],
        compiler_params=pltpu.CompilerParams(dimension_semantics=("arbitrary",),
                                             vmem_limit_bytes=VMEM_LIMIT_BYTES),
        name="plan",
    )(cnt, eid, rank, (e[None, :] < e[:, None]).astype(BF16), (s[:, None] < s[None, :]).astype(BF16))


def _sc_mesh():
    return plsc.VectorSubcoreMesh(core_axis_name="c", subcore_axis_name="s")


def _dispatch_call(rows, pos, n_sorted):
    n, w = rows.shape
    per_worker = n // SC_WORKERS
    nsteps = per_worker // SC_ROWS
    assert per_worker * SC_WORKERS == n and nsteps * SC_ROWS == per_worker and nsteps % 2 == 0

    @functools.partial(
        pl.kernel, mesh=_sc_mesh(),
        out_type=jax.ShapeDtypeStruct((n_sorted, w), rows.dtype),
        scratch_types=[pltpu.VMEM((2, TOP_K, SC_ROWS), I32),
                       pltpu.VMEM((2, SC_ROWS, w), rows.dtype),
                       pltpu.SemaphoreType.DMA((2,)),
                       pltpu.SemaphoreType.DMA((2,))],
    )
    def dispatch(x_hbm, pos_hbm, out_hbm, idx_v, rows_v, load_sem, scat_sem):
        base = (lax.axis_index("s") * SC_CORES + lax.axis_index("c")) * per_worker

        def load(i, b):
            return pltpu.make_async_copy(x_hbm.at[pl.ds(base + i * SC_ROWS, SC_ROWS)], rows_v.at[b], load_sem.at[b])

        def scatter(b, k):
            return pltpu.make_async_copy(rows_v.at[b], out_hbm.at[idx_v.at[b, k]], scat_sem.at[b])

        def load_start(i, b):
            for k in range(TOP_K):
                pltpu.sync_copy(pos_hbm.at[pl.ds(k * n + base + i * SC_ROWS, SC_ROWS)], idx_v.at[b, k])
            load(i, b).start()

        load_start(0, 0)

        @pl.loop(0, nsteps, step=2)
        def _(i):
            for b in range(2):
                ii = i + b

                @pl.when(ii >= 1)
                def _():
                    for k in range(TOP_K):
                        scatter(1 - b, k).wait()

                @pl.when(ii + 1 < nsteps)
                def _():
                    load_start(ii + 1, 1 - b)

                load(ii, b).wait()
                for k in range(TOP_K):
                    scatter(b, k).start()

        for k in range(TOP_K):
            scatter((nsteps - 1) % 2, k).wait()

    return dispatch(rows, pos)


def _collect_call(table, idx):
    n = idx.shape[0]
    w = table.shape[1]
    per_worker = n // SC_WORKERS
    nsteps = per_worker // SC_ROWS
    assert per_worker * SC_WORKERS == n and nsteps * SC_ROWS == per_worker and nsteps % 2 == 0

    @functools.partial(
        pl.kernel, mesh=_sc_mesh(),
        out_type=jax.ShapeDtypeStruct((n, w), table.dtype),
        scratch_types=[pltpu.VMEM((2, SC_ROWS), I32),
                       pltpu.VMEM((2, SC_ROWS, w), table.dtype),
                       pltpu.SemaphoreType.DMA((2,)),
                       pltpu.SemaphoreType.DMA((2,))],
    )
    def collect(table_hbm, idx_hbm, out_hbm, idx_v, rows_v, gather_sem, write_sem):
        base = (lax.axis_index("s") * SC_CORES + lax.axis_index("c")) * per_worker

        def gather(b):
            return pltpu.make_async_copy(table_hbm.at[idx_v.at[b]], rows_v.at[b], gather_sem.at[b])

        def write(i, b):
            return pltpu.make_async_copy(rows_v.at[b], out_hbm.at[pl.ds(base + i * SC_ROWS, SC_ROWS)], write_sem.at[b])

        def gather_start(i, b):
            pltpu.sync_copy(idx_hbm.at[pl.ds(base + i * SC_ROWS, SC_ROWS)], idx_v.at[b])
            gather(b).start()

        gather_start(0, 0)

        @pl.loop(0, nsteps, step=2)
        def _(i):
            for b in range(2):
                ii = i + b

                @pl.when(ii >= 1)
                def _():
                    write(ii - 1, 1 - b).wait()

                @pl.when(ii + 1 < nsteps)
                def _():
                    gather_start(ii + 1, 1 - b)

                gather(b).wait()
                write(ii, b).start()

        write(nsteps - 1, (nsteps - 1) % 2).wait()

    return collect(table, idx)


def _expert_ffn_kernel(blk_ref, nblk_ref, x_ref, wg_ref, wu_ref, wd_ref, o_ref):
    @pl.when(pl.program_id(0) < nblk_ref[0])
    def _():
        a, b = _unpack(x_ref[...])
        a, b = a.astype(BF16), b.astype(BF16)
        hg = _dot_halves(a, b, wg_ref.at[0])
        act = hg * _sigmoid_t(hg) * _dot_halves(a, b, wu_ref.at[0])
        y = _dot(act.astype(BF16), wd_ref[0])
        half = y.shape[1] // 2
        o_ref[...] = _pack(y[:, 0:half], y[:, half:2 * half])


def _expert_ffn_call(xs, blk, nblk, p):
    n_sorted, w = xs.shape
    d = 2 * w
    ff = p["w_e_gate"].shape[2]

    def rows(j, blk, nblk):
        return (jnp.minimum(j, nblk[0] - 1), 0)

    def expert(j, blk, nblk):
        return (blk[j], 0, 0)

    return pl.pallas_call(
        _expert_ffn_kernel,
        out_shape=jax.ShapeDtypeStruct((n_sorted, w), I32),
        grid_spec=pltpu.PrefetchScalarGridSpec(
            num_scalar_prefetch=2,
            grid=(n_sorted // ROW_BLOCK,),
            in_specs=[pl.BlockSpec((ROW_BLOCK, w), rows),
                      pl.BlockSpec((1, d, ff), expert),
                      pl.BlockSpec((1, d, ff), expert),
                      pl.BlockSpec((1, ff, d), expert)],
            out_specs=pl.BlockSpec((ROW_BLOCK, w), rows)),
        compiler_params=pltpu.CompilerParams(dimension_semantics=("arbitrary",),
                                             vmem_limit_bytes=VMEM_LIMIT_BYTES),
        name="expert_ffn",
    )(blk, nblk, xs, p["w_e_gate"], p["w_e_up"], p["w_e_down"])


def _swiglu_halves(a, b, wg_ref, wu_ref):
    hg = _dot_halves(a, b, wg_ref)
    return hg * _sigmoid_t(hg) * _dot_halves(a, b, wu_ref)


def _moe_out_kernel(x1_ref, h_ref, y_ref, w_ref, g2_ref, wsg_ref, wsu_ref, wsd_ref, o_ref):
    a, b = _unpack(h_ref[...])
    shared = _dot(_swiglu_halves(a.astype(BF16), b.astype(BF16), wsg_ref, wsu_ref).astype(BF16), wsd_ref[...])
    half = h_ref.shape[1]
    w = w_ref[...]
    acc_a, acc_b = shared[:, 0:half], shared[:, half:2 * half]
    for k in range(TOP_K):
        ya, yb = _unpack(y_ref[k])
        acc_a = acc_a + w[:, k:k + 1] * ya
        acc_b = acc_b + w[:, k:k + 1] * yb
    g2 = g2_ref[...]
    o_ref[:, 0:half] = x1_ref[:, 0:half] + g2[:, 0:half] * acc_a
    o_ref[:, half:2 * half] = x1_ref[:, half:2 * half] + g2[:, half:2 * half] * acc_b


def _moe_out_call(x1, h2, yg, w_tok, g2, p, tm):
    n, d = x1.shape
    sff = p["w_s_gate"].shape[1]
    per_row = n // g2.shape[0]
    g2 = g2.reshape(g2.shape[0], 1, d)

    def const(shape):
        return pl.BlockSpec(shape, lambda i: (0,) * len(shape), pipeline_mode=pl.Buffered(1))

    return pl.pallas_call(
        _moe_out_kernel,
        out_shape=jax.ShapeDtypeStruct((n, d), F32),
        grid=(n // tm,),
        in_specs=[pl.BlockSpec((tm, d), lambda i: (i, 0)),
                  pl.BlockSpec((tm, d // 2), lambda i: (i, 0)),
                  pl.BlockSpec((TOP_K, tm, d // 2), lambda i: (0, i, 0)),
                  pl.BlockSpec((tm, LANES), lambda i: (i, 0)),
                  pl.BlockSpec((None, 1, d), lambda i: (i * tm // per_row, 0, 0)),
                  const((d, sff)), const((d, sff)), const((sff, d))],
        out_specs=pl.BlockSpec((tm, d), lambda i: (i, 0)),
        compiler_params=pltpu.CompilerParams(dimension_semantics=("arbitrary",),
                                             vmem_limit_bytes=VMEM_LIMIT_BYTES),
        name="moe_out",
    )(x1, h2, yg, w_tok, g2, p["w_s_gate"], p["w_s_up"], p["w_s_down"])


def _moe_dense_kernel(h_ref, gates_ref, x1_ref, g2_ref, wsg_ref, wsu_ref, wsd_ref, wg_ref, wu_ref, wd_ref,
                      o_ref, acc_ref):
    e = pl.program_id(1)
    a, b = _unpack(h_ref[...])
    a, b = a.astype(BF16), b.astype(BF16)

    @pl.when(e == 0)
    def _():
        acc_ref[...] = _dot(_swiglu_halves(a, b, wsg_ref, wsu_ref).astype(BF16), wsd_ref[...])

    gates = gates_ref[...]
    lane = lax.broadcasted_iota(I32, gates.shape, 1)
    gate = jnp.sum(jnp.where(lane == e, gates, 0.0), axis=1, keepdims=True)
    act = _swiglu_halves(a, b, wg_ref.at[0], wu_ref.at[0]) * gate
    acc_ref[...] += _dot(act.astype(BF16), wd_ref[0])

    @pl.when(e == pl.num_programs(1) - 1)
    def _():
        o_ref[...] = x1_ref[...] + g2_ref[...] * acc_ref[...]


def _moe_dense_call(h2, gates, x1, g2_rows, p):
    n, d = x1.shape
    ff = p["w_e_gate"].shape[2]
    sff = p["w_s_gate"].shape[1]

    def const(shape):
        return pl.BlockSpec(shape, lambda i, e: (0,) * len(shape), pipeline_mode=pl.Buffered(1))

    return pl.pallas_call(
        _moe_dense_kernel,
        out_shape=jax.ShapeDtypeStruct((n, d), F32),
        grid=(1, N_EXPERTS),
        in_specs=[const((n, d // 2)), const((n, LANES)), const((n, d)), const((n, d)),
                  const((d, sff)), const((d, sff)), const((sff, d)),
                  pl.BlockSpec((1, d, ff), lambda i, e: (e, 0, 0)),
                  pl.BlockSpec((1, d, ff), lambda i, e: (e, 0, 0)),
                  pl.BlockSpec((1, ff, d), lambda i, e: (e, 0, 0))],
        out_specs=pl.BlockSpec((n, d), lambda i, e: (0, 0)),
        scratch_shapes=[pltpu.VMEM((n, d), F32)],
        compiler_params=pltpu.CompilerParams(dimension_semantics=("arbitrary", "arbitrary"),
                                             vmem_limit_bytes=VMEM_LIMIT_BYTES),
        name="moe_dense",
    )(h2, gates, x1, g2_rows, p["w_s_gate"], p["w_s_up"], p["w_s_down"], p["w_e_gate"], p["w_e_up"], p["w_e_down"])


PROMPT_CFG = MixerCfg(tt=512, qb=256, kb=768, n_qblk=2, kstride=256, has_cache=False, sparse=True)
MOE_OUT_TILE = 512


def _layer_params(l, w_in, q_norm_g, k_norm_g, w_attn_o, conv_w, conv_b, conv_ln_g, conv_ln_b, w_conv_o,
                  b_conv_o, b_gate, w_out, norm1_g, norm2_g, w_router, router_bias,
                  w_e_gate, w_e_up, w_e_down, w_s_gate, w_s_up, w_s_down):
    row = lambda a: a[l].reshape(1, -1)
    wr_t = w_router[l].T
    wr_hi = wr_t.astype(BF16)
    head = jnp.arange(ATTN_DIM) // HEAD_DIM
    return {
        "norm1_g": row(norm1_g), "norm2_g": row(norm2_g), "w_in": w_in[l].astype(BF16),
        "q_norm_g": jnp.tile(q_norm_g[l], N_HEADS).reshape(1, -1),
        "k_norm_g": jnp.tile(k_norm_g[l], N_HEADS).reshape(1, -1),
        "hsum": (head[:, None] == head[None, :]).astype(BF16),
        "w_attn_o": w_attn_o[l].astype(BF16), "conv_w": conv_w[l], "conv_b": row(conv_b),
        "conv_ln_g": row(conv_ln_g), "conv_ln_b": row(conv_ln_b), "w_conv_o": w_conv_o[l].astype(BF16),
        "b_conv_o": row(b_conv_o), "b_gate": row(b_gate), "w_out": w_out[l].astype(BF16),
        "wr_hi": wr_hi, "wr_lo": (wr_t - wr_hi.astype(F32)).astype(BF16),
        "router_bias": router_bias[l].reshape(-1, 1),
        "w_e_gate": w_e_gate[l].astype(BF16), "w_e_up": w_e_up[l].astype(BF16),
        "w_e_down": w_e_down[l].astype(BF16), "w_s_gate": w_s_gate[l].astype(BF16),
        "w_s_up": w_s_up[l].astype(BF16), "w_s_down": w_s_down[l].astype(BF16),
    }


def _sparse_moe(x1, h2, w_tok, eid, rank, cnt, g2, p, tt):
    n, d = x1.shape
    n_sorted = n * TOP_K + N_EXPERTS * ROW_BLOCK
    pos, blk, nblk = _plan_call(cnt, eid, rank, tt, n_sorted // ROW_BLOCK)
    pos = pos.reshape(TOP_K * n)
    xs = _dispatch_call(h2, pos, n_sorted)
    ys = _expert_ffn_call(xs, blk.reshape(-1), nblk.reshape(-1), p)
    yg = _collect_call(ys, pos).reshape(TOP_K, n, d // 2)
    return _moe_out_call(x1, h2, yg, w_tok, g2, p, MOE_OUT_TILE)


def kernel(x_prompt, x_sample, c_prompt, c_sample, cache_k, cache_v, state_conv, w_mod, b_mod, norm1_g, w_in, q_norm_g, k_norm_g, rel_bias, w_attn_o, conv_w, conv_b, conv_ln_g, conv_ln_b, w_conv_o, b_conv_o, b_gate, w_out, norm2_g, w_router, router_bias, w_e_gate, w_e_up, w_e_down, w_s_gate, w_s_up, w_s_down):
    depth = w_mod.shape[0]
    bp, tp, d = x_prompt.shape
    bs, ts, _ = x_sample.shape
    assert cache_k.shape[2] == min(PREV, PAST_LEN) == PREV
    sample_cfg = MixerCfg(tt=ts, qb=ts, kb=PREV + ts, n_qblk=1, kstride=0, has_cache=True, sparse=False)

    yp, ys = x_prompt, x_sample
    outs = [[] for _ in range(6)]
    for l in range(depth):
        p = _layer_params(l, w_in, q_norm_g, k_norm_g, w_attn_o, conv_w, conv_b, conv_ln_g, conv_ln_b,
                          w_conv_o, b_conv_o, b_gate, w_out, norm1_g, norm2_g, w_router, router_bias,
                          w_e_gate, w_e_up, w_e_down, w_s_gate, w_s_up, w_s_down)
        mod = _mod_call(jnp.concatenate([c_prompt, c_sample], axis=0), w_mod[l], b_mod[l])
        mod = mod.reshape(bp + bs, 6, d)
        mod_p, mod_s = mod[:bp], mod[bp:]

        p["bias"] = _rel_bias_blocks(rel_bias[l], PROMPT_CFG, PREV, 0)
        x1, h2, w_tok, kp, vp, cp, eid, rank, cnt = _mixer_call(PROMPT_CFG, yp, mod_p, p)
        yp = _sparse_moe(x1.reshape(bp * tp, d), h2.reshape(bp * tp, d // 2), w_tok.reshape(bp * tp, LANES),
                         eid, rank, cnt, mod_p[:, 5, :], p, PROMPT_CFG.tt).reshape(bp, tp, d)
        outs[0].append(kp.reshape(bp, -1, N_HEADS, HEAD_DIM))
        outs[1].append(vp.reshape(bp, -1, N_HEADS, HEAD_DIM))
        outs[2].append(cp)

        p["bias"] = _rel_bias_blocks(rel_bias[l], sample_cfg, PAST_LEN, PAST_LEN - PREV)
        cache = (cache_k[l].reshape(bs, PREV, ATTN_DIM), cache_v[l].reshape(bs, PREV, ATTN_DIM), state_conv[l])
        x1, h2, gates, ks, vs, cs = _mixer_call(sample_cfg, ys, mod_s, p, cache)
        ys = _moe_dense_call(h2.reshape(bs * ts, d // 2), gates.reshape(bs * ts, LANES), x1.reshape(bs * ts, d),
                             jnp.repeat(mod_s[:, 5, :], ts, axis=0), p).reshape(bs, ts, d)
        outs[3].append(ks.reshape(bs, ts, N_HEADS, HEAD_DIM))
        outs[4].append(vs.reshape(bs, ts, N_HEADS, HEAD_DIM))
        outs[5].append(cs)
    return (yp, ys) + tuple(jnp.stack(o) for o in outs)
```

```python
import dataclasses
import functools

import jax
import jax.numpy as jnp
from jax import lax
from jax.experimental import pallas as pl
from jax.experimental.pallas import tpu as pltpu
from jax.experimental.pallas import tpu_sc as plsc

F32 = jnp.float32
BF16 = jnp.bfloat16
I32 = jnp.int32

CHUNK = 64
BAND_CHUNKS = 8
PREV = BAND_CHUNKS * CHUNK
PAST_LEN = 1024
N_HEADS = 8
HEAD_DIM = 64
ATTN_DIM = N_HEADS * HEAD_DIM
MAX_REL = 128
CONV_WIDTH = 31
CONV_PAD = 32
N_EXPERTS = 64
N_GROUPS = 8
GROUP_SIZE = N_EXPERTS // N_GROUPS
TOPK_GROUPS = 4
TOP_K = 8
ROUTE_SCALE = 2.5
EPS = 1e-6
LANES = 128
SUBLANES = 8
NEG = -1e30

VMEM_LIMIT_BYTES = 60 * 1024 * 1024

SC_CORES = 2
SC_SUBCORES = 16
SC_WORKERS = SC_CORES * SC_SUBCORES
SC_ROWS = 64

MAX_TILES = LANES
ROW_BLOCK = 1024


def _sigmoid(x):
    return 1.0 / (1.0 + jnp.exp(-x))


def _sigmoid_t(x):
    return 0.5 * jnp.tanh(0.5 * x) + 0.5


def _split_bf16(x):
    hi = x.astype(BF16)
    lo = (x - hi.astype(F32)).astype(BF16)
    return hi, lo


def _dot(a, b):
    return jnp.dot(a, b, preferred_element_type=F32)


def _dot3(a, b):
    a_hi, a_lo = _split_bf16(a)
    b_hi, b_lo = _split_bf16(b)
    return _dot(a_hi, b_hi) + _dot(a_hi, b_lo) + _dot(a_lo, b_hi)


def _pack(a, b):
    ia = lax.bitcast_convert_type(a.astype(BF16).astype(F32), I32)
    ib = lax.bitcast_convert_type(b.astype(BF16).astype(F32), I32)
    return ia | lax.shift_right_logical(ib, 16)


def _unpack(p):
    a = lax.bitcast_convert_type(p & jnp.int32(-65536), F32)
    b = lax.bitcast_convert_type(lax.shift_left(p, 16), F32)
    return a, b


def _dot_halves(a, b, w_ref):
    half = a.shape[1]
    return _dot(a, w_ref[0:half, :].astype(BF16)) + _dot(b, w_ref[half:2 * half, :].astype(BF16))


def _mod_kernel(c_ref, w_ref, b_ref, o_ref):
    c = c_ref[...]
    o_ref[...] = _dot3(c * _sigmoid(c), w_ref[...]) + b_ref[...]


def _mod_call(c, w_mod, b_mod):
    n, d = c.shape
    dout = w_mod.shape[1]
    bn = 1536
    return pl.pallas_call(
        _mod_kernel,
        out_shape=jax.ShapeDtypeStruct((n, dout), F32),
        grid=(dout // bn,),
        in_specs=[pl.BlockSpec((n, d), lambda j: (0, 0)),
                  pl.BlockSpec((d, bn), lambda j: (0, j)),
                  pl.BlockSpec((1, bn), lambda j: (0, j))],
        out_specs=pl.BlockSpec((n, bn), lambda j: (0, j)),
        compiler_params=pltpu.CompilerParams(dimension_semantics=("arbitrary",),
                                             vmem_limit_bytes=VMEM_LIMIT_BYTES),
        name="mod",
    )(c, w_mod, b_mod.reshape(1, dout))


@dataclasses.dataclass(frozen=True)
class MixerCfg:
    tt: int
    qb: int
    kb: int
    n_qblk: int
    kstride: int
    has_cache: bool
    sparse: bool


def _rms(x, g):
    ms = jnp.mean(x * x, axis=-1, keepdims=True)
    return x * lax.rsqrt(ms + EPS) * g


def _head_rms(z, hsum, g):
    ss = _dot((z * z).astype(BF16), hsum)
    return z * lax.rsqrt(ss * (1.0 / HEAD_DIM) + EPS) * g


def _expert_ids(t):
    sub = lax.broadcasted_iota(I32, (GROUP_SIZE, t), 0).astype(F32)
    return [sub + float(g * GROUP_SIZE) for g in range(N_GROUPS)]


def _pick(ids, eid, slabs):
    acc = functools.reduce(jnp.add, [jnp.where(ids[g] == eid, slabs[g], 0.0) for g in range(N_GROUPS)])
    return jnp.sum(acc, axis=0, keepdims=True)


def _route(choice, s):
    t = choice.shape[1]
    sub = lax.broadcasted_iota(I32, (GROUP_SIZE, t), 0).astype(F32)
    slabs = [choice[g * GROUP_SIZE:(g + 1) * GROUP_SIZE, :] for g in range(N_GROUPS)]
    s_slabs = [s[g * GROUP_SIZE:(g + 1) * GROUP_SIZE, :] for g in range(N_GROUPS)]
    gscore = []
    for c in slabs:
        m1 = jnp.max(c, axis=0, keepdims=True)
        first = jnp.min(jnp.where(c == m1, sub, float(GROUP_SIZE)), axis=0, keepdims=True)
        m2 = jnp.max(jnp.where(sub == first, -jnp.inf, c), axis=0, keepdims=True)
        gscore.append(m1 + m2)
    gsel = [jnp.zeros((1, t), F32) for _ in range(N_GROUPS)]
    for _ in range(TOPK_GROUPS):
        m = functools.reduce(jnp.maximum, gscore)
        first = functools.reduce(jnp.minimum,
                                 [jnp.where(gscore[g] == m, float(g), float(N_GROUPS)) for g in range(N_GROUPS)])
        for g in range(N_GROUPS):
            hit = first == float(g)
            gsel[g] = jnp.where(hit, 1.0, gsel[g])
            gscore[g] = jnp.where(hit, -jnp.inf, gscore[g])
    masked = [jnp.where(gsel[g] > 0.5, slabs[g], -jnp.inf) for g in range(N_GROUPS)]
    ids = _expert_ids(t)
    eids, raw = [], []
    for _ in range(TOP_K):
        m = jnp.max(functools.reduce(jnp.maximum, masked), axis=0, keepdims=True)
        cand = [jnp.where(masked[g] == m, ids[g], float(N_EXPERTS)) for g in range(N_GROUPS)]
        first = jnp.min(functools.reduce(jnp.minimum, cand), axis=0, keepdims=True)
        eids.append(first)
        raw.append(_pick(ids, first, s_slabs))
        masked = [jnp.where(ids[g] == first, -jnp.inf, masked[g]) for g in range(N_GROUPS)]
    wsum = functools.reduce(jnp.add, raw)
    return eids, [r / wsum * ROUTE_SCALE for r in raw]


def _lane_dense_rows(rows, t):
    pad = jnp.zeros((LANES - len(rows), t), F32)
    return jnp.concatenate(rows + [pad], axis=0).T


def _mixer_kernel(cfg, n_state, *refs):
    (x_ref, mod_ref, n1g_ref, n2g_ref, win_ref, qg_ref, kg_ref, hsum_ref, bias_ref, wao_ref,
     cw_ref, cb_ref, lng_ref, lnb_ref, wco_ref, bco_ref, bg_ref, wout_ref,
     wrh_ref, wrl_ref, rb_ref) = refs[:21]
    refs = refs[21:]
    if cfg.has_cache:
        kc_ref, vc_ref, cs_ref = refs[:3]
        refs = refs[3:]
    if cfg.sparse:
        tri_ref = refs[0]
        refs = refs[1:]
    refs = refs[n_state:]
    x1_ref, h2_ref, tok_ref, kout_ref, vout_ref, cout_ref = refs[:6]
    refs = refs[6:]
    if cfg.sparse:
        eid_ref, rank_ref, cnt_ref = refs[:3]
        refs = refs[3:]
    qbuf, kbuf, vbuf, obuf, uext, cvbuf = refs

    tt = cfg.tt
    t = pl.program_id(1)

    if cfg.has_cache:
        kbuf[0:PREV, :] = kc_ref[0].astype(BF16)
        vbuf[0:PREV, :] = vc_ref[0].astype(BF16)
        uext[CONV_PAD - (CONV_WIDTH - 1):CONV_PAD, :] = cs_ref[0]
    else:
        @pl.when(t == 0)
        def _():
            kbuf[0:PREV, :] = jnp.zeros((PREV, ATTN_DIM), BF16)
            vbuf[0:PREV, :] = jnp.zeros((PREV, ATTN_DIM), BF16)
            uext[0:CONV_PAD, :] = jnp.zeros((CONV_PAD, uext.shape[1]), F32)

    mod = mod_ref[0]
    sh1, sc1, g1, sh2, sc2 = (mod[i:i + 1, :] for i in range(5))

    x = x_ref[0]
    hb = (_rms(x, n1g_ref[...]) * (1.0 + sc1) + sh1).astype(BF16)

    a0, a1, a2, a3, a4, a5 = (0, ATTN_DIM, 2 * ATTN_DIM, 3 * ATTN_DIM,
                              3 * ATTN_DIM + cw_ref.shape[1], 3 * ATTN_DIM + 2 * cw_ref.shape[1])
    d_model = x.shape[1]

    q = _head_rms(_dot(hb, win_ref[:, a0:a1]), hsum_ref[...], qg_ref[...])
    qbuf[...] = (q * (HEAD_DIM ** -0.5)).astype(BF16)
    k = _head_rms(_dot(hb, win_ref[:, a1:a2]), hsum_ref[...], kg_ref[...])
    kout_ref[0] = k
    kbuf[PREV:PREV + tt, :] = k.astype(BF16)
    v = _dot(hb, win_ref[:, a2:a3])
    vout_ref[0] = v
    vbuf[PREV:PREV + tt, :] = v.astype(BF16)

    u = _dot(hb, win_ref[:, a3:a4]) * _sigmoid_t(_dot(hb, win_ref[:, a4:a5]))
    uext[CONV_PAD:CONV_PAD + tt, :] = u
    cout_ref[0] = uext[CONV_PAD + tt - (CONV_WIDTH - 1):CONV_PAD + tt, :]

    def attend(r0, k0):
        if not cfg.has_cache:
            col = lax.broadcasted_iota(I32, (1, cfg.kb), 1)
            valid = jnp.logical_or(col >= PREV - k0, t > 0)
        heads = []
        for h in range(N_HEADS):
            c0, c1 = h * HEAD_DIM, (h + 1) * HEAD_DIM
            s = lax.dot_general(qbuf[pl.ds(r0, cfg.qb), c0:c1], kbuf[pl.ds(k0, cfg.kb), c0:c1],
                                (((1,), (1,)), ((), ())), preferred_element_type=F32)
            s = s + bias_ref[h]
            if not cfg.has_cache:
                s = jnp.where(valid, s, NEG)
            p = jnp.exp(s - jnp.max(s, axis=-1, keepdims=True))
            l = jnp.sum(p, axis=-1, keepdims=True)
            heads.append(_dot(p.astype(BF16), vbuf[pl.ds(k0, cfg.kb), c0:c1]) * (1.0 / l))
        obuf[pl.ds(r0, cfg.qb), :] = jnp.concatenate(heads, axis=1).astype(BF16)

    rc = min(cfg.qb, 64)
    cc = min(u.shape[1], 256)
    shifts = [j + CONV_PAD - (CONV_WIDTH - 1) for j in range(CONV_WIDTH)]

    def conv_rows(r0):
        for c0 in range(0, u.shape[1], cc):
            acc = jnp.broadcast_to(cb_ref[:, c0:c0 + cc], (rc, cc))
            for res in range(SUBLANES):
                group = [s for s in shifts if s % SUBLANES == res]
                if not group:
                    continue
                lo, hi = min(group) - res, max(group) - res
                rows = hi - lo + rc + (SUBLANES if res else 0)
                slab = uext[pl.ds(r0 + lo, rows), c0:c0 + cc]
                if res:
                    slab = pltpu.roll(slab, rows - res, 0)
                for s in group:
                    j = s - shifts[0]
                    a = s - res - lo
                    acc = acc + cw_ref[j:j + 1, c0:c0 + cc] * slab[a:a + rc, :]
            cvbuf[pl.ds(r0, rc), c0:c0 + cc] = acc

    def mix_block(r0, k0):
        attend(r0, k0)
        for i in range(cfg.qb // rc):
            conv_rows(r0 + i * rc)

    if cfg.n_qblk == 1:
        mix_block(0, 0)
    else:
        def mix_block_step(j, carry):
            mix_block(pl.multiple_of(j * cfg.qb, cfg.qb), pl.multiple_of(j * cfg.kstride, cfg.kstride))
            return carry
        lax.fori_loop(0, cfg.n_qblk, mix_block_step, 0)
    y_attn = _dot(obuf[...], wao_ref[...])

    if not cfg.has_cache:
        uext[0:CONV_PAD, :] = uext[tt:tt + CONV_PAD, :]
    cv = cvbuf[...]
    mu = jnp.mean(cv, axis=-1, keepdims=True)
    xc = cv - mu
    var = jnp.mean(xc * xc, axis=-1, keepdims=True)
    cv = xc * lax.rsqrt(var + EPS) * lng_ref[...] + lnb_ref[...]
    cv = cv * _sigmoid_t(cv)
    y_conv = _dot(cv.astype(BF16), wco_ref[...]) + bco_ref[...]

    g_attn = _sigmoid_t(_dot(hb, win_ref[:, a5:a5 + d_model]) + bg_ref[:, 0:d_model])
    g_conv = _sigmoid_t(_dot(hb, win_ref[:, a5 + d_model:a5 + 2 * d_model]) + bg_ref[:, d_model:2 * d_model])
    m = g_attn * y_attn + g_conv * y_conv
    x1 = x + g1 * _dot(m.astype(BF16), wout_ref[...])
    x1_ref[0] = x1

    h2 = _rms(x1, n2g_ref[...]) * (1.0 + sc2) + sh2
    h2_ref[0] = _pack(h2[:, 0:d_model // 2], h2[:, d_model // 2:d_model])
    h2_hi, h2_lo = _split_bf16(h2)
    nt_dims = (((1,), (1,)), ((), ()))
    logits = (lax.dot_general(wrh_ref[...], h2_hi, nt_dims, preferred_element_type=F32)
              + lax.dot_general(wrl_ref[...], h2_hi, nt_dims, preferred_element_type=F32)
              + lax.dot_general(wrh_ref[...], h2_lo, nt_dims, preferred_element_type=F32))
    s = _sigmoid(logits)
    eids, weights = _route(s + rb_ref[...], s)
    ids = _expert_ids(tt)
    if cfg.sparse:
        sel = [functools.reduce(jnp.add, [jnp.where(ids[g] == e, 1.0, 0.0) for e in eids]) for g in range(N_GROUPS)]
        sel = jnp.concatenate(sel, axis=0)
        rank = _dot(sel.astype(BF16), tri_ref[...])
        rank_slabs = [rank[g * GROUP_SIZE:(g + 1) * GROUP_SIZE, :] for g in range(N_GROUPS)]
        tok_ref[0] = _lane_dense_rows(weights, tt)
        eid_ref[...] = jnp.concatenate(eids, axis=0)
        rank_ref[...] = jnp.concatenate([_pick(ids, e, rank_slabs) for e in eids], axis=0)
        step = pl.program_id(0) * pl.num_programs(1) + t

        @pl.when(step == 0)
        def _():
            cnt_ref[...] = jnp.zeros(cnt_ref.shape, F32)

        lane = lax.broadcasted_iota(I32, cnt_ref.shape, 1)
        cnt_ref[...] = jnp.where(lane == step, jnp.sum(sel, axis=1, keepdims=True), cnt_ref[...])
    else:
        gates = [functools.reduce(jnp.add, [jnp.where(ids[g] == e, w, 0.0) for e, w in zip(eids, weights)])
                 for g in range(N_GROUPS)]
        gates = jnp.concatenate(gates + [jnp.zeros((LANES - N_EXPERTS, tt), F32)], axis=0)
        tok_ref[0] = gates.T


def _toeplitz(v, rows, cols):
    w = rows + cols
    flat = jnp.tile(v, (1, rows))[:, :rows * (w - 1)]
    return flat.reshape(v.shape[0], rows, w - 1)[:, :, :cols]


def _rel_bias_blocks(table, cfg, q_pos0, k_pos0):
    q_pos = q_pos0 + jnp.arange(cfg.qb)
    k_pos = k_pos0 + jnp.arange(cfg.kb)
    qc = q_pos // CHUNK
    kc = k_pos // CHUNK
    allowed = (kc[None, :] <= qc[:, None]) & (kc[None, :] >= qc[:, None] - BAND_CHUNKS)
    w = cfg.qb + cfg.kb
    dj = jnp.arange(w)
    dj = jnp.where(dj < cfg.kb, dj, dj - w)
    rel = jnp.clip(q_pos0 - k_pos0 - dj, -MAX_REL, MAX_REL) + MAX_REL
    bias = _toeplitz(table[:, rel].astype(F32), cfg.qb, cfg.kb)
    return jnp.where(allowed[None], bias, NEG)


def _mixer_call(cfg, x, mod, p, cache=None, batch0=0, nbatch=None, state=None):
    b_all, t_total, d = x.shape
    nb = b_all if nbatch is None else nbatch
    nt = t_total // cfg.tt
    conv_dim = p["conv_w"].shape[1]
    n_in = p["w_in"].shape[1]

    def const(shape):
        return pl.BlockSpec(shape, lambda i, j: (0,) * len(shape), pipeline_mode=pl.Buffered(1))

    def per_row(shape):
        return pl.BlockSpec(shape, lambda i, j: (i + batch0, 0, 0))

    in_specs = [
        pl.BlockSpec((1, cfg.tt, d), lambda i, j: (i + batch0, j, 0)),
        per_row((1, 6, d)),
        const((1, d)), const((1, d)), const((d, n_in)),
        const((1, ATTN_DIM)), const((1, ATTN_DIM)), const((ATTN_DIM, ATTN_DIM)),
        const((N_HEADS, cfg.qb, cfg.kb)), const((ATTN_DIM, d)),
        const((CONV_WIDTH, conv_dim)), const((1, conv_dim)), const((1, conv_dim)), const((1, conv_dim)),
        const((conv_dim, d)), const((1, d)), const((1, 2 * d)), const((d, d)),
        const((N_EXPERTS, d)), const((N_EXPERTS, d)), const((N_EXPERTS, 1)),
    ]
    args = [x, mod, p["norm1_g"], p["norm2_g"], p["w_in"], p["q_norm_g"], p["k_norm_g"], p["hsum"],
            p["bias"], p["w_attn_o"], p["conv_w"], p["conv_b"], p["conv_ln_g"], p["conv_ln_b"],
            p["w_conv_o"], p["b_conv_o"], p["b_gate"], p["w_out"], p["wr_hi"], p["wr_lo"], p["router_bias"]]
    if cfg.has_cache:
        in_specs += [per_row((1, PREV, ATTN_DIM)), per_row((1, PREV, ATTN_DIM)),
                     per_row((1, CONV_WIDTH - 1, conv_dim))]
        args += list(cache)
    if cfg.sparse:
        assert nb * nt <= MAX_TILES
        in_specs += [const((cfg.tt, cfg.tt))]
        tok = jnp.arange(cfg.tt)
        args += [(tok[:, None] < tok[None, :]).astype(BF16)]
    aliases = {}
    if state is not None:
        for i, arr in enumerate(state):
            aliases[len(args)] = 3 + i
            in_specs.append(pl.BlockSpec(memory_space=pl.ANY))
            args.append(arr)
    keep = min(PREV, t_total)
    assert keep == cfg.tt
    out_shape = [jax.ShapeDtypeStruct((nb, t_total, d), F32),
                 jax.ShapeDtypeStruct((nb, t_total, d // 2), I32),
                 jax.ShapeDtypeStruct((nb, t_total, LANES), F32),
                 jax.ShapeDtypeStruct((b_all, keep, ATTN_DIM), F32),
                 jax.ShapeDtypeStruct((b_all, keep, ATTN_DIM), F32),
                 jax.ShapeDtypeStruct((b_all, CONV_WIDTH - 1, conv_dim), F32)]
    out_specs = [pl.BlockSpec((1, cfg.tt, d), lambda i, j: (i, j, 0)),
                 pl.BlockSpec((1, cfg.tt, d // 2), lambda i, j: (i, j, 0)),
                 pl.BlockSpec((1, cfg.tt, LANES), lambda i, j: (i, j, 0)),
                 per_row((1, keep, ATTN_DIM)), per_row((1, keep, ATTN_DIM)),
                 per_row((1, CONV_WIDTH - 1, conv_dim))]
    if cfg.sparse:
        out_shape += [jax.ShapeDtypeStruct((TOP_K, nb * t_total), F32),
                      jax.ShapeDtypeStruct((TOP_K, nb * t_total), F32),
                      jax.ShapeDtypeStruct((N_EXPERTS, MAX_TILES), F32)]
        out_specs += [pl.BlockSpec((TOP_K, cfg.tt), lambda i, j: (0, i * nt + j)),
                      pl.BlockSpec((TOP_K, cfg.tt), lambda i, j: (0, i * nt + j)),
                      pl.BlockSpec((N_EXPERTS, MAX_TILES), lambda i, j: (0, 0))]
    scratch = [pltpu.VMEM((cfg.tt, ATTN_DIM), BF16),
               pltpu.VMEM((PREV + cfg.tt, ATTN_DIM), BF16),
               pltpu.VMEM((PREV + cfg.tt, ATTN_DIM), BF16),
               pltpu.VMEM((cfg.tt, ATTN_DIM), BF16),
               pltpu.VMEM((CONV_PAD + cfg.tt, conv_dim), F32),
               pltpu.VMEM((cfg.tt, conv_dim), F32)]
    return pl.pallas_call(
        functools.partial(_mixer_kernel, cfg, 0 if state is None else len(state)),
        out_shape=out_shape,
        grid=(nb, nt),
        in_specs=in_specs,
        out_specs=out_specs,
        scratch_shapes=scratch,
        input_output_aliases=aliases,
        compiler_params=pltpu.CompilerParams(dimension_semantics=("arbitrary", "arbitrary"),
                                             vmem_limit_bytes=VMEM_LIMIT_BYTES),
        name="mixer_sample" if cfg.has_cache else "mixer_prompt",
    )(*args)


def _exact_parts(x, n):
    parts = []
    for _ in range(n):
        part = x.astype(BF16)
        parts.append(part)
        x = x - part.astype(F32)
    return parts


def _plan_kernel(tt, cnt_ref, eid_ref, rank_ref, tri_e_ref, tri_t_ref, pos_ref, blk_ref, nblk_ref, base_ref):
    step = pl.program_id(0)

    @pl.when(step == 0)
    def _():
        cnt = cnt_ref[...]
        total = jnp.sum(cnt, axis=1, keepdims=True)
        padded = jnp.floor((total + float(ROW_BLOCK - 1)) * (1.0 / ROW_BLOCK)) * float(ROW_BLOCK)
        padded = jnp.broadcast_to(padded, cnt.shape)
        start = functools.reduce(jnp.add, [_dot(tri_e_ref[...], part) for part in _exact_parts(padded, 3)])
        before = functools.reduce(jnp.add, [_dot(part, tri_t_ref[...]) for part in _exact_parts(cnt, 2)])
        base_ref[...] = start + before
        end = (start + padded)[:, 0:1]
        first_row = lax.broadcasted_iota(I32, blk_ref.shape, 1).astype(F32) * float(ROW_BLOCK)
        owner = jnp.sum(jnp.where(end <= first_row, 1.0, 0.0), axis=0, keepdims=True)
        blk_ref[...] = jnp.minimum(owner, float(N_EXPERTS - 1)).astype(I32)
        nblk = jnp.max(end, axis=0, keepdims=True) * (1.0 / ROW_BLOCK)
        nblk_ref[...] = jnp.broadcast_to(nblk, nblk_ref.shape).astype(I32)

    lane = lax.broadcasted_iota(I32, base_ref.shape, 1)
    col = jnp.sum(jnp.where(lane == step, base_ref[...], 0.0), axis=1, keepdims=True)
    ids = _expert_ids(tt)
    col_slabs = [jnp.broadcast_to(col[g * GROUP_SIZE:(g + 1) * GROUP_SIZE, :], (GROUP_SIZE, tt))
                 for g in range(N_GROUPS)]
    eid = eid_ref[...]
    rows = [_pick(ids, eid[k:k + 1, :], col_slabs) for k in range(TOP_K)]
    pos_ref[...] = (jnp.concatenate(rows, axis=0) + rank_ref[...]).astype(I32)


def _plan_call(cnt, eid, rank, tt, n_blocks):
    n = eid.shape[1]
    e = jnp.arange(N_EXPERTS)
    s = jnp.arange(MAX_TILES)
    nb_pad = -(-n_blocks // LANES) * LANES
    return pl.pallas_call(
        functools.partial(_plan_kernel, tt),
        out_shape=[jax.ShapeDtypeStruct((TOP_K, n), I32),
                   jax.ShapeDtypeStruct((1, nb_pad), I32),
                   jax.ShapeDtypeStruct((1, LANES), I32)],
        grid=(n // tt,),
        in_specs=[pl.BlockSpec((N_EXPERTS, MAX_TILES), lambda i: (0, 0)),
                  pl.BlockSpec((TOP_K, tt), lambda i: (0, i)),
                  pl.BlockSpec((TOP_K, tt), lambda i: (0, i)),
                  pl.BlockSpec((N_EXPERTS, N_EXPERTS), lambda i: (0, 0)),
                  pl.BlockSpec((MAX_TILES, MAX_TILES), lambda i: (0, 0))],
        out_specs=[pl.BlockSpec((TOP_K, tt), lambda i: (0, i)),
                   pl.BlockSpec((1, nb_pad), lambda i: (0, 0)),
                   pl.BlockSpec((1, LANES), lambda i: (0, 0))],
        scratch_shapes=[pltpu.VMEM((N_EXPERTS, MAX_TILES), F32)],
        compiler_params=pltpu.CompilerParams(dimension_semantics=("arbitrary",),
                                             vmem_limit_bytes=VMEM_LIMIT_BYTES),
        name="plan",
    )(cnt, eid, rank, (e[None, :] < e[:, None]).astype(BF16), (s[:, None] < s[None, :]).astype(BF16))


def _sc_mesh():
    return plsc.VectorSubcoreMesh(core_axis_name="c", subcore_axis_name="s")


def _dispatch_call(rows, pos, n_sorted):
    n, w = rows.shape
    per_worker = n // SC_WORKERS
    nsteps = per_worker // SC_ROWS
    assert per_worker * SC_WORKERS == n and nsteps * SC_ROWS == per_worker and nsteps % 2 == 0

    @functools.partial(
        pl.kernel, mesh=_sc_mesh(),
        out_type=jax.ShapeDtypeStruct((n_sorted, w), rows.dtype),
        scratch_types=[pltpu.VMEM((2, TOP_K, SC_ROWS), I32),
                       pltpu.VMEM((2, SC_ROWS, w), rows.dtype),
                       pltpu.SemaphoreType.DMA((2,)),
                       pltpu.SemaphoreType.DMA((2,))],
    )
    def dispatch(x_hbm, pos_hbm, out_hbm, idx_v, rows_v, load_sem, scat_sem):
        base = (lax.axis_index("s") * SC_CORES + lax.axis_index("c")) * per_worker

        def load(i, b):
            return pltpu.make_async_copy(x_hbm.at[pl.ds(base + i * SC_ROWS, SC_ROWS)], rows_v.at[b], load_sem.at[b])

        def scatter(b, k):
            return pltpu.make_async_copy(rows_v.at[b], out_hbm.at[idx_v.at[b, k]], scat_sem.at[b])

        def load_start(i, b):
            for k in range(TOP_K):
                pltpu.sync_copy(pos_hbm.at[pl.ds(k * n + base + i * SC_ROWS, SC_ROWS)], idx_v.at[b, k])
            load(i, b).start()

        load_start(0, 0)

        @pl.loop(0, nsteps, step=2)
        def _(i):
            for b in range(2):
                ii = i + b

                @pl.when(ii >= 1)
                def _():
                    for k in range(TOP_K):
                        scatter(1 - b, k).wait()

                @pl.when(ii + 1 < nsteps)
                def _():
                    load_start(ii + 1, 1 - b)

                load(ii, b).wait()
                for k in range(TOP_K):
                    scatter(b, k).start()

        for k in range(TOP_K):
            scatter((nsteps - 1) % 2, k).wait()

    return dispatch(rows, pos)


def _collect_call(table, idx):
    n = idx.shape[0]
    w = table.shape[1]
    per_worker = n // SC_WORKERS
    nsteps = per_worker // SC_ROWS
    assert per_worker * SC_WORKERS == n and nsteps * SC_ROWS == per_worker and nsteps % 2 == 0

    @functools.partial(
        pl.kernel, mesh=_sc_mesh(),
        out_type=jax.ShapeDtypeStruct((n, w), table.dtype),
        scratch_types=[pltpu.VMEM((2, SC_ROWS), I32),
                       pltpu.VMEM((2, SC_ROWS, w), table.dtype),
                       pltpu.SemaphoreType.DMA((2,)),
                       pltpu.SemaphoreType.DMA((2,))],
    )
    def collect(table_hbm, idx_hbm, out_hbm, idx_v, rows_v, gather_sem, write_sem):
        base = (lax.axis_index("s") * SC_CORES + lax.axis_index("c")) * per_worker

        def gather(b):
            return pltpu.make_async_copy(table_hbm.at[idx_v.at[b]], rows_v.at[b], gather_sem.at[b])

        def write(i, b):
            return pltpu.make_async_copy(rows_v.at[b], out_hbm.at[pl.ds(base + i * SC_ROWS, SC_ROWS)], write_sem.at[b])

        def gather_start(i, b):
            pltpu.sync_copy(idx_hbm.at[pl.ds(base + i * SC_ROWS, SC_ROWS)], idx_v.at[b])
            gather(b).start()

        gather_start(0, 0)

        @pl.loop(0, nsteps, step=2)
        def _(i):
            for b in range(2):
                ii = i + b

                @pl.when(ii >= 1)
                def _():
                    write(ii - 1, 1 - b).wait()

                @pl.when(ii + 1 < nsteps)
                def _():
                    gather_start(ii + 1, 1 - b)

                gather(b).wait()
                write(ii, b).start()

        write(nsteps - 1, (nsteps - 1) % 2).wait()

    return collect(table, idx)


def _expert_ffn_kernel(blk_ref, nblk_ref, x_ref, wg_ref, wu_ref, wd_ref, o_ref):
    @pl.when(pl.program_id(0) < nblk_ref[0])
    def _():
        a, b = _unpack(x_ref[...])
        a, b = a.astype(BF16), b.astype(BF16)
        hg = _dot_halves(a, b, wg_ref.at[0])
        act = hg * _sigmoid_t(hg) * _dot_halves(a, b, wu_ref.at[0])
        y = _dot(act.astype(BF16), wd_ref[0].astype(BF16))
        half = y.shape[1] // 2
        o_ref[...] = _pack(y[:, 0:half], y[:, half:2 * half])


def _expert_ffn_call(xs, blk, nblk, p):
    n_sorted, w = xs.shape
    d = 2 * w
    ff = p["w_e_gate"].shape[2]

    def rows(j, blk, nblk):
        return (jnp.minimum(j, nblk[0] - 1), 0)

    def expert(j, blk, nblk):
        return (blk[j], 0, 0)

    return pl.pallas_call(
        _expert_ffn_kernel,
        out_shape=jax.ShapeDtypeStruct((n_sorted, w), I32),
        grid_spec=pltpu.PrefetchScalarGridSpec(
            num_scalar_prefetch=2,
            grid=(n_sorted // ROW_BLOCK,),
            in_specs=[pl.BlockSpec((ROW_BLOCK, w), rows),
                      pl.BlockSpec((1, d, ff), expert),
                      pl.BlockSpec((1, d, ff), expert),
                      pl.BlockSpec((1, ff, d), expert)],
            out_specs=pl.BlockSpec((ROW_BLOCK, w), rows)),
        compiler_params=pltpu.CompilerParams(dimension_semantics=("arbitrary",),
                                             vmem_limit_bytes=VMEM_LIMIT_BYTES),
        name="expert_ffn",
    )(blk, nblk, xs, p["w_e_gate"], p["w_e_up"], p["w_e_down"])


def _swiglu_halves(a, b, wg_ref, wu_ref):
    hg = _dot_halves(a, b, wg_ref)
    return hg * _sigmoid_t(hg) * _dot_halves(a, b, wu_ref)


def _moe_out_kernel(n_groups, tiles, *refs):
    groups = [refs[4 * q:4 * q + 4] for q in range(n_groups)]
    g2_ref, wsg_ref, wsu_ref, wsd_ref, o_ref = refs[4 * n_groups:]

    def combine(x1_ref, h_ref, y_ref, w_ref):
        a, b = _unpack(h_ref[...])
        shared = _dot(_swiglu_halves(a.astype(BF16), b.astype(BF16), wsg_ref, wsu_ref).astype(BF16),
                      wsd_ref[...].astype(BF16))
        half = h_ref.shape[1]
        w = w_ref[...]
        acc_a, acc_b = shared[:, 0:half], shared[:, half:2 * half]
        for k in range(TOP_K):
            ya, yb = _unpack(y_ref[k])
            acc_a = acc_a + w[:, k:k + 1] * ya
            acc_b = acc_b + w[:, k:k + 1] * yb
        g2 = g2_ref[...]
        o_ref[:, 0:half] = x1_ref[:, 0:half] + g2[:, 0:half] * acc_a
        o_ref[:, half:2 * half] = x1_ref[:, half:2 * half] + g2[:, half:2 * half] * acc_b

    for q in range(n_groups):
        pl.when(pl.program_id(0) // tiles == q)(functools.partial(combine, *groups[q]))


def _moe_out_call(groups, g2, p, tm):
    n, d = groups[0][0].shape
    n_all = n * len(groups)
    sff = p["w_s_gate"].shape[1]
    per_row = n_all // g2.shape[0]
    g2 = g2.reshape(g2.shape[0], 1, d)
    tiles = n // tm

    def const(shape):
        return pl.BlockSpec(shape, lambda i: (0,) * len(shape), pipeline_mode=pl.Buffered(1))

    in_specs, args = [], []
    for q, group in enumerate(groups):
        tile = lambda i, q=q: jnp.clip(i - q * tiles, 0, tiles - 1)
        in_specs += [pl.BlockSpec((tm, d), lambda i, tile=tile: (tile(i), 0)),
                     pl.BlockSpec((tm, d // 2), lambda i, tile=tile: (tile(i), 0)),
                     pl.BlockSpec((TOP_K, tm, d // 2), lambda i, tile=tile: (0, tile(i), 0)),
                     pl.BlockSpec((tm, LANES), lambda i, tile=tile: (tile(i), 0))]
        args += list(group)
    in_specs += [pl.BlockSpec((None, 1, d), lambda i: (i * tm // per_row, 0, 0)),
                 const((d, sff)), const((d, sff)), const((sff, d))]
    args += [g2, p["w_s_gate"], p["w_s_up"], p["w_s_down"]]
    return pl.pallas_call(
        functools.partial(_moe_out_kernel, len(groups), tiles),
        out_shape=jax.ShapeDtypeStruct((n_all, d), F32),
        grid=(n_all // tm,),
        in_specs=in_specs,
        out_specs=pl.BlockSpec((tm, d), lambda i: (i, 0)),
        compiler_params=pltpu.CompilerParams(dimension_semantics=("arbitrary",),
                                             vmem_limit_bytes=VMEM_LIMIT_BYTES),
        name="moe_out",
    )(*args)


def _moe_dense_kernel(h_ref, gates_ref, x1_ref, g2_ref, wsg_ref, wsu_ref, wsd_ref, wg_ref, wu_ref, wd_ref,
                      o_ref, acc_ref):
    e = pl.program_id(1)
    a, b = _unpack(h_ref[...])
    a, b = a.astype(BF16), b.astype(BF16)

    @pl.when(e == 0)
    def _():
        acc_ref[...] = _dot(_swiglu_halves(a, b, wsg_ref, wsu_ref).astype(BF16), wsd_ref[...].astype(BF16))

    gates = gates_ref[...]
    lane = lax.broadcasted_iota(I32, gates.shape, 1)
    gate = jnp.sum(jnp.where(lane == e, gates, 0.0), axis=1, keepdims=True)
    act = _swiglu_halves(a, b, wg_ref.at[0], wu_ref.at[0]) * gate
    acc_ref[...] += _dot(act.astype(BF16), wd_ref[0].astype(BF16))

    @pl.when(e == pl.num_programs(1) - 1)
    def _():
        o_ref[...] = x1_ref[...] + g2_ref[...] * acc_ref[...]


def _moe_dense_call(h2, gates, x1, g2_rows, p):
    n, d = x1.shape
    ff = p["w_e_gate"].shape[2]
    sff = p["w_s_gate"].shape[1]

    def const(shape):
        return pl.BlockSpec(shape, lambda i, e: (0,) * len(shape), pipeline_mode=pl.Buffered(1))

    return pl.pallas_call(
        _moe_dense_kernel,
        out_shape=jax.ShapeDtypeStruct((n, d), F32),
        grid=(1, N_EXPERTS),
        in_specs=[const((n, d // 2)), const((n, LANES)), const((n, d)), const((n, d)),
                  const((d, sff)), const((d, sff)), const((sff, d)),
                  pl.BlockSpec((1, d, ff), lambda i, e: (e, 0, 0)),
                  pl.BlockSpec((1, d, ff), lambda i, e: (e, 0, 0)),
                  pl.BlockSpec((1, ff, d), lambda i, e: (e, 0, 0))],
        out_specs=pl.BlockSpec((n, d), lambda i, e: (0, 0)),
        scratch_shapes=[pltpu.VMEM((n, d), F32)],
        compiler_params=pltpu.CompilerParams(dimension_semantics=("arbitrary", "arbitrary"),
                                             vmem_limit_bytes=VMEM_LIMIT_BYTES),
        name="moe_dense",
    )(h2, gates, x1, g2_rows, p["w_s_gate"], p["w_s_up"], p["w_s_down"], p["w_e_gate"], p["w_e_up"], p["w_e_down"])


PROMPT_CFG = MixerCfg(tt=512, qb=256, kb=768, n_qblk=2, kstride=256, has_cache=False, sparse=True)
MOE_OUT_TILE = 256
PROMPT_GROUPS = 2


def _layer_params(l, w_in, q_norm_g, k_norm_g, w_attn_o, conv_w, conv_b, conv_ln_g, conv_ln_b, w_conv_o,
                  b_conv_o, b_gate, w_out, norm1_g, norm2_g, w_router, router_bias,
                  w_e_gate, w_e_up, w_e_down, w_s_gate, w_s_up, w_s_down):
    row = lambda a: a[l].reshape(1, -1)
    wr_t = w_router[l].T
    wr_hi = wr_t.astype(BF16)
    head = jnp.arange(ATTN_DIM) // HEAD_DIM
    return {
        "norm1_g": row(norm1_g), "norm2_g": row(norm2_g), "w_in": w_in[l].astype(BF16),
        "q_norm_g": jnp.tile(q_norm_g[l], N_HEADS).reshape(1, -1),
        "k_norm_g": jnp.tile(k_norm_g[l], N_HEADS).reshape(1, -1),
        "hsum": (head[:, None] == head[None, :]).astype(BF16),
        "w_attn_o": w_attn_o[l].astype(BF16), "conv_w": conv_w[l], "conv_b": row(conv_b),
        "conv_ln_g": row(conv_ln_g), "conv_ln_b": row(conv_ln_b), "w_conv_o": w_conv_o[l].astype(BF16),
        "b_conv_o": row(b_conv_o), "b_gate": row(b_gate), "w_out": w_out[l].astype(BF16),
        "wr_hi": wr_hi, "wr_lo": (wr_t - wr_hi.astype(F32)).astype(BF16),
        "router_bias": router_bias[l].reshape(-1, 1),
        "w_e_gate": w_e_gate[l], "w_e_up": w_e_up[l], "w_e_down": w_e_down[l],
        "w_s_gate": w_s_gate[l], "w_s_up": w_s_up[l], "w_s_down": w_s_down[l],
    }


def _sorted_rows(h2, eid, rank, cnt, tt):
    n = h2.shape[0]
    n_sorted = n * TOP_K + N_EXPERTS * ROW_BLOCK
    pos, blk, nblk = _plan_call(cnt, eid, rank, tt, n_sorted // ROW_BLOCK)
    pos = pos.reshape(TOP_K * n)
    return _dispatch_call(h2, pos, n_sorted), pos, blk.reshape(-1), nblk.reshape(-1)


def _expert_rows(xs, pos, blk, nblk, p):
    ys = _expert_ffn_call(xs, blk, nblk, p)
    return _collect_call(ys, pos).reshape(TOP_K, pos.shape[0] // TOP_K, xs.shape[1])


def kernel(x_prompt, x_sample, c_prompt, c_sample, cache_k, cache_v, state_conv, w_mod, b_mod, norm1_g, w_in, q_norm_g, k_norm_g, rel_bias, w_attn_o, conv_w, conv_b, conv_ln_g, conv_ln_b, w_conv_o, b_conv_o, b_gate, w_out, norm2_g, w_router, router_bias, w_e_gate, w_e_up, w_e_down, w_s_gate, w_s_up, w_s_down):
    depth = w_mod.shape[0]
    bp, tp, d = x_prompt.shape
    bs, ts, _ = x_sample.shape
    assert cache_k.shape[2] == min(PREV, PAST_LEN) == PREV
    sample_cfg = MixerCfg(tt=ts, qb=ts, kb=PREV + ts, n_qblk=1, kstride=0, has_cache=True, sparse=False)

    yp, ys = x_prompt, x_sample
    outs = [[] for _ in range(6)]
    for l in range(depth):
        p = _layer_params(l, w_in, q_norm_g, k_norm_g, w_attn_o, conv_w, conv_b, conv_ln_g, conv_ln_b,
                          w_conv_o, b_conv_o, b_gate, w_out, norm1_g, norm2_g, w_router, router_bias,
                          w_e_gate, w_e_up, w_e_down, w_s_gate, w_s_up, w_s_down)
        mod = _mod_call(jnp.concatenate([c_prompt, c_sample], axis=0), w_mod[l], b_mod[l])
        mod = mod.reshape(bp + bs, 6, d)
        mod_p, mod_s = mod[:bp], mod[bp:]

        p["bias"] = _rel_bias_blocks(rel_bias[l], PROMPT_CFG, PREV, 0)
        gb = bp // PROMPT_GROUPS
        keep = min(PREV, tp)
        state = [jnp.zeros((bp, keep, ATTN_DIM), F32), jnp.zeros((bp, keep, ATTN_DIM), F32),
                 jnp.zeros((bp, CONV_WIDTH - 1, conv_w.shape[2]), F32)]
        mixed, moved = [], []
        for g in range(PROMPT_GROUPS):
            x1, h2, w_tok, *state, eid, rank, cnt = _mixer_call(PROMPT_CFG, yp, mod_p, p, batch0=g * gb, nbatch=gb,
                                                                state=state)
            x1, h2, w_tok = x1.reshape(gb * tp, d), h2.reshape(gb * tp, d // 2), w_tok.reshape(gb * tp, LANES)
            mixed.append((x1, h2, w_tok))
            moved.append(_sorted_rows(h2, eid, rank, cnt, PROMPT_CFG.tt))
        kp, vp, cp = state
        groups = [(x1, h2, _expert_rows(*rows, p), w_tok) for (x1, h2, w_tok), rows in zip(mixed, moved)]
        yp = _moe_out_call(groups, mod_p[:, 5, :], p, MOE_OUT_TILE).reshape(bp, tp, d)
        outs[0].append(kp.reshape(bp, -1, N_HEADS, HEAD_DIM))
        outs[1].append(vp.reshape(bp, -1, N_HEADS, HEAD_DIM))
        outs[2].append(cp)

        p["bias"] = _rel_bias_blocks(rel_bias[l], sample_cfg, PAST_LEN, PAST_LEN - PREV)
        cache = (cache_k[l].reshape(bs, PREV, ATTN_DIM), cache_v[l].reshape(bs, PREV, ATTN_DIM), state_conv[l])
        x1, h2, gates, ks, vs, cs = _mixer_call(sample_cfg, ys, mod_s, p, cache)
        ys = _moe_dense_call(h2.reshape(bs * ts, d // 2), gates.reshape(bs * ts, LANES), x1.reshape(bs * ts, d),
                             jnp.repeat(mod_s[:, 5, :], ts, axis=0), p).reshape(bs, ts, d)
        outs[3].append(ks.reshape(bs, ts, N_HEADS, HEAD_DIM))
        outs[4].append(vs.reshape(bs, ts, N_HEADS, HEAD_DIM))
        outs[5].append(cs)
    return (yp, ys) + tuple(jnp.stack(o) for o in outs)
```

```python
import dataclasses
import functools

import jax
import jax.numpy as jnp
from jax import lax
from jax.experimental import pallas as pl
from jax.experimental.pallas import tpu as pltpu
from jax.experimental.pallas import tpu_sc as plsc

F32 = jnp.float32
BF16 = jnp.bfloat16
I32 = jnp.int32

CHUNK = 64
BAND_CHUNKS = 8
PREV = BAND_CHUNKS * CHUNK
PAST_LEN = 1024
N_HEADS = 8
HEAD_DIM = 64
ATTN_DIM = N_HEADS * HEAD_DIM
MAX_REL = 128
CONV_WIDTH = 31
CONV_PAD = 32
N_EXPERTS = 64
N_GROUPS = 8
GROUP_SIZE = N_EXPERTS // N_GROUPS
TOPK_GROUPS = 4
TOP_K = 8
ROUTE_SCALE = 2.5
EPS = 1e-6
LANES = 128
SUBLANES = 8
NEG = -1e30

VMEM_LIMIT_BYTES = 60 * 1024 * 1024

SC_CORES = 2
SC_SUBCORES = 16
SC_WORKERS = SC_CORES * SC_SUBCORES
SC_ROWS = 64

MAX_TILES = LANES
ROW_BLOCK = 1024


def _sigmoid(x):
    return 1.0 / (1.0 + jnp.exp(-x))


def _sigmoid_t(x):
    return 0.5 * jnp.tanh(0.5 * x) + 0.5


def _split_bf16(x):
    hi = x.astype(BF16)
    lo = (x - hi.astype(F32)).astype(BF16)
    return hi, lo


def _dot(a, b):
    return jnp.dot(a, b, preferred_element_type=F32)


def _dot3(a, b):
    a_hi, a_lo = _split_bf16(a)
    b_hi, b_lo = _split_bf16(b)
    return _dot(a_hi, b_hi) + _dot(a_hi, b_lo) + _dot(a_lo, b_hi)


def _pack(a, b):
    ia = lax.bitcast_convert_type(a.astype(BF16).astype(F32), I32)
    ib = lax.bitcast_convert_type(b.astype(BF16).astype(F32), I32)
    return ia | lax.shift_right_logical(ib, 16)


def _unpack(p):
    a = lax.bitcast_convert_type(p & jnp.int32(-65536), F32)
    b = lax.bitcast_convert_type(lax.shift_left(p, 16), F32)
    return a, b


def _dot_halves(a, b, w_ref):
    half = a.shape[1]
    return _dot(a, w_ref[0:half, :].astype(BF16)) + _dot(b, w_ref[half:2 * half, :].astype(BF16))


def _mod_kernel(c_ref, w_ref, b_ref, o_ref):
    c = c_ref[...]
    o_ref[...] = _dot3(c * _sigmoid(c), w_ref[...]) + b_ref[...]


def _mod_call(c, w_mod, b_mod):
    n, d = c.shape
    dout = w_mod.shape[1]
    bn = 1536
    return pl.pallas_call(
        _mod_kernel,
        out_shape=jax.ShapeDtypeStruct((n, dout), F32),
        grid=(dout // bn,),
        in_specs=[pl.BlockSpec((n, d), lambda j: (0, 0)),
                  pl.BlockSpec((d, bn), lambda j: (0, j)),
                  pl.BlockSpec((1, bn), lambda j: (0, j))],
        out_specs=pl.BlockSpec((n, bn), lambda j: (0, j)),
        compiler_params=pltpu.CompilerParams(dimension_semantics=("arbitrary",),
                                             vmem_limit_bytes=VMEM_LIMIT_BYTES),
        name="mod",
    )(c, w_mod, b_mod.reshape(1, dout))


@dataclasses.dataclass(frozen=True)
class MixerCfg:
    tt: int
    qb: int
    kb: int
    n_qblk: int
    kstride: int
    has_cache: bool
    sparse: bool


def _rms(x, g):
    ms = jnp.mean(x * x, axis=-1, keepdims=True)
    return x * lax.rsqrt(ms + EPS) * g


def _head_rms(z, hsum, g):
    ss = _dot((z * z).astype(BF16), hsum)
    return z * lax.rsqrt(ss * (1.0 / HEAD_DIM) + EPS) * g


def _expert_ids(t):
    sub = lax.broadcasted_iota(I32, (GROUP_SIZE, t), 0).astype(F32)
    return [sub + float(g * GROUP_SIZE) for g in range(N_GROUPS)]


def _pick(ids, eid, slabs):
    acc = functools.reduce(jnp.add, [jnp.where(ids[g] == eid, slabs[g], 0.0) for g in range(N_GROUPS)])
    return jnp.sum(acc, axis=0, keepdims=True)


def _route(choice, s):
    t = choice.shape[1]
    sub = lax.broadcasted_iota(I32, (GROUP_SIZE, t), 0).astype(F32)
    slabs = [choice[g * GROUP_SIZE:(g + 1) * GROUP_SIZE, :] for g in range(N_GROUPS)]
    s_slabs = [s[g * GROUP_SIZE:(g + 1) * GROUP_SIZE, :] for g in range(N_GROUPS)]
    gscore = []
    for c in slabs:
        m1 = jnp.max(c, axis=0, keepdims=True)
        first = jnp.min(jnp.where(c == m1, sub, float(GROUP_SIZE)), axis=0, keepdims=True)
        m2 = jnp.max(jnp.where(sub == first, -jnp.inf, c), axis=0, keepdims=True)
        gscore.append(m1 + m2)
    gsel = [jnp.zeros((1, t), F32) for _ in range(N_GROUPS)]
    for _ in range(TOPK_GROUPS):
        m = functools.reduce(jnp.maximum, gscore)
        first = functools.reduce(jnp.minimum,
                                 [jnp.where(gscore[g] == m, float(g), float(N_GROUPS)) for g in range(N_GROUPS)])
        for g in range(N_GROUPS):
            hit = first == float(g)
            gsel[g] = jnp.where(hit, 1.0, gsel[g])
            gscore[g] = jnp.where(hit, -jnp.inf, gscore[g])
    masked = [jnp.where(gsel[g] > 0.5, slabs[g], -jnp.inf) for g in range(N_GROUPS)]
    ids = _expert_ids(t)
    eids, raw = [], []
    for _ in range(TOP_K):
        m = jnp.max(functools.reduce(jnp.maximum, masked), axis=0, keepdims=True)
        cand = [jnp.where(masked[g] == m, ids[g], float(N_EXPERTS)) for g in range(N_GROUPS)]
        first = jnp.min(functools.reduce(jnp.minimum, cand), axis=0, keepdims=True)
        eids.append(first)
        raw.append(_pick(ids, first, s_slabs))
        masked = [jnp.where(ids[g] == first, -jnp.inf, masked[g]) for g in range(N_GROUPS)]
    wsum = functools.reduce(jnp.add, raw)
    return eids, [r / wsum * ROUTE_SCALE for r in raw]


def _lane_dense_rows(rows, t):
    pad = jnp.zeros((LANES - len(rows), t), F32)
    return jnp.concatenate(rows + [pad], axis=0).T


def _mixer_kernel(cfg, n_state, *refs):
    (x_ref, mod_ref, n1g_ref, n2g_ref, win_ref, qg_ref, kg_ref, hsum_ref, bias_ref, wao_ref,
     cw_ref, cb_ref, lng_ref, lnb_ref, wco_ref, bco_ref, bg_ref, wout_ref,
     wrh_ref, wrl_ref, rb_ref) = refs[:21]
    refs = refs[21:]
    if cfg.has_cache:
        kc_ref, vc_ref, cs_ref = refs[:3]
        refs = refs[3:]
    if cfg.sparse:
        tri_ref = refs[0]
        refs = refs[1:]
    refs = refs[n_state:]
    x1_ref, h2_ref, tok_ref, kout_ref, vout_ref, cout_ref = refs[:6]
    refs = refs[6:]
    if cfg.sparse:
        eid_ref, rank_ref, cnt_ref = refs[:3]
        refs = refs[3:]
    qbuf, kbuf, vbuf, obuf, uext, cvbuf = refs

    tt = cfg.tt
    t = pl.program_id(1)

    if cfg.has_cache:
        kbuf[0:PREV, :] = kc_ref[0].astype(BF16)
        vbuf[0:PREV, :] = vc_ref[0].astype(BF16)
        uext[CONV_PAD - (CONV_WIDTH - 1):CONV_PAD, :] = cs_ref[0]
    else:
        @pl.when(t == 0)
        def _():
            kbuf[0:PREV, :] = jnp.zeros((PREV, ATTN_DIM), BF16)
            vbuf[0:PREV, :] = jnp.zeros((PREV, ATTN_DIM), BF16)
            uext[0:CONV_PAD, :] = jnp.zeros((CONV_PAD, uext.shape[1]), F32)

    mod = mod_ref[0]
    sh1, sc1, g1, sh2, sc2 = (mod[i:i + 1, :] for i in range(5))

    x = x_ref[0]
    hb = (_rms(x, n1g_ref[...]) * (1.0 + sc1) + sh1).astype(BF16)

    a0, a1, a2, a3, a4, a5 = (0, ATTN_DIM, 2 * ATTN_DIM, 3 * ATTN_DIM,
                              3 * ATTN_DIM + cw_ref.shape[1], 3 * ATTN_DIM + 2 * cw_ref.shape[1])
    d_model = x.shape[1]

    q = _head_rms(_dot(hb, win_ref[:, a0:a1]), hsum_ref[...], qg_ref[...])
    qbuf[...] = (q * (HEAD_DIM ** -0.5)).astype(BF16)
    k = _head_rms(_dot(hb, win_ref[:, a1:a2]), hsum_ref[...], kg_ref[...])
    kout_ref[0] = k
    kbuf[PREV:PREV + tt, :] = k.astype(BF16)
    v = _dot(hb, win_ref[:, a2:a3])
    vout_ref[0] = v
    vbuf[PREV:PREV + tt, :] = v.astype(BF16)

    u = _dot(hb, win_ref[:, a3:a4]) * _sigmoid_t(_dot(hb, win_ref[:, a4:a5]))
    uext[CONV_PAD:CONV_PAD + tt, :] = u
    cout_ref[0] = uext[CONV_PAD + tt - (CONV_WIDTH - 1):CONV_PAD + tt, :]

    rc = min(cfg.qb, 64)
    cc = min(u.shape[1], 256)
    shifts = [j + CONV_PAD - (CONV_WIDTH - 1) for j in range(CONV_WIDTH)]

    def conv_unit(r0, c0):
        acc = jnp.broadcast_to(cb_ref[:, c0:c0 + cc], (rc, cc))
        for res in range(SUBLANES):
            group = [s for s in shifts if s % SUBLANES == res]
            if not group:
                continue
            lo, hi = min(group) - res, max(group) - res
            rows = hi - lo + rc + (SUBLANES if res else 0)
            slab = uext[pl.ds(r0 + lo, rows), c0:c0 + cc]
            if res:
                slab = pltpu.roll(slab, rows - res, 0)
            for s in group:
                j = s - shifts[0]
                a = s - res - lo
                acc = acc + cw_ref[j:j + 1, c0:c0 + cc] * slab[a:a + rc, :]
        cvbuf[pl.ds(r0, rc), c0:c0 + cc] = acc

    def mix_block(r0, k0):
        units = [(r0 + i * rc, c0) for i in range(cfg.qb // rc) for c0 in range(0, u.shape[1], cc)]
        if not cfg.has_cache:
            col = lax.broadcasted_iota(I32, (1, cfg.kb), 1)
            valid = jnp.logical_or(col >= PREV - k0, t > 0)
        heads = []
        for h in range(N_HEADS):
            c0, c1 = h * HEAD_DIM, (h + 1) * HEAD_DIM
            s = lax.dot_general(qbuf[pl.ds(r0, cfg.qb), c0:c1], kbuf[pl.ds(k0, cfg.kb), c0:c1],
                                (((1,), (1,)), ((), ())), preferred_element_type=F32)
            s = s + bias_ref[h]
            if not cfg.has_cache:
                s = jnp.where(valid, s, NEG)
            p = jnp.exp(s - jnp.max(s, axis=-1, keepdims=True))
            l = jnp.sum(p, axis=-1, keepdims=True)
            heads.append(_dot(p.astype(BF16), vbuf[pl.ds(k0, cfg.kb), c0:c1]) * (1.0 / l))
            for unit in units[h::N_HEADS]:
                conv_unit(*unit)
        obuf[pl.ds(r0, cfg.qb), :] = jnp.concatenate(heads, axis=1).astype(BF16)

    if cfg.n_qblk == 1:
        mix_block(0, 0)
    else:
        def mix_block_step(j, carry):
            mix_block(pl.multiple_of(j * cfg.qb, cfg.qb), pl.multiple_of(j * cfg.kstride, cfg.kstride))
            return carry
        lax.fori_loop(0, cfg.n_qblk, mix_block_step, 0)
    y_attn = _dot(obuf[...], wao_ref[...])
    if not cfg.has_cache:
        kbuf[0:PREV, :] = kbuf[tt:tt + PREV, :]
        vbuf[0:PREV, :] = vbuf[tt:tt + PREV, :]

    if not cfg.has_cache:
        uext[0:CONV_PAD, :] = uext[tt:tt + CONV_PAD, :]
    cv = cvbuf[...]
    mu = jnp.mean(cv, axis=-1, keepdims=True)
    xc = cv - mu
    var = jnp.mean(xc * xc, axis=-1, keepdims=True)
    cv = xc * lax.rsqrt(var + EPS) * lng_ref[...] + lnb_ref[...]
    cv = cv * _sigmoid_t(cv)
    y_conv = _dot(cv.astype(BF16), wco_ref[...]) + bco_ref[...]

    g_attn = _sigmoid_t(_dot(hb, win_ref[:, a5:a5 + d_model]) + bg_ref[:, 0:d_model])
    g_conv = _sigmoid_t(_dot(hb, win_ref[:, a5 + d_model:a5 + 2 * d_model]) + bg_ref[:, d_model:2 * d_model])
    m = g_attn * y_attn + g_conv * y_conv
    x1 = x + g1 * _dot(m.astype(BF16), wout_ref[...])
    x1_ref[0] = x1

    h2 = _rms(x1, n2g_ref[...]) * (1.0 + sc2) + sh2
    h2_ref[0] = _pack(h2[:, 0:d_model // 2], h2[:, d_model // 2:d_model])
    h2_hi, h2_lo = _split_bf16(h2)
    nt_dims = (((1,), (1,)), ((), ()))
    logits = (lax.dot_general(wrh_ref[...], h2_hi, nt_dims, preferred_element_type=F32)
              + lax.dot_general(wrl_ref[...], h2_hi, nt_dims, preferred_element_type=F32)
              + lax.dot_general(wrh_ref[...], h2_lo, nt_dims, preferred_element_type=F32))
    s = _sigmoid(logits)
    eids, weights = _route(s + rb_ref[...], s)
    ids = _expert_ids(tt)
    if cfg.sparse:
        sel = [functools.reduce(jnp.add, [jnp.where(ids[g] == e, 1.0, 0.0) for e in eids]) for g in range(N_GROUPS)]
        sel = jnp.concatenate(sel, axis=0)
        rank = _dot(sel.astype(BF16), tri_ref[...])
        rank_slabs = [rank[g * GROUP_SIZE:(g + 1) * GROUP_SIZE, :] for g in range(N_GROUPS)]
        tok_ref[0] = _lane_dense_rows(weights, tt)
        eid_ref[...] = jnp.concatenate(eids, axis=0)
        rank_ref[...] = jnp.concatenate([_pick(ids, e, rank_slabs) for e in eids], axis=0)
        step = pl.program_id(0) * pl.num_programs(1) + t

        @pl.when(step == 0)
        def _():
            cnt_ref[...] = jnp.zeros(cnt_ref.shape, F32)

        lane = lax.broadcasted_iota(I32, cnt_ref.shape, 1)
        cnt_ref[...] = jnp.where(lane == step, jnp.sum(sel, axis=1, keepdims=True), cnt_ref[...])
    else:
        gates = [functools.reduce(jnp.add, [jnp.where(ids[g] == e, w, 0.0) for e, w in zip(eids, weights)])
                 for g in range(N_GROUPS)]
        gates = jnp.concatenate(gates + [jnp.zeros((LANES - N_EXPERTS, tt), F32)], axis=0)
        tok_ref[0] = gates.T


def _toeplitz(v, rows, cols):
    w = rows + cols
    flat = jnp.tile(v, (1, rows))[:, :rows * (w - 1)]
    return flat.reshape(v.shape[0], rows, w - 1)[:, :, :cols]


def _rel_bias_blocks(table, cfg, q_pos0, k_pos0):
    q_pos = q_pos0 + jnp.arange(cfg.qb)
    k_pos = k_pos0 + jnp.arange(cfg.kb)
    qc = q_pos // CHUNK
    kc = k_pos // CHUNK
    allowed = (kc[None, :] <= qc[:, None]) & (kc[None, :] >= qc[:, None] - BAND_CHUNKS)
    w = cfg.qb + cfg.kb
    dj = jnp.arange(w)
    dj = jnp.where(dj < cfg.kb, dj, dj - w)
    rel = jnp.clip(q_pos0 - k_pos0 - dj, -MAX_REL, MAX_REL) + MAX_REL
    bias = _toeplitz(table[:, rel].astype(F32), cfg.qb, cfg.kb)
    return jnp.where(allowed[None], bias, NEG)


def _mixer_call(cfg, x, mod, p, cache=None, batch0=0, nbatch=None, state=None):
    b_all, t_total, d = x.shape
    nb = b_all if nbatch is None else nbatch
    nt = t_total // cfg.tt
    conv_dim = p["conv_w"].shape[1]
    n_in = p["w_in"].shape[1]

    def const(shape):
        return pl.BlockSpec(shape, lambda i, j: (0,) * len(shape), pipeline_mode=pl.Buffered(1))

    def per_row(shape):
        return pl.BlockSpec(shape, lambda i, j: (i + batch0, 0, 0))

    in_specs = [
        pl.BlockSpec((1, cfg.tt, d), lambda i, j: (i + batch0, j, 0)),
        per_row((1, 6, d)),
        const((1, d)), const((1, d)), const((d, n_in)),
        const((1, ATTN_DIM)), const((1, ATTN_DIM)), const((ATTN_DIM, ATTN_DIM)),
        const((N_HEADS, cfg.qb, cfg.kb)), const((ATTN_DIM, d)),
        const((CONV_WIDTH, conv_dim)), const((1, conv_dim)), const((1, conv_dim)), const((1, conv_dim)),
        const((conv_dim, d)), const((1, d)), const((1, 2 * d)), const((d, d)),
        const((N_EXPERTS, d)), const((N_EXPERTS, d)), const((N_EXPERTS, 1)),
    ]
    args = [x, mod, p["norm1_g"], p["norm2_g"], p["w_in"], p["q_norm_g"], p["k_norm_g"], p["hsum"],
            p["bias"], p["w_attn_o"], p["conv_w"], p["conv_b"], p["conv_ln_g"], p["conv_ln_b"],
            p["w_conv_o"], p["b_conv_o"], p["b_gate"], p["w_out"], p["wr_hi"], p["wr_lo"], p["router_bias"]]
    if cfg.has_cache:
        in_specs += [per_row((1, PREV, ATTN_DIM)), per_row((1, PREV, ATTN_DIM)),
                     per_row((1, CONV_WIDTH - 1, conv_dim))]
        args += list(cache)
    if cfg.sparse:
        assert nb * nt <= MAX_TILES
        in_specs += [const((cfg.tt, cfg.tt))]
        tok = jnp.arange(cfg.tt)
        args += [(tok[:, None] < tok[None, :]).astype(BF16)]
    aliases = {}
    if state is not None:
        for i, arr in enumerate(state):
            aliases[len(args)] = 3 + i
            in_specs.append(pl.BlockSpec(memory_space=pl.ANY))
            args.append(arr)
    keep = min(PREV, t_total)
    assert keep == cfg.tt
    out_shape = [jax.ShapeDtypeStruct((nb, t_total, d), F32),
                 jax.ShapeDtypeStruct((nb, t_total, d // 2), I32),
                 jax.ShapeDtypeStruct((nb, t_total, LANES), F32),
                 jax.ShapeDtypeStruct((b_all, keep, ATTN_DIM), F32),
                 jax.ShapeDtypeStruct((b_all, keep, ATTN_DIM), F32),
                 jax.ShapeDtypeStruct((b_all, CONV_WIDTH - 1, conv_dim), F32)]
    out_specs = [pl.BlockSpec((1, cfg.tt, d), lambda i, j: (i, j, 0)),
                 pl.BlockSpec((1, cfg.tt, d // 2), lambda i, j: (i, j, 0)),
                 pl.BlockSpec((1, cfg.tt, LANES), lambda i, j: (i, j, 0)),
                 per_row((1, keep, ATTN_DIM)), per_row((1, keep, ATTN_DIM)),
                 per_row((1, CONV_WIDTH - 1, conv_dim))]
    if cfg.sparse:
        out_shape += [jax.ShapeDtypeStruct((TOP_K, nb * t_total), F32),
                      jax.ShapeDtypeStruct((TOP_K, nb * t_total), F32),
                      jax.ShapeDtypeStruct((N_EXPERTS, MAX_TILES), F32)]
        out_specs += [pl.BlockSpec((TOP_K, cfg.tt), lambda i, j: (0, i * nt + j)),
                      pl.BlockSpec((TOP_K, cfg.tt), lambda i, j: (0, i * nt + j)),
                      pl.BlockSpec((N_EXPERTS, MAX_TILES), lambda i, j: (0, 0))]
    scratch = [pltpu.VMEM((cfg.tt, ATTN_DIM), BF16),
               pltpu.VMEM((PREV + cfg.tt, ATTN_DIM), BF16),
               pltpu.VMEM((PREV + cfg.tt, ATTN_DIM), BF16),
               pltpu.VMEM((cfg.tt, ATTN_DIM), BF16),
               pltpu.VMEM((CONV_PAD + cfg.tt, conv_dim), F32),
               pltpu.VMEM((cfg.tt, conv_dim), F32)]
    return pl.pallas_call(
        functools.partial(_mixer_kernel, cfg, 0 if state is None else len(state)),
        out_shape=out_shape,
        grid=(nb, nt),
        in_specs=in_specs,
        out_specs=out_specs,
        scratch_shapes=scratch,
        input_output_aliases=aliases,
        compiler_params=pltpu.CompilerParams(dimension_semantics=("arbitrary", "arbitrary"),
                                             vmem_limit_bytes=VMEM_LIMIT_BYTES),
        name="mixer_sample" if cfg.has_cache else "mixer_prompt",
    )(*args)


def _exact_parts(x, n):
    parts = []
    for _ in range(n):
        part = x.astype(BF16)
        parts.append(part)
        x = x - part.astype(F32)
    return parts


def _plan_kernel(tt, cnt_ref, eid_ref, rank_ref, tri_e_ref, tri_t_ref, pos_ref, blk_ref, nblk_ref, base_ref):
    step = pl.program_id(0)

    @pl.when(step == 0)
    def _():
        cnt = cnt_ref[...]
        total = jnp.sum(cnt, axis=1, keepdims=True)
        padded = jnp.floor((total + float(ROW_BLOCK - 1)) * (1.0 / ROW_BLOCK)) * float(ROW_BLOCK)
        padded = jnp.broadcast_to(padded, cnt.shape)
        start = functools.reduce(jnp.add, [_dot(tri_e_ref[...], part) for part in _exact_parts(padded, 3)])
        before = functools.reduce(jnp.add, [_dot(part, tri_t_ref[...]) for part in _exact_parts(cnt, 2)])
        base_ref[...] = start + before
        end = (start + padded)[:, 0:1]
        first_row = lax.broadcasted_iota(I32, blk_ref.shape, 1).astype(F32) * float(ROW_BLOCK)
        owner = jnp.sum(jnp.where(end <= first_row, 1.0, 0.0), axis=0, keepdims=True)
        blk_ref[...] = jnp.minimum(owner, float(N_EXPERTS - 1)).astype(I32)
        nblk = jnp.max(end, axis=0, keepdims=True) * (1.0 / ROW_BLOCK)
        nblk_ref[...] = jnp.broadcast_to(nblk, nblk_ref.shape).astype(I32)

    lane = lax.broadcasted_iota(I32, base_ref.shape, 1)
    col = jnp.sum(jnp.where(lane == step, base_ref[...], 0.0), axis=1, keepdims=True)
    ids = _expert_ids(tt)
    col_slabs = [jnp.broadcast_to(col[g * GROUP_SIZE:(g + 1) * GROUP_SIZE, :], (GROUP_SIZE, tt))
                 for g in range(N_GROUPS)]
    eid = eid_ref[...]
    rows = [_pick(ids, eid[k:k + 1, :], col_slabs) for k in range(TOP_K)]
    pos_ref[...] = (jnp.concatenate(rows, axis=0) + rank_ref[...]).astype(I32)


def _plan_call(cnt, eid, rank, tt, n_blocks):
    n = eid.shape[1]
    e = jnp.arange(N_EXPERTS)
    s = jnp.arange(MAX_TILES)
    nb_pad = -(-n_blocks // LANES) * LANES
    return pl.pallas_call(
        functools.partial(_plan_kernel, tt),
        out_shape=[jax.ShapeDtypeStruct((TOP_K, n), I32),
                   jax.ShapeDtypeStruct((1, nb_pad), I32),
                   jax.ShapeDtypeStruct((1, LANES), I32)],
        grid=(n // tt,),
        in_specs=[pl.BlockSpec((N_EXPERTS, MAX_TILES), lambda i: (0, 0)),
                  pl.BlockSpec((TOP_K, tt), lambda i: (0, i)),
                  pl.BlockSpec((TOP_K, tt), lambda i: (0, i)),
                  pl.BlockSpec((N_EXPERTS, N_EXPERTS), lambda i: (0, 0)),
                  pl.BlockSpec((MAX_TILES, MAX_TILES), lambda i: (0, 0))],
        out_specs=[pl.BlockSpec((TOP_K, tt), lambda i: (0, i)),
                   pl.BlockSpec((1, nb_pad), lambda i: (0, 0)),
                   pl.BlockSpec((1, LANES), lambda i: (0, 0))],
        scratch_shapes=[pltpu.VMEM((N_EXPERTS, MAX_TILES), F32)],
        compiler_params=pltpu.CompilerParams(dimension_semantics=("arbitrary",),
                                             vmem_limit_bytes=VMEM_LIMIT_BYTES),
        name="plan",
    )(cnt, eid, rank, (e[None, :] < e[:, None]).astype(BF16), (s[:, None] < s[None, :]).astype(BF16))


def _sc_mesh():
    return plsc.VectorSubcoreMesh(core_axis_name="c", subcore_axis_name="s")


def _dispatch_call(rows, pos, n_sorted):
    n, w = rows.shape
    per_worker = n // SC_WORKERS
    nsteps = per_worker // SC_ROWS
    assert per_worker * SC_WORKERS == n and nsteps * SC_ROWS == per_worker and nsteps % 2 == 0

    @functools.partial(
        pl.kernel, mesh=_sc_mesh(),
        out_type=jax.ShapeDtypeStruct((n_sorted, w), rows.dtype),
        scratch_types=[pltpu.VMEM((2, TOP_K, SC_ROWS), I32),
                       pltpu.VMEM((2, SC_ROWS, w), rows.dtype),
                       pltpu.SemaphoreType.DMA((2,)),
                       pltpu.SemaphoreType.DMA((2,))],
    )
    def dispatch(x_hbm, pos_hbm, out_hbm, idx_v, rows_v, load_sem, scat_sem):
        base = (lax.axis_index("s") * SC_CORES + lax.axis_index("c")) * per_worker

        def load(i, b):
            return pltpu.make_async_copy(x_hbm.at[pl.ds(base + i * SC_ROWS, SC_ROWS)], rows_v.at[b], load_sem.at[b])

        def scatter(b, k):
            return pltpu.make_async_copy(rows_v.at[b], out_hbm.at[idx_v.at[b, k]], scat_sem.at[b])

        def load_start(i, b):
            for k in range(TOP_K):
                pltpu.sync_copy(pos_hbm.at[pl.ds(k * n + base + i * SC_ROWS, SC_ROWS)], idx_v.at[b, k])
            load(i, b).start()

        load_start(0, 0)

        @pl.loop(0, nsteps, step=2)
        def _(i):
            for b in range(2):
                ii = i + b

                @pl.when(ii >= 1)
                def _():
                    for k in range(TOP_K):
                        scatter(1 - b, k).wait()

                @pl.when(ii + 1 < nsteps)
                def _():
                    load_start(ii + 1, 1 - b)

                load(ii, b).wait()
                for k in range(TOP_K):
                    scatter(b, k).start()

        for k in range(TOP_K):
            scatter((nsteps - 1) % 2, k).wait()

    return dispatch(rows, pos)


def _collect_call(table, idx):
    n = idx.shape[0]
    w = table.shape[1]
    per_worker = n // SC_WORKERS
    nsteps = per_worker // SC_ROWS
    assert per_worker * SC_WORKERS == n and nsteps * SC_ROWS == per_worker and nsteps % 2 == 0

    @functools.partial(
        pl.kernel, mesh=_sc_mesh(),
        out_type=jax.ShapeDtypeStruct((n, w), table.dtype),
        scratch_types=[pltpu.VMEM((2, SC_ROWS), I32),
                       pltpu.VMEM((2, SC_ROWS, w), table.dtype),
                       pltpu.SemaphoreType.DMA((2,)),
                       pltpu.SemaphoreType.DMA((2,))],
    )
    def collect(table_hbm, idx_hbm, out_hbm, idx_v, rows_v, gather_sem, write_sem):
        base = (lax.axis_index("s") * SC_CORES + lax.axis_index("c")) * per_worker

        def gather(b):
            return pltpu.make_async_copy(table_hbm.at[idx_v.at[b]], rows_v.at[b], gather_sem.at[b])

        def write(i, b):
            return pltpu.make_async_copy(rows_v.at[b], out_hbm.at[pl.ds(base + i * SC_ROWS, SC_ROWS)], write_sem.at[b])

        def gather_start(i, b):
            pltpu.sync_copy(idx_hbm.at[pl.ds(base + i * SC_ROWS, SC_ROWS)], idx_v.at[b])
            gather(b).start()

        gather_start(0, 0)

        @pl.loop(0, nsteps, step=2)
        def _(i):
            for b in range(2):
                ii = i + b

                @pl.when(ii >= 1)
                def _():
                    write(ii - 1, 1 - b).wait()

                @pl.when(ii + 1 < nsteps)
                def _():
                    gather_start(ii + 1, 1 - b)

                gather(b).wait()
                write(ii, b).start()

        write(nsteps - 1, (nsteps - 1) % 2).wait()

    return collect(table, idx)


def _expert_ffn_kernel(blk_ref, nblk_ref, x_ref, wg_ref, wu_ref, wd_ref, o_ref, wg_s, wu_s, wd_s):
    j = pl.program_id(0)

    @pl.when(jnp.logical_or(j == 0, blk_ref[j] != blk_ref[jnp.maximum(j - 1, 0)]))
    def _():
        wg_s[...] = wg_ref[0].astype(BF16)
        wu_s[...] = wu_ref[0].astype(BF16)
        wd_s[...] = wd_ref[0].astype(BF16)

    @pl.when(j < nblk_ref[0])
    def _():
        a, b = _unpack(x_ref[...])
        a, b = a.astype(BF16), b.astype(BF16)
        hg = _dot_halves(a, b, wg_s)
        act = hg * _sigmoid_t(hg) * _dot_halves(a, b, wu_s)
        y = _dot(act.astype(BF16), wd_s[...])
        half = y.shape[1] // 2
        o_ref[...] = _pack(y[:, 0:half], y[:, half:2 * half])


def _expert_ffn_call(xs, blk, nblk, p):
    n_sorted, w = xs.shape
    d = 2 * w
    ff = p["w_e_gate"].shape[2]

    def rows(j, blk, nblk):
        return (jnp.minimum(j, nblk[0] - 1), 0)

    def expert(j, blk, nblk):
        return (blk[j], 0, 0)

    return pl.pallas_call(
        _expert_ffn_kernel,
        out_shape=jax.ShapeDtypeStruct((n_sorted, w), I32),
        grid_spec=pltpu.PrefetchScalarGridSpec(
            num_scalar_prefetch=2,
            grid=(n_sorted // ROW_BLOCK,),
            in_specs=[pl.BlockSpec((ROW_BLOCK, w), rows),
                      pl.BlockSpec((1, d, ff), expert),
                      pl.BlockSpec((1, d, ff), expert),
                      pl.BlockSpec((1, ff, d), expert)],
            out_specs=pl.BlockSpec((ROW_BLOCK, w), rows),
            scratch_shapes=[pltpu.VMEM((d, ff), BF16), pltpu.VMEM((d, ff), BF16), pltpu.VMEM((ff, d), BF16)]),
        compiler_params=pltpu.CompilerParams(dimension_semantics=("arbitrary",),
                                             vmem_limit_bytes=VMEM_LIMIT_BYTES),
        name="expert_ffn",
    )(blk, nblk, xs, p["w_e_gate"], p["w_e_up"], p["w_e_down"])


def _swiglu_halves(a, b, wg_ref, wu_ref):
    hg = _dot_halves(a, b, wg_ref)
    return hg * _sigmoid_t(hg) * _dot_halves(a, b, wu_ref)


def _moe_out_kernel(n_groups, tiles, *refs):
    groups = [refs[4 * q:4 * q + 4] for q in range(n_groups)]
    g2_ref, wsg_ref, wsu_ref, wsd_ref, o_ref = refs[4 * n_groups:]

    def combine(x1_ref, h_ref, y_ref, w_ref):
        a, b = _unpack(h_ref[...])
        shared = _dot(_swiglu_halves(a.astype(BF16), b.astype(BF16), wsg_ref, wsu_ref).astype(BF16),
                      wsd_ref[...].astype(BF16))
        half = h_ref.shape[1]
        w = w_ref[...]
        acc_a, acc_b = shared[:, 0:half], shared[:, half:2 * half]
        for k in range(TOP_K):
            ya, yb = _unpack(y_ref[k])
            acc_a = acc_a + w[:, k:k + 1] * ya
            acc_b = acc_b + w[:, k:k + 1] * yb
        g2 = g2_ref[...]
        o_ref[:, 0:half] = x1_ref[:, 0:half] + g2[:, 0:half] * acc_a
        o_ref[:, half:2 * half] = x1_ref[:, half:2 * half] + g2[:, half:2 * half] * acc_b

    for q in range(n_groups):
        pl.when(pl.program_id(0) // tiles == q)(functools.partial(combine, *groups[q]))


def _moe_out_call(groups, g2, p, tm):
    n, d = groups[0][0].shape
    n_all = n * len(groups)
    sff = p["w_s_gate"].shape[1]
    per_row = n_all // g2.shape[0]
    g2 = g2.reshape(g2.shape[0], 1, d)
    tiles = n // tm

    def const(shape):
        return pl.BlockSpec(shape, lambda i: (0,) * len(shape), pipeline_mode=pl.Buffered(1))

    in_specs, args = [], []
    for q, group in enumerate(groups):
        tile = lambda i, q=q: jnp.clip(i - q * tiles, 0, tiles - 1)
        in_specs += [pl.BlockSpec((tm, d), lambda i, tile=tile: (tile(i), 0)),
                     pl.BlockSpec((tm, d // 2), lambda i, tile=tile: (tile(i), 0)),
                     pl.BlockSpec((TOP_K, tm, d // 2), lambda i, tile=tile: (0, tile(i), 0)),
                     pl.BlockSpec((tm, LANES), lambda i, tile=tile: (tile(i), 0))]
        args += list(group)
    in_specs += [pl.BlockSpec((None, 1, d), lambda i: (i * tm // per_row, 0, 0)),
                 const((d, sff)), const((d, sff)), const((sff, d))]
    args += [g2, p["w_s_gate"], p["w_s_up"], p["w_s_down"]]
    return pl.pallas_call(
        functools.partial(_moe_out_kernel, len(groups), tiles),
        out_shape=jax.ShapeDtypeStruct((n_all, d), F32),
        grid=(n_all // tm,),
        in_specs=in_specs,
        out_specs=pl.BlockSpec((tm, d), lambda i: (i, 0)),
        compiler_params=pltpu.CompilerParams(dimension_semantics=("arbitrary",),
                                             vmem_limit_bytes=VMEM_LIMIT_BYTES),
        name="moe_out",
    )(*args)


def _moe_dense_kernel(h_ref, gates_ref, x1_ref, g2_ref, wsg_ref, wsu_ref, wsd_ref, wg_ref, wu_ref, wd_ref,
                      o_ref, acc_ref):
    e = pl.program_id(1)
    a, b = _unpack(h_ref[...])
    a, b = a.astype(BF16), b.astype(BF16)

    @pl.when(e == 0)
    def _():
        acc_ref[...] = _dot(_swiglu_halves(a, b, wsg_ref, wsu_ref).astype(BF16), wsd_ref[...].astype(BF16))

    gates = gates_ref[...]
    lane = lax.broadcasted_iota(I32, gates.shape, 1)
    gate = jnp.sum(jnp.where(lane == e, gates, 0.0), axis=1, keepdims=True)
    act = _swiglu_halves(a, b, wg_ref.at[0], wu_ref.at[0]) * gate
    acc_ref[...] += _dot(act.astype(BF16), wd_ref[0].astype(BF16))

    @pl.when(e == pl.num_programs(1) - 1)
    def _():
        o_ref[...] = x1_ref[...] + g2_ref[...] * acc_ref[...]


def _moe_dense_call(h2, gates, x1, g2_rows, p):
    n, d = x1.shape
    ff = p["w_e_gate"].shape[2]
    sff = p["w_s_gate"].shape[1]

    def const(shape):
        return pl.BlockSpec(shape, lambda i, e: (0,) * len(shape), pipeline_mode=pl.Buffered(1))

    return pl.pallas_call(
        _moe_dense_kernel,
        out_shape=jax.ShapeDtypeStruct((n, d), F32),
        grid=(1, N_EXPERTS),
        in_specs=[const((n, d // 2)), const((n, LANES)), const((n, d)), const((n, d)),
                  const((d, sff)), const((d, sff)), const((sff, d)),
                  pl.BlockSpec((1, d, ff), lambda i, e: (e, 0, 0)),
                  pl.BlockSpec((1, d, ff), lambda i, e: (e, 0, 0)),
                  pl.BlockSpec((1, ff, d), lambda i, e: (e, 0, 0))],
        out_specs=pl.BlockSpec((n, d), lambda i, e: (0, 0)),
        scratch_shapes=[pltpu.VMEM((n, d), F32)],
        compiler_params=pltpu.CompilerParams(dimension_semantics=("arbitrary", "arbitrary"),
                                             vmem_limit_bytes=VMEM_LIMIT_BYTES),
        name="moe_dense",
    )(h2, gates, x1, g2_rows, p["w_s_gate"], p["w_s_up"], p["w_s_down"], p["w_e_gate"], p["w_e_up"], p["w_e_down"])


PROMPT_CFG = MixerCfg(tt=512, qb=256, kb=768, n_qblk=2, kstride=256, has_cache=False, sparse=True)
MOE_OUT_TILE = 256
PROMPT_GROUPS = 2


def _layer_params(l, w_in, q_norm_g, k_norm_g, w_attn_o, conv_w, conv_b, conv_ln_g, conv_ln_b, w_conv_o,
                  b_conv_o, b_gate, w_out, norm1_g, norm2_g, w_router, router_bias,
                  w_e_gate, w_e_up, w_e_down, w_s_gate, w_s_up, w_s_down):
    row = lambda a: a[l].reshape(1, -1)
    wr_t = w_router[l].T
    wr_hi = wr_t.astype(BF16)
    head = jnp.arange(ATTN_DIM) // HEAD_DIM
    return {
        "norm1_g": row(norm1_g), "norm2_g": row(norm2_g), "w_in": w_in[l].astype(BF16),
        "q_norm_g": jnp.tile(q_norm_g[l], N_HEADS).reshape(1, -1),
        "k_norm_g": jnp.tile(k_norm_g[l], N_HEADS).reshape(1, -1),
        "hsum": (head[:, None] == head[None, :]).astype(BF16),
        "w_attn_o": w_attn_o[l].astype(BF16), "conv_w": conv_w[l], "conv_b": row(conv_b),
        "conv_ln_g": row(conv_ln_g), "conv_ln_b": row(conv_ln_b), "w_conv_o": w_conv_o[l].astype(BF16),
        "b_conv_o": row(b_conv_o), "b_gate": row(b_gate), "w_out": w_out[l].astype(BF16),
        "wr_hi": wr_hi, "wr_lo": (wr_t - wr_hi.astype(F32)).astype(BF16),
        "router_bias": router_bias[l].reshape(-1, 1),
        "w_e_gate": w_e_gate[l], "w_e_up": w_e_up[l], "w_e_down": w_e_down[l],
        "w_s_gate": w_s_gate[l], "w_s_up": w_s_up[l], "w_s_down": w_s_down[l],
    }


def _sorted_rows(h2, eid, rank, cnt, tt):
    n = h2.shape[0]
    n_sorted = n * TOP_K + N_EXPERTS * ROW_BLOCK
    pos, blk, nblk = _plan_call(cnt, eid, rank, tt, n_sorted // ROW_BLOCK)
    pos = pos.reshape(TOP_K * n)
    return _dispatch_call(h2, pos, n_sorted), pos, blk.reshape(-1), nblk.reshape(-1)


def _expert_rows(xs, pos, blk, nblk, p):
    ys = _expert_ffn_call(xs, blk, nblk, p)
    return _collect_call(ys, pos).reshape(TOP_K, pos.shape[0] // TOP_K, xs.shape[1])


def kernel(x_prompt, x_sample, c_prompt, c_sample, cache_k, cache_v, state_conv, w_mod, b_mod, norm1_g, w_in, q_norm_g, k_norm_g, rel_bias, w_attn_o, conv_w, conv_b, conv_ln_g, conv_ln_b, w_conv_o, b_conv_o, b_gate, w_out, norm2_g, w_router, router_bias, w_e_gate, w_e_up, w_e_down, w_s_gate, w_s_up, w_s_down):
    depth = w_mod.shape[0]
    bp, tp, d = x_prompt.shape
    bs, ts, _ = x_sample.shape
    assert cache_k.shape[2] == min(PREV, PAST_LEN) == PREV
    sample_cfg = MixerCfg(tt=ts, qb=ts, kb=PREV + ts, n_qblk=1, kstride=0, has_cache=True, sparse=False)

    yp, ys = x_prompt, x_sample
    outs = [[] for _ in range(6)]
    for l in range(depth):
        p = _layer_params(l, w_in, q_norm_g, k_norm_g, w_attn_o, conv_w, conv_b, conv_ln_g, conv_ln_b,
                          w_conv_o, b_conv_o, b_gate, w_out, norm1_g, norm2_g, w_router, router_bias,
                          w_e_gate, w_e_up, w_e_down, w_s_gate, w_s_up, w_s_down)
        mod = _mod_call(jnp.concatenate([c_prompt, c_sample], axis=0), w_mod[l], b_mod[l])
        mod = mod.reshape(bp + bs, 6, d)
        mod_p, mod_s = mod[:bp], mod[bp:]

        p["bias"] = _rel_bias_blocks(rel_bias[l], PROMPT_CFG, PREV, 0)
        gb = bp // PROMPT_GROUPS
        keep = min(PREV, tp)
        state = [jnp.zeros((bp, keep, ATTN_DIM), F32), jnp.zeros((bp, keep, ATTN_DIM), F32),
                 jnp.zeros((bp, CONV_WIDTH - 1, conv_w.shape[2]), F32)]
        mixed, moved = [], []
        for g in range(PROMPT_GROUPS):
            x1, h2, w_tok, *state, eid, rank, cnt = _mixer_call(PROMPT_CFG, yp, mod_p, p, batch0=g * gb, nbatch=gb,
                                                                state=state)
            x1, h2, w_tok = x1.reshape(gb * tp, d), h2.reshape(gb * tp, d // 2), w_tok.reshape(gb * tp, LANES)
            mixed.append((x1, h2, w_tok))
            moved.append(_sorted_rows(h2, eid, rank, cnt, PROMPT_CFG.tt))
        kp, vp, cp = state
        groups = [(x1, h2, _expert_rows(*rows, p), w_tok) for (x1, h2, w_tok), rows in zip(mixed, moved)]
        yp = _moe_out_call(groups, mod_p[:, 5, :], p, MOE_OUT_TILE).reshape(bp, tp, d)
        outs[0].append(kp.reshape(bp, -1, N_HEADS, HEAD_DIM))
        outs[1].append(vp.reshape(bp, -1, N_HEADS, HEAD_DIM))
        outs[2].append(cp)

        p["bias"] = _rel_bias_blocks(rel_bias[l], sample_cfg, PAST_LEN, PAST_LEN - PREV)
        cache = (cache_k[l].reshape(bs, PREV, ATTN_DIM), cache_v[l].reshape(bs, PREV, ATTN_DIM), state_conv[l])
        x1, h2, gates, ks, vs, cs = _mixer_call(sample_cfg, ys, mod_s, p, cache)
        ys = _moe_dense_call(h2.reshape(bs * ts, d // 2), gates.reshape(bs * ts, LANES), x1.reshape(bs * ts, d),
                             jnp.repeat(mod_s[:, 5, :], ts, axis=0), p).reshape(bs, ts, d)
        outs[3].append(ks.reshape(bs, ts, N_HEADS, HEAD_DIM))
        outs[4].append(vs.reshape(bs, ts, N_HEADS, HEAD_DIM))
        outs[5].append(cs)
    return (yp, ys) + tuple(jnp.stack(o) for o in outs)
```

```python
import dataclasses
import functools

import jax
import jax.numpy as jnp
from jax import lax
from jax.experimental import pallas as pl
from jax.experimental.pallas import tpu as pltpu
from jax.experimental.pallas import tpu_sc as plsc

F32 = jnp.float32
BF16 = jnp.bfloat16
I32 = jnp.int32

CHUNK = 64
BAND_CHUNKS = 8
PREV = BAND_CHUNKS * CHUNK
PAST_LEN = 1024
N_HEADS = 8
HEAD_DIM = 64
ATTN_DIM = N_HEADS * HEAD_DIM
MAX_REL = 128
CONV_WIDTH = 31
CONV_PAD = 32
N_EXPERTS = 64
N_GROUPS = 8
GROUP_SIZE = N_EXPERTS // N_GROUPS
TOPK_GROUPS = 4
TOP_K = 8
ROUTE_SCALE = 2.5
EPS = 1e-6
LANES = 128
SUBLANES = 8
NEG = -1e30

VMEM_LIMIT_BYTES = 60 * 1024 * 1024

SC_CORES = 2
SC_SUBCORES = 16
SC_WORKERS = SC_CORES * SC_SUBCORES
SC_ROWS = 64

MAX_TILES = LANES
ROW_BLOCK = 1024


def _sigmoid(x):
    return 1.0 / (1.0 + jnp.exp(-x))


def _sigmoid_t(x):
    return 0.5 * jnp.tanh(0.5 * x) + 0.5


def _split_bf16(x):
    hi = x.astype(BF16)
    lo = (x - hi.astype(F32)).astype(BF16)
    return hi, lo


def _dot(a, b):
    return jnp.dot(a, b, preferred_element_type=F32)


def _dot3(a, b):
    a_hi, a_lo = _split_bf16(a)
    b_hi, b_lo = _split_bf16(b)
    return _dot(a_hi, b_hi) + _dot(a_hi, b_lo) + _dot(a_lo, b_hi)


def _pack(a, b):
    ia = lax.bitcast_convert_type(a.astype(BF16).astype(F32), I32)
    ib = lax.bitcast_convert_type(b.astype(BF16).astype(F32), I32)
    return ia | lax.shift_right_logical(ib, 16)


def _unpack(p):
    a = lax.bitcast_convert_type(p & jnp.int32(-65536), F32)
    b = lax.bitcast_convert_type(lax.shift_left(p, 16), F32)
    return a, b


def _dot_halves(a, b, w_ref):
    half = a.shape[1]
    return _dot(a, w_ref[0:half, :].astype(BF16)) + _dot(b, w_ref[half:2 * half, :].astype(BF16))


def _mod_kernel(c_ref, w_ref, b_ref, o_ref):
    c = c_ref[...]
    o_ref[...] = _dot3(c * _sigmoid(c), w_ref[...]) + b_ref[...]


def _mod_call(c, w_mod, b_mod):
    n, d = c.shape
    dout = w_mod.shape[1]
    bn = 1536
    return pl.pallas_call(
        _mod_kernel,
        out_shape=jax.ShapeDtypeStruct((n, dout), F32),
        grid=(dout // bn,),
        in_specs=[pl.BlockSpec((n, d), lambda j: (0, 0)),
                  pl.BlockSpec((d, bn), lambda j: (0, j)),
                  pl.BlockSpec((1, bn), lambda j: (0, j))],
        out_specs=pl.BlockSpec((n, bn), lambda j: (0, j)),
        compiler_params=pltpu.CompilerParams(dimension_semantics=("arbitrary",),
                                             vmem_limit_bytes=VMEM_LIMIT_BYTES),
        name="mod",
    )(c, w_mod, b_mod.reshape(1, dout))


@dataclasses.dataclass(frozen=True)
class MixerCfg:
    tt: int
    qb: int
    kb: int
    n_qblk: int
    kstride: int
    has_cache: bool
    sparse: bool


def _rms(x, g):
    ms = jnp.mean(x * x, axis=-1, keepdims=True)
    return x * lax.rsqrt(ms + EPS) * g


def _head_rms(z, hsum, g):
    ss = _dot((z * z).astype(BF16), hsum)
    return z * lax.rsqrt(ss * (1.0 / HEAD_DIM) + EPS) * g


def _expert_ids(t):
    sub = lax.broadcasted_iota(I32, (GROUP_SIZE, t), 0).astype(F32)
    return [sub + float(g * GROUP_SIZE) for g in range(N_GROUPS)]


def _pick(ids, eid, slabs):
    acc = functools.reduce(jnp.add, [jnp.where(ids[g] == eid, slabs[g], 0.0) for g in range(N_GROUPS)])
    return jnp.sum(acc, axis=0, keepdims=True)


def _route(choice, s):
    t = choice.shape[1]
    sub = lax.broadcasted_iota(I32, (GROUP_SIZE, t), 0).astype(F32)
    slabs = [choice[g * GROUP_SIZE:(g + 1) * GROUP_SIZE, :] for g in range(N_GROUPS)]
    s_slabs = [s[g * GROUP_SIZE:(g + 1) * GROUP_SIZE, :] for g in range(N_GROUPS)]
    gscore = []
    for c in slabs:
        m1 = jnp.max(c, axis=0, keepdims=True)
        first = jnp.min(jnp.where(c == m1, sub, float(GROUP_SIZE)), axis=0, keepdims=True)
        m2 = jnp.max(jnp.where(sub == first, -jnp.inf, c), axis=0, keepdims=True)
        gscore.append(m1 + m2)
    gsel = [jnp.zeros((1, t), F32) for _ in range(N_GROUPS)]
    for _ in range(TOPK_GROUPS):
        m = functools.reduce(jnp.maximum, gscore)
        first = functools.reduce(jnp.minimum,
                                 [jnp.where(gscore[g] == m, float(g), float(N_GROUPS)) for g in range(N_GROUPS)])
        for g in range(N_GROUPS):
            hit = first == float(g)
            gsel[g] = jnp.where(hit, 1.0, gsel[g])
            gscore[g] = jnp.where(hit, -jnp.inf, gscore[g])
    masked = [jnp.where(gsel[g] > 0.5, slabs[g], -jnp.inf) for g in range(N_GROUPS)]
    ids = _expert_ids(t)
    eids, raw = [], []
    for _ in range(TOP_K):
        m = jnp.max(functools.reduce(jnp.maximum, masked), axis=0, keepdims=True)
        cand = [jnp.where(masked[g] == m, ids[g], float(N_EXPERTS)) for g in range(N_GROUPS)]
        first = jnp.min(functools.reduce(jnp.minimum, cand), axis=0, keepdims=True)
        eids.append(first)
        raw.append(_pick(ids, first, s_slabs))
        masked = [jnp.where(ids[g] == first, -jnp.inf, masked[g]) for g in range(N_GROUPS)]
    wsum = functools.reduce(jnp.add, raw)
    return eids, [r / wsum * ROUTE_SCALE for r in raw]


def _lane_dense_rows(rows, t):
    pad = jnp.zeros((LANES - len(rows), t), F32)
    return jnp.concatenate(rows + [pad], axis=0).T


def _mixer_kernel(cfg, n_state, *refs):
    (x_ref, mod_ref, n1g_ref, n2g_ref, win_ref, qg_ref, kg_ref, hsum_ref, bias_ref, wao_ref,
     cw_ref, cb_ref, lng_ref, lnb_ref, wco_ref, bco_ref, bg_ref, wout_ref,
     wrh_ref, wrl_ref, rb_ref) = refs[:21]
    refs = refs[21:]
    if cfg.has_cache:
        kc_ref, vc_ref, cs_ref = refs[:3]
        refs = refs[3:]
    if cfg.sparse:
        tri_ref = refs[0]
        refs = refs[1:]
    refs = refs[n_state:]
    x1_ref, h2_ref, tok_ref, kout_ref, vout_ref, cout_ref = refs[:6]
    refs = refs[6:]
    if cfg.sparse:
        eid_ref, rank_ref, cnt_ref = refs[:3]
        refs = refs[3:]
    qbuf, kbuf, vbuf, obuf, uext, cvbuf = refs

    tt = cfg.tt
    t = pl.program_id(1)

    if cfg.has_cache:
        kbuf[0:PREV, :] = kc_ref[0].astype(BF16)
        vbuf[0:PREV, :] = vc_ref[0].astype(BF16)
        uext[CONV_PAD - (CONV_WIDTH - 1):CONV_PAD, :] = cs_ref[0]
    else:
        @pl.when(t == 0)
        def _():
            kbuf[0:PREV, :] = jnp.zeros((PREV, ATTN_DIM), BF16)
            vbuf[0:PREV, :] = jnp.zeros((PREV, ATTN_DIM), BF16)
            uext[0:CONV_PAD, :] = jnp.zeros((CONV_PAD, uext.shape[1]), F32)

    mod = mod_ref[0]
    sh1, sc1, g1, sh2, sc2 = (mod[i:i + 1, :] for i in range(5))

    x = x_ref[0]
    hb = (_rms(x, n1g_ref[...]) * (1.0 + sc1) + sh1).astype(BF16)

    a0, a1, a2, a3, a4, a5 = (0, ATTN_DIM, 2 * ATTN_DIM, 3 * ATTN_DIM,
                              3 * ATTN_DIM + cw_ref.shape[1], 3 * ATTN_DIM + 2 * cw_ref.shape[1])
    d_model = x.shape[1]

    q = _head_rms(_dot(hb, win_ref[:, a0:a1]), hsum_ref[...], qg_ref[...])
    qbuf[...] = (q * (HEAD_DIM ** -0.5)).astype(BF16)
    k = _head_rms(_dot(hb, win_ref[:, a1:a2]), hsum_ref[...], kg_ref[...])
    kout_ref[0] = k
    kbuf[PREV:PREV + tt, :] = k.astype(BF16)
    v = _dot(hb, win_ref[:, a2:a3])
    vout_ref[0] = v
    vbuf[PREV:PREV + tt, :] = v.astype(BF16)

    u = _dot(hb, win_ref[:, a3:a4]) * _sigmoid_t(_dot(hb, win_ref[:, a4:a5]))
    uext[CONV_PAD:CONV_PAD + tt, :] = u
    cout_ref[0] = uext[CONV_PAD + tt - (CONV_WIDTH - 1):CONV_PAD + tt, :]

    rc = min(cfg.qb, 64)
    cc = min(u.shape[1], 256)
    shifts = [j + CONV_PAD - (CONV_WIDTH - 1) for j in range(CONV_WIDTH)]

    def conv_unit(r0, c0):
        acc = jnp.broadcast_to(cb_ref[:, c0:c0 + cc], (rc, cc))
        for res in range(SUBLANES):
            group = [s for s in shifts if s % SUBLANES == res]
            if not group:
                continue
            lo, hi = min(group) - res, max(group) - res
            rows = hi - lo + rc + (SUBLANES if res else 0)
            slab = uext[pl.ds(r0 + lo, rows), c0:c0 + cc]
            if res:
                slab = pltpu.roll(slab, rows - res, 0)
            for s in group:
                j = s - shifts[0]
                a = s - res - lo
                acc = acc + cw_ref[j:j + 1, c0:c0 + cc] * slab[a:a + rc, :]
        cvbuf[pl.ds(r0, rc), c0:c0 + cc] = acc

    def mix_block(r0, k0):
        units = [(r0 + i * rc, c0) for i in range(cfg.qb // rc) for c0 in range(0, u.shape[1], cc)]
        if not cfg.has_cache:
            col = lax.broadcasted_iota(I32, (1, cfg.kb), 1)
            valid = jnp.logical_or(col >= PREV - k0, t > 0)
        heads = []
        for h in range(N_HEADS):
            c0, c1 = h * HEAD_DIM, (h + 1) * HEAD_DIM
            s = lax.dot_general(qbuf[pl.ds(r0, cfg.qb), c0:c1], kbuf[pl.ds(k0, cfg.kb), c0:c1],
                                (((1,), (1,)), ((), ())), preferred_element_type=F32)
            s = s + bias_ref[h]
            if not cfg.has_cache:
                s = jnp.where(valid, s, NEG)
            p = jnp.exp(s - jnp.max(s, axis=-1, keepdims=True))
            l = jnp.sum(p, axis=-1, keepdims=True)
            heads.append(_dot(p.astype(BF16), vbuf[pl.ds(k0, cfg.kb), c0:c1]) * (1.0 / l))
            for unit in units[h::N_HEADS]:
                conv_unit(*unit)
        obuf[pl.ds(r0, cfg.qb), :] = jnp.concatenate(heads, axis=1).astype(BF16)

    if cfg.n_qblk == 1:
        mix_block(0, 0)
    else:
        def mix_block_step(j, carry):
            mix_block(pl.multiple_of(j * cfg.qb, cfg.qb), pl.multiple_of(j * cfg.kstride, cfg.kstride))
            return carry
        lax.fori_loop(0, cfg.n_qblk, mix_block_step, 0)
    y_attn = _dot(obuf[...], wao_ref[...])
    if not cfg.has_cache:
        kbuf[0:PREV, :] = kbuf[tt:tt + PREV, :]
        vbuf[0:PREV, :] = vbuf[tt:tt + PREV, :]

    if not cfg.has_cache:
        uext[0:CONV_PAD, :] = uext[tt:tt + CONV_PAD, :]
    cv = cvbuf[...]
    mu = jnp.mean(cv, axis=-1, keepdims=True)
    xc = cv - mu
    var = jnp.mean(xc * xc, axis=-1, keepdims=True)
    cv = xc * lax.rsqrt(var + EPS) * lng_ref[...] + lnb_ref[...]
    cv = cv * _sigmoid_t(cv)
    y_conv = _dot(cv.astype(BF16), wco_ref[...]) + bco_ref[...]

    g_attn = _sigmoid_t(_dot(hb, win_ref[:, a5:a5 + d_model]) + bg_ref[:, 0:d_model])
    g_conv = _sigmoid_t(_dot(hb, win_ref[:, a5 + d_model:a5 + 2 * d_model]) + bg_ref[:, d_model:2 * d_model])
    m = g_attn * y_attn + g_conv * y_conv
    x1 = x + g1 * _dot(m.astype(BF16), wout_ref[...])
    x1_ref[0] = x1

    h2 = _rms(x1, n2g_ref[...]) * (1.0 + sc2) + sh2
    h2_ref[0] = _pack(h2[:, 0:d_model // 2], h2[:, d_model // 2:d_model])
    h2_hi, h2_lo = _split_bf16(h2)
    nt_dims = (((1,), (1,)), ((), ()))
    logits = (lax.dot_general(wrh_ref[...], h2_hi, nt_dims, preferred_element_type=F32)
              + lax.dot_general(wrl_ref[...], h2_hi, nt_dims, preferred_element_type=F32)
              + lax.dot_general(wrh_ref[...], h2_lo, nt_dims, preferred_element_type=F32))
    s = _sigmoid(logits)
    eids, weights = _route(s + rb_ref[...], s)
    ids = _expert_ids(tt)
    if cfg.sparse:
        sel = [functools.reduce(jnp.add, [jnp.where(ids[g] == e, 1.0, 0.0) for e in eids]) for g in range(N_GROUPS)]
        sel = jnp.concatenate(sel, axis=0)
        rank = _dot(sel.astype(BF16), tri_ref[...])
        rank_slabs = [rank[g * GROUP_SIZE:(g + 1) * GROUP_SIZE, :] for g in range(N_GROUPS)]
        tok_ref[0] = _lane_dense_rows(weights, tt)
        eid_ref[...] = jnp.concatenate(eids, axis=0)
        rank_ref[...] = jnp.concatenate([_pick(ids, e, rank_slabs) for e in eids], axis=0)
        step = pl.program_id(0) * pl.num_programs(1) + t

        @pl.when(step == 0)
        def _():
            cnt_ref[...] = jnp.zeros(cnt_ref.shape, F32)

        lane = lax.broadcasted_iota(I32, cnt_ref.shape, 1)
        cnt_ref[...] = jnp.where(lane == step, jnp.sum(sel, axis=1, keepdims=True), cnt_ref[...])
    else:
        gates = [functools.reduce(jnp.add, [jnp.where(ids[g] == e, w, 0.0) for e, w in zip(eids, weights)])
                 for g in range(N_GROUPS)]
        gates = jnp.concatenate(gates + [jnp.zeros((LANES - N_EXPERTS, tt), F32)], axis=0)
        tok_ref[0] = gates.T


def _toeplitz(v, rows, cols):
    w = rows + cols
    flat = jnp.tile(v, (1, rows))[:, :rows * (w - 1)]
    return flat.reshape(v.shape[0], rows, w - 1)[:, :, :cols]


def _rel_bias_blocks(table, cfg, q_pos0, k_pos0):
    q_pos = q_pos0 + jnp.arange(cfg.qb)
    k_pos = k_pos0 + jnp.arange(cfg.kb)
    qc = q_pos // CHUNK
    kc = k_pos // CHUNK
    allowed = (kc[None, :] <= qc[:, None]) & (kc[None, :] >= qc[:, None] - BAND_CHUNKS)
    w = cfg.qb + cfg.kb
    dj = jnp.arange(w)
    dj = jnp.where(dj < cfg.kb, dj, dj - w)
    rel = jnp.clip(q_pos0 - k_pos0 - dj, -MAX_REL, MAX_REL) + MAX_REL
    bias = _toeplitz(table[:, rel].astype(F32), cfg.qb, cfg.kb)
    return jnp.where(allowed[None], bias, NEG)


def _mixer_call(cfg, x, mod, p, cache=None, batch0=0, nbatch=None, state=None):
    b_all, t_total, d = x.shape
    nb = b_all if nbatch is None else nbatch
    nt = t_total // cfg.tt
    conv_dim = p["conv_w"].shape[1]
    n_in = p["w_in"].shape[1]

    def const(shape):
        return pl.BlockSpec(shape, lambda i, j: (0,) * len(shape), pipeline_mode=pl.Buffered(1))

    def per_row(shape):
        return pl.BlockSpec(shape, lambda i, j: (i + batch0, 0, 0))

    in_specs = [
        pl.BlockSpec((1, cfg.tt, d), lambda i, j: (i + batch0, j, 0)),
        per_row((1, 6, d)),
        const((1, d)), const((1, d)), const((d, n_in)),
        const((1, ATTN_DIM)), const((1, ATTN_DIM)), const((ATTN_DIM, ATTN_DIM)),
        const((N_HEADS, cfg.qb, cfg.kb)), const((ATTN_DIM, d)),
        const((CONV_WIDTH, conv_dim)), const((1, conv_dim)), const((1, conv_dim)), const((1, conv_dim)),
        const((conv_dim, d)), const((1, d)), const((1, 2 * d)), const((d, d)),
        const((N_EXPERTS, d)), const((N_EXPERTS, d)), const((N_EXPERTS, 1)),
    ]
    args = [x, mod, p["norm1_g"], p["norm2_g"], p["w_in"], p["q_norm_g"], p["k_norm_g"], p["hsum"],
            p["bias"], p["w_attn_o"], p["conv_w"], p["conv_b"], p["conv_ln_g"], p["conv_ln_b"],
            p["w_conv_o"], p["b_conv_o"], p["b_gate"], p["w_out"], p["wr_hi"], p["wr_lo"], p["router_bias"]]
    if cfg.has_cache:
        in_specs += [per_row((1, PREV, ATTN_DIM)), per_row((1, PREV, ATTN_DIM)),
                     per_row((1, CONV_WIDTH - 1, conv_dim))]
        args += list(cache)
    if cfg.sparse:
        assert nb * nt <= MAX_TILES
        in_specs += [const((cfg.tt, cfg.tt))]
        tok = jnp.arange(cfg.tt)
        args += [(tok[:, None] < tok[None, :]).astype(BF16)]
    aliases = {}
    if state is not None:
        for i, arr in enumerate(state):
            aliases[len(args)] = 3 + i
            in_specs.append(pl.BlockSpec(memory_space=pl.ANY))
            args.append(arr)
    keep = min(PREV, t_total)
    assert keep == cfg.tt
    out_shape = [jax.ShapeDtypeStruct((nb, t_total, d), F32),
                 jax.ShapeDtypeStruct((nb, t_total, d // 2), I32),
                 jax.ShapeDtypeStruct((nb, t_total, LANES), F32),
                 jax.ShapeDtypeStruct((b_all, keep, ATTN_DIM), F32),
                 jax.ShapeDtypeStruct((b_all, keep, ATTN_DIM), F32),
                 jax.ShapeDtypeStruct((b_all, CONV_WIDTH - 1, conv_dim), F32)]
    out_specs = [pl.BlockSpec((1, cfg.tt, d), lambda i, j: (i, j, 0)),
                 pl.BlockSpec((1, cfg.tt, d // 2), lambda i, j: (i, j, 0)),
                 pl.BlockSpec((1, cfg.tt, LANES), lambda i, j: (i, j, 0)),
                 per_row((1, keep, ATTN_DIM)), per_row((1, keep, ATTN_DIM)),
                 per_row((1, CONV_WIDTH - 1, conv_dim))]
    if cfg.sparse:
        out_shape += [jax.ShapeDtypeStruct((TOP_K, nb * t_total), F32),
                      jax.ShapeDtypeStruct((TOP_K, nb * t_total), F32),
                      jax.ShapeDtypeStruct((N_EXPERTS, MAX_TILES), F32)]
        out_specs += [pl.BlockSpec((TOP_K, cfg.tt), lambda i, j: (0, i * nt + j)),
                      pl.BlockSpec((TOP_K, cfg.tt), lambda i, j: (0, i * nt + j)),
                      pl.BlockSpec((N_EXPERTS, MAX_TILES), lambda i, j: (0, 0))]
    scratch = [pltpu.VMEM((cfg.tt, ATTN_DIM), BF16),
               pltpu.VMEM((PREV + cfg.tt, ATTN_DIM), BF16),
               pltpu.VMEM((PREV + cfg.tt, ATTN_DIM), BF16),
               pltpu.VMEM((cfg.tt, ATTN_DIM), BF16),
               pltpu.VMEM((CONV_PAD + cfg.tt, conv_dim), F32),
               pltpu.VMEM((cfg.tt, conv_dim), F32)]
    return pl.pallas_call(
        functools.partial(_mixer_kernel, cfg, 0 if state is None else len(state)),
        out_shape=out_shape,
        grid=(nb, nt),
        in_specs=in_specs,
        out_specs=out_specs,
        scratch_shapes=scratch,
        input_output_aliases=aliases,
        compiler_params=pltpu.CompilerParams(dimension_semantics=("arbitrary", "arbitrary"),
                                             vmem_limit_bytes=VMEM_LIMIT_BYTES),
        name="mixer_sample" if cfg.has_cache else "mixer_prompt",
    )(*args)


def _exact_parts(x, n):
    parts = []
    for _ in range(n):
        part = x.astype(BF16)
        parts.append(part)
        x = x - part.astype(F32)
    return parts


def _plan_kernel(tt, cnt_ref, eid_ref, rank_ref, tri_e_ref, tri_t_ref, pos_ref, blk_ref, nblk_ref, base_ref):
    step = pl.program_id(0)

    @pl.when(step == 0)
    def _():
        cnt = cnt_ref[...]
        total = jnp.sum(cnt, axis=1, keepdims=True)
        padded = jnp.floor((total + float(ROW_BLOCK - 1)) * (1.0 / ROW_BLOCK)) * float(ROW_BLOCK)
        padded = jnp.broadcast_to(padded, cnt.shape)
        start = functools.reduce(jnp.add, [_dot(tri_e_ref[...], part) for part in _exact_parts(padded, 3)])
        before = functools.reduce(jnp.add, [_dot(part, tri_t_ref[...]) for part in _exact_parts(cnt, 2)])
        base_ref[...] = start + before
        end = (start + padded)[:, 0:1]
        first_row = lax.broadcasted_iota(I32, blk_ref.shape, 1).astype(F32) * float(ROW_BLOCK)
        owner = jnp.sum(jnp.where(end <= first_row, 1.0, 0.0), axis=0, keepdims=True)
        blk_ref[...] = jnp.minimum(owner, float(N_EXPERTS - 1)).astype(I32)
        nblk = jnp.max(end, axis=0, keepdims=True) * (1.0 / ROW_BLOCK)
        nblk_ref[...] = jnp.broadcast_to(nblk, nblk_ref.shape).astype(I32)

    lane = lax.broadcasted_iota(I32, base_ref.shape, 1)
    col = jnp.sum(jnp.where(lane == step, base_ref[...], 0.0), axis=1, keepdims=True)
    ids = _expert_ids(tt)
    col_slabs = [jnp.broadcast_to(col[g * GROUP_SIZE:(g + 1) * GROUP_SIZE, :], (GROUP_SIZE, tt))
                 for g in range(N_GROUPS)]
    eid = eid_ref[...]
    rows = [_pick(ids, eid[k:k + 1, :], col_slabs) for k in range(TOP_K)]
    pos_ref[...] = (jnp.concatenate(rows, axis=0) + rank_ref[...]).astype(I32)


def _plan_call(cnt, eid, rank, tt, n_blocks):
    n = eid.shape[1]
    e = jnp.arange(N_EXPERTS)
    s = jnp.arange(MAX_TILES)
    nb_pad = -(-n_blocks // LANES) * LANES
    return pl.pallas_call(
        functools.partial(_plan_kernel, tt),
        out_shape=[jax.ShapeDtypeStruct((TOP_K, n), I32),
                   jax.ShapeDtypeStruct((1, nb_pad), I32),
                   jax.ShapeDtypeStruct((1, LANES), I32)],
        grid=(n // tt,),
        in_specs=[pl.BlockSpec((N_EXPERTS, MAX_TILES), lambda i: (0, 0)),
                  pl.BlockSpec((TOP_K, tt), lambda i: (0, i)),
                  pl.BlockSpec((TOP_K, tt), lambda i: (0, i)),
                  pl.BlockSpec((N_EXPERTS, N_EXPERTS), lambda i: (0, 0)),
                  pl.BlockSpec((MAX_TILES, MAX_TILES), lambda i: (0, 0))],
        out_specs=[pl.BlockSpec((TOP_K, tt), lambda i: (0, i)),
                   pl.BlockSpec((1, nb_pad), lambda i: (0, 0)),
                   pl.BlockSpec((1, LANES), lambda i: (0, 0))],
        scratch_shapes=[pltpu.VMEM((N_EXPERTS, MAX_TILES), F32)],
        compiler_params=pltpu.CompilerParams(dimension_semantics=("arbitrary",),
                                             vmem_limit_bytes=VMEM_LIMIT_BYTES),
        name="plan",
    )(cnt, eid, rank, (e[None, :] < e[:, None]).astype(BF16), (s[:, None] < s[None, :]).astype(BF16))


def _sc_mesh():
    return plsc.VectorSubcoreMesh(core_axis_name="c", subcore_axis_name="s")


def _dispatch_call(rows, pos, n_sorted):
    n, w = rows.shape
    per_worker = n // SC_WORKERS
    nsteps = per_worker // SC_ROWS
    assert per_worker * SC_WORKERS == n and nsteps * SC_ROWS == per_worker and nsteps % 2 == 0

    @functools.partial(
        pl.kernel, mesh=_sc_mesh(),
        out_type=jax.ShapeDtypeStruct((n_sorted, w), rows.dtype),
        scratch_types=[pltpu.VMEM((2, TOP_K, SC_ROWS), I32),
                       pltpu.VMEM((2, SC_ROWS, w), rows.dtype),
                       pltpu.SemaphoreType.DMA((2,)),
                       pltpu.SemaphoreType.DMA((2,))],
    )
    def dispatch(x_hbm, pos_hbm, out_hbm, idx_v, rows_v, load_sem, scat_sem):
        base = (lax.axis_index("s") * SC_CORES + lax.axis_index("c")) * per_worker

        def load(i, b):
            return pltpu.make_async_copy(x_hbm.at[pl.ds(base + i * SC_ROWS, SC_ROWS)], rows_v.at[b], load_sem.at[b])

        def scatter(b, k):
            return pltpu.make_async_copy(rows_v.at[b], out_hbm.at[idx_v.at[b, k]], scat_sem.at[b])

        def load_start(i, b):
            for k in range(TOP_K):
                pltpu.sync_copy(pos_hbm.at[pl.ds(k * n + base + i * SC_ROWS, SC_ROWS)], idx_v.at[b, k])
            load(i, b).start()

        load_start(0, 0)

        @pl.loop(0, nsteps, step=2)
        def _(i):
            for b in range(2):
                ii = i + b

                @pl.when(ii >= 1)
                def _():
                    for k in range(TOP_K):
                        scatter(1 - b, k).wait()

                @pl.when(ii + 1 < nsteps)
                def _():
                    load_start(ii + 1, 1 - b)

                load(ii, b).wait()
                for k in range(TOP_K):
                    scatter(b, k).start()

        for k in range(TOP_K):
            scatter((nsteps - 1) % 2, k).wait()

    return dispatch(rows, pos)


def _collect_call(table, idx):
    n = idx.shape[0]
    w = table.shape[1]
    per_worker = n // SC_WORKERS
    nsteps = per_worker // SC_ROWS
    assert per_worker * SC_WORKERS == n and nsteps * SC_ROWS == per_worker and nsteps % 2 == 0

    @functools.partial(
        pl.kernel, mesh=_sc_mesh(),
        out_type=jax.ShapeDtypeStruct((n, w), table.dtype),
        scratch_types=[pltpu.VMEM((2, SC_ROWS), I32),
                       pltpu.VMEM((2, SC_ROWS, w), table.dtype),
                       pltpu.SemaphoreType.DMA((2,)),
                       pltpu.SemaphoreType.DMA((2,))],
    )
    def collect(table_hbm, idx_hbm, out_hbm, idx_v, rows_v, gather_sem, write_sem):
        base = (lax.axis_index("s") * SC_CORES + lax.axis_index("c")) * per_worker

        def gather(b):
            return pltpu.make_async_copy(table_hbm.at[idx_v.at[b]], rows_v.at[b], gather_sem.at[b])

        def write(i, b):
            return pltpu.make_async_copy(rows_v.at[b], out_hbm.at[pl.ds(base + i * SC_ROWS, SC_ROWS)], write_sem.at[b])

        def gather_start(i, b):
            pltpu.sync_copy(idx_hbm.at[pl.ds(base + i * SC_ROWS, SC_ROWS)], idx_v.at[b])
            gather(b).start()

        gather_start(0, 0)

        @pl.loop(0, nsteps, step=2)
        def _(i):
            for b in range(2):
                ii = i + b

                @pl.when(ii >= 1)
                def _():
                    write(ii - 1, 1 - b).wait()

                @pl.when(ii + 1 < nsteps)
                def _():
                    gather_start(ii + 1, 1 - b)

                gather(b).wait()
                write(ii, b).start()

        write(nsteps - 1, (nsteps - 1) % 2).wait()

    return collect(table, idx)


def _expert_ffn_kernel(blk_ref, nblk_ref, x_ref, wg_ref, wu_ref, wd_ref, o_ref, wg_s, wu_s, wd_s):
    j = pl.program_id(0)

    @pl.when(jnp.logical_or(j == 0, blk_ref[j] != blk_ref[jnp.maximum(j - 1, 0)]))
    def _():
        wg_s[...] = wg_ref[0].astype(BF16)
        wu_s[...] = wu_ref[0].astype(BF16)
        wd_s[...] = wd_ref[0].astype(BF16)

    @pl.when(j < nblk_ref[0])
    def _():
        a, b = _unpack(x_ref[...])
        a, b = a.astype(BF16), b.astype(BF16)
        hg = _dot_halves(a, b, wg_s)
        act = hg * _sigmoid_t(hg) * _dot_halves(a, b, wu_s)
        y = _dot(act.astype(BF16), wd_s[...])
        half = y.shape[1] // 2
        o_ref[...] = _pack(y[:, 0:half], y[:, half:2 * half])


def _expert_ffn_call(xs, blk, nblk, p):
    n_sorted, w = xs.shape
    d = 2 * w
    ff = p["w_e_gate"].shape[2]

    def rows(j, blk, nblk):
        return (jnp.minimum(j, nblk[0] - 1), 0)

    def expert(j, blk, nblk):
        return (blk[j], 0, 0)

    return pl.pallas_call(
        _expert_ffn_kernel,
        out_shape=jax.ShapeDtypeStruct((n_sorted, w), I32),
        grid_spec=pltpu.PrefetchScalarGridSpec(
            num_scalar_prefetch=2,
            grid=(n_sorted // ROW_BLOCK,),
            in_specs=[pl.BlockSpec((ROW_BLOCK, w), rows),
                      pl.BlockSpec((1, d, ff), expert),
                      pl.BlockSpec((1, d, ff), expert),
                      pl.BlockSpec((1, ff, d), expert)],
            out_specs=pl.BlockSpec((ROW_BLOCK, w), rows),
            scratch_shapes=[pltpu.VMEM((d, ff), BF16), pltpu.VMEM((d, ff), BF16), pltpu.VMEM((ff, d), BF16)]),
        compiler_params=pltpu.CompilerParams(dimension_semantics=("arbitrary",),
                                             vmem_limit_bytes=VMEM_LIMIT_BYTES),
        name="expert_ffn",
    )(blk, nblk, xs, p["w_e_gate"], p["w_e_up"], p["w_e_down"])


def _swiglu_halves(a, b, wg_ref, wu_ref):
    hg = _dot_halves(a, b, wg_ref)
    return hg * _sigmoid_t(hg) * _dot_halves(a, b, wu_ref)


def _moe_out_kernel(n_groups, tiles, *refs):
    groups = [refs[4 * q:4 * q + 4] for q in range(n_groups)]
    g2_ref, wsg_ref, wsu_ref, wsd_ref, o_ref = refs[4 * n_groups:]

    def combine(x1_ref, h_ref, y_ref, w_ref):
        a, b = _unpack(h_ref[...])
        shared = _dot(_swiglu_halves(a.astype(BF16), b.astype(BF16), wsg_ref, wsu_ref).astype(BF16),
                      wsd_ref[...].astype(BF16))
        half = h_ref.shape[1]
        w = w_ref[...]
        acc_a, acc_b = shared[:, 0:half], shared[:, half:2 * half]
        for k in range(TOP_K):
            ya, yb = _unpack(y_ref[k])
            acc_a = acc_a + w[:, k:k + 1] * ya
            acc_b = acc_b + w[:, k:k + 1] * yb
        g2 = g2_ref[...]
        o_ref[:, 0:half] = x1_ref[:, 0:half] + g2[:, 0:half] * acc_a
        o_ref[:, half:2 * half] = x1_ref[:, half:2 * half] + g2[:, half:2 * half] * acc_b

    for q in range(n_groups):
        pl.when(pl.program_id(0) // tiles == q)(functools.partial(combine, *groups[q]))


def _moe_out_call(groups, g2, p, tm):
    n, d = groups[0][0].shape
    n_all = n * len(groups)
    sff = p["w_s_gate"].shape[1]
    per_row = n_all // g2.shape[0]
    g2 = g2.reshape(g2.shape[0], 1, d)
    tiles = n // tm

    def const(shape):
        return pl.BlockSpec(shape, lambda i: (0,) * len(shape), pipeline_mode=pl.Buffered(1))

    in_specs, args = [], []
    for q, group in enumerate(groups):
        tile = lambda i, q=q: jnp.clip(i - q * tiles, 0, tiles - 1)
        in_specs += [pl.BlockSpec((tm, d), lambda i, tile=tile: (tile(i), 0)),
                     pl.BlockSpec((tm, d // 2), lambda i, tile=tile: (tile(i), 0)),
                     pl.BlockSpec((TOP_K, tm, d // 2), lambda i, tile=tile: (0, tile(i), 0)),
                     pl.BlockSpec((tm, LANES), lambda i, tile=tile: (tile(i), 0))]
        args += list(group)
    in_specs += [pl.BlockSpec((None, 1, d), lambda i: (i * tm // per_row, 0, 0)),
                 const((d, sff)), const((d, sff)), const((sff, d))]
    args += [g2, p["w_s_gate"], p["w_s_up"], p["w_s_down"]]
    return pl.pallas_call(
        functools.partial(_moe_out_kernel, len(groups), tiles),
        out_shape=jax.ShapeDtypeStruct((n_all, d), F32),
        grid=(n_all // tm,),
        in_specs=in_specs,
        out_specs=pl.BlockSpec((tm, d), lambda i: (i, 0)),
        compiler_params=pltpu.CompilerParams(dimension_semantics=("arbitrary",),
                                             vmem_limit_bytes=VMEM_LIMIT_BYTES),
        name="moe_out",
    )(*args)


def _moe_dense_kernel(h_ref, gates_ref, x1_ref, g2_ref, wsg_ref, wsu_ref, wsd_ref, wg_ref, wu_ref, wd_ref,
                      o_ref, acc_ref):
    e = pl.program_id(1)
    a, b = _unpack(h_ref[...])
    a, b = a.astype(BF16), b.astype(BF16)

    @pl.when(e == 0)
    def _():
        acc_ref[...] = _dot(_swiglu_halves(a, b, wsg_ref, wsu_ref).astype(BF16), wsd_ref[...].astype(BF16))

    gates = gates_ref[...]
    lane = lax.broadcasted_iota(I32, gates.shape, 1)
    gate = jnp.sum(jnp.where(lane == e, gates, 0.0), axis=1, keepdims=True)
    act = _swiglu_halves(a, b, wg_ref.at[0], wu_ref.at[0]) * gate
    acc_ref[...] += _dot(act.astype(BF16), wd_ref[0].astype(BF16))

    @pl.when(e == pl.num_programs(1) - 1)
    def _():
        o_ref[...] = x1_ref[...] + g2_ref[...] * acc_ref[...]


def _moe_dense_call(h2, gates, x1, g2_rows, p):
    n, d = x1.shape
    ff = p["w_e_gate"].shape[2]
    sff = p["w_s_gate"].shape[1]

    def const(shape):
        return pl.BlockSpec(shape, lambda i, e: (0,) * len(shape), pipeline_mode=pl.Buffered(1))

    return pl.pallas_call(
        _moe_dense_kernel,
        out_shape=jax.ShapeDtypeStruct((n, d), F32),
        grid=(1, N_EXPERTS),
        in_specs=[const((n, d // 2)), const((n, LANES)), const((n, d)), const((n, d)),
                  const((d, sff)), const((d, sff)), const((sff, d)),
                  pl.BlockSpec((1, d, ff), lambda i, e: (e, 0, 0)),
                  pl.BlockSpec((1, d, ff), lambda i, e: (e, 0, 0)),
                  pl.BlockSpec((1, ff, d), lambda i, e: (e, 0, 0))],
        out_specs=pl.BlockSpec((n, d), lambda i, e: (0, 0)),
        scratch_shapes=[pltpu.VMEM((n, d), F32)],
        compiler_params=pltpu.CompilerParams(dimension_semantics=("arbitrary", "arbitrary"),
                                             vmem_limit_bytes=VMEM_LIMIT_BYTES),
        name="moe_dense",
    )(h2, gates, x1, g2_rows, p["w_s_gate"], p["w_s_up"], p["w_s_down"], p["w_e_gate"], p["w_e_up"], p["w_e_down"])


PROMPT_CFG = MixerCfg(tt=512, qb=256, kb=768, n_qblk=2, kstride=256, has_cache=False, sparse=True)
MOE_OUT_TILE = 256
PROMPT_GROUPS = 2


def _layer_params(l, w_in, q_norm_g, k_norm_g, w_attn_o, conv_w, conv_b, conv_ln_g, conv_ln_b, w_conv_o,
                  b_conv_o, b_gate, w_out, norm1_g, norm2_g, w_router, router_bias,
                  w_e_gate, w_e_up, w_e_down, w_s_gate, w_s_up, w_s_down):
    row = lambda a: a[l].reshape(1, -1)
    wr_t = w_router[l].T
    wr_hi = wr_t.astype(BF16)
    head = jnp.arange(ATTN_DIM) // HEAD_DIM
    return {
        "norm1_g": row(norm1_g), "norm2_g": row(norm2_g), "w_in": w_in[l].astype(BF16),
        "q_norm_g": jnp.tile(q_norm_g[l], N_HEADS).reshape(1, -1),
        "k_norm_g": jnp.tile(k_norm_g[l], N_HEADS).reshape(1, -1),
        "hsum": (head[:, None] == head[None, :]).astype(BF16),
        "w_attn_o": w_attn_o[l].astype(BF16), "conv_w": conv_w[l], "conv_b": row(conv_b),
        "conv_ln_g": row(conv_ln_g), "conv_ln_b": row(conv_ln_b), "w_conv_o": w_conv_o[l].astype(BF16),
        "b_conv_o": row(b_conv_o), "b_gate": row(b_gate), "w_out": w_out[l].astype(BF16),
        "wr_hi": wr_hi, "wr_lo": (wr_t - wr_hi.astype(F32)).astype(BF16),
        "router_bias": router_bias[l].reshape(-1, 1),
        "w_e_gate": w_e_gate[l], "w_e_up": w_e_up[l], "w_e_down": w_e_down[l],
        "w_s_gate": w_s_gate[l], "w_s_up": w_s_up[l], "w_s_down": w_s_down[l],
    }


def _sorted_rows(h2, eid, rank, cnt, tt):
    n = h2.shape[0]
    n_sorted = n * TOP_K + N_EXPERTS * ROW_BLOCK
    pos, blk, nblk = _plan_call(cnt, eid, rank, tt, n_sorted // ROW_BLOCK)
    pos = pos.reshape(TOP_K * n)
    return _dispatch_call(h2, pos, n_sorted), pos, blk.reshape(-1), nblk.reshape(-1)


def _expert_rows(xs, pos, blk, nblk, p):
    ys = _expert_ffn_call(xs, blk, nblk, p)
    return _collect_call(ys, pos).reshape(TOP_K, pos.shape[0] // TOP_K, xs.shape[1])


def kernel(x_prompt, x_sample, c_prompt, c_sample, cache_k, cache_v, state_conv, w_mod, b_mod, norm1_g, w_in, q_norm_g, k_norm_g, rel_bias, w_attn_o, conv_w, conv_b, conv_ln_g, conv_ln_b, w_conv_o, b_conv_o, b_gate, w_out, norm2_g, w_router, router_bias, w_e_gate, w_e_up, w_e_down, w_s_gate, w_s_up, w_s_down):
    depth = w_mod.shape[0]
    bp, tp, d = x_prompt.shape
    bs, ts, _ = x_sample.shape
    assert cache_k.shape[2] == min(PREV, PAST_LEN) == PREV
    sample_cfg = MixerCfg(tt=ts, qb=ts, kb=PREV + ts, n_qblk=1, kstride=0, has_cache=True, sparse=False)

    yp, ys = x_prompt, x_sample
    outs = [[] for _ in range(6)]
    for l in range(depth):
        p = _layer_params(l, w_in, q_norm_g, k_norm_g, w_attn_o, conv_w, conv_b, conv_ln_g, conv_ln_b,
                          w_conv_o, b_conv_o, b_gate, w_out, norm1_g, norm2_g, w_router, router_bias,
                          w_e_gate, w_e_up, w_e_down, w_s_gate, w_s_up, w_s_down)
        mod = _mod_call(jnp.concatenate([c_prompt, c_sample], axis=0), w_mod[l], b_mod[l])
        mod = mod.reshape(bp + bs, 6, d)
        mod_p, mod_s = mod[:bp], mod[bp:]

        p["bias"] = _rel_bias_blocks(rel_bias[l], PROMPT_CFG, PREV, 0)
        gb = bp // PROMPT_GROUPS
        keep = min(PREV, tp)
        state = [jnp.zeros((bp, keep, ATTN_DIM), F32), jnp.zeros((bp, keep, ATTN_DIM), F32),
                 jnp.zeros((bp, CONV_WIDTH - 1, conv_w.shape[2]), F32)]
        mixed, moved = [], []
        for g in range(PROMPT_GROUPS):
            x1, h2, w_tok, *state, eid, rank, cnt = _mixer_call(PROMPT_CFG, yp, mod_p, p, batch0=g * gb, nbatch=gb,
                                                                state=state)
            x1, h2, w_tok = x1.reshape(gb * tp, d), h2.reshape(gb * tp, d // 2), w_tok.reshape(gb * tp, LANES)
            mixed.append((x1, h2, w_tok))
            moved.append(_sorted_rows(h2, eid, rank, cnt, PROMPT_CFG.tt))
            state, _ = lax.optimization_barrier((state, moved[-1][1]))
        kp, vp, cp = state
        groups = [(x1, h2, _expert_rows(*rows, p), w_tok) for (x1, h2, w_tok), rows in zip(mixed, moved)]
        yp = _moe_out_call(groups, mod_p[:, 5, :], p, MOE_OUT_TILE).reshape(bp, tp, d)
        outs[0].append(kp.reshape(bp, -1, N_HEADS, HEAD_DIM))
        outs[1].append(vp.reshape(bp, -1, N_HEADS, HEAD_DIM))
        outs[2].append(cp)

        p["bias"] = _rel_bias_blocks(rel_bias[l], sample_cfg, PAST_LEN, PAST_LEN - PREV)
        cache = (cache_k[l].reshape(bs, PREV, ATTN_DIM), cache_v[l].reshape(bs, PREV, ATTN_DIM), state_conv[l])
        x1, h2, gates, ks, vs, cs = _mixer_call(sample_cfg, ys, mod_s, p, cache)
        ys = _moe_dense_call(h2.reshape(bs * ts, d // 2), gates.reshape(bs * ts, LANES), x1.reshape(bs * ts, d),
                             jnp.repeat(mod_s[:, 5, :], ts, axis=0), p).reshape(bs, ts, d)
        outs[3].append(ks.reshape(bs, ts, N_HEADS, HEAD_DIM))
        outs[4].append(vs.reshape(bs, ts, N_HEADS, HEAD_DIM))
        outs[5].append(cs)
    return (yp, ys) + tuple(jnp.stack(o) for o in outs)
```

```python
import dataclasses
import functools

import jax
import jax.numpy as jnp
from jax import lax
from jax.experimental import pallas as pl
from jax.experimental.pallas import tpu as pltpu
from jax.experimental.pallas import tpu_sc as plsc

F32 = jnp.float32
BF16 = jnp.bfloat16
I32 = jnp.int32

CHUNK = 64
BAND_CHUNKS = 8
PREV = BAND_CHUNKS * CHUNK
PAST_LEN = 1024
N_HEADS = 8
HEAD_DIM = 64
ATTN_DIM = N_HEADS * HEAD_DIM
MAX_REL = 128
CONV_WIDTH = 31
CONV_PAD = 32
N_EXPERTS = 64
N_GROUPS = 8
GROUP_SIZE = N_EXPERTS // N_GROUPS
TOPK_GROUPS = 4
TOP_K = 8
ROUTE_SCALE = 2.5
EPS = 1e-6
LANES = 128
SUBLANES = 8
NEG = -1e30
LOG2_E = 1.4426950408889634

VMEM_LIMIT_BYTES = 60 * 1024 * 1024

SC_CORES = 2
SC_SUBCORES = 16
SC_WORKERS = SC_CORES * SC_SUBCORES
SC_ROWS = 64

MAX_TILES = LANES
ROW_BLOCK = 1024


def _sigmoid(x):
    return 1.0 / (1.0 + jnp.exp(-x))


def _sigmoid_t(x):
    return 0.5 * jnp.tanh(0.5 * x) + 0.5


def _split_bf16(x):
    hi = x.astype(BF16)
    lo = (x - hi.astype(F32)).astype(BF16)
    return hi, lo


def _dot(a, b):
    return jnp.dot(a, b, preferred_element_type=F32)


def _dot3(a, b):
    a_hi, a_lo = _split_bf16(a)
    b_hi, b_lo = _split_bf16(b)
    return _dot(a_hi, b_hi) + _dot(a_hi, b_lo) + _dot(a_lo, b_hi)


def _pack(a, b):
    ia = lax.bitcast_convert_type(a.astype(BF16).astype(F32), I32)
    ib = lax.bitcast_convert_type(b.astype(BF16).astype(F32), I32)
    return ia | lax.shift_right_logical(ib, 16)


def _unpack(p):
    a = lax.bitcast_convert_type(p & jnp.int32(-65536), F32)
    b = lax.bitcast_convert_type(lax.shift_left(p, 16), F32)
    return a, b


def _dot_halves(a, b, w_ref):
    half = a.shape[1]
    return _dot(a, w_ref[0:half, :].astype(BF16)) + _dot(b, w_ref[half:2 * half, :].astype(BF16))


def _mod_kernel(c_ref, w_ref, b_ref, o_ref):
    c = c_ref[...]
    o_ref[...] = _dot3(c * _sigmoid(c), w_ref[...]) + b_ref[...]


def _mod_call(c, w_mod, b_mod):
    n, d = c.shape
    dout = w_mod.shape[1]
    bn = 1536
    return pl.pallas_call(
        _mod_kernel,
        out_shape=jax.ShapeDtypeStruct((n, dout), F32),
        grid=(dout // bn,),
        in_specs=[pl.BlockSpec((n, d), lambda j: (0, 0)),
                  pl.BlockSpec((d, bn), lambda j: (0, j)),
                  pl.BlockSpec((1, bn), lambda j: (0, j))],
        out_specs=pl.BlockSpec((n, bn), lambda j: (0, j)),
        compiler_params=pltpu.CompilerParams(dimension_semantics=("arbitrary",),
                                             vmem_limit_bytes=VMEM_LIMIT_BYTES),
        name="mod",
    )(c, w_mod, b_mod.reshape(1, dout))


@dataclasses.dataclass(frozen=True)
class MixerCfg:
    tt: int
    qb: int
    kb: int
    n_qblk: int
    kstride: int
    has_cache: bool
    sparse: bool


def _rms(x, g):
    ms = jnp.mean(x * x, axis=-1, keepdims=True)
    return x * lax.rsqrt(ms + EPS) * g


def _head_rms(z, hsum, g):
    ss = _dot((z * z).astype(BF16), hsum)
    return z * lax.rsqrt(ss * (1.0 / HEAD_DIM) + EPS) * g


def _expert_ids(t):
    sub = lax.broadcasted_iota(I32, (GROUP_SIZE, t), 0).astype(F32)
    return [sub + float(g * GROUP_SIZE) for g in range(N_GROUPS)]


def _pick(ids, eid, slabs):
    acc = functools.reduce(jnp.add, [jnp.where(ids[g] == eid, slabs[g], 0.0) for g in range(N_GROUPS)])
    return jnp.sum(acc, axis=0, keepdims=True)


def _route(choice, s):
    t = choice.shape[1]
    sub = lax.broadcasted_iota(I32, (GROUP_SIZE, t), 0).astype(F32)
    slabs = [choice[g * GROUP_SIZE:(g + 1) * GROUP_SIZE, :] for g in range(N_GROUPS)]
    s_slabs = [s[g * GROUP_SIZE:(g + 1) * GROUP_SIZE, :] for g in range(N_GROUPS)]
    gscore = []
    for c in slabs:
        m1 = jnp.max(c, axis=0, keepdims=True)
        first = jnp.min(jnp.where(c == m1, sub, float(GROUP_SIZE)), axis=0, keepdims=True)
        m2 = jnp.max(jnp.where(sub == first, -jnp.inf, c), axis=0, keepdims=True)
        gscore.append(m1 + m2)
    gsel = [jnp.zeros((1, t), F32) for _ in range(N_GROUPS)]
    for _ in range(TOPK_GROUPS):
        m = functools.reduce(jnp.maximum, gscore)
        first = functools.reduce(jnp.minimum,
                                 [jnp.where(gscore[g] == m, float(g), float(N_GROUPS)) for g in range(N_GROUPS)])
        for g in range(N_GROUPS):
            hit = first == float(g)
            gsel[g] = jnp.where(hit, 1.0, gsel[g])
            gscore[g] = jnp.where(hit, -jnp.inf, gscore[g])
    masked = [jnp.where(gsel[g] > 0.5, slabs[g], -jnp.inf) for g in range(N_GROUPS)]
    ids = _expert_ids(t)
    eids, raw = [], []
    for _ in range(TOP_K):
        m = jnp.max(functools.reduce(jnp.maximum, masked), axis=0, keepdims=True)
        cand = [jnp.where(masked[g] == m, ids[g], float(N_EXPERTS)) for g in range(N_GROUPS)]
        first = jnp.min(functools.reduce(jnp.minimum, cand), axis=0, keepdims=True)
        eids.append(first)
        raw.append(_pick(ids, first, s_slabs))
        masked = [jnp.where(ids[g] == first, -jnp.inf, masked[g]) for g in range(N_GROUPS)]
    wsum = functools.reduce(jnp.add, raw)
    return eids, [r / wsum * ROUTE_SCALE for r in raw]


def _lane_dense_rows(rows, t):
    pad = jnp.zeros((LANES - len(rows), t), F32)
    return jnp.concatenate(rows + [pad], axis=0).T


def _mixer_kernel(cfg, n_state, *refs):
    (x_ref, mod_ref, n1g_ref, n2g_ref, win_ref, qg_ref, kg_ref, hsum_ref, bias_ref, wao_ref,
     cw_ref, cb_ref, lng_ref, lnb_ref, wco_ref, bco_ref, bg_ref, wout_ref,
     wrh_ref, wrl_ref, rb_ref) = refs[:21]
    refs = refs[21:]
    if cfg.has_cache:
        kc_ref, vc_ref, cs_ref = refs[:3]
        refs = refs[3:]
    if cfg.sparse:
        tri_ref = refs[0]
        refs = refs[1:]
    refs = refs[n_state:]
    x1_ref, h2_ref, tok_ref, kout_ref, vout_ref, cout_ref = refs[:6]
    refs = refs[6:]
    if cfg.sparse:
        eid_ref, rank_ref, cnt_ref = refs[:3]
        refs = refs[3:]
    qbuf, kbuf, vbuf, obuf, uext, cvbuf = refs

    tt = cfg.tt
    t = pl.program_id(1)

    if cfg.has_cache:
        kbuf[0:PREV, :] = kc_ref[0].astype(BF16)
        vbuf[0:PREV, :] = vc_ref[0].astype(BF16)
        uext[CONV_PAD - (CONV_WIDTH - 1):CONV_PAD, :] = cs_ref[0]
    else:
        @pl.when(t == 0)
        def _():
            kbuf[0:PREV, :] = jnp.zeros((PREV, ATTN_DIM), BF16)
            vbuf[0:PREV, :] = jnp.zeros((PREV, ATTN_DIM), BF16)
            uext[0:CONV_PAD, :] = jnp.zeros((CONV_PAD, uext.shape[1]), F32)

    mod = mod_ref[0]
    sh1, sc1, g1, sh2, sc2 = (mod[i:i + 1, :] for i in range(5))

    x = x_ref[0]
    hb = (_rms(x, n1g_ref[...]) * (1.0 + sc1) + sh1).astype(BF16)

    a0, a1, a2, a3, a4, a5 = (0, ATTN_DIM, 2 * ATTN_DIM, 3 * ATTN_DIM,
                              3 * ATTN_DIM + cw_ref.shape[1], 3 * ATTN_DIM + 2 * cw_ref.shape[1])
    d_model = x.shape[1]

    q = _head_rms(_dot(hb, win_ref[:, a0:a1]), hsum_ref[...], qg_ref[...])
    qbuf[...] = (q * (HEAD_DIM ** -0.5 * LOG2_E)).astype(BF16)
    k = _head_rms(_dot(hb, win_ref[:, a1:a2]), hsum_ref[...], kg_ref[...])
    kout_ref[0] = k
    kbuf[PREV:PREV + tt, :] = k.astype(BF16)
    v = _dot(hb, win_ref[:, a2:a3])
    vout_ref[0] = v
    vbuf[PREV:PREV + tt, :] = v.astype(BF16)

    u = _dot(hb, win_ref[:, a3:a4]) * _sigmoid_t(_dot(hb, win_ref[:, a4:a5]))
    uext[CONV_PAD:CONV_PAD + tt, :] = u
    cout_ref[0] = uext[CONV_PAD + tt - (CONV_WIDTH - 1):CONV_PAD + tt, :]

    rc = min(cfg.qb, 64)
    cc = min(u.shape[1], 256)
    shifts = [j + CONV_PAD - (CONV_WIDTH - 1) for j in range(CONV_WIDTH)]

    def conv_unit(r0, c0):
        acc = jnp.broadcast_to(cb_ref[:, c0:c0 + cc], (rc, cc))
        for res in range(SUBLANES):
            group = [s for s in shifts if s % SUBLANES == res]
            if not group:
                continue
            lo, hi = min(group) - res, max(group) - res
            rows = hi - lo + rc + (SUBLANES if res else 0)
            slab = uext[pl.ds(r0 + lo, rows), c0:c0 + cc]
            if res:
                slab = pltpu.roll(slab, rows - res, 0)
            for s in group:
                j = s - shifts[0]
                a = s - res - lo
                acc = acc + cw_ref[j:j + 1, c0:c0 + cc] * slab[a:a + rc, :]
        cvbuf[pl.ds(r0, rc), c0:c0 + cc] = acc

    def mix_block(r0, k0):
        units = [(r0 + i * rc, c0) for i in range(cfg.qb // rc) for c0 in range(0, u.shape[1], cc)]
        if not cfg.has_cache:
            col = lax.broadcasted_iota(I32, (1, cfg.kb), 1)
            valid = jnp.logical_or(col >= PREV - k0, t > 0)
        heads = []
        for h in range(N_HEADS):
            c0, c1 = h * HEAD_DIM, (h + 1) * HEAD_DIM
            s = lax.dot_general(qbuf[pl.ds(r0, cfg.qb), c0:c1], kbuf[pl.ds(k0, cfg.kb), c0:c1],
                                (((1,), (1,)), ((), ())), preferred_element_type=F32)
            s = s + bias_ref[h]
            if not cfg.has_cache:
                s = jnp.where(valid, s, NEG)
            p = jnp.exp2(s - jnp.max(s, axis=-1, keepdims=True))
            l = jnp.sum(p, axis=-1, keepdims=True)
            heads.append(_dot(p.astype(BF16), vbuf[pl.ds(k0, cfg.kb), c0:c1]) * (1.0 / l))
            for unit in units[h::N_HEADS]:
                conv_unit(*unit)
        obuf[pl.ds(r0, cfg.qb), :] = jnp.concatenate(heads, axis=1).astype(BF16)

    if cfg.n_qblk == 1:
        mix_block(0, 0)
    else:
        def mix_block_step(j, carry):
            mix_block(pl.multiple_of(j * cfg.qb, cfg.qb), pl.multiple_of(j * cfg.kstride, cfg.kstride))
            return carry
        lax.fori_loop(0, cfg.n_qblk, mix_block_step, 0)
    y_attn = _dot(obuf[...], wao_ref[...])
    if not cfg.has_cache:
        kbuf[0:PREV, :] = kbuf[tt:tt + PREV, :]
        vbuf[0:PREV, :] = vbuf[tt:tt + PREV, :]

    if not cfg.has_cache:
        uext[0:CONV_PAD, :] = uext[tt:tt + CONV_PAD, :]
    cv = cvbuf[...]
    mu = jnp.mean(cv, axis=-1, keepdims=True)
    xc = cv - mu
    var = jnp.mean(xc * xc, axis=-1, keepdims=True)
    cv = xc * lax.rsqrt(var + EPS) * lng_ref[...] + lnb_ref[...]
    cv = cv * _sigmoid_t(cv)
    y_conv = _dot(cv.astype(BF16), wco_ref[...]) + bco_ref[...]

    g_attn = _sigmoid_t(_dot(hb, win_ref[:, a5:a5 + d_model]) + bg_ref[:, 0:d_model])
    g_conv = _sigmoid_t(_dot(hb, win_ref[:, a5 + d_model:a5 + 2 * d_model]) + bg_ref[:, d_model:2 * d_model])
    m = g_attn * y_attn + g_conv * y_conv
    x1 = x + g1 * _dot(m.astype(BF16), wout_ref[...])
    x1_ref[0] = x1

    h2 = _rms(x1, n2g_ref[...]) * (1.0 + sc2) + sh2
    h2_ref[0] = _pack(h2[:, 0:d_model // 2], h2[:, d_model // 2:d_model])
    h2_hi, h2_lo = _split_bf16(h2)
    nt_dims = (((1,), (1,)), ((), ()))
    logits = (lax.dot_general(wrh_ref[...], h2_hi, nt_dims, preferred_element_type=F32)
              + lax.dot_general(wrl_ref[...], h2_hi, nt_dims, preferred_element_type=F32)
              + lax.dot_general(wrh_ref[...], h2_lo, nt_dims, preferred_element_type=F32))
    s = _sigmoid(logits)
    eids, weights = _route(s + rb_ref[...], s)
    ids = _expert_ids(tt)
    if cfg.sparse:
        sel = [functools.reduce(jnp.add, [jnp.where(ids[g] == e, 1.0, 0.0) for e in eids]) for g in range(N_GROUPS)]
        sel = jnp.concatenate(sel, axis=0)
        rank = _dot(sel.astype(BF16), tri_ref[...])
        rank_slabs = [rank[g * GROUP_SIZE:(g + 1) * GROUP_SIZE, :] for g in range(N_GROUPS)]
        tok_ref[0] = _lane_dense_rows(weights, tt)
        eid_ref[...] = jnp.concatenate(eids, axis=0)
        rank_ref[...] = jnp.concatenate([_pick(ids, e, rank_slabs) for e in eids], axis=0)
        step = pl.program_id(0) * pl.num_programs(1) + t

        @pl.when(step == 0)
        def _():
            cnt_ref[...] = jnp.zeros(cnt_ref.shape, F32)

        lane = lax.broadcasted_iota(I32, cnt_ref.shape, 1)
        cnt_ref[...] = jnp.where(lane == step, jnp.sum(sel, axis=1, keepdims=True), cnt_ref[...])
    else:
        gates = [functools.reduce(jnp.add, [jnp.where(ids[g] == e, w, 0.0) for e, w in zip(eids, weights)])
                 for g in range(N_GROUPS)]
        gates = jnp.concatenate(gates + [jnp.zeros((LANES - N_EXPERTS, tt), F32)], axis=0)
        tok_ref[0] = gates.T


def _toeplitz(v, rows, cols):
    w = rows + cols
    flat = jnp.tile(v, (1, rows))[:, :rows * (w - 1)]
    return flat.reshape(v.shape[0], rows, w - 1)[:, :, :cols]


def _rel_bias_blocks(table, cfg, q_pos0, k_pos0):
    q_pos = q_pos0 + jnp.arange(cfg.qb)
    k_pos = k_pos0 + jnp.arange(cfg.kb)
    qc = q_pos // CHUNK
    kc = k_pos // CHUNK
    allowed = (kc[None, :] <= qc[:, None]) & (kc[None, :] >= qc[:, None] - BAND_CHUNKS)
    w = cfg.qb + cfg.kb
    dj = jnp.arange(w)
    dj = jnp.where(dj < cfg.kb, dj, dj - w)
    rel = jnp.clip(q_pos0 - k_pos0 - dj, -MAX_REL, MAX_REL) + MAX_REL
    bias = _toeplitz(table[:, rel].astype(F32), cfg.qb, cfg.kb)
    return jnp.where(allowed[None], bias * LOG2_E, NEG)


def _mixer_cost(cfg, tokens, d, n_in, conv_dim):
    per_token = (2 * d * n_in + 2 * (ATTN_DIM + conv_dim + d) * d + 4 * ATTN_DIM * cfg.kb
                 + 4 * ATTN_DIM * ATTN_DIM + 6 * N_EXPERTS * d + 2 * CONV_WIDTH * conv_dim)
    return pl.CostEstimate(flops=tokens * per_token,
                           transcendentals=tokens * (N_HEADS * cfg.kb + 2 * d + 2 * conv_dim + N_EXPERTS),
                           bytes_accessed=tokens * (4 * d + 4 * d + 2 * d + 4 * LANES) + 2 * d * n_in)


def _mixer_call(cfg, x, mod, p, cache=None, batch0=0, nbatch=None, state=None):
    b_all, t_total, d = x.shape
    nb = b_all if nbatch is None else nbatch
    nt = t_total // cfg.tt
    conv_dim = p["conv_w"].shape[1]
    n_in = p["w_in"].shape[1]

    def const(shape):
        return pl.BlockSpec(shape, lambda i, j: (0,) * len(shape), pipeline_mode=pl.Buffered(1))

    def per_row(shape):
        return pl.BlockSpec(shape, lambda i, j: (i + batch0, 0, 0))

    in_specs = [
        pl.BlockSpec((1, cfg.tt, d), lambda i, j: (i + batch0, j, 0)),
        per_row((1, 6, d)),
        const((1, d)), const((1, d)), const((d, n_in)),
        const((1, ATTN_DIM)), const((1, ATTN_DIM)), const((ATTN_DIM, ATTN_DIM)),
        const((N_HEADS, cfg.qb, cfg.kb)), const((ATTN_DIM, d)),
        const((CONV_WIDTH, conv_dim)), const((1, conv_dim)), const((1, conv_dim)), const((1, conv_dim)),
        const((conv_dim, d)), const((1, d)), const((1, 2 * d)), const((d, d)),
        const((N_EXPERTS, d)), const((N_EXPERTS, d)), const((N_EXPERTS, 1)),
    ]
    args = [x, mod, p["norm1_g"], p["norm2_g"], p["w_in"], p["q_norm_g"], p["k_norm_g"], p["hsum"],
            p["bias"], p["w_attn_o"], p["conv_w"], p["conv_b"], p["conv_ln_g"], p["conv_ln_b"],
            p["w_conv_o"], p["b_conv_o"], p["b_gate"], p["w_out"], p["wr_hi"], p["wr_lo"], p["router_bias"]]
    if cfg.has_cache:
        in_specs += [per_row((1, PREV, ATTN_DIM)), per_row((1, PREV, ATTN_DIM)),
                     per_row((1, CONV_WIDTH - 1, conv_dim))]
        args += list(cache)
    if cfg.sparse:
        assert nb * nt <= MAX_TILES
        in_specs += [const((cfg.tt, cfg.tt))]
        tok = jnp.arange(cfg.tt)
        args += [(tok[:, None] < tok[None, :]).astype(BF16)]
    aliases = {}
    if state is not None:
        for i, arr in enumerate(state):
            aliases[len(args)] = 3 + i
            in_specs.append(pl.BlockSpec(memory_space=pl.ANY))
            args.append(arr)
    keep = min(PREV, t_total)
    assert keep == cfg.tt
    out_shape = [jax.ShapeDtypeStruct((nb, t_total, d), F32),
                 jax.ShapeDtypeStruct((nb, t_total, d // 2), I32),
                 jax.ShapeDtypeStruct((nb, t_total, LANES), F32),
                 jax.ShapeDtypeStruct((b_all, keep, ATTN_DIM), F32),
                 jax.ShapeDtypeStruct((b_all, keep, ATTN_DIM), F32),
                 jax.ShapeDtypeStruct((b_all, CONV_WIDTH - 1, conv_dim), F32)]
    out_specs = [pl.BlockSpec((1, cfg.tt, d), lambda i, j: (i, j, 0)),
                 pl.BlockSpec((1, cfg.tt, d // 2), lambda i, j: (i, j, 0)),
                 pl.BlockSpec((1, cfg.tt, LANES), lambda i, j: (i, j, 0)),
                 per_row((1, keep, ATTN_DIM)), per_row((1, keep, ATTN_DIM)),
                 per_row((1, CONV_WIDTH - 1, conv_dim))]
    if cfg.sparse:
        out_shape += [jax.ShapeDtypeStruct((TOP_K, nb * t_total), F32),
                      jax.ShapeDtypeStruct((TOP_K, nb * t_total), F32),
                      jax.ShapeDtypeStruct((N_EXPERTS, MAX_TILES), F32)]
        out_specs += [pl.BlockSpec((TOP_K, cfg.tt), lambda i, j: (0, i * nt + j)),
                      pl.BlockSpec((TOP_K, cfg.tt), lambda i, j: (0, i * nt + j)),
                      pl.BlockSpec((N_EXPERTS, MAX_TILES), lambda i, j: (0, 0))]
    scratch = [pltpu.VMEM((cfg.tt, ATTN_DIM), BF16),
               pltpu.VMEM((PREV + cfg.tt, ATTN_DIM), BF16),
               pltpu.VMEM((PREV + cfg.tt, ATTN_DIM), BF16),
               pltpu.VMEM((cfg.tt, ATTN_DIM), BF16),
               pltpu.VMEM((CONV_PAD + cfg.tt, conv_dim), F32),
               pltpu.VMEM((cfg.tt, conv_dim), F32)]
    return pl.pallas_call(
        functools.partial(_mixer_kernel, cfg, 0 if state is None else len(state)),
        out_shape=out_shape,
        grid=(nb, nt),
        in_specs=in_specs,
        out_specs=out_specs,
        scratch_shapes=scratch,
        input_output_aliases=aliases,
        cost_estimate=_mixer_cost(cfg, nb * t_total, d, n_in, conv_dim),
        compiler_params=pltpu.CompilerParams(dimension_semantics=("arbitrary", "arbitrary"),
                                             vmem_limit_bytes=VMEM_LIMIT_BYTES),
        name="mixer_sample" if cfg.has_cache else "mixer_prompt",
    )(*args)


def _exact_parts(x, n):
    parts = []
    for _ in range(n):
        part = x.astype(BF16)
        parts.append(part)
        x = x - part.astype(F32)
    return parts


def _plan_kernel(tt, cnt_ref, eid_ref, rank_ref, tri_e_ref, tri_t_ref, pos_ref, blk_ref, nblk_ref, base_ref):
    step = pl.program_id(0)

    @pl.when(step == 0)
    def _():
        cnt = cnt_ref[...]
        total = jnp.sum(cnt, axis=1, keepdims=True)
        padded = jnp.floor((total + float(ROW_BLOCK - 1)) * (1.0 / ROW_BLOCK)) * float(ROW_BLOCK)
        padded = jnp.broadcast_to(padded, cnt.shape)
        start = functools.reduce(jnp.add, [_dot(tri_e_ref[...], part) for part in _exact_parts(padded, 3)])
        before = functools.reduce(jnp.add, [_dot(part, tri_t_ref[...]) for part in _exact_parts(cnt, 2)])
        base_ref[...] = start + before
        end = (start + padded)[:, 0:1]
        first_row = lax.broadcasted_iota(I32, blk_ref.shape, 1).astype(F32) * float(ROW_BLOCK)
        owner = jnp.sum(jnp.where(end <= first_row, 1.0, 0.0), axis=0, keepdims=True)
        blk_ref[...] = jnp.minimum(owner, float(N_EXPERTS - 1)).astype(I32)
        nblk = jnp.max(end, axis=0, keepdims=True) * (1.0 / ROW_BLOCK)
        nblk_ref[...] = jnp.broadcast_to(nblk, nblk_ref.shape).astype(I32)

    lane = lax.broadcasted_iota(I32, base_ref.shape, 1)
    col = jnp.sum(jnp.where(lane == step, base_ref[...], 0.0), axis=1, keepdims=True)
    ids = _expert_ids(tt)
    col_slabs = [jnp.broadcast_to(col[g * GROUP_SIZE:(g + 1) * GROUP_SIZE, :], (GROUP_SIZE, tt))
                 for g in range(N_GROUPS)]
    eid = eid_ref[...]
    rows = [_pick(ids, eid[k:k + 1, :], col_slabs) for k in range(TOP_K)]
    pos_ref[...] = (jnp.concatenate(rows, axis=0) + rank_ref[...]).astype(I32)


def _plan_call(cnt, eid, rank, tt, n_blocks):
    n = eid.shape[1]
    e = jnp.arange(N_EXPERTS)
    s = jnp.arange(MAX_TILES)
    nb_pad = -(-n_blocks // LANES) * LANES
    return pl.pallas_call(
        functools.partial(_plan_kernel, tt),
        out_shape=[jax.ShapeDtypeStruct((TOP_K, n), I32),
                   jax.ShapeDtypeStruct((1, nb_pad), I32),
                   jax.ShapeDtypeStruct((1, LANES), I32)],
        grid=(n // tt,),
        in_specs=[pl.BlockSpec((N_EXPERTS, MAX_TILES), lambda i: (0, 0)),
                  pl.BlockSpec((TOP_K, tt), lambda i: (0, i)),
                  pl.BlockSpec((TOP_K, tt), lambda i: (0, i)),
                  pl.BlockSpec((N_EXPERTS, N_EXPERTS), lambda i: (0, 0)),
                  pl.BlockSpec((MAX_TILES, MAX_TILES), lambda i: (0, 0))],
        out_specs=[pl.BlockSpec((TOP_K, tt), lambda i: (0, i)),
                   pl.BlockSpec((1, nb_pad), lambda i: (0, 0)),
                   pl.BlockSpec((1, LANES), lambda i: (0, 0))],
        scratch_shapes=[pltpu.VMEM((N_EXPERTS, MAX_TILES), F32)],
        compiler_params=pltpu.CompilerParams(dimension_semantics=("arbitrary",),
                                             vmem_limit_bytes=VMEM_LIMIT_BYTES),
        name="plan",
    )(cnt, eid, rank, (e[None, :] < e[:, None]).astype(BF16), (s[:, None] < s[None, :]).astype(BF16))


def _sc_mesh():
    return plsc.VectorSubcoreMesh(core_axis_name="c", subcore_axis_name="s")


def _dispatch_call(rows, pos, n_sorted):
    n, w = rows.shape
    per_worker = n // SC_WORKERS
    nsteps = per_worker // SC_ROWS
    assert per_worker * SC_WORKERS == n and nsteps * SC_ROWS == per_worker and nsteps % 2 == 0

    @functools.partial(
        pl.kernel, mesh=_sc_mesh(),
        out_type=jax.ShapeDtypeStruct((n_sorted, w), rows.dtype),
        scratch_types=[pltpu.VMEM((2, TOP_K, SC_ROWS), I32),
                       pltpu.VMEM((2, SC_ROWS, w), rows.dtype),
                       pltpu.SemaphoreType.DMA((2,)),
                       pltpu.SemaphoreType.DMA((2,))],
        cost_estimate=pl.CostEstimate(flops=0, transcendentals=0,
                                      bytes_accessed=(1 + TOP_K) * n * w * rows.dtype.itemsize + 4 * TOP_K * n),
    )
    def dispatch(x_hbm, pos_hbm, out_hbm, idx_v, rows_v, load_sem, scat_sem):
        base = (lax.axis_index("s") * SC_CORES + lax.axis_index("c")) * per_worker

        def load(i, b):
            return pltpu.make_async_copy(x_hbm.at[pl.ds(base + i * SC_ROWS, SC_ROWS)], rows_v.at[b], load_sem.at[b])

        def scatter(b, k):
            return pltpu.make_async_copy(rows_v.at[b], out_hbm.at[idx_v.at[b, k]], scat_sem.at[b])

        def load_start(i, b):
            for k in range(TOP_K):
                pltpu.sync_copy(pos_hbm.at[pl.ds(k * n + base + i * SC_ROWS, SC_ROWS)], idx_v.at[b, k])
            load(i, b).start()

        load_start(0, 0)

        @pl.loop(0, nsteps, step=2)
        def _(i):
            for b in range(2):
                ii = i + b

                @pl.when(ii >= 1)
                def _():
                    for k in range(TOP_K):
                        scatter(1 - b, k).wait()

                @pl.when(ii + 1 < nsteps)
                def _():
                    load_start(ii + 1, 1 - b)

                load(ii, b).wait()
                for k in range(TOP_K):
                    scatter(b, k).start()

        for k in range(TOP_K):
            scatter((nsteps - 1) % 2, k).wait()

    return dispatch(rows, pos)


def _collect_call(table, idx):
    n = idx.shape[0]
    w = table.shape[1]
    per_worker = n // SC_WORKERS
    nsteps = per_worker // SC_ROWS
    assert per_worker * SC_WORKERS == n and nsteps * SC_ROWS == per_worker and nsteps % 2 == 0

    @functools.partial(
        pl.kernel, mesh=_sc_mesh(),
        out_type=jax.ShapeDtypeStruct((n, w), table.dtype),
        scratch_types=[pltpu.VMEM((2, SC_ROWS), I32),
                       pltpu.VMEM((2, SC_ROWS, w), table.dtype),
                       pltpu.SemaphoreType.DMA((2,)),
                       pltpu.SemaphoreType.DMA((2,))],
        cost_estimate=pl.CostEstimate(flops=0, transcendentals=0,
                                      bytes_accessed=2 * n * w * table.dtype.itemsize + 4 * n),
    )
    def collect(table_hbm, idx_hbm, out_hbm, idx_v, rows_v, gather_sem, write_sem):
        base = (lax.axis_index("s") * SC_CORES + lax.axis_index("c")) * per_worker

        def gather(b):
            return pltpu.make_async_copy(table_hbm.at[idx_v.at[b]], rows_v.at[b], gather_sem.at[b])

        def write(i, b):
            return pltpu.make_async_copy(rows_v.at[b], out_hbm.at[pl.ds(base + i * SC_ROWS, SC_ROWS)], write_sem.at[b])

        def gather_start(i, b):
            pltpu.sync_copy(idx_hbm.at[pl.ds(base + i * SC_ROWS, SC_ROWS)], idx_v.at[b])
            gather(b).start()

        gather_start(0, 0)

        @pl.loop(0, nsteps, step=2)
        def _(i):
            for b in range(2):
                ii = i + b

                @pl.when(ii >= 1)
                def _():
                    write(ii - 1, 1 - b).wait()

                @pl.when(ii + 1 < nsteps)
                def _():
                    gather_start(ii + 1, 1 - b)

                gather(b).wait()
                write(ii, b).start()

        write(nsteps - 1, (nsteps - 1) % 2).wait()

    return collect(table, idx)


def _expert_ffn_kernel(blk_ref, nblk_ref, x_ref, wg_ref, wu_ref, wd_ref, o_ref, wg_s, wu_s, wd_s):
    j = pl.program_id(0)

    @pl.when(jnp.logical_or(j == 0, blk_ref[j] != blk_ref[jnp.maximum(j - 1, 0)]))
    def _():
        wg_s[...] = wg_ref[0].astype(BF16)
        wu_s[...] = wu_ref[0].astype(BF16)
        wd_s[...] = wd_ref[0].astype(BF16)

    @pl.when(j < nblk_ref[0])
    def _():
        a, b = _unpack(x_ref[...])
        a, b = a.astype(BF16), b.astype(BF16)
        hg = _dot_halves(a, b, wg_s)
        act = hg * _sigmoid_t(hg) * _dot_halves(a, b, wu_s)
        y = _dot(act.astype(BF16), wd_s[...])
        half = y.shape[1] // 2
        o_ref[...] = _pack(y[:, 0:half], y[:, half:2 * half])


def _expert_ffn_call(xs, blk, nblk, p):
    n_sorted, w = xs.shape
    d = 2 * w
    ff = p["w_e_gate"].shape[2]

    def rows(j, blk, nblk):
        return (jnp.minimum(j, nblk[0] - 1), 0)

    def expert(j, blk, nblk):
        return (blk[j], 0, 0)

    return pl.pallas_call(
        _expert_ffn_kernel,
        out_shape=jax.ShapeDtypeStruct((n_sorted, w), I32),
        grid_spec=pltpu.PrefetchScalarGridSpec(
            num_scalar_prefetch=2,
            grid=(n_sorted // ROW_BLOCK,),
            in_specs=[pl.BlockSpec((ROW_BLOCK, w), rows),
                      pl.BlockSpec((1, d, ff), expert),
                      pl.BlockSpec((1, d, ff), expert),
                      pl.BlockSpec((1, ff, d), expert)],
            out_specs=pl.BlockSpec((ROW_BLOCK, w), rows),
            scratch_shapes=[pltpu.VMEM((d, ff), BF16), pltpu.VMEM((d, ff), BF16), pltpu.VMEM((ff, d), BF16)]),
        cost_estimate=pl.CostEstimate(flops=6 * n_sorted * d * ff, transcendentals=n_sorted * ff,
                                      bytes_accessed=8 * n_sorted * w + 12 * N_EXPERTS * d * ff),
        compiler_params=pltpu.CompilerParams(dimension_semantics=("arbitrary",),
                                             vmem_limit_bytes=VMEM_LIMIT_BYTES),
        name="expert_ffn",
    )(blk, nblk, xs, p["w_e_gate"], p["w_e_up"], p["w_e_down"])


def _swiglu_halves(a, b, wg_ref, wu_ref):
    hg = _dot_halves(a, b, wg_ref)
    return hg * _sigmoid_t(hg) * _dot_halves(a, b, wu_ref)


def _moe_out_kernel(n_groups, tiles, *refs):
    groups = [refs[4 * q:4 * q + 4] for q in range(n_groups)]
    g2_ref, wsg_ref, wsu_ref, wsd_ref, o_ref = refs[4 * n_groups:]

    def combine(x1_ref, h_ref, y_ref, w_ref):
        a, b = _unpack(h_ref[...])
        shared = _dot(_swiglu_halves(a.astype(BF16), b.astype(BF16), wsg_ref, wsu_ref).astype(BF16),
                      wsd_ref[...].astype(BF16))
        half = h_ref.shape[1]
        w = w_ref[...]
        acc_a, acc_b = shared[:, 0:half], shared[:, half:2 * half]
        for k in range(TOP_K):
            ya, yb = _unpack(y_ref[k])
            acc_a = acc_a + w[:, k:k + 1] * ya
            acc_b = acc_b + w[:, k:k + 1] * yb
        g2 = g2_ref[...]
        o_ref[:, 0:half] = x1_ref[:, 0:half] + g2[:, 0:half] * acc_a
        o_ref[:, half:2 * half] = x1_ref[:, half:2 * half] + g2[:, half:2 * half] * acc_b

    for q in range(n_groups):
        pl.when(pl.program_id(0) // tiles == q)(functools.partial(combine, *groups[q]))


def _moe_out_call(groups, g2, p, tm):
    n, d = groups[0][0].shape
    n_all = n * len(groups)
    sff = p["w_s_gate"].shape[1]
    per_row = n_all // g2.shape[0]
    g2 = g2.reshape(g2.shape[0], 1, d)
    tiles = n // tm

    def const(shape):
        return pl.BlockSpec(shape, lambda i: (0,) * len(shape), pipeline_mode=pl.Buffered(1))

    in_specs, args = [], []
    for q, group in enumerate(groups):
        tile = lambda i, q=q: jnp.clip(i - q * tiles, 0, tiles - 1)
        in_specs += [pl.BlockSpec((tm, d), lambda i, tile=tile: (tile(i), 0)),
                     pl.BlockSpec((tm, d // 2), lambda i, tile=tile: (tile(i), 0)),
                     pl.BlockSpec((TOP_K, tm, d // 2), lambda i, tile=tile: (0, tile(i), 0)),
                     pl.BlockSpec((tm, LANES), lambda i, tile=tile: (tile(i), 0))]
        args += list(group)
    in_specs += [pl.BlockSpec((None, 1, d), lambda i: (i * tm // per_row, 0, 0)),
                 const((d, sff)), const((d, sff)), const((sff, d))]
    args += [g2, p["w_s_gate"], p["w_s_up"], p["w_s_down"]]
    return pl.pallas_call(
        functools.partial(_moe_out_kernel, len(groups), tiles),
        out_shape=jax.ShapeDtypeStruct((n_all, d), F32),
        grid=(n_all // tm,),
        in_specs=in_specs,
        out_specs=pl.BlockSpec((tm, d), lambda i: (i, 0)),
        cost_estimate=pl.CostEstimate(flops=n_all * (6 * d * sff + 2 * TOP_K * d), transcendentals=n_all * sff,
                                      bytes_accessed=n_all * (8 * d + 2 * d + 2 * TOP_K * d + 4 * LANES)),
        compiler_params=pltpu.CompilerParams(dimension_semantics=("arbitrary",),
                                             vmem_limit_bytes=VMEM_LIMIT_BYTES),
        name="moe_out",
    )(*args)


def _moe_dense_kernel(h_ref, gates_ref, x1_ref, g2_ref, wsg_ref, wsu_ref, wsd_ref, wg_ref, wu_ref, wd_ref,
                      o_ref, acc_ref):
    e = pl.program_id(1)
    a, b = _unpack(h_ref[...])
    a, b = a.astype(BF16), b.astype(BF16)

    @pl.when(e == 0)
    def _():
        acc_ref[...] = _dot(_swiglu_halves(a, b, wsg_ref, wsu_ref).astype(BF16), wsd_ref[...].astype(BF16))

    gates = gates_ref[...]
    lane = lax.broadcasted_iota(I32, gates.shape, 1)
    gate = jnp.sum(jnp.where(lane == e, gates, 0.0), axis=1, keepdims=True)
    act = _swiglu_halves(a, b, wg_ref.at[0], wu_ref.at[0]) * gate
    acc_ref[...] += _dot(act.astype(BF16), wd_ref[0].astype(BF16))

    @pl.when(e == pl.num_programs(1) - 1)
    def _():
        o_ref[...] = x1_ref[...] + g2_ref[...] * acc_ref[...]


def _moe_dense_call(h2, gates, x1, g2_rows, p):
    n, d = x1.shape
    ff = p["w_e_gate"].shape[2]
    sff = p["w_s_gate"].shape[1]

    def const(shape):
        return pl.BlockSpec(shape, lambda i, e: (0,) * len(shape), pipeline_mode=pl.Buffered(1))

    return pl.pallas_call(
        _moe_dense_kernel,
        out_shape=jax.ShapeDtypeStruct((n, d), F32),
        grid=(1, N_EXPERTS),
        in_specs=[const((n, d // 2)), const((n, LANES)), const((n, d)), const((n, d)),
                  const((d, sff)), const((d, sff)), const((sff, d)),
                  pl.BlockSpec((1, d, ff), lambda i, e: (e, 0, 0)),
                  pl.BlockSpec((1, d, ff), lambda i, e: (e, 0, 0)),
                  pl.BlockSpec((1, ff, d), lambda i, e: (e, 0, 0))],
        out_specs=pl.BlockSpec((n, d), lambda i, e: (0, 0)),
        scratch_shapes=[pltpu.VMEM((n, d), F32)],
        compiler_params=pltpu.CompilerParams(dimension_semantics=("arbitrary", "arbitrary"),
                                             vmem_limit_bytes=VMEM_LIMIT_BYTES),
        name="moe_dense",
    )(h2, gates, x1, g2_rows, p["w_s_gate"], p["w_s_up"], p["w_s_down"], p["w_e_gate"], p["w_e_up"], p["w_e_down"])


PROMPT_CFG = MixerCfg(tt=512, qb=256, kb=768, n_qblk=2, kstride=256, has_cache=False, sparse=True)
MOE_OUT_TILE = 256
PROMPT_GROUPS = 2


def _layer_params(l, w_in, q_norm_g, k_norm_g, w_attn_o, conv_w, conv_b, conv_ln_g, conv_ln_b, w_conv_o,
                  b_conv_o, b_gate, w_out, norm1_g, norm2_g, w_router, router_bias,
                  w_e_gate, w_e_up, w_e_down, w_s_gate, w_s_up, w_s_down):
    row = lambda a: a[l].reshape(1, -1)
    wr_t = w_router[l].T
    wr_hi = wr_t.astype(BF16)
    head = jnp.arange(ATTN_DIM) // HEAD_DIM
    return {
        "norm1_g": row(norm1_g), "norm2_g": row(norm2_g), "w_in": w_in[l].astype(BF16),
        "q_norm_g": jnp.tile(q_norm_g[l], N_HEADS).reshape(1, -1),
        "k_norm_g": jnp.tile(k_norm_g[l], N_HEADS).reshape(1, -1),
        "hsum": (head[:, None] == head[None, :]).astype(BF16),
        "w_attn_o": w_attn_o[l].astype(BF16), "conv_w": conv_w[l], "conv_b": row(conv_b),
        "conv_ln_g": row(conv_ln_g), "conv_ln_b": row(conv_ln_b), "w_conv_o": w_conv_o[l].astype(BF16),
        "b_conv_o": row(b_conv_o), "b_gate": row(b_gate), "w_out": w_out[l].astype(BF16),
        "wr_hi": wr_hi, "wr_lo": (wr_t - wr_hi.astype(F32)).astype(BF16),
        "router_bias": router_bias[l].reshape(-1, 1),
        "w_e_gate": w_e_gate[l], "w_e_up": w_e_up[l], "w_e_down": w_e_down[l],
        "w_s_gate": w_s_gate[l], "w_s_up": w_s_up[l], "w_s_down": w_s_down[l],
    }


def _sorted_rows(h2, eid, rank, cnt, tt):
    n = h2.shape[0]
    n_sorted = n * TOP_K + N_EXPERTS * ROW_BLOCK
    pos, blk, nblk = _plan_call(cnt, eid, rank, tt, n_sorted // ROW_BLOCK)
    pos = pos.reshape(TOP_K * n)
    return _dispatch_call(h2, pos, n_sorted), pos, blk.reshape(-1), nblk.reshape(-1)


def _expert_rows(xs, pos, blk, nblk, p):
    ys = _expert_ffn_call(xs, blk, nblk, p)
    return _collect_call(ys, pos).reshape(TOP_K, pos.shape[0] // TOP_K, xs.shape[1])


def kernel(x_prompt, x_sample, c_prompt, c_sample, cache_k, cache_v, state_conv, w_mod, b_mod, norm1_g, w_in, q_norm_g, k_norm_g, rel_bias, w_attn_o, conv_w, conv_b, conv_ln_g, conv_ln_b, w_conv_o, b_conv_o, b_gate, w_out, norm2_g, w_router, router_bias, w_e_gate, w_e_up, w_e_down, w_s_gate, w_s_up, w_s_down):
    depth = w_mod.shape[0]
    bp, tp, d = x_prompt.shape
    bs, ts, _ = x_sample.shape
    assert cache_k.shape[2] == min(PREV, PAST_LEN) == PREV
    sample_cfg = MixerCfg(tt=ts, qb=ts, kb=PREV + ts, n_qblk=1, kstride=0, has_cache=True, sparse=False)

    yp, ys = x_prompt, x_sample
    outs = [[] for _ in range(6)]
    for l in range(depth):
        p = _layer_params(l, w_in, q_norm_g, k_norm_g, w_attn_o, conv_w, conv_b, conv_ln_g, conv_ln_b,
                          w_conv_o, b_conv_o, b_gate, w_out, norm1_g, norm2_g, w_router, router_bias,
                          w_e_gate, w_e_up, w_e_down, w_s_gate, w_s_up, w_s_down)
        mod = _mod_call(jnp.concatenate([c_prompt, c_sample], axis=0), w_mod[l], b_mod[l])
        mod = mod.reshape(bp + bs, 6, d)
        mod_p, mod_s = mod[:bp], mod[bp:]

        p["bias"] = _rel_bias_blocks(rel_bias[l], PROMPT_CFG, PREV, 0)
        gb = bp // PROMPT_GROUPS
        keep = min(PREV, tp)
        state = [jnp.zeros((bp, keep, ATTN_DIM), F32), jnp.zeros((bp, keep, ATTN_DIM), F32),
                 jnp.zeros((bp, CONV_WIDTH - 1, conv_w.shape[2]), F32)]
        mixed, moved = [], []
        for g in range(PROMPT_GROUPS):
            x1, h2, w_tok, *state, eid, rank, cnt = _mixer_call(PROMPT_CFG, yp, mod_p, p, batch0=g * gb, nbatch=gb,
                                                                state=state)
            x1, h2, w_tok = x1.reshape(gb * tp, d), h2.reshape(gb * tp, d // 2), w_tok.reshape(gb * tp, LANES)
            mixed.append((x1, h2, w_tok))
            moved.append(_sorted_rows(h2, eid, rank, cnt, PROMPT_CFG.tt))
        kp, vp, cp = state
        groups = [(x1, h2, _expert_rows(*rows, p), w_tok) for (x1, h2, w_tok), rows in zip(mixed, moved)]
        yp = _moe_out_call(groups, mod_p[:, 5, :], p, MOE_OUT_TILE).reshape(bp, tp, d)
        outs[0].append(kp.reshape(bp, -1, N_HEADS, HEAD_DIM))
        outs[1].append(vp.reshape(bp, -1, N_HEADS, HEAD_DIM))
        outs[2].append(cp)

        p["bias"] = _rel_bias_blocks(rel_bias[l], sample_cfg, PAST_LEN, PAST_LEN - PREV)
        cache = (cache_k[l].reshape(bs, PREV, ATTN_DIM), cache_v[l].reshape(bs, PREV, ATTN_DIM), state_conv[l])
        x1, h2, gates, ks, vs, cs = _mixer_call(sample_cfg, ys, mod_s, p, cache)
        ys = _moe_dense_call(h2.reshape(bs * ts, d // 2), gates.reshape(bs * ts, LANES), x1.reshape(bs * ts, d),
                             jnp.repeat(mod_s[:, 5, :], ts, axis=0), p).reshape(bs, ts, d)
        outs[3].append(ks.reshape(bs, ts, N_HEADS, HEAD_DIM))
        outs[4].append(vs.reshape(bs, ts, N_HEADS, HEAD_DIM))
        outs[5].append(cs)
    return (yp, ys) + tuple(jnp.stack(o) for o in outs)
```

```python
import dataclasses
import functools

import jax
import jax.numpy as jnp
from jax import lax
from jax.experimental import pallas as pl
from jax.experimental.pallas import tpu as pltpu
from jax.experimental.pallas import tpu_sc as plsc

F32 = jnp.float32
BF16 = jnp.bfloat16
I32 = jnp.int32

CHUNK = 64
BAND_CHUNKS = 8
PREV = BAND_CHUNKS * CHUNK
PAST_LEN = 1024
N_HEADS = 8
HEAD_DIM = 64
ATTN_DIM = N_HEADS * HEAD_DIM
MAX_REL = 128
CONV_WIDTH = 31
CONV_PAD = 32
N_EXPERTS = 64
N_GROUPS = 8
GROUP_SIZE = N_EXPERTS // N_GROUPS
TOPK_GROUPS = 4
TOP_K = 8
ROUTE_SCALE = 2.5
EPS = 1e-6
LANES = 128
SUBLANES = 8
NEG = -1e30
LOG2_E = 1.4426950408889634

VMEM_LIMIT_BYTES = 60 * 1024 * 1024

SC_CORES = 2
SC_SUBCORES = 16
SC_WORKERS = SC_CORES * SC_SUBCORES
SC_ROWS = 64

MAX_TILES = LANES
ROW_BLOCK = 1024


def _sigmoid(x):
    return 1.0 / (1.0 + jnp.exp(-x))


def _sigmoid_t(x):
    return 0.5 * jnp.tanh(0.5 * x) + 0.5


def _split_bf16(x):
    hi = x.astype(BF16)
    lo = (x - hi.astype(F32)).astype(BF16)
    return hi, lo


def _dot(a, b):
    return jnp.dot(a, b, preferred_element_type=F32)


def _dot3(a, b):
    a_hi, a_lo = _split_bf16(a)
    b_hi, b_lo = _split_bf16(b)
    return _dot(a_hi, b_hi) + _dot(a_hi, b_lo) + _dot(a_lo, b_hi)


def _pack(a, b):
    ia = lax.bitcast_convert_type(a.astype(BF16).astype(F32), I32)
    ib = lax.bitcast_convert_type(b.astype(BF16).astype(F32), I32)
    return ia | lax.shift_right_logical(ib, 16)


def _unpack(p):
    a = lax.bitcast_convert_type(p & jnp.int32(-65536), F32)
    b = lax.bitcast_convert_type(lax.shift_left(p, 16), F32)
    return a, b


def _dot_halves(a, b, w_ref):
    half = a.shape[1]
    return _dot(a, w_ref[0:half, :].astype(BF16)) + _dot(b, w_ref[half:2 * half, :].astype(BF16))


def _mod_kernel(c_ref, w_ref, b_ref, o_ref):
    c = c_ref[...]
    o_ref[...] = _dot3(c * _sigmoid(c), w_ref[...]) + b_ref[...]


def _mod_call(c, w_mod, b_mod):
    n, d = c.shape
    dout = w_mod.shape[1]
    bn = 1536
    return pl.pallas_call(
        _mod_kernel,
        out_shape=jax.ShapeDtypeStruct((n, dout), F32),
        grid=(dout // bn,),
        in_specs=[pl.BlockSpec((n, d), lambda j: (0, 0)),
                  pl.BlockSpec((d, bn), lambda j: (0, j)),
                  pl.BlockSpec((1, bn), lambda j: (0, j))],
        out_specs=pl.BlockSpec((n, bn), lambda j: (0, j)),
        compiler_params=pltpu.CompilerParams(dimension_semantics=("arbitrary",),
                                             vmem_limit_bytes=VMEM_LIMIT_BYTES),
        name="mod",
    )(c, w_mod, b_mod.reshape(1, dout))


@dataclasses.dataclass(frozen=True)
class MixerCfg:
    tt: int
    qb: int
    kb: int
    n_qblk: int
    kstride: int
    has_cache: bool
    sparse: bool


def _rms(x, g):
    ms = jnp.mean(x * x, axis=-1, keepdims=True)
    return x * lax.rsqrt(ms + EPS) * g


def _head_rms(z, hsum, g):
    ss = _dot((z * z).astype(BF16), hsum)
    return z * lax.rsqrt(ss * (1.0 / HEAD_DIM) + EPS) * g


def _expert_ids(t):
    sub = lax.broadcasted_iota(I32, (GROUP_SIZE, t), 0).astype(F32)
    return [sub + float(g * GROUP_SIZE) for g in range(N_GROUPS)]


def _pick(ids, eid, slabs):
    acc = functools.reduce(jnp.add, [jnp.where(ids[g] == eid, slabs[g], 0.0) for g in range(N_GROUPS)])
    return jnp.sum(acc, axis=0, keepdims=True)


def _route(choice, s):
    t = choice.shape[1]
    sub = lax.broadcasted_iota(I32, (GROUP_SIZE, t), 0).astype(F32)
    slabs = [choice[g * GROUP_SIZE:(g + 1) * GROUP_SIZE, :] for g in range(N_GROUPS)]
    s_slabs = [s[g * GROUP_SIZE:(g + 1) * GROUP_SIZE, :] for g in range(N_GROUPS)]
    gscore = []
    for c in slabs:
        m1 = jnp.max(c, axis=0, keepdims=True)
        first = jnp.min(jnp.where(c == m1, sub, float(GROUP_SIZE)), axis=0, keepdims=True)
        m2 = jnp.max(jnp.where(sub == first, -jnp.inf, c), axis=0, keepdims=True)
        gscore.append(m1 + m2)
    gsel = [jnp.zeros((1, t), F32) for _ in range(N_GROUPS)]
    for _ in range(TOPK_GROUPS):
        m = functools.reduce(jnp.maximum, gscore)
        first = functools.reduce(jnp.minimum,
                                 [jnp.where(gscore[g] == m, float(g), float(N_GROUPS)) for g in range(N_GROUPS)])
        for g in range(N_GROUPS):
            hit = first == float(g)
            gsel[g] = jnp.where(hit, 1.0, gsel[g])
            gscore[g] = jnp.where(hit, -jnp.inf, gscore[g])
    masked = [jnp.where(gsel[g] > 0.5, slabs[g], -jnp.inf) for g in range(N_GROUPS)]
    ids = _expert_ids(t)
    eids, raw = [], []
    for _ in range(TOP_K):
        m = jnp.max(functools.reduce(jnp.maximum, masked), axis=0, keepdims=True)
        cand = [jnp.where(masked[g] == m, ids[g], float(N_EXPERTS)) for g in range(N_GROUPS)]
        first = jnp.min(functools.reduce(jnp.minimum, cand), axis=0, keepdims=True)
        eids.append(first)
        raw.append(_pick(ids, first, s_slabs))
        masked = [jnp.where(ids[g] == first, -jnp.inf, masked[g]) for g in range(N_GROUPS)]
    wsum = functools.reduce(jnp.add, raw)
    return eids, [r / wsum * ROUTE_SCALE for r in raw]


def _lane_dense_rows(rows, t):
    pad = jnp.zeros((LANES - len(rows), t), F32)
    return jnp.concatenate(rows + [pad], axis=0).T


def _mixer_kernel(cfg, n_state, *refs):
    (x_ref, mod_ref, n1g_ref, n2g_ref, win_ref, qg_ref, kg_ref, hsum_ref, bias_ref, wao_ref,
     cw_ref, cb_ref, lng_ref, lnb_ref, wco_ref, bco_ref, bg_ref, wout_ref,
     wrh_ref, wrl_ref, rb_ref) = refs[:21]
    refs = refs[21:]
    if cfg.has_cache:
        kc_ref, vc_ref, cs_ref = refs[:3]
        refs = refs[3:]
    if cfg.sparse:
        tri_ref = refs[0]
        refs = refs[1:]
    refs = refs[n_state:]
    x1_ref, h2_ref, tok_ref, kout_ref, vout_ref, cout_ref = refs[:6]
    refs = refs[6:]
    if cfg.sparse:
        eid_ref, rank_ref, cnt_ref = refs[:3]
        refs = refs[3:]
    qbuf, kbuf, vbuf, obuf, uext, cvbuf = refs

    tt = cfg.tt
    t = pl.program_id(1)

    if cfg.has_cache:
        kbuf[0:PREV, :] = kc_ref[0].astype(BF16)
        vbuf[0:PREV, :] = vc_ref[0].astype(BF16)
        uext[CONV_PAD - (CONV_WIDTH - 1):CONV_PAD, :] = cs_ref[0]
    else:
        @pl.when(t == 0)
        def _():
            kbuf[0:PREV, :] = jnp.zeros((PREV, ATTN_DIM), BF16)
            vbuf[0:PREV, :] = jnp.zeros((PREV, ATTN_DIM), BF16)
            uext[0:CONV_PAD, :] = jnp.zeros((CONV_PAD, uext.shape[1]), F32)

    mod = mod_ref[0]
    sh1, sc1, g1, sh2, sc2 = (mod[i:i + 1, :] for i in range(5))

    x = x_ref[0]
    hb = (_rms(x, n1g_ref[...]) * (1.0 + sc1) + sh1).astype(BF16)

    a0, a1, a2, a3, a4, a5 = (0, ATTN_DIM, 2 * ATTN_DIM, 3 * ATTN_DIM,
                              3 * ATTN_DIM + cw_ref.shape[1], 3 * ATTN_DIM + 2 * cw_ref.shape[1])
    d_model = x.shape[1]

    q = _head_rms(_dot(hb, win_ref[:, a0:a1]), hsum_ref[...], qg_ref[...])
    qbuf[...] = (q * (HEAD_DIM ** -0.5 * LOG2_E)).astype(BF16)
    k = _head_rms(_dot(hb, win_ref[:, a1:a2]), hsum_ref[...], kg_ref[...])
    kout_ref[0] = k
    kbuf[PREV:PREV + tt, :] = k.astype(BF16)
    v = _dot(hb, win_ref[:, a2:a3])
    vout_ref[0] = v
    vbuf[PREV:PREV + tt, :] = v.astype(BF16)

    u = _dot(hb, win_ref[:, a3:a4]) * _sigmoid_t(_dot(hb, win_ref[:, a4:a5]))
    uext[CONV_PAD:CONV_PAD + tt, :] = u
    cout_ref[0] = uext[CONV_PAD + tt - (CONV_WIDTH - 1):CONV_PAD + tt, :]

    rc = min(tt, 64)
    cc = min(u.shape[1], LANES)
    shifts = [j + CONV_PAD - (CONV_WIDTH - 1) for j in range(CONV_WIDTH)]

    def conv_unit(r0, c0):
        acc = jnp.broadcast_to(cb_ref[:, c0:c0 + cc], (rc, cc))
        for res in range(SUBLANES):
            group = [s for s in shifts if s % SUBLANES == res]
            if not group:
                continue
            lo, hi = min(group) - res, max(group) - res
            rows = hi - lo + rc + (SUBLANES if res else 0)
            slab = uext[pl.ds(r0 + lo, rows), c0:c0 + cc]
            if res:
                slab = pltpu.roll(slab, rows - res, 0)
            for s in group:
                j = s - shifts[0]
                a = s - res - lo
                acc = acc + cw_ref[j:j + 1, c0:c0 + cc] * slab[a:a + rc, :]
        cvbuf[pl.ds(r0, rc), c0:c0 + cc] = acc

    units = [(r0, c0) for r0 in range(0, tt, rc) for c0 in range(0, u.shape[1], cc)]
    gw = 256
    n_chunks = 2 * d_model // gw
    gates = []
    for ci in range(n_chunks):
        cols = slice(ci * gw, (ci + 1) * gw)
        gates.append(_sigmoid_t(_dot(hb, win_ref[:, a5 + ci * gw:a5 + (ci + 1) * gw]) + bg_ref[:, cols]))
        for unit in units[ci::n_chunks]:
            conv_unit(*unit)
    g_attn = jnp.concatenate(gates[:n_chunks // 2], axis=1)
    g_conv = jnp.concatenate(gates[n_chunks // 2:], axis=1)

    def attend(r0, k0):
        if not cfg.has_cache:
            col = lax.broadcasted_iota(I32, (1, cfg.kb), 1)
            valid = jnp.logical_or(col >= PREV - k0, t > 0)
        heads = []
        for h in range(N_HEADS):
            c0, c1 = h * HEAD_DIM, (h + 1) * HEAD_DIM
            s = lax.dot_general(qbuf[pl.ds(r0, cfg.qb), c0:c1], kbuf[pl.ds(k0, cfg.kb), c0:c1],
                                (((1,), (1,)), ((), ())), preferred_element_type=F32)
            s = s + bias_ref[h]
            if not cfg.has_cache:
                s = jnp.where(valid, s, NEG)
            p = jnp.exp2(s - jnp.max(s, axis=-1, keepdims=True))
            l = jnp.sum(p, axis=-1, keepdims=True)
            heads.append(_dot(p.astype(BF16), vbuf[pl.ds(k0, cfg.kb), c0:c1]) * (1.0 / l))
        obuf[pl.ds(r0, cfg.qb), :] = jnp.concatenate(heads, axis=1).astype(BF16)

    if cfg.n_qblk == 1:
        attend(0, 0)
    else:
        def attend_step(j, carry):
            attend(pl.multiple_of(j * cfg.qb, cfg.qb), pl.multiple_of(j * cfg.kstride, cfg.kstride))
            return carry
        lax.fori_loop(0, cfg.n_qblk, attend_step, 0)
    y_attn = _dot(obuf[...], wao_ref[...])
    if not cfg.has_cache:
        kbuf[0:PREV, :] = kbuf[tt:tt + PREV, :]
        vbuf[0:PREV, :] = vbuf[tt:tt + PREV, :]

    if not cfg.has_cache:
        uext[0:CONV_PAD, :] = uext[tt:tt + CONV_PAD, :]
    cv = cvbuf[...]
    mu = jnp.mean(cv, axis=-1, keepdims=True)
    xc = cv - mu
    var = jnp.mean(xc * xc, axis=-1, keepdims=True)
    cv = xc * lax.rsqrt(var + EPS) * lng_ref[...] + lnb_ref[...]
    cv = cv * _sigmoid_t(cv)
    y_conv = _dot(cv.astype(BF16), wco_ref[...]) + bco_ref[...]

    m = g_attn * y_attn + g_conv * y_conv
    x1 = x + g1 * _dot(m.astype(BF16), wout_ref[...])
    x1_ref[0] = x1

    h2 = _rms(x1, n2g_ref[...]) * (1.0 + sc2) + sh2
    h2_ref[0] = _pack(h2[:, 0:d_model // 2], h2[:, d_model // 2:d_model])
    h2_hi, h2_lo = _split_bf16(h2)
    nt_dims = (((1,), (1,)), ((), ()))
    logits = (lax.dot_general(wrh_ref[...], h2_hi, nt_dims, preferred_element_type=F32)
              + lax.dot_general(wrl_ref[...], h2_hi, nt_dims, preferred_element_type=F32)
              + lax.dot_general(wrh_ref[...], h2_lo, nt_dims, preferred_element_type=F32))
    s = _sigmoid(logits)
    eids, weights = _route(s + rb_ref[...], s)
    ids = _expert_ids(tt)
    if cfg.sparse:
        sel = [functools.reduce(jnp.add, [jnp.where(ids[g] == e, 1.0, 0.0) for e in eids]) for g in range(N_GROUPS)]
        sel = jnp.concatenate(sel, axis=0)
        rank = _dot(sel.astype(BF16), tri_ref[...])
        rank_slabs = [rank[g * GROUP_SIZE:(g + 1) * GROUP_SIZE, :] for g in range(N_GROUPS)]
        tok_ref[0] = _lane_dense_rows(weights, tt)
        eid_ref[...] = jnp.concatenate(eids, axis=0)
        rank_ref[...] = jnp.concatenate([_pick(ids, e, rank_slabs) for e in eids], axis=0)
        step = pl.program_id(0) * pl.num_programs(1) + t

        @pl.when(step == 0)
        def _():
            cnt_ref[...] = jnp.zeros(cnt_ref.shape, F32)

        lane = lax.broadcasted_iota(I32, cnt_ref.shape, 1)
        cnt_ref[...] = jnp.where(lane == step, jnp.sum(sel, axis=1, keepdims=True), cnt_ref[...])
    else:
        gates = [functools.reduce(jnp.add, [jnp.where(ids[g] == e, w, 0.0) for e, w in zip(eids, weights)])
                 for g in range(N_GROUPS)]
        gates = jnp.concatenate(gates + [jnp.zeros((LANES - N_EXPERTS, tt), F32)], axis=0)
        tok_ref[0] = gates.T


def _toeplitz(v, rows, cols):
    w = rows + cols
    flat = jnp.tile(v, (1, rows))[:, :rows * (w - 1)]
    return flat.reshape(v.shape[0], rows, w - 1)[:, :, :cols]


def _rel_bias_blocks(table, cfg, q_pos0, k_pos0):
    q_pos = q_pos0 + jnp.arange(cfg.qb)
    k_pos = k_pos0 + jnp.arange(cfg.kb)
    qc = q_pos // CHUNK
    kc = k_pos // CHUNK
    allowed = (kc[None, :] <= qc[:, None]) & (kc[None, :] >= qc[:, None] - BAND_CHUNKS)
    w = cfg.qb + cfg.kb
    dj = jnp.arange(w)
    dj = jnp.where(dj < cfg.kb, dj, dj - w)
    rel = jnp.clip(q_pos0 - k_pos0 - dj, -MAX_REL, MAX_REL) + MAX_REL
    bias = _toeplitz(table[:, rel].astype(F32), cfg.qb, cfg.kb)
    return jnp.where(allowed[None], bias * LOG2_E, NEG)


def _mixer_cost(cfg, tokens, d, n_in, conv_dim):
    per_token = (2 * d * n_in + 2 * (ATTN_DIM + conv_dim + d) * d + 4 * ATTN_DIM * cfg.kb
                 + 4 * ATTN_DIM * ATTN_DIM + 6 * N_EXPERTS * d + 2 * CONV_WIDTH * conv_dim)
    return pl.CostEstimate(flops=tokens * per_token,
                           transcendentals=tokens * (N_HEADS * cfg.kb + 2 * d + 2 * conv_dim + N_EXPERTS),
                           bytes_accessed=tokens * (4 * d + 4 * d + 2 * d + 4 * LANES) + 2 * d * n_in)


def _mixer_call(cfg, x, mod, p, cache=None, batch0=0, nbatch=None, state=None):
    b_all, t_total, d = x.shape
    nb = b_all if nbatch is None else nbatch
    nt = t_total // cfg.tt
    conv_dim = p["conv_w"].shape[1]
    n_in = p["w_in"].shape[1]

    def const(shape):
        return pl.BlockSpec(shape, lambda i, j: (0,) * len(shape), pipeline_mode=pl.Buffered(1))

    def per_row(shape):
        return pl.BlockSpec(shape, lambda i, j: (i + batch0, 0, 0))

    in_specs = [
        pl.BlockSpec((1, cfg.tt, d), lambda i, j: (i + batch0, j, 0)),
        per_row((1, 6, d)),
        const((1, d)), const((1, d)), const((d, n_in)),
        const((1, ATTN_DIM)), const((1, ATTN_DIM)), const((ATTN_DIM, ATTN_DIM)),
        const((N_HEADS, cfg.qb, cfg.kb)), const((ATTN_DIM, d)),
        const((CONV_WIDTH, conv_dim)), const((1, conv_dim)), const((1, conv_dim)), const((1, conv_dim)),
        const((conv_dim, d)), const((1, d)), const((1, 2 * d)), const((d, d)),
        const((N_EXPERTS, d)), const((N_EXPERTS, d)), const((N_EXPERTS, 1)),
    ]
    args = [x, mod, p["norm1_g"], p["norm2_g"], p["w_in"], p["q_norm_g"], p["k_norm_g"], p["hsum"],
            p["bias"], p["w_attn_o"], p["conv_w"], p["conv_b"], p["conv_ln_g"], p["conv_ln_b"],
            p["w_conv_o"], p["b_conv_o"], p["b_gate"], p["w_out"], p["wr_hi"], p["wr_lo"], p["router_bias"]]
    if cfg.has_cache:
        in_specs += [per_row((1, PREV, ATTN_DIM)), per_row((1, PREV, ATTN_DIM)),
                     per_row((1, CONV_WIDTH - 1, conv_dim))]
        args += list(cache)
    if cfg.sparse:
        assert nb * nt <= MAX_TILES
        in_specs += [const((cfg.tt, cfg.tt))]
        tok = jnp.arange(cfg.tt)
        args += [(tok[:, None] < tok[None, :]).astype(BF16)]
    aliases = {}
    if state is not None:
        for i, arr in enumerate(state):
            aliases[len(args)] = 3 + i
            in_specs.append(pl.BlockSpec(memory_space=pl.ANY))
            args.append(arr)
    keep = min(PREV, t_total)
    assert keep == cfg.tt
    out_shape = [jax.ShapeDtypeStruct((nb, t_total, d), F32),
                 jax.ShapeDtypeStruct((nb, t_total, d // 2), I32),
                 jax.ShapeDtypeStruct((nb, t_total, LANES), F32),
                 jax.ShapeDtypeStruct((b_all, keep, ATTN_DIM), F32),
                 jax.ShapeDtypeStruct((b_all, keep, ATTN_DIM), F32),
                 jax.ShapeDtypeStruct((b_all, CONV_WIDTH - 1, conv_dim), F32)]
    out_specs = [pl.BlockSpec((1, cfg.tt, d), lambda i, j: (i, j, 0)),
                 pl.BlockSpec((1, cfg.tt, d // 2), lambda i, j: (i, j, 0)),
                 pl.BlockSpec((1, cfg.tt, LANES), lambda i, j: (i, j, 0)),
                 per_row((1, keep, ATTN_DIM)), per_row((1, keep, ATTN_DIM)),
                 per_row((1, CONV_WIDTH - 1, conv_dim))]
    if cfg.sparse:
        out_shape += [jax.ShapeDtypeStruct((TOP_K, nb * t_total), F32),
                      jax.ShapeDtypeStruct((TOP_K, nb * t_total), F32),
                      jax.ShapeDtypeStruct((N_EXPERTS, MAX_TILES), F32)]
        out_specs += [pl.BlockSpec((TOP_K, cfg.tt), lambda i, j: (0, i * nt + j)),
                      pl.BlockSpec((TOP_K, cfg.tt), lambda i, j: (0, i * nt + j)),
                      pl.BlockSpec((N_EXPERTS, MAX_TILES), lambda i, j: (0, 0))]
    scratch = [pltpu.VMEM((cfg.tt, ATTN_DIM), BF16),
               pltpu.VMEM((PREV + cfg.tt, ATTN_DIM), BF16),
               pltpu.VMEM((PREV + cfg.tt, ATTN_DIM), BF16),
               pltpu.VMEM((cfg.tt, ATTN_DIM), BF16),
               pltpu.VMEM((CONV_PAD + cfg.tt, conv_dim), F32),
               pltpu.VMEM((cfg.tt, conv_dim), F32)]
    return pl.pallas_call(
        functools.partial(_mixer_kernel, cfg, 0 if state is None else len(state)),
        out_shape=out_shape,
        grid=(nb, nt),
        in_specs=in_specs,
        out_specs=out_specs,
        scratch_shapes=scratch,
        input_output_aliases=aliases,
        cost_estimate=_mixer_cost(cfg, nb * t_total, d, n_in, conv_dim),
        compiler_params=pltpu.CompilerParams(dimension_semantics=("arbitrary", "arbitrary"),
                                             vmem_limit_bytes=VMEM_LIMIT_BYTES),
        name="mixer_sample" if cfg.has_cache else "mixer_prompt",
    )(*args)


def _exact_parts(x, n):
    parts = []
    for _ in range(n):
        part = x.astype(BF16)
        parts.append(part)
        x = x - part.astype(F32)
    return parts


def _plan_kernel(tt, cnt_ref, eid_ref, rank_ref, tri_e_ref, tri_t_ref, pos_ref, blk_ref, nblk_ref, base_ref):
    step = pl.program_id(0)

    @pl.when(step == 0)
    def _():
        cnt = cnt_ref[...]
        total = jnp.sum(cnt, axis=1, keepdims=True)
        padded = jnp.floor((total + float(ROW_BLOCK - 1)) * (1.0 / ROW_BLOCK)) * float(ROW_BLOCK)
        padded = jnp.broadcast_to(padded, cnt.shape)
        start = functools.reduce(jnp.add, [_dot(tri_e_ref[...], part) for part in _exact_parts(padded, 3)])
        before = functools.reduce(jnp.add, [_dot(part, tri_t_ref[...]) for part in _exact_parts(cnt, 2)])
        base_ref[...] = start + before
        end = (start + padded)[:, 0:1]
        first_row = lax.broadcasted_iota(I32, blk_ref.shape, 1).astype(F32) * float(ROW_BLOCK)
        owner = jnp.sum(jnp.where(end <= first_row, 1.0, 0.0), axis=0, keepdims=True)
        blk_ref[...] = jnp.minimum(owner, float(N_EXPERTS - 1)).astype(I32)
        nblk = jnp.max(end, axis=0, keepdims=True) * (1.0 / ROW_BLOCK)
        nblk_ref[...] = jnp.broadcast_to(nblk, nblk_ref.shape).astype(I32)

    lane = lax.broadcasted_iota(I32, base_ref.shape, 1)
    col = jnp.sum(jnp.where(lane == step, base_ref[...], 0.0), axis=1, keepdims=True)
    ids = _expert_ids(tt)
    col_slabs = [jnp.broadcast_to(col[g * GROUP_SIZE:(g + 1) * GROUP_SIZE, :], (GROUP_SIZE, tt))
                 for g in range(N_GROUPS)]
    eid = eid_ref[...]
    rows = [_pick(ids, eid[k:k + 1, :], col_slabs) for k in range(TOP_K)]
    pos_ref[...] = (jnp.concatenate(rows, axis=0) + rank_ref[...]).astype(I32)


def _plan_call(cnt, eid, rank, tt, n_blocks):
    n = eid.shape[1]
    e = jnp.arange(N_EXPERTS)
    s = jnp.arange(MAX_TILES)
    nb_pad = -(-n_blocks // LANES) * LANES
    return pl.pallas_call(
        functools.partial(_plan_kernel, tt),
        out_shape=[jax.ShapeDtypeStruct((TOP_K, n), I32),
                   jax.ShapeDtypeStruct((1, nb_pad), I32),
                   jax.ShapeDtypeStruct((1, LANES), I32)],
        grid=(n // tt,),
        in_specs=[pl.BlockSpec((N_EXPERTS, MAX_TILES), lambda i: (0, 0)),
                  pl.BlockSpec((TOP_K, tt), lambda i: (0, i)),
                  pl.BlockSpec((TOP_K, tt), lambda i: (0, i)),
                  pl.BlockSpec((N_EXPERTS, N_EXPERTS), lambda i: (0, 0)),
                  pl.BlockSpec((MAX_TILES, MAX_TILES), lambda i: (0, 0))],
        out_specs=[pl.BlockSpec((TOP_K, tt), lambda i: (0, i)),
                   pl.BlockSpec((1, nb_pad), lambda i: (0, 0)),
                   pl.BlockSpec((1, LANES), lambda i: (0, 0))],
        scratch_shapes=[pltpu.VMEM((N_EXPERTS, MAX_TILES), F32)],
        compiler_params=pltpu.CompilerParams(dimension_semantics=("arbitrary",),
                                             vmem_limit_bytes=VMEM_LIMIT_BYTES),
        name="plan",
    )(cnt, eid, rank, (e[None, :] < e[:, None]).astype(BF16), (s[:, None] < s[None, :]).astype(BF16))


def _sc_mesh():
    return plsc.VectorSubcoreMesh(core_axis_name="c", subcore_axis_name="s")


def _dispatch_call(rows, pos, n_sorted):
    n, w = rows.shape
    per_worker = n // SC_WORKERS
    nsteps = per_worker // SC_ROWS
    assert per_worker * SC_WORKERS == n and nsteps * SC_ROWS == per_worker and nsteps % 2 == 0

    @functools.partial(
        pl.kernel, mesh=_sc_mesh(),
        out_type=jax.ShapeDtypeStruct((n_sorted, w), rows.dtype),
        scratch_types=[pltpu.VMEM((2, TOP_K, SC_ROWS), I32),
                       pltpu.VMEM((2, SC_ROWS, w), rows.dtype),
                       pltpu.SemaphoreType.DMA((2,)),
                       pltpu.SemaphoreType.DMA((2,))],
        cost_estimate=pl.CostEstimate(flops=0, transcendentals=0,
                                      bytes_accessed=(1 + TOP_K) * n * w * rows.dtype.itemsize + 4 * TOP_K * n),
    )
    def dispatch(x_hbm, pos_hbm, out_hbm, idx_v, rows_v, load_sem, scat_sem):
        base = (lax.axis_index("s") * SC_CORES + lax.axis_index("c")) * per_worker

        def load(i, b):
            return pltpu.make_async_copy(x_hbm.at[pl.ds(base + i * SC_ROWS, SC_ROWS)], rows_v.at[b], load_sem.at[b])

        def scatter(b, k):
            return pltpu.make_async_copy(rows_v.at[b], out_hbm.at[idx_v.at[b, k]], scat_sem.at[b])

        def load_start(i, b):
            for k in range(TOP_K):
                pltpu.sync_copy(pos_hbm.at[pl.ds(k * n + base + i * SC_ROWS, SC_ROWS)], idx_v.at[b, k])
            load(i, b).start()

        load_start(0, 0)

        @pl.loop(0, nsteps, step=2)
        def _(i):
            for b in range(2):
                ii = i + b

                @pl.when(ii >= 1)
                def _():
                    for k in range(TOP_K):
                        scatter(1 - b, k).wait()

                @pl.when(ii + 1 < nsteps)
                def _():
                    load_start(ii + 1, 1 - b)

                load(ii, b).wait()
                for k in range(TOP_K):
                    scatter(b, k).start()

        for k in range(TOP_K):
            scatter((nsteps - 1) % 2, k).wait()

    return dispatch(rows, pos)


def _collect_call(table, idx):
    n = idx.shape[0]
    w = table.shape[1]
    per_worker = n // SC_WORKERS
    nsteps = per_worker // SC_ROWS
    assert per_worker * SC_WORKERS == n and nsteps * SC_ROWS == per_worker and nsteps % 2 == 0

    @functools.partial(
        pl.kernel, mesh=_sc_mesh(),
        out_type=jax.ShapeDtypeStruct((n, w), table.dtype),
        scratch_types=[pltpu.VMEM((2, SC_ROWS), I32),
                       pltpu.VMEM((2, SC_ROWS, w), table.dtype),
                       pltpu.SemaphoreType.DMA((2,)),
                       pltpu.SemaphoreType.DMA((2,))],
        cost_estimate=pl.CostEstimate(flops=0, transcendentals=0,
                                      bytes_accessed=2 * n * w * table.dtype.itemsize + 4 * n),
    )
    def collect(table_hbm, idx_hbm, out_hbm, idx_v, rows_v, gather_sem, write_sem):
        base = (lax.axis_index("s") * SC_CORES + lax.axis_index("c")) * per_worker

        def gather(b):
            return pltpu.make_async_copy(table_hbm.at[idx_v.at[b]], rows_v.at[b], gather_sem.at[b])

        def write(i, b):
            return pltpu.make_async_copy(rows_v.at[b], out_hbm.at[pl.ds(base + i * SC_ROWS, SC_ROWS)], write_sem.at[b])

        def gather_start(i, b):
            pltpu.sync_copy(idx_hbm.at[pl.ds(base + i * SC_ROWS, SC_ROWS)], idx_v.at[b])
            gather(b).start()

        gather_start(0, 0)

        @pl.loop(0, nsteps, step=2)
        def _(i):
            for b in range(2):
                ii = i + b

                @pl.when(ii >= 1)
                def _():
                    write(ii - 1, 1 - b).wait()

                @pl.when(ii + 1 < nsteps)
                def _():
                    gather_start(ii + 1, 1 - b)

                gather(b).wait()
                write(ii, b).start()

        write(nsteps - 1, (nsteps - 1) % 2).wait()

    return collect(table, idx)


def _expert_ffn_kernel(blk_ref, nblk_ref, x_ref, wg_ref, wu_ref, wd_ref, o_ref, wg_s, wu_s, wd_s):
    j = pl.program_id(0)

    @pl.when(jnp.logical_or(j == 0, blk_ref[j] != blk_ref[jnp.maximum(j - 1, 0)]))
    def _():
        wg_s[...] = wg_ref[0].astype(BF16)
        wu_s[...] = wu_ref[0].astype(BF16)
        wd_s[...] = wd_ref[0].astype(BF16)

    @pl.when(j < nblk_ref[0])
    def _():
        a, b = _unpack(x_ref[...])
        a, b = a.astype(BF16), b.astype(BF16)
        hg = _dot_halves(a, b, wg_s)
        act = hg * _sigmoid_t(hg) * _dot_halves(a, b, wu_s)
        y = _dot(act.astype(BF16), wd_s[...])
        half = y.shape[1] // 2
        o_ref[...] = _pack(y[:, 0:half], y[:, half:2 * half])


def _expert_ffn_call(xs, blk, nblk, p):
    n_sorted, w = xs.shape
    d = 2 * w
    ff = p["w_e_gate"].shape[2]

    def rows(j, blk, nblk):
        return (jnp.minimum(j, nblk[0] - 1), 0)

    def expert(j, blk, nblk):
        return (blk[j], 0, 0)

    return pl.pallas_call(
        _expert_ffn_kernel,
        out_shape=jax.ShapeDtypeStruct((n_sorted, w), I32),
        grid_spec=pltpu.PrefetchScalarGridSpec(
            num_scalar_prefetch=2,
            grid=(n_sorted // ROW_BLOCK,),
            in_specs=[pl.BlockSpec((ROW_BLOCK, w), rows),
                      pl.BlockSpec((1, d, ff), expert),
                      pl.BlockSpec((1, d, ff), expert),
                      pl.BlockSpec((1, ff, d), expert)],
            out_specs=pl.BlockSpec((ROW_BLOCK, w), rows),
            scratch_shapes=[pltpu.VMEM((d, ff), BF16), pltpu.VMEM((d, ff), BF16), pltpu.VMEM((ff, d), BF16)]),
        cost_estimate=pl.CostEstimate(flops=6 * n_sorted * d * ff, transcendentals=n_sorted * ff,
                                      bytes_accessed=8 * n_sorted * w + 12 * N_EXPERTS * d * ff),
        compiler_params=pltpu.CompilerParams(dimension_semantics=("arbitrary",),
                                             vmem_limit_bytes=VMEM_LIMIT_BYTES),
        name="expert_ffn",
    )(blk, nblk, xs, p["w_e_gate"], p["w_e_up"], p["w_e_down"])


def _swiglu_halves(a, b, wg_ref, wu_ref):
    hg = _dot_halves(a, b, wg_ref)
    return hg * _sigmoid_t(hg) * _dot_halves(a, b, wu_ref)


def _moe_out_kernel(n_groups, tiles, *refs):
    groups = [refs[4 * q:4 * q + 4] for q in range(n_groups)]
    g2_ref, wsg_ref, wsu_ref, wsd_ref, o_ref = refs[4 * n_groups:]

    def combine(x1_ref, h_ref, y_ref, w_ref):
        a, b = _unpack(h_ref[...])
        shared = _dot(_swiglu_halves(a.astype(BF16), b.astype(BF16), wsg_ref, wsu_ref).astype(BF16),
                      wsd_ref[...].astype(BF16))
        half = h_ref.shape[1]
        w = w_ref[...]
        acc_a, acc_b = shared[:, 0:half], shared[:, half:2 * half]
        for k in range(TOP_K):
            ya, yb = _unpack(y_ref[k])
            acc_a = acc_a + w[:, k:k + 1] * ya
            acc_b = acc_b + w[:, k:k + 1] * yb
        g2 = g2_ref[...]
        o_ref[:, 0:half] = x1_ref[:, 0:half] + g2[:, 0:half] * acc_a
        o_ref[:, half:2 * half] = x1_ref[:, half:2 * half] + g2[:, half:2 * half] * acc_b

    for q in range(n_groups):
        pl.when(pl.program_id(0) // tiles == q)(functools.partial(combine, *groups[q]))


def _moe_out_call(groups, g2, p, tm):
    n, d = groups[0][0].shape
    n_all = n * len(groups)
    sff = p["w_s_gate"].shape[1]
    per_row = n_all // g2.shape[0]
    g2 = g2.reshape(g2.shape[0], 1, d)
    tiles = n // tm

    def const(shape):
        return pl.BlockSpec(shape, lambda i: (0,) * len(shape), pipeline_mode=pl.Buffered(1))

    in_specs, args = [], []
    for q, group in enumerate(groups):
        tile = lambda i, q=q: jnp.clip(i - q * tiles, 0, tiles - 1)
        in_specs += [pl.BlockSpec((tm, d), lambda i, tile=tile: (tile(i), 0)),
                     pl.BlockSpec((tm, d // 2), lambda i, tile=tile: (tile(i), 0)),
                     pl.BlockSpec((TOP_K, tm, d // 2), lambda i, tile=tile: (0, tile(i), 0)),
                     pl.BlockSpec((tm, LANES), lambda i, tile=tile: (tile(i), 0))]
        args += list(group)
    in_specs += [pl.BlockSpec((None, 1, d), lambda i: (i * tm // per_row, 0, 0)),
                 const((d, sff)), const((d, sff)), const((sff, d))]
    args += [g2, p["w_s_gate"], p["w_s_up"], p["w_s_down"]]
    return pl.pallas_call(
        functools.partial(_moe_out_kernel, len(groups), tiles),
        out_shape=jax.ShapeDtypeStruct((n_all, d), F32),
        grid=(n_all // tm,),
        in_specs=in_specs,
        out_specs=pl.BlockSpec((tm, d), lambda i: (i, 0)),
        cost_estimate=pl.CostEstimate(flops=n_all * (6 * d * sff + 2 * TOP_K * d), transcendentals=n_all * sff,
                                      bytes_accessed=n_all * (8 * d + 2 * d + 2 * TOP_K * d + 4 * LANES)),
        compiler_params=pltpu.CompilerParams(dimension_semantics=("arbitrary",),
                                             vmem_limit_bytes=VMEM_LIMIT_BYTES),
        name="moe_out",
    )(*args)


def _moe_dense_kernel(h_ref, gates_ref, x1_ref, g2_ref, wsg_ref, wsu_ref, wsd_ref, wg_ref, wu_ref, wd_ref,
                      o_ref, acc_ref):
    e = pl.program_id(1)
    a, b = _unpack(h_ref[...])
    a, b = a.astype(BF16), b.astype(BF16)

    @pl.when(e == 0)
    def _():
        acc_ref[...] = _dot(_swiglu_halves(a, b, wsg_ref, wsu_ref).astype(BF16), wsd_ref[...].astype(BF16))

    gates = gates_ref[...]
    lane = lax.broadcasted_iota(I32, gates.shape, 1)
    gate = jnp.sum(jnp.where(lane == e, gates, 0.0), axis=1, keepdims=True)
    act = _swiglu_halves(a, b, wg_ref.at[0], wu_ref.at[0]) * gate
    acc_ref[...] += _dot(act.astype(BF16), wd_ref[0].astype(BF16))

    @pl.when(e == pl.num_programs(1) - 1)
    def _():
        o_ref[...] = x1_ref[...] + g2_ref[...] * acc_ref[...]


def _moe_dense_call(h2, gates, x1, g2_rows, p):
    n, d = x1.shape
    ff = p["w_e_gate"].shape[2]
    sff = p["w_s_gate"].shape[1]

    def const(shape):
        return pl.BlockSpec(shape, lambda i, e: (0,) * len(shape), pipeline_mode=pl.Buffered(1))

    return pl.pallas_call(
        _moe_dense_kernel,
        out_shape=jax.ShapeDtypeStruct((n, d), F32),
        grid=(1, N_EXPERTS),
        in_specs=[const((n, d // 2)), const((n, LANES)), const((n, d)), const((n, d)),
                  const((d, sff)), const((d, sff)), const((sff, d)),
                  pl.BlockSpec((1, d, ff), lambda i, e: (e, 0, 0)),
                  pl.BlockSpec((1, d, ff), lambda i, e: (e, 0, 0)),
                  pl.BlockSpec((1, ff, d), lambda i, e: (e, 0, 0))],
        out_specs=pl.BlockSpec((n, d), lambda i, e: (0, 0)),
        scratch_shapes=[pltpu.VMEM((n, d), F32)],
        compiler_params=pltpu.CompilerParams(dimension_semantics=("arbitrary", "arbitrary"),
                                             vmem_limit_bytes=VMEM_LIMIT_BYTES),
        name="moe_dense",
    )(h2, gates, x1, g2_rows, p["w_s_gate"], p["w_s_up"], p["w_s_down"], p["w_e_gate"], p["w_e_up"], p["w_e_down"])


PROMPT_CFG = MixerCfg(tt=512, qb=256, kb=768, n_qblk=2, kstride=256, has_cache=False, sparse=True)
MOE_OUT_TILE = 256
PROMPT_GROUPS = 2


def _layer_params(l, w_in, q_norm_g, k_norm_g, w_attn_o, conv_w, conv_b, conv_ln_g, conv_ln_b, w_conv_o,
                  b_conv_o, b_gate, w_out, norm1_g, norm2_g, w_router, router_bias,
                  w_e_gate, w_e_up, w_e_down, w_s_gate, w_s_up, w_s_down):
    row = lambda a: a[l].reshape(1, -1)
    wr_t = w_router[l].T
    wr_hi = wr_t.astype(BF16)
    head = jnp.arange(ATTN_DIM) // HEAD_DIM
    return {
        "norm1_g": row(norm1_g), "norm2_g": row(norm2_g), "w_in": w_in[l].astype(BF16),
        "q_norm_g": jnp.tile(q_norm_g[l], N_HEADS).reshape(1, -1),
        "k_norm_g": jnp.tile(k_norm_g[l], N_HEADS).reshape(1, -1),
        "hsum": (head[:, None] == head[None, :]).astype(BF16),
        "w_attn_o": w_attn_o[l].astype(BF16), "conv_w": conv_w[l], "conv_b": row(conv_b),
        "conv_ln_g": row(conv_ln_g), "conv_ln_b": row(conv_ln_b), "w_conv_o": w_conv_o[l].astype(BF16),
        "b_conv_o": row(b_conv_o), "b_gate": row(b_gate), "w_out": w_out[l].astype(BF16),
        "wr_hi": wr_hi, "wr_lo": (wr_t - wr_hi.astype(F32)).astype(BF16),
        "router_bias": router_bias[l].reshape(-1, 1),
        "w_e_gate": w_e_gate[l], "w_e_up": w_e_up[l], "w_e_down": w_e_down[l],
        "w_s_gate": w_s_gate[l], "w_s_up": w_s_up[l], "w_s_down": w_s_down[l],
    }


def _sorted_rows(h2, eid, rank, cnt, tt):
    n = h2.shape[0]
    n_sorted = n * TOP_K + N_EXPERTS * ROW_BLOCK
    pos, blk, nblk = _plan_call(cnt, eid, rank, tt, n_sorted // ROW_BLOCK)
    pos = pos.reshape(TOP_K * n)
    return _dispatch_call(h2, pos, n_sorted), pos, blk.reshape(-1), nblk.reshape(-1)


def kernel(x_prompt, x_sample, c_prompt, c_sample, cache_k, cache_v, state_conv, w_mod, b_mod, norm1_g, w_in, q_norm_g, k_norm_g, rel_bias, w_attn_o, conv_w, conv_b, conv_ln_g, conv_ln_b, w_conv_o, b_conv_o, b_gate, w_out, norm2_g, w_router, router_bias, w_e_gate, w_e_up, w_e_down, w_s_gate, w_s_up, w_s_down):
    depth = w_mod.shape[0]
    bp, tp, d = x_prompt.shape
    bs, ts, _ = x_sample.shape
    assert cache_k.shape[2] == min(PREV, PAST_LEN) == PREV
    sample_cfg = MixerCfg(tt=ts, qb=ts, kb=PREV + ts, n_qblk=1, kstride=0, has_cache=True, sparse=False)

    yp, ys = x_prompt, x_sample
    outs = [[] for _ in range(6)]
    for l in range(depth):
        p = _layer_params(l, w_in, q_norm_g, k_norm_g, w_attn_o, conv_w, conv_b, conv_ln_g, conv_ln_b,
                          w_conv_o, b_conv_o, b_gate, w_out, norm1_g, norm2_g, w_router, router_bias,
                          w_e_gate, w_e_up, w_e_down, w_s_gate, w_s_up, w_s_down)
        mod = _mod_call(jnp.concatenate([c_prompt, c_sample], axis=0), w_mod[l], b_mod[l])
        mod = mod.reshape(bp + bs, 6, d)
        mod_p, mod_s = mod[:bp], mod[bp:]

        p["bias"] = _rel_bias_blocks(rel_bias[l], PROMPT_CFG, PREV, 0)
        gb = bp // PROMPT_GROUPS
        keep = min(PREV, tp)
        state = [jnp.zeros((bp, keep, ATTN_DIM), F32), jnp.zeros((bp, keep, ATTN_DIM), F32),
                 jnp.zeros((bp, CONV_WIDTH - 1, conv_w.shape[2]), F32)]
        mixed, moved = [], []
        for g in range(PROMPT_GROUPS):
            x1, h2, w_tok, *state, eid, rank, cnt = _mixer_call(PROMPT_CFG, yp, mod_p, p, batch0=g * gb, nbatch=gb,
                                                                state=state)
            x1, h2, w_tok = x1.reshape(gb * tp, d), h2.reshape(gb * tp, d // 2), w_tok.reshape(gb * tp, LANES)
            mixed.append((x1, h2, w_tok))
            moved.append(_sorted_rows(h2, eid, rank, cnt, PROMPT_CFG.tt))
            state, _ = lax.optimization_barrier((state, moved[-1][1]))
        kp, vp, cp = state
        groups, gate = [], cp
        for (x1, h2, w_tok), (xs, pos, blk, nblk) in zip(mixed, moved):
            xs, _ = lax.optimization_barrier((xs, gate))
            gate = _expert_ffn_call(xs, blk, nblk, p)
            yg = _collect_call(gate, pos).reshape(TOP_K, h2.shape[0], h2.shape[1])
            groups.append((x1, h2, yg, w_tok))
        yp = _moe_out_call(groups, mod_p[:, 5, :], p, MOE_OUT_TILE).reshape(bp, tp, d)
        outs[0].append(kp.reshape(bp, -1, N_HEADS, HEAD_DIM))
        outs[1].append(vp.reshape(bp, -1, N_HEADS, HEAD_DIM))
        outs[2].append(cp)

        p["bias"] = _rel_bias_blocks(rel_bias[l], sample_cfg, PAST_LEN, PAST_LEN - PREV)
        cache = (cache_k[l].reshape(bs, PREV, ATTN_DIM), cache_v[l].reshape(bs, PREV, ATTN_DIM), state_conv[l])
        x1, h2, gates, ks, vs, cs = _mixer_call(sample_cfg, ys, mod_s, p, cache)
        ys = _moe_dense_call(h2.reshape(bs * ts, d // 2), gates.reshape(bs * ts, LANES), x1.reshape(bs * ts, d),
                             jnp.repeat(mod_s[:, 5, :], ts, axis=0), p).reshape(bs, ts, d)
        outs[3].append(ks.reshape(bs, ts, N_HEADS, HEAD_DIM))
        outs[4].append(vs.reshape(bs, ts, N_HEADS, HEAD_DIM))
        outs[5].append(cs)
    return (yp, ys) + tuple(jnp.stack(o) for o in outs)
```

```python
import dataclasses
import functools

import jax
import jax.numpy as jnp
from jax import lax
from jax.experimental import pallas as pl
from jax.experimental.pallas import tpu as pltpu
from jax.experimental.pallas import tpu_sc as plsc

F32 = jnp.float32
BF16 = jnp.bfloat16
I32 = jnp.int32

CHUNK = 64
BAND_CHUNKS = 8
PREV = BAND_CHUNKS * CHUNK
PAST_LEN = 1024
N_HEADS = 8
HEAD_DIM = 64
ATTN_DIM = N_HEADS * HEAD_DIM
MAX_REL = 128
CONV_WIDTH = 31
CONV_PAD = 32
N_EXPERTS = 64
N_GROUPS = 8
GROUP_SIZE = N_EXPERTS // N_GROUPS
TOPK_GROUPS = 4
TOP_K = 8
ROUTE_SCALE = 2.5
EPS = 1e-6
LANES = 128
SUBLANES = 8
NEG = -1e30
LOG2_E = 1.4426950408889634

VMEM_LIMIT_BYTES = 60 * 1024 * 1024

SC_CORES = 2
SC_SUBCORES = 16
SC_WORKERS = SC_CORES * SC_SUBCORES
SC_ROWS = 64

MAX_TILES = LANES
ROW_BLOCK = 1024


def _sigmoid(x):
    return 1.0 / (1.0 + jnp.exp(-x))


def _sigmoid_t(x):
    return 0.5 * jnp.tanh(0.5 * x) + 0.5


def _split_bf16(x):
    hi = x.astype(BF16)
    lo = (x - hi.astype(F32)).astype(BF16)
    return hi, lo


def _dot(a, b):
    return jnp.dot(a, b, preferred_element_type=F32)


def _dot3(a, b):
    a_hi, a_lo = _split_bf16(a)
    b_hi, b_lo = _split_bf16(b)
    return _dot(a_hi, b_hi) + _dot(a_hi, b_lo) + _dot(a_lo, b_hi)


def _pack(a, b):
    ia = lax.bitcast_convert_type(a.astype(BF16).astype(F32), I32)
    ib = lax.bitcast_convert_type(b.astype(BF16).astype(F32), I32)
    return ia | lax.shift_right_logical(ib, 16)


def _unpack(p):
    a = lax.bitcast_convert_type(p & jnp.int32(-65536), F32)
    b = lax.bitcast_convert_type(lax.shift_left(p, 16), F32)
    return a, b


def _dot_halves(a, b, w_ref):
    half = a.shape[1]
    return _dot(a, w_ref[0:half, :].astype(BF16)) + _dot(b, w_ref[half:2 * half, :].astype(BF16))


def _mod_kernel(c_ref, w_ref, b_ref, o_ref):
    c = c_ref[...]
    o_ref[...] = _dot3(c * _sigmoid(c), w_ref[...]) + b_ref[...]


def _mod_call(c, w_mod, b_mod):
    n, d = c.shape
    dout = w_mod.shape[1]
    bn = 1536
    return pl.pallas_call(
        _mod_kernel,
        out_shape=jax.ShapeDtypeStruct((n, dout), F32),
        grid=(dout // bn,),
        in_specs=[pl.BlockSpec((n, d), lambda j: (0, 0)),
                  pl.BlockSpec((d, bn), lambda j: (0, j)),
                  pl.BlockSpec((1, bn), lambda j: (0, j))],
        out_specs=pl.BlockSpec((n, bn), lambda j: (0, j)),
        compiler_params=pltpu.CompilerParams(dimension_semantics=("arbitrary",),
                                             vmem_limit_bytes=VMEM_LIMIT_BYTES),
        name="mod",
    )(c, w_mod, b_mod.reshape(1, dout))


@dataclasses.dataclass(frozen=True)
class MixerCfg:
    tt: int
    qb: int
    kb: int
    n_qblk: int
    kstride: int
    has_cache: bool
    sparse: bool


def _rms(x, g):
    ms = jnp.mean(x * x, axis=-1, keepdims=True)
    return x * lax.rsqrt(ms + EPS) * g


def _head_rms(z, hsum, g):
    ss = _dot((z * z).astype(BF16), hsum)
    return z * lax.rsqrt(ss * (1.0 / HEAD_DIM) + EPS) * g


def _expert_ids(t):
    sub = lax.broadcasted_iota(I32, (GROUP_SIZE, t), 0).astype(F32)
    return [sub + float(g * GROUP_SIZE) for g in range(N_GROUPS)]


def _pick(ids, eid, slabs):
    acc = functools.reduce(jnp.add, [jnp.where(ids[g] == eid, slabs[g], 0.0) for g in range(N_GROUPS)])
    return jnp.sum(acc, axis=0, keepdims=True)


def _route(choice, s):
    t = choice.shape[1]
    sub = lax.broadcasted_iota(I32, (GROUP_SIZE, t), 0).astype(F32)
    slabs = [choice[g * GROUP_SIZE:(g + 1) * GROUP_SIZE, :] for g in range(N_GROUPS)]
    s_slabs = [s[g * GROUP_SIZE:(g + 1) * GROUP_SIZE, :] for g in range(N_GROUPS)]
    gscore = []
    for c in slabs:
        m1 = jnp.max(c, axis=0, keepdims=True)
        first = jnp.min(jnp.where(c == m1, sub, float(GROUP_SIZE)), axis=0, keepdims=True)
        m2 = jnp.max(jnp.where(sub == first, -jnp.inf, c), axis=0, keepdims=True)
        gscore.append(m1 + m2)
    gsel = [jnp.zeros((1, t), F32) for _ in range(N_GROUPS)]
    for _ in range(TOPK_GROUPS):
        m = functools.reduce(jnp.maximum, gscore)
        first = functools.reduce(jnp.minimum,
                                 [jnp.where(gscore[g] == m, float(g), float(N_GROUPS)) for g in range(N_GROUPS)])
        for g in range(N_GROUPS):
            hit = first == float(g)
            gsel[g] = jnp.where(hit, 1.0, gsel[g])
            gscore[g] = jnp.where(hit, -jnp.inf, gscore[g])
    masked = [jnp.where(gsel[g] > 0.5, slabs[g], -jnp.inf) for g in range(N_GROUPS)]
    ids = _expert_ids(t)
    eids, raw = [], []
    for _ in range(TOP_K):
        m = jnp.max(functools.reduce(jnp.maximum, masked), axis=0, keepdims=True)
        cand = [jnp.where(masked[g] == m, ids[g], float(N_EXPERTS)) for g in range(N_GROUPS)]
        first = jnp.min(functools.reduce(jnp.minimum, cand), axis=0, keepdims=True)
        eids.append(first)
        raw.append(_pick(ids, first, s_slabs))
        masked = [jnp.where(ids[g] == first, -jnp.inf, masked[g]) for g in range(N_GROUPS)]
    wsum = functools.reduce(jnp.add, raw)
    return eids, [r / wsum * ROUTE_SCALE for r in raw]


def _lane_dense_rows(rows, t):
    pad = jnp.zeros((LANES - len(rows), t), F32)
    return jnp.concatenate(rows + [pad], axis=0).T


def _mixer_kernel(cfg, n_state, *refs):
    (x_ref, mod_ref, n1g_ref, n2g_ref, win_ref, qg_ref, kg_ref, hsum_ref, bias_ref, wao_ref,
     cw_ref, cb_ref, lng_ref, lnb_ref, wco_ref, bco_ref, bg_ref, wout_ref,
     wrh_ref, wrl_ref, rb_ref) = refs[:21]
    refs = refs[21:]
    if cfg.has_cache:
        kc_ref, vc_ref, cs_ref = refs[:3]
        refs = refs[3:]
    if cfg.sparse:
        tri_ref = refs[0]
        refs = refs[1:]
    refs = refs[n_state:]
    x1_ref, h2_ref, tok_ref, kout_ref, vout_ref, cout_ref = refs[:6]
    refs = refs[6:]
    if cfg.sparse:
        eid_ref, rank_ref, cnt_ref = refs[:3]
        refs = refs[3:]
    qbuf, kbuf, vbuf, obuf, uext, cvbuf = refs

    tt = cfg.tt
    t = pl.program_id(1)

    if cfg.has_cache:
        kbuf[0:PREV, :] = kc_ref[0].astype(BF16)
        vbuf[0:PREV, :] = vc_ref[0].astype(BF16)
        uext[CONV_PAD - (CONV_WIDTH - 1):CONV_PAD, :] = cs_ref[0]
    else:
        @pl.when(t == 0)
        def _():
            kbuf[0:PREV, :] = jnp.zeros((PREV, ATTN_DIM), BF16)
            vbuf[0:PREV, :] = jnp.zeros((PREV, ATTN_DIM), BF16)
            uext[0:CONV_PAD, :] = jnp.zeros((CONV_PAD, uext.shape[1]), F32)

    mod = mod_ref[0]
    sh1, sc1, g1, sh2, sc2 = (mod[i:i + 1, :] for i in range(5))

    x = x_ref[0]
    hb = (_rms(x, n1g_ref[...]) * (1.0 + sc1) + sh1).astype(BF16)

    a0, a1, a2, a3, a4, a5 = (0, ATTN_DIM, 2 * ATTN_DIM, 3 * ATTN_DIM,
                              3 * ATTN_DIM + cw_ref.shape[1], 3 * ATTN_DIM + 2 * cw_ref.shape[1])
    d_model = x.shape[1]

    q = _head_rms(_dot(hb, win_ref[:, a0:a1]), hsum_ref[...], qg_ref[...])
    qbuf[...] = (q * (HEAD_DIM ** -0.5 * LOG2_E)).astype(BF16)
    k = _head_rms(_dot(hb, win_ref[:, a1:a2]), hsum_ref[...], kg_ref[...])
    kout_ref[0] = k
    kbuf[PREV:PREV + tt, :] = k.astype(BF16)
    v = _dot(hb, win_ref[:, a2:a3])
    vout_ref[0] = v
    vbuf[PREV:PREV + tt, :] = v.astype(BF16)

    u = _dot(hb, win_ref[:, a3:a4]) * _sigmoid_t(_dot(hb, win_ref[:, a4:a5]))
    uext[CONV_PAD:CONV_PAD + tt, :] = u
    cout_ref[0] = uext[CONV_PAD + tt - (CONV_WIDTH - 1):CONV_PAD + tt, :]

    rc = min(tt, 64)
    cc = min(u.shape[1], LANES)
    shifts = [j + CONV_PAD - (CONV_WIDTH - 1) for j in range(CONV_WIDTH)]

    def conv_unit(r0, c0):
        acc = jnp.broadcast_to(cb_ref[:, c0:c0 + cc], (rc, cc))
        for res in range(SUBLANES):
            group = [s for s in shifts if s % SUBLANES == res]
            if not group:
                continue
            lo, hi = min(group) - res, max(group) - res
            rows = hi - lo + rc + (SUBLANES if res else 0)
            slab = uext[pl.ds(r0 + lo, rows), c0:c0 + cc]
            if res:
                slab = pltpu.roll(slab, rows - res, 0)
            for s in group:
                j = s - shifts[0]
                a = s - res - lo
                acc = acc + cw_ref[j:j + 1, c0:c0 + cc] * slab[a:a + rc, :]
        cvbuf[pl.ds(r0, rc), c0:c0 + cc] = acc

    units = [(r0, c0) for r0 in range(0, tt, rc) for c0 in range(0, u.shape[1], cc)]
    gw = 256
    n_chunks = 2 * d_model // gw
    gates = []
    for ci in range(n_chunks):
        cols = slice(ci * gw, (ci + 1) * gw)
        gates.append(_sigmoid_t(_dot(hb, win_ref[:, a5 + ci * gw:a5 + (ci + 1) * gw]) + bg_ref[:, cols]))
        for unit in units[ci::n_chunks]:
            conv_unit(*unit)
    g_attn = jnp.concatenate(gates[:n_chunks // 2], axis=1)
    g_conv = jnp.concatenate(gates[n_chunks // 2:], axis=1)

    def attend(r0, k0):
        if not cfg.has_cache:
            col = lax.broadcasted_iota(I32, (1, cfg.kb), 1)
            valid = jnp.logical_or(col >= PREV - k0, t > 0)
        heads = []
        for h in range(N_HEADS):
            c0, c1 = h * HEAD_DIM, (h + 1) * HEAD_DIM
            s = lax.dot_general(qbuf[pl.ds(r0, cfg.qb), c0:c1], kbuf[pl.ds(k0, cfg.kb), c0:c1],
                                (((1,), (1,)), ((), ())), preferred_element_type=F32)
            s = s + bias_ref[h]
            if not cfg.has_cache:
                s = jnp.where(valid, s, NEG)
            p = jnp.exp2(s - jnp.max(s, axis=-1, keepdims=True))
            l = jnp.sum(p, axis=-1, keepdims=True)
            heads.append(_dot(p.astype(BF16), vbuf[pl.ds(k0, cfg.kb), c0:c1]) * (1.0 / l))
        obuf[pl.ds(r0, cfg.qb), :] = jnp.concatenate(heads, axis=1).astype(BF16)

    if cfg.n_qblk == 1:
        attend(0, 0)
    else:
        def attend_step(j, carry):
            attend(pl.multiple_of(j * cfg.qb, cfg.qb), pl.multiple_of(j * cfg.kstride, cfg.kstride))
            return carry
        lax.fori_loop(0, cfg.n_qblk, attend_step, 0)
    y_attn = _dot(obuf[...], wao_ref[...])
    if not cfg.has_cache:
        kbuf[0:PREV, :] = kbuf[tt:tt + PREV, :]
        vbuf[0:PREV, :] = vbuf[tt:tt + PREV, :]

    if not cfg.has_cache:
        uext[0:CONV_PAD, :] = uext[tt:tt + CONV_PAD, :]
    cv = cvbuf[...]
    mu = jnp.mean(cv, axis=-1, keepdims=True)
    xc = cv - mu
    var = jnp.mean(xc * xc, axis=-1, keepdims=True)
    cv = xc * lax.rsqrt(var + EPS) * lng_ref[...] + lnb_ref[...]
    cv = cv * _sigmoid_t(cv)
    y_conv = _dot(cv.astype(BF16), wco_ref[...]) + bco_ref[...]

    m = g_attn * y_attn + g_conv * y_conv
    x1 = x + g1 * _dot(m.astype(BF16), wout_ref[...])
    x1_ref[0] = x1

    h2 = _rms(x1, n2g_ref[...]) * (1.0 + sc2) + sh2
    h2_ref[0] = _pack(h2[:, 0:d_model // 2], h2[:, d_model // 2:d_model])
    h2_hi, h2_lo = _split_bf16(h2)
    nt_dims = (((1,), (1,)), ((), ()))
    logits = (lax.dot_general(wrh_ref[...], h2_hi, nt_dims, preferred_element_type=F32)
              + lax.dot_general(wrl_ref[...], h2_hi, nt_dims, preferred_element_type=F32)
              + lax.dot_general(wrh_ref[...], h2_lo, nt_dims, preferred_element_type=F32))
    s = _sigmoid(logits)
    eids, weights = _route(s + rb_ref[...], s)
    ids = _expert_ids(tt)
    if cfg.sparse:
        sel = [functools.reduce(jnp.add, [jnp.where(ids[g] == e, 1.0, 0.0) for e in eids]) for g in range(N_GROUPS)]
        sel = jnp.concatenate(sel, axis=0)
        rank = _dot(sel.astype(BF16), tri_ref[...])
        rank_slabs = [rank[g * GROUP_SIZE:(g + 1) * GROUP_SIZE, :] for g in range(N_GROUPS)]
        tok_ref[0] = _lane_dense_rows(weights, tt)
        eid_ref[...] = jnp.concatenate(eids, axis=0)
        rank_ref[...] = jnp.concatenate([_pick(ids, e, rank_slabs) for e in eids], axis=0)
        step = pl.program_id(0) * pl.num_programs(1) + t

        @pl.when(step == 0)
        def _():
            cnt_ref[...] = jnp.zeros(cnt_ref.shape, F32)

        lane = lax.broadcasted_iota(I32, cnt_ref.shape, 1)
        cnt_ref[...] = jnp.where(lane == step, jnp.sum(sel, axis=1, keepdims=True), cnt_ref[...])
    else:
        gates = [functools.reduce(jnp.add, [jnp.where(ids[g] == e, w, 0.0) for e, w in zip(eids, weights)])
                 for g in range(N_GROUPS)]
        gates = jnp.concatenate(gates + [jnp.zeros((LANES - N_EXPERTS, tt), F32)], axis=0)
        tok_ref[0] = gates.T


def _toeplitz(v, rows, cols):
    w = rows + cols
    flat = jnp.tile(v, (1, rows))[:, :rows * (w - 1)]
    return flat.reshape(v.shape[0], rows, w - 1)[:, :, :cols]


def _rel_bias_blocks(table, cfg, q_pos0, k_pos0):
    q_pos = q_pos0 + jnp.arange(cfg.qb)
    k_pos = k_pos0 + jnp.arange(cfg.kb)
    qc = q_pos // CHUNK
    kc = k_pos // CHUNK
    allowed = (kc[None, :] <= qc[:, None]) & (kc[None, :] >= qc[:, None] - BAND_CHUNKS)
    w = cfg.qb + cfg.kb
    dj = jnp.arange(w)
    dj = jnp.where(dj < cfg.kb, dj, dj - w)
    rel = jnp.clip(q_pos0 - k_pos0 - dj, -MAX_REL, MAX_REL) + MAX_REL
    bias = _toeplitz(table[:, rel].astype(F32), cfg.qb, cfg.kb)
    return jnp.where(allowed[None], bias * LOG2_E, NEG)


def _mixer_cost(cfg, tokens, d, n_in, conv_dim):
    per_token = (2 * d * n_in + 2 * (ATTN_DIM + conv_dim + d) * d + 4 * ATTN_DIM * cfg.kb
                 + 4 * ATTN_DIM * ATTN_DIM + 6 * N_EXPERTS * d + 2 * CONV_WIDTH * conv_dim)
    return pl.CostEstimate(flops=tokens * per_token,
                           transcendentals=tokens * (N_HEADS * cfg.kb + 2 * d + 2 * conv_dim + N_EXPERTS),
                           bytes_accessed=tokens * (4 * d + 4 * d + 2 * d + 4 * LANES) + 2 * d * n_in)


def _mixer_call(cfg, x, mod, p, cache=None, batch0=0, nbatch=None, state=None):
    b_all, t_total, d = x.shape
    nb = b_all if nbatch is None else nbatch
    nt = t_total // cfg.tt
    conv_dim = p["conv_w"].shape[1]
    n_in = p["w_in"].shape[1]

    def const(shape):
        return pl.BlockSpec(shape, lambda i, j: (0,) * len(shape), pipeline_mode=pl.Buffered(1))

    def per_row(shape):
        return pl.BlockSpec(shape, lambda i, j: (i + batch0, 0, 0))

    in_specs = [
        pl.BlockSpec((1, cfg.tt, d), lambda i, j: (i + batch0, j, 0)),
        per_row((1, 6, d)),
        const((1, d)), const((1, d)), const((d, n_in)),
        const((1, ATTN_DIM)), const((1, ATTN_DIM)), const((ATTN_DIM, ATTN_DIM)),
        const((N_HEADS, cfg.qb, cfg.kb)), const((ATTN_DIM, d)),
        const((CONV_WIDTH, conv_dim)), const((1, conv_dim)), const((1, conv_dim)), const((1, conv_dim)),
        const((conv_dim, d)), const((1, d)), const((1, 2 * d)), const((d, d)),
        const((N_EXPERTS, d)), const((N_EXPERTS, d)), const((N_EXPERTS, 1)),
    ]
    args = [x, mod, p["norm1_g"], p["norm2_g"], p["w_in"], p["q_norm_g"], p["k_norm_g"], p["hsum"],
            p["bias"], p["w_attn_o"], p["conv_w"], p["conv_b"], p["conv_ln_g"], p["conv_ln_b"],
            p["w_conv_o"], p["b_conv_o"], p["b_gate"], p["w_out"], p["wr_hi"], p["wr_lo"], p["router_bias"]]
    if cfg.has_cache:
        in_specs += [per_row((1, PREV, ATTN_DIM)), per_row((1, PREV, ATTN_DIM)),
                     per_row((1, CONV_WIDTH - 1, conv_dim))]
        args += list(cache)
    if cfg.sparse:
        assert nb * nt <= MAX_TILES
        in_specs += [const((cfg.tt, cfg.tt))]
        tok = jnp.arange(cfg.tt)
        args += [(tok[:, None] < tok[None, :]).astype(BF16)]
    aliases = {}
    if state is not None:
        for i, arr in enumerate(state):
            aliases[len(args)] = 3 + i
            in_specs.append(pl.BlockSpec(memory_space=pl.ANY))
            args.append(arr)
    keep = min(PREV, t_total)
    assert keep == cfg.tt
    out_shape = [jax.ShapeDtypeStruct((nb, t_total, d), F32),
                 jax.ShapeDtypeStruct((nb, t_total, d // 2), I32),
                 jax.ShapeDtypeStruct((nb, t_total, LANES), F32),
                 jax.ShapeDtypeStruct((b_all, keep, ATTN_DIM), F32),
                 jax.ShapeDtypeStruct((b_all, keep, ATTN_DIM), F32),
                 jax.ShapeDtypeStruct((b_all, CONV_WIDTH - 1, conv_dim), F32)]
    out_specs = [pl.BlockSpec((1, cfg.tt, d), lambda i, j: (i, j, 0)),
                 pl.BlockSpec((1, cfg.tt, d // 2), lambda i, j: (i, j, 0)),
                 pl.BlockSpec((1, cfg.tt, LANES), lambda i, j: (i, j, 0)),
                 per_row((1, keep, ATTN_DIM)), per_row((1, keep, ATTN_DIM)),
                 per_row((1, CONV_WIDTH - 1, conv_dim))]
    if cfg.sparse:
        out_shape += [jax.ShapeDtypeStruct((TOP_K, nb * t_total), F32),
                      jax.ShapeDtypeStruct((TOP_K, nb * t_total), F32),
                      jax.ShapeDtypeStruct((N_EXPERTS, MAX_TILES), F32)]
        out_specs += [pl.BlockSpec((TOP_K, cfg.tt), lambda i, j: (0, i * nt + j)),
                      pl.BlockSpec((TOP_K, cfg.tt), lambda i, j: (0, i * nt + j)),
                      pl.BlockSpec((N_EXPERTS, MAX_TILES), lambda i, j: (0, 0))]
    scratch = [pltpu.VMEM((cfg.tt, ATTN_DIM), BF16),
               pltpu.VMEM((PREV + cfg.tt, ATTN_DIM), BF16),
               pltpu.VMEM((PREV + cfg.tt, ATTN_DIM), BF16),
               pltpu.VMEM((cfg.tt, ATTN_DIM), BF16),
               pltpu.VMEM((CONV_PAD + cfg.tt, conv_dim), F32),
               pltpu.VMEM((cfg.tt, conv_dim), F32)]
    return pl.pallas_call(
        functools.partial(_mixer_kernel, cfg, 0 if state is None else len(state)),
        out_shape=out_shape,
        grid=(nb, nt),
        in_specs=in_specs,
        out_specs=out_specs,
        scratch_shapes=scratch,
        input_output_aliases=aliases,
        cost_estimate=_mixer_cost(cfg, nb * t_total, d, n_in, conv_dim),
        compiler_params=pltpu.CompilerParams(dimension_semantics=("arbitrary", "arbitrary"),
                                             vmem_limit_bytes=VMEM_LIMIT_BYTES),
        name="mixer_sample" if cfg.has_cache else "mixer_prompt",
    )(*args)


def _exact_parts(x, n):
    parts = []
    for _ in range(n):
        part = x.astype(BF16)
        parts.append(part)
        x = x - part.astype(F32)
    return parts


def _plan_kernel(tt, cnt_ref, eid_ref, rank_ref, tri_e_ref, tri_t_ref, pos_ref, blk_ref, nblk_ref, base_ref):
    step = pl.program_id(0)

    @pl.when(step == 0)
    def _():
        cnt = cnt_ref[...]
        total = jnp.sum(cnt, axis=1, keepdims=True)
        padded = jnp.floor((total + float(ROW_BLOCK - 1)) * (1.0 / ROW_BLOCK)) * float(ROW_BLOCK)
        padded = jnp.broadcast_to(padded, cnt.shape)
        start = functools.reduce(jnp.add, [_dot(tri_e_ref[...], part) for part in _exact_parts(padded, 3)])
        before = functools.reduce(jnp.add, [_dot(part, tri_t_ref[...]) for part in _exact_parts(cnt, 2)])
        base_ref[...] = start + before
        end = (start + padded)[:, 0:1]
        first_row = lax.broadcasted_iota(I32, blk_ref.shape, 1).astype(F32) * float(ROW_BLOCK)
        owner = jnp.sum(jnp.where(end <= first_row, 1.0, 0.0), axis=0, keepdims=True)
        blk_ref[...] = jnp.minimum(owner, float(N_EXPERTS - 1)).astype(I32)
        nblk = jnp.max(end, axis=0, keepdims=True) * (1.0 / ROW_BLOCK)
        nblk_ref[...] = jnp.broadcast_to(nblk, nblk_ref.shape).astype(I32)

    lane = lax.broadcasted_iota(I32, base_ref.shape, 1)
    col = jnp.sum(jnp.where(lane == step, base_ref[...], 0.0), axis=1, keepdims=True)
    ids = _expert_ids(tt)
    col_slabs = [jnp.broadcast_to(col[g * GROUP_SIZE:(g + 1) * GROUP_SIZE, :], (GROUP_SIZE, tt))
                 for g in range(N_GROUPS)]
    eid = eid_ref[...]
    rows = [_pick(ids, eid[k:k + 1, :], col_slabs) for k in range(TOP_K)]
    pos_ref[...] = (jnp.concatenate(rows, axis=0) + rank_ref[...]).astype(I32)


def _plan_call(cnt, eid, rank, tt, n_blocks):
    n = eid.shape[1]
    e = jnp.arange(N_EXPERTS)
    s = jnp.arange(MAX_TILES)
    nb_pad = -(-n_blocks // LANES) * LANES
    return pl.pallas_call(
        functools.partial(_plan_kernel, tt),
        out_shape=[jax.ShapeDtypeStruct((TOP_K, n), I32),
                   jax.ShapeDtypeStruct((1, nb_pad), I32),
                   jax.ShapeDtypeStruct((1, LANES), I32)],
        grid=(n // tt,),
        in_specs=[pl.BlockSpec((N_EXPERTS, MAX_TILES), lambda i: (0, 0)),
                  pl.BlockSpec((TOP_K, tt), lambda i: (0, i)),
                  pl.BlockSpec((TOP_K, tt), lambda i: (0, i)),
                  pl.BlockSpec((N_EXPERTS, N_EXPERTS), lambda i: (0, 0)),
                  pl.BlockSpec((MAX_TILES, MAX_TILES), lambda i: (0, 0))],
        out_specs=[pl.BlockSpec((TOP_K, tt), lambda i: (0, i)),
                   pl.BlockSpec((1, nb_pad), lambda i: (0, 0)),
                   pl.BlockSpec((1, LANES), lambda i: (0, 0))],
        scratch_shapes=[pltpu.VMEM((N_EXPERTS, MAX_TILES), F32)],
        compiler_params=pltpu.CompilerParams(dimension_semantics=("arbitrary",),
                                             vmem_limit_bytes=VMEM_LIMIT_BYTES),
        name="plan",
    )(cnt, eid, rank, (e[None, :] < e[:, None]).astype(BF16), (s[:, None] < s[None, :]).astype(BF16))


def _sc_mesh():
    return plsc.VectorSubcoreMesh(core_axis_name="c", subcore_axis_name="s")


def _dispatch_call(rows, pos, n_sorted):
    n, w = rows.shape
    per_worker = n // SC_WORKERS
    nsteps = per_worker // SC_ROWS
    assert per_worker * SC_WORKERS == n and nsteps * SC_ROWS == per_worker and nsteps % 2 == 0

    @functools.partial(
        pl.kernel, mesh=_sc_mesh(),
        out_type=jax.ShapeDtypeStruct((n_sorted, w), rows.dtype),
        scratch_types=[pltpu.VMEM((2, TOP_K, SC_ROWS), I32),
                       pltpu.VMEM((2, SC_ROWS, w), rows.dtype),
                       pltpu.SemaphoreType.DMA((2,)),
                       pltpu.SemaphoreType.DMA((2,))],
        cost_estimate=pl.CostEstimate(flops=0, transcendentals=0,
                                      bytes_accessed=(1 + TOP_K) * n * w * rows.dtype.itemsize + 4 * TOP_K * n),
    )
    def dispatch(x_hbm, pos_hbm, out_hbm, idx_v, rows_v, load_sem, scat_sem):
        base = (lax.axis_index("s") * SC_CORES + lax.axis_index("c")) * per_worker

        def load(i, b):
            return pltpu.make_async_copy(x_hbm.at[pl.ds(base + i * SC_ROWS, SC_ROWS)], rows_v.at[b], load_sem.at[b])

        def scatter(b, k):
            return pltpu.make_async_copy(rows_v.at[b], out_hbm.at[idx_v.at[b, k]], scat_sem.at[b])

        def load_start(i, b):
            for k in range(TOP_K):
                pltpu.sync_copy(pos_hbm.at[pl.ds(k * n + base + i * SC_ROWS, SC_ROWS)], idx_v.at[b, k])
            load(i, b).start()

        load_start(0, 0)

        @pl.loop(0, nsteps, step=2)
        def _(i):
            for b in range(2):
                ii = i + b

                @pl.when(ii >= 1)
                def _():
                    for k in range(TOP_K):
                        scatter(1 - b, k).wait()

                @pl.when(ii + 1 < nsteps)
                def _():
                    load_start(ii + 1, 1 - b)

                load(ii, b).wait()
                for k in range(TOP_K):
                    scatter(b, k).start()

        for k in range(TOP_K):
            scatter((nsteps - 1) % 2, k).wait()

    return dispatch(rows, pos)


def _collect_call(table, idx):
    n = idx.shape[0]
    w = table.shape[1]
    per_worker = n // SC_WORKERS
    nsteps = per_worker // SC_ROWS
    assert per_worker * SC_WORKERS == n and nsteps * SC_ROWS == per_worker and nsteps % 2 == 0

    @functools.partial(
        pl.kernel, mesh=_sc_mesh(),
        out_type=jax.ShapeDtypeStruct((n, w), table.dtype),
        scratch_types=[pltpu.VMEM((2, SC_ROWS), I32),
                       pltpu.VMEM((2, SC_ROWS, w), table.dtype),
                       pltpu.SemaphoreType.DMA((2,)),
                       pltpu.SemaphoreType.DMA((2,))],
        cost_estimate=pl.CostEstimate(flops=0, transcendentals=0,
                                      bytes_accessed=2 * n * w * table.dtype.itemsize + 4 * n),
    )
    def collect(table_hbm, idx_hbm, out_hbm, idx_v, rows_v, gather_sem, write_sem):
        base = (lax.axis_index("s") * SC_CORES + lax.axis_index("c")) * per_worker

        def gather(b):
            return pltpu.make_async_copy(table_hbm.at[idx_v.at[b]], rows_v.at[b], gather_sem.at[b])

        def write(i, b):
            return pltpu.make_async_copy(rows_v.at[b], out_hbm.at[pl.ds(base + i * SC_ROWS, SC_ROWS)], write_sem.at[b])

        def gather_start(i, b):
            pltpu.sync_copy(idx_hbm.at[pl.ds(base + i * SC_ROWS, SC_ROWS)], idx_v.at[b])
            gather(b).start()

        gather_start(0, 0)

        @pl.loop(0, nsteps, step=2)
        def _(i):
            for b in range(2):
                ii = i + b

                @pl.when(ii >= 1)
                def _():
                    write(ii - 1, 1 - b).wait()

                @pl.when(ii + 1 < nsteps)
                def _():
                    gather_start(ii + 1, 1 - b)

                gather(b).wait()
                write(ii, b).start()

        write(nsteps - 1, (nsteps - 1) % 2).wait()

    return collect(table, idx)


def _expert_ffn_kernel(blk_ref, nblk_ref, x_ref, wg_ref, wu_ref, wd_ref, o_ref, wg_s, wu_s, wd_s):
    j = pl.program_id(0)

    @pl.when(jnp.logical_or(j == 0, blk_ref[j] != blk_ref[jnp.maximum(j - 1, 0)]))
    def _():
        wg_s[...] = wg_ref[0].astype(BF16)
        wu_s[...] = wu_ref[0].astype(BF16)
        wd_s[...] = wd_ref[0].astype(BF16)

    @pl.when(j < nblk_ref[0])
    def _():
        a, b = _unpack(x_ref[...])
        a, b = a.astype(BF16), b.astype(BF16)
        hg = _dot_halves(a, b, wg_s)
        act = hg * _sigmoid_t(hg) * _dot_halves(a, b, wu_s)
        y = _dot(act.astype(BF16), wd_s[...])
        half = y.shape[1] // 2
        o_ref[...] = _pack(y[:, 0:half], y[:, half:2 * half])


def _expert_ffn_call(xs, blk, nblk, p):
    n_sorted, w = xs.shape
    d = 2 * w
    ff = p["w_e_gate"].shape[2]

    def rows(j, blk, nblk):
        return (jnp.minimum(j, nblk[0] - 1), 0)

    def expert(j, blk, nblk):
        return (blk[j], 0, 0)

    return pl.pallas_call(
        _expert_ffn_kernel,
        out_shape=jax.ShapeDtypeStruct((n_sorted, w), I32),
        grid_spec=pltpu.PrefetchScalarGridSpec(
            num_scalar_prefetch=2,
            grid=(n_sorted // ROW_BLOCK,),
            in_specs=[pl.BlockSpec((ROW_BLOCK, w), rows),
                      pl.BlockSpec((1, d, ff), expert),
                      pl.BlockSpec((1, d, ff), expert),
                      pl.BlockSpec((1, ff, d), expert)],
            out_specs=pl.BlockSpec((ROW_BLOCK, w), rows),
            scratch_shapes=[pltpu.VMEM((d, ff), BF16), pltpu.VMEM((d, ff), BF16), pltpu.VMEM((ff, d), BF16)]),
        cost_estimate=pl.CostEstimate(flops=6 * n_sorted * d * ff, transcendentals=n_sorted * ff,
                                      bytes_accessed=8 * n_sorted * w + 12 * N_EXPERTS * d * ff),
        compiler_params=pltpu.CompilerParams(dimension_semantics=("arbitrary",),
                                             vmem_limit_bytes=VMEM_LIMIT_BYTES),
        name="expert_ffn",
    )(blk, nblk, xs, p["w_e_gate"], p["w_e_up"], p["w_e_down"])


def _swiglu_halves(a, b, wg_ref, wu_ref):
    hg = _dot_halves(a, b, wg_ref)
    return hg * _sigmoid_t(hg) * _dot_halves(a, b, wu_ref)


def _moe_out_kernel(n_groups, starts, *refs):
    groups = [refs[4 * q:4 * q + 4] for q in range(n_groups)]
    g2_ref, wsg_ref, wsu_ref, wsd_ref, o_ref = refs[4 * n_groups:]

    def combine(x1_ref, h_ref, y_ref, w_ref):
        a, b = _unpack(h_ref[...])
        shared = _dot(_swiglu_halves(a.astype(BF16), b.astype(BF16), wsg_ref, wsu_ref).astype(BF16),
                      wsd_ref[...].astype(BF16))
        half = h_ref.shape[1]
        w = w_ref[...]
        acc_a, acc_b = shared[:, 0:half], shared[:, half:2 * half]
        for k in range(TOP_K):
            ya, yb = _unpack(y_ref[k])
            acc_a = acc_a + w[:, k:k + 1] * ya
            acc_b = acc_b + w[:, k:k + 1] * yb
        g2 = g2_ref[...]
        o_ref[:, 0:half] = x1_ref[:, 0:half] + g2[:, 0:half] * acc_a
        o_ref[:, half:2 * half] = x1_ref[:, half:2 * half] + g2[:, half:2 * half] * acc_b

    i = pl.program_id(0)
    for q in range(n_groups):
        pl.when(jnp.logical_and(i >= starts[q], i < starts[q + 1]))(functools.partial(combine, *groups[q]))


def _moe_out_call(groups, g2, p, tm):
    d = groups[0][0].shape[1]
    starts = [0]
    for group in groups:
        starts.append(starts[-1] + group[0].shape[0] // tm)
    n_all = starts[-1] * tm
    sff = p["w_s_gate"].shape[1]
    per_row = n_all // g2.shape[0]
    g2 = g2.reshape(g2.shape[0], 1, d)

    def const(shape):
        return pl.BlockSpec(shape, lambda i: (0,) * len(shape), pipeline_mode=pl.Buffered(1))

    in_specs, args = [], []
    for q, group in enumerate(groups):
        tile = lambda i, q=q: jnp.clip(i - starts[q], 0, starts[q + 1] - starts[q] - 1)
        in_specs += [pl.BlockSpec((tm, d), lambda i, tile=tile: (tile(i), 0)),
                     pl.BlockSpec((tm, d // 2), lambda i, tile=tile: (tile(i), 0)),
                     pl.BlockSpec((TOP_K, tm, d // 2), lambda i, tile=tile: (0, tile(i), 0)),
                     pl.BlockSpec((tm, LANES), lambda i, tile=tile: (tile(i), 0))]
        args += list(group)
    in_specs += [pl.BlockSpec((None, 1, d), lambda i: (i * tm // per_row, 0, 0)),
                 const((d, sff)), const((d, sff)), const((sff, d))]
    args += [g2, p["w_s_gate"], p["w_s_up"], p["w_s_down"]]
    return pl.pallas_call(
        functools.partial(_moe_out_kernel, len(groups), tuple(starts)),
        out_shape=jax.ShapeDtypeStruct((n_all, d), F32),
        grid=(n_all // tm,),
        in_specs=in_specs,
        out_specs=pl.BlockSpec((tm, d), lambda i: (i, 0)),
        cost_estimate=pl.CostEstimate(flops=n_all * (6 * d * sff + 2 * TOP_K * d), transcendentals=n_all * sff,
                                      bytes_accessed=n_all * (8 * d + 2 * d + 2 * TOP_K * d + 4 * LANES)),
        compiler_params=pltpu.CompilerParams(dimension_semantics=("arbitrary",),
                                             vmem_limit_bytes=VMEM_LIMIT_BYTES),
        name="moe_out",
    )(*args)


def _moe_dense_kernel(h_ref, gates_ref, x1_ref, g2_ref, wsg_ref, wsu_ref, wsd_ref, wg_ref, wu_ref, wd_ref,
                      o_ref, acc_ref):
    e = pl.program_id(1)
    a, b = _unpack(h_ref[...])
    a, b = a.astype(BF16), b.astype(BF16)

    @pl.when(e == 0)
    def _():
        acc_ref[...] = _dot(_swiglu_halves(a, b, wsg_ref, wsu_ref).astype(BF16), wsd_ref[...].astype(BF16))

    gates = gates_ref[...]
    lane = lax.broadcasted_iota(I32, gates.shape, 1)
    gate = jnp.sum(jnp.where(lane == e, gates, 0.0), axis=1, keepdims=True)
    act = _swiglu_halves(a, b, wg_ref.at[0], wu_ref.at[0]) * gate
    acc_ref[...] += _dot(act.astype(BF16), wd_ref[0].astype(BF16))

    @pl.when(e == pl.num_programs(1) - 1)
    def _():
        o_ref[...] = x1_ref[...] + g2_ref[...] * acc_ref[...]


def _moe_dense_call(h2, gates, x1, g2_rows, p):
    n, d = x1.shape
    ff = p["w_e_gate"].shape[2]
    sff = p["w_s_gate"].shape[1]

    def const(shape):
        return pl.BlockSpec(shape, lambda i, e: (0,) * len(shape), pipeline_mode=pl.Buffered(1))

    return pl.pallas_call(
        _moe_dense_kernel,
        out_shape=jax.ShapeDtypeStruct((n, d), F32),
        grid=(1, N_EXPERTS),
        in_specs=[const((n, d // 2)), const((n, LANES)), const((n, d)), const((n, d)),
                  const((d, sff)), const((d, sff)), const((sff, d)),
                  pl.BlockSpec((1, d, ff), lambda i, e: (e, 0, 0)),
                  pl.BlockSpec((1, d, ff), lambda i, e: (e, 0, 0)),
                  pl.BlockSpec((1, ff, d), lambda i, e: (e, 0, 0))],
        out_specs=pl.BlockSpec((n, d), lambda i, e: (0, 0)),
        scratch_shapes=[pltpu.VMEM((n, d), F32)],
        compiler_params=pltpu.CompilerParams(dimension_semantics=("arbitrary", "arbitrary"),
                                             vmem_limit_bytes=VMEM_LIMIT_BYTES),
        name="moe_dense",
    )(h2, gates, x1, g2_rows, p["w_s_gate"], p["w_s_up"], p["w_s_down"], p["w_e_gate"], p["w_e_up"], p["w_e_down"])


PROMPT_CFG = MixerCfg(tt=512, qb=256, kb=768, n_qblk=2, kstride=256, has_cache=False, sparse=True)
MOE_OUT_TILE = 256
PROMPT_SPLIT = (3, 1)


def _layer_params(l, w_in, q_norm_g, k_norm_g, w_attn_o, conv_w, conv_b, conv_ln_g, conv_ln_b, w_conv_o,
                  b_conv_o, b_gate, w_out, norm1_g, norm2_g, w_router, router_bias,
                  w_e_gate, w_e_up, w_e_down, w_s_gate, w_s_up, w_s_down):
    row = lambda a: a[l].reshape(1, -1)
    wr_t = w_router[l].T
    wr_hi = wr_t.astype(BF16)
    head = jnp.arange(ATTN_DIM) // HEAD_DIM
    return {
        "norm1_g": row(norm1_g), "norm2_g": row(norm2_g), "w_in": w_in[l].astype(BF16),
        "q_norm_g": jnp.tile(q_norm_g[l], N_HEADS).reshape(1, -1),
        "k_norm_g": jnp.tile(k_norm_g[l], N_HEADS).reshape(1, -1),
        "hsum": (head[:, None] == head[None, :]).astype(BF16),
        "w_attn_o": w_attn_o[l].astype(BF16), "conv_w": conv_w[l], "conv_b": row(conv_b),
        "conv_ln_g": row(conv_ln_g), "conv_ln_b": row(conv_ln_b), "w_conv_o": w_conv_o[l].astype(BF16),
        "b_conv_o": row(b_conv_o), "b_gate": row(b_gate), "w_out": w_out[l].astype(BF16),
        "wr_hi": wr_hi, "wr_lo": (wr_t - wr_hi.astype(F32)).astype(BF16),
        "router_bias": router_bias[l].reshape(-1, 1),
        "w_e_gate": w_e_gate[l], "w_e_up": w_e_up[l], "w_e_down": w_e_down[l],
        "w_s_gate": w_s_gate[l], "w_s_up": w_s_up[l], "w_s_down": w_s_down[l],
    }


def _sorted_rows(h2, eid, rank, cnt, tt):
    n = h2.shape[0]
    n_sorted = n * TOP_K + N_EXPERTS * ROW_BLOCK
    pos, blk, nblk = _plan_call(cnt, eid, rank, tt, n_sorted // ROW_BLOCK)
    pos = pos.reshape(TOP_K * n)
    return _dispatch_call(h2, pos, n_sorted), pos, blk.reshape(-1), nblk.reshape(-1)


def kernel(x_prompt, x_sample, c_prompt, c_sample, cache_k, cache_v, state_conv, w_mod, b_mod, norm1_g, w_in, q_norm_g, k_norm_g, rel_bias, w_attn_o, conv_w, conv_b, conv_ln_g, conv_ln_b, w_conv_o, b_conv_o, b_gate, w_out, norm2_g, w_router, router_bias, w_e_gate, w_e_up, w_e_down, w_s_gate, w_s_up, w_s_down):
    depth = w_mod.shape[0]
    bp, tp, d = x_prompt.shape
    bs, ts, _ = x_sample.shape
    assert cache_k.shape[2] == min(PREV, PAST_LEN) == PREV
    sample_cfg = MixerCfg(tt=ts, qb=ts, kb=PREV + ts, n_qblk=1, kstride=0, has_cache=True, sparse=False)

    yp, ys = x_prompt, x_sample
    outs = [[] for _ in range(6)]
    for l in range(depth):
        p = _layer_params(l, w_in, q_norm_g, k_norm_g, w_attn_o, conv_w, conv_b, conv_ln_g, conv_ln_b,
                          w_conv_o, b_conv_o, b_gate, w_out, norm1_g, norm2_g, w_router, router_bias,
                          w_e_gate, w_e_up, w_e_down, w_s_gate, w_s_up, w_s_down)
        mod = _mod_call(jnp.concatenate([c_prompt, c_sample], axis=0), w_mod[l], b_mod[l])
        mod = mod.reshape(bp + bs, 6, d)
        mod_p, mod_s = mod[:bp], mod[bp:]

        p["bias"] = _rel_bias_blocks(rel_bias[l], PROMPT_CFG, PREV, 0)
        big = max(bp * PROMPT_SPLIT[0] // sum(PROMPT_SPLIT), 1)
        group_rows = [big, bp - big] if bp > big else [bp]
        keep = min(PREV, tp)
        state = [jnp.zeros((bp, keep, ATTN_DIM), F32), jnp.zeros((bp, keep, ATTN_DIM), F32),
                 jnp.zeros((bp, CONV_WIDTH - 1, conv_w.shape[2]), F32)]
        mixed, moved = [], []
        for g, gb in enumerate(group_rows):
            x1, h2, w_tok, *state, eid, rank, cnt = _mixer_call(PROMPT_CFG, yp, mod_p, p, batch0=sum(group_rows[:g]),
                                                                nbatch=gb, state=state)
            x1, h2, w_tok = x1.reshape(gb * tp, d), h2.reshape(gb * tp, d // 2), w_tok.reshape(gb * tp, LANES)
            mixed.append((x1, h2, w_tok))
            moved.append(_sorted_rows(h2, eid, rank, cnt, PROMPT_CFG.tt))
            state, _ = lax.optimization_barrier((state, moved[-1][1]))
        kp, vp, cp = state
        groups, gate = [], cp
        for (x1, h2, w_tok), (xs, pos, blk, nblk) in zip(mixed, moved):
            xs, _ = lax.optimization_barrier((xs, gate))
            gate = _expert_ffn_call(xs, blk, nblk, p)
            yg = _collect_call(gate, pos).reshape(TOP_K, h2.shape[0], h2.shape[1])
            groups.append((x1, h2, yg, w_tok))
        yp = _moe_out_call(groups, mod_p[:, 5, :], p, MOE_OUT_TILE).reshape(bp, tp, d)
        outs[0].append(kp.reshape(bp, -1, N_HEADS, HEAD_DIM))
        outs[1].append(vp.reshape(bp, -1, N_HEADS, HEAD_DIM))
        outs[2].append(cp)

        p["bias"] = _rel_bias_blocks(rel_bias[l], sample_cfg, PAST_LEN, PAST_LEN - PREV)
        cache = (cache_k[l].reshape(bs, PREV, ATTN_DIM), cache_v[l].reshape(bs, PREV, ATTN_DIM), state_conv[l])
        x1, h2, gates, ks, vs, cs = _mixer_call(sample_cfg, ys, mod_s, p, cache)
        ys = _moe_dense_call(h2.reshape(bs * ts, d // 2), gates.reshape(bs * ts, LANES), x1.reshape(bs * ts, d),
                             jnp.repeat(mod_s[:, 5, :], ts, axis=0), p).reshape(bs, ts, d)
        outs[3].append(ks.reshape(bs, ts, N_HEADS, HEAD_DIM))
        outs[4].append(vs.reshape(bs, ts, N_HEADS, HEAD_DIM))
        outs[5].append(cs)
    return (yp, ys) + tuple(jnp.stack(o) for o in outs)
```

```python
import dataclasses
import functools

import jax
import jax.numpy as jnp
from jax import lax
from jax.experimental import pallas as pl
from jax.experimental.pallas import tpu as pltpu
from jax.experimental.pallas import tpu_sc as plsc

F32 = jnp.float32
BF16 = jnp.bfloat16
I32 = jnp.int32

CHUNK = 64
BAND_CHUNKS = 8
PREV = BAND_CHUNKS * CHUNK
PAST_LEN = 1024
N_HEADS = 8
HEAD_DIM = 64
ATTN_DIM = N_HEADS * HEAD_DIM
MAX_REL = 128
CONV_WIDTH = 31
CONV_PAD = 32
N_EXPERTS = 64
N_GROUPS = 8
GROUP_SIZE = N_EXPERTS // N_GROUPS
TOPK_GROUPS = 4
TOP_K = 8
ROUTE_SCALE = 2.5
EPS = 1e-6
LANES = 128
SUBLANES = 8
NEG = -1e30
LOG2_E = 1.4426950408889634

VMEM_LIMIT_BYTES = 60 * 1024 * 1024

SC_CORES = 2
SC_SUBCORES = 16
SC_WORKERS = SC_CORES * SC_SUBCORES
SC_ROWS = 64

MAX_TILES = LANES
ROW_BLOCK = 1024


def _sigmoid(x):
    return 1.0 / (1.0 + jnp.exp(-x))


def _sigmoid_t(x):
    return 0.5 * jnp.tanh(0.5 * x) + 0.5


def _split_bf16(x):
    hi = x.astype(BF16)
    lo = (x - hi.astype(F32)).astype(BF16)
    return hi, lo


def _dot(a, b):
    return jnp.dot(a, b, preferred_element_type=F32)


def _dot3(a, b):
    a_hi, a_lo = _split_bf16(a)
    b_hi, b_lo = _split_bf16(b)
    return _dot(a_hi, b_hi) + _dot(a_hi, b_lo) + _dot(a_lo, b_hi)


def _pack(a, b):
    ia = lax.bitcast_convert_type(a.astype(BF16).astype(F32), I32)
    ib = lax.bitcast_convert_type(b.astype(BF16).astype(F32), I32)
    return ia | lax.shift_right_logical(ib, 16)


def _unpack(p):
    a = lax.bitcast_convert_type(p & jnp.int32(-65536), F32)
    b = lax.bitcast_convert_type(lax.shift_left(p, 16), F32)
    return a, b


def _dot_halves(a, b, w_ref):
    half = a.shape[1]
    return _dot(a, w_ref[0:half, :].astype(BF16)) + _dot(b, w_ref[half:2 * half, :].astype(BF16))


def _mod_kernel(c_ref, w_ref, b_ref, o_ref):
    c = c_ref[...]
    o_ref[...] = _dot3(c * _sigmoid(c), w_ref[...]) + b_ref[...]


def _mod_call(c, w_mod, b_mod):
    n, d = c.shape
    dout = w_mod.shape[1]
    bn = 1536
    return pl.pallas_call(
        _mod_kernel,
        out_shape=jax.ShapeDtypeStruct((n, dout), F32),
        grid=(dout // bn,),
        in_specs=[pl.BlockSpec((n, d), lambda j: (0, 0)),
                  pl.BlockSpec((d, bn), lambda j: (0, j)),
                  pl.BlockSpec((1, bn), lambda j: (0, j))],
        out_specs=pl.BlockSpec((n, bn), lambda j: (0, j)),
        compiler_params=pltpu.CompilerParams(dimension_semantics=("arbitrary",),
                                             vmem_limit_bytes=VMEM_LIMIT_BYTES),
        name="mod",
    )(c, w_mod, b_mod.reshape(1, dout))


@dataclasses.dataclass(frozen=True)
class MixerCfg:
    tt: int
    qb: int
    kb: int
    n_qblk: int
    kstride: int
    has_cache: bool
    sparse: bool


def _rms(x, g):
    ms = jnp.mean(x * x, axis=-1, keepdims=True)
    return x * lax.rsqrt(ms + EPS) * g


def _head_rms(z, hsum, g):
    ss = _dot((z * z).astype(BF16), hsum)
    return z * lax.rsqrt(ss * (1.0 / HEAD_DIM) + EPS) * g


def _expert_ids(t):
    sub = lax.broadcasted_iota(I32, (GROUP_SIZE, t), 0).astype(F32)
    return [sub + float(g * GROUP_SIZE) for g in range(N_GROUPS)]


def _pick(ids, eid, slabs):
    acc = functools.reduce(jnp.add, [jnp.where(ids[g] == eid, slabs[g], 0.0) for g in range(N_GROUPS)])
    return jnp.sum(acc, axis=0, keepdims=True)


def _route(choice, s):
    t = choice.shape[1]
    sub = lax.broadcasted_iota(I32, (GROUP_SIZE, t), 0).astype(F32)
    slabs = [choice[g * GROUP_SIZE:(g + 1) * GROUP_SIZE, :] for g in range(N_GROUPS)]
    s_slabs = [s[g * GROUP_SIZE:(g + 1) * GROUP_SIZE, :] for g in range(N_GROUPS)]
    gscore = []
    for c in slabs:
        m1 = jnp.max(c, axis=0, keepdims=True)
        first = jnp.min(jnp.where(c == m1, sub, float(GROUP_SIZE)), axis=0, keepdims=True)
        m2 = jnp.max(jnp.where(sub == first, -jnp.inf, c), axis=0, keepdims=True)
        gscore.append(m1 + m2)
    gsel = [jnp.zeros((1, t), F32) for _ in range(N_GROUPS)]
    for _ in range(TOPK_GROUPS):
        m = functools.reduce(jnp.maximum, gscore)
        first = functools.reduce(jnp.minimum,
                                 [jnp.where(gscore[g] == m, float(g), float(N_GROUPS)) for g in range(N_GROUPS)])
        for g in range(N_GROUPS):
            hit = first == float(g)
            gsel[g] = jnp.where(hit, 1.0, gsel[g])
            gscore[g] = jnp.where(hit, -jnp.inf, gscore[g])
    masked = [jnp.where(gsel[g] > 0.5, slabs[g], -jnp.inf) for g in range(N_GROUPS)]
    ids = _expert_ids(t)
    eids, raw = [], []
    for _ in range(TOP_K):
        m = jnp.max(functools.reduce(jnp.maximum, masked), axis=0, keepdims=True)
        cand = [jnp.where(masked[g] == m, ids[g], float(N_EXPERTS)) for g in range(N_GROUPS)]
        first = jnp.min(functools.reduce(jnp.minimum, cand), axis=0, keepdims=True)
        eids.append(first)
        raw.append(_pick(ids, first, s_slabs))
        masked = [jnp.where(ids[g] == first, -jnp.inf, masked[g]) for g in range(N_GROUPS)]
    wsum = functools.reduce(jnp.add, raw)
    return eids, [r / wsum * ROUTE_SCALE for r in raw]


def _lane_dense_rows(rows, t):
    pad = jnp.zeros((LANES - len(rows), t), F32)
    return jnp.concatenate(rows + [pad], axis=0).T


def _mixer_kernel(cfg, n_state, *refs):
    (x_ref, mod_ref, n1g_ref, n2g_ref, win_ref, qg_ref, kg_ref, hsum_ref, bias_ref, wao_ref,
     cw_ref, cb_ref, lng_ref, lnb_ref, wco_ref, bco_ref, bg_ref, wout_ref,
     wrh_ref, wrl_ref, rb_ref) = refs[:21]
    refs = refs[21:]
    if cfg.has_cache:
        kc_ref, vc_ref, cs_ref = refs[:3]
        refs = refs[3:]
    if cfg.sparse:
        tri_ref = refs[0]
        refs = refs[1:]
    refs = refs[n_state:]
    x1_ref, h2_ref, tok_ref, kout_ref, vout_ref, cout_ref = refs[:6]
    refs = refs[6:]
    if cfg.sparse:
        eid_ref, rank_ref, cnt_ref = refs[:3]
        refs = refs[3:]
    qbuf, kbuf, vbuf, obuf, uext, cvbuf = refs

    tt = cfg.tt
    t = pl.program_id(1)

    if cfg.has_cache:
        kbuf[0:PREV, :] = kc_ref[0].astype(BF16)
        vbuf[0:PREV, :] = vc_ref[0].astype(BF16)
        uext[CONV_PAD - (CONV_WIDTH - 1):CONV_PAD, :] = cs_ref[0]
    else:
        @pl.when(t == 0)
        def _():
            kbuf[0:PREV, :] = jnp.zeros((PREV, ATTN_DIM), BF16)
            vbuf[0:PREV, :] = jnp.zeros((PREV, ATTN_DIM), BF16)
            uext[0:CONV_PAD, :] = jnp.zeros((CONV_PAD, uext.shape[1]), F32)

    mod = mod_ref[0]
    sh1, sc1, g1, sh2, sc2 = (mod[i:i + 1, :] for i in range(5))

    x = x_ref[0]
    hb = (_rms(x, n1g_ref[...]) * (1.0 + sc1) + sh1).astype(BF16)

    a0, a1, a2, a3, a4, a5 = (0, ATTN_DIM, 2 * ATTN_DIM, 3 * ATTN_DIM,
                              3 * ATTN_DIM + cw_ref.shape[1], 3 * ATTN_DIM + 2 * cw_ref.shape[1])
    d_model = x.shape[1]

    q = _head_rms(_dot(hb, win_ref[:, a0:a1]), hsum_ref[...], qg_ref[...])
    qbuf[...] = (q * (HEAD_DIM ** -0.5 * LOG2_E)).astype(BF16)
    k = _head_rms(_dot(hb, win_ref[:, a1:a2]), hsum_ref[...], kg_ref[...])
    kout_ref[0] = k
    kbuf[PREV:PREV + tt, :] = k.astype(BF16)
    v = _dot(hb, win_ref[:, a2:a3])
    vout_ref[0] = v
    vbuf[PREV:PREV + tt, :] = v.astype(BF16)

    u = _dot(hb, win_ref[:, a3:a4]) * _sigmoid_t(_dot(hb, win_ref[:, a4:a5]))
    uext[CONV_PAD:CONV_PAD + tt, :] = u
    cout_ref[0] = uext[CONV_PAD + tt - (CONV_WIDTH - 1):CONV_PAD + tt, :]

    rc = min(tt, 64)
    cc = min(u.shape[1], LANES)
    shifts = [j + CONV_PAD - (CONV_WIDTH - 1) for j in range(CONV_WIDTH)]

    def conv_unit(r0, c0):
        acc = jnp.broadcast_to(cb_ref[:, c0:c0 + cc], (rc, cc))
        for res in range(SUBLANES):
            group = [s for s in shifts if s % SUBLANES == res]
            if not group:
                continue
            lo, hi = min(group) - res, max(group) - res
            rows = hi - lo + rc + (SUBLANES if res else 0)
            slab = uext[pl.ds(r0 + lo, rows), c0:c0 + cc]
            if res:
                slab = pltpu.roll(slab, rows - res, 0)
            for s in group:
                j = s - shifts[0]
                a = s - res - lo
                acc = acc + cw_ref[j:j + 1, c0:c0 + cc] * slab[a:a + rc, :]
        cvbuf[pl.ds(r0, rc), c0:c0 + cc] = acc

    units = [(r0, c0) for r0 in range(0, tt, rc) for c0 in range(0, u.shape[1], cc)]
    gw = 256
    n_chunks = 2 * d_model // gw
    gates = []
    for ci in range(n_chunks):
        cols = slice(ci * gw, (ci + 1) * gw)
        gates.append(_sigmoid_t(_dot(hb, win_ref[:, a5 + ci * gw:a5 + (ci + 1) * gw]) + bg_ref[:, cols]))
        for unit in units[ci::n_chunks]:
            conv_unit(*unit)
    g_attn = jnp.concatenate(gates[:n_chunks // 2], axis=1)
    g_conv = jnp.concatenate(gates[n_chunks // 2:], axis=1)

    def attend(r0, k0):
        if not cfg.has_cache:
            col = lax.broadcasted_iota(I32, (1, cfg.kb), 1)
            valid = jnp.logical_or(col >= PREV - k0, t > 0)
        heads = []
        for h in range(N_HEADS):
            c0, c1 = h * HEAD_DIM, (h + 1) * HEAD_DIM
            s = lax.dot_general(qbuf[pl.ds(r0, cfg.qb), c0:c1], kbuf[pl.ds(k0, cfg.kb), c0:c1],
                                (((1,), (1,)), ((), ())), preferred_element_type=F32)
            s = s + bias_ref[h]
            if not cfg.has_cache:
                s = jnp.where(valid, s, NEG)
            p = jnp.exp2(s - jnp.max(s, axis=-1, keepdims=True))
            l = jnp.sum(p, axis=-1, keepdims=True)
            heads.append(_dot(p.astype(BF16), vbuf[pl.ds(k0, cfg.kb), c0:c1]) * (1.0 / l))
        obuf[pl.ds(r0, cfg.qb), :] = jnp.concatenate(heads, axis=1).astype(BF16)

    for j in range(cfg.n_qblk):
        attend(j * cfg.qb, j * cfg.kstride)
    y_attn = _dot(obuf[...], wao_ref[...])
    if not cfg.has_cache:
        kbuf[0:PREV, :] = kbuf[tt:tt + PREV, :]
        vbuf[0:PREV, :] = vbuf[tt:tt + PREV, :]

    if not cfg.has_cache:
        uext[0:CONV_PAD, :] = uext[tt:tt + CONV_PAD, :]
    cv = cvbuf[...]
    mu = jnp.mean(cv, axis=-1, keepdims=True)
    xc = cv - mu
    var = jnp.mean(xc * xc, axis=-1, keepdims=True)
    cv = xc * lax.rsqrt(var + EPS) * lng_ref[...] + lnb_ref[...]
    cv = cv * _sigmoid_t(cv)
    y_conv = _dot(cv.astype(BF16), wco_ref[...]) + bco_ref[...]

    m = g_attn * y_attn + g_conv * y_conv
    x1 = x + g1 * _dot(m.astype(BF16), wout_ref[...])
    x1_ref[0] = x1

    h2 = _rms(x1, n2g_ref[...]) * (1.0 + sc2) + sh2
    h2_ref[0] = _pack(h2[:, 0:d_model // 2], h2[:, d_model // 2:d_model])
    h2_hi, h2_lo = _split_bf16(h2)
    nt_dims = (((1,), (1,)), ((), ()))
    logits = (lax.dot_general(wrh_ref[...], h2_hi, nt_dims, preferred_element_type=F32)
              + lax.dot_general(wrl_ref[...], h2_hi, nt_dims, preferred_element_type=F32)
              + lax.dot_general(wrh_ref[...], h2_lo, nt_dims, preferred_element_type=F32))
    s = _sigmoid(logits)
    eids, weights = _route(s + rb_ref[...], s)
    ids = _expert_ids(tt)
    if cfg.sparse:
        sel = [functools.reduce(jnp.add, [jnp.where(ids[g] == e, 1.0, 0.0) for e in eids]) for g in range(N_GROUPS)]
        sel = jnp.concatenate(sel, axis=0)
        rank = _dot(sel.astype(BF16), tri_ref[...])
        rank_slabs = [rank[g * GROUP_SIZE:(g + 1) * GROUP_SIZE, :] for g in range(N_GROUPS)]
        tok_ref[0] = _lane_dense_rows(weights, tt)
        eid_ref[...] = jnp.concatenate(eids, axis=0)
        rank_ref[...] = jnp.concatenate([_pick(ids, e, rank_slabs) for e in eids], axis=0)
        step = pl.program_id(0) * pl.num_programs(1) + t

        @pl.when(step == 0)
        def _():
            cnt_ref[...] = jnp.zeros(cnt_ref.shape, F32)

        lane = lax.broadcasted_iota(I32, cnt_ref.shape, 1)
        cnt_ref[...] = jnp.where(lane == step, jnp.sum(sel, axis=1, keepdims=True), cnt_ref[...])
    else:
        gates = [functools.reduce(jnp.add, [jnp.where(ids[g] == e, w, 0.0) for e, w in zip(eids, weights)])
                 for g in range(N_GROUPS)]
        gates = jnp.concatenate(gates + [jnp.zeros((LANES - N_EXPERTS, tt), F32)], axis=0)
        tok_ref[0] = gates.T


def _toeplitz(v, rows, cols):
    w = rows + cols
    flat = jnp.tile(v, (1, rows))[:, :rows * (w - 1)]
    return flat.reshape(v.shape[0], rows, w - 1)[:, :, :cols]


def _rel_bias_blocks(table, cfg, q_pos0, k_pos0):
    q_pos = q_pos0 + jnp.arange(cfg.qb)
    k_pos = k_pos0 + jnp.arange(cfg.kb)
    qc = q_pos // CHUNK
    kc = k_pos // CHUNK
    allowed = (kc[None, :] <= qc[:, None]) & (kc[None, :] >= qc[:, None] - BAND_CHUNKS)
    w = cfg.qb + cfg.kb
    dj = jnp.arange(w)
    dj = jnp.where(dj < cfg.kb, dj, dj - w)
    rel = jnp.clip(q_pos0 - k_pos0 - dj, -MAX_REL, MAX_REL) + MAX_REL
    bias = _toeplitz(table[:, rel].astype(F32), cfg.qb, cfg.kb)
    return jnp.where(allowed[None], bias * LOG2_E, NEG)


def _mixer_cost(cfg, tokens, d, n_in, conv_dim):
    per_token = (2 * d * n_in + 2 * (ATTN_DIM + conv_dim + d) * d + 4 * ATTN_DIM * cfg.kb
                 + 4 * ATTN_DIM * ATTN_DIM + 6 * N_EXPERTS * d + 2 * CONV_WIDTH * conv_dim)
    return pl.CostEstimate(flops=tokens * per_token,
                           transcendentals=tokens * (N_HEADS * cfg.kb + 2 * d + 2 * conv_dim + N_EXPERTS),
                           bytes_accessed=tokens * (4 * d + 4 * d + 2 * d + 4 * LANES) + 2 * d * n_in)


def _mixer_call(cfg, x, mod, p, cache=None, batch0=0, nbatch=None, state=None):
    b_all, t_total, d = x.shape
    nb = b_all if nbatch is None else nbatch
    nt = t_total // cfg.tt
    conv_dim = p["conv_w"].shape[1]
    n_in = p["w_in"].shape[1]

    def const(shape):
        return pl.BlockSpec(shape, lambda i, j: (0,) * len(shape), pipeline_mode=pl.Buffered(1))

    def per_row(shape):
        return pl.BlockSpec(shape, lambda i, j: (i + batch0, 0, 0))

    in_specs = [
        pl.BlockSpec((1, cfg.tt, d), lambda i, j: (i + batch0, j, 0)),
        per_row((1, 6, d)),
        const((1, d)), const((1, d)), const((d, n_in)),
        const((1, ATTN_DIM)), const((1, ATTN_DIM)), const((ATTN_DIM, ATTN_DIM)),
        const((N_HEADS, cfg.qb, cfg.kb)), const((ATTN_DIM, d)),
        const((CONV_WIDTH, conv_dim)), const((1, conv_dim)), const((1, conv_dim)), const((1, conv_dim)),
        const((conv_dim, d)), const((1, d)), const((1, 2 * d)), const((d, d)),
        const((N_EXPERTS, d)), const((N_EXPERTS, d)), const((N_EXPERTS, 1)),
    ]
    args = [x, mod, p["norm1_g"], p["norm2_g"], p["w_in"], p["q_norm_g"], p["k_norm_g"], p["hsum"],
            p["bias"], p["w_attn_o"], p["conv_w"], p["conv_b"], p["conv_ln_g"], p["conv_ln_b"],
            p["w_conv_o"], p["b_conv_o"], p["b_gate"], p["w_out"], p["wr_hi"], p["wr_lo"], p["router_bias"]]
    if cfg.has_cache:
        in_specs += [per_row((1, PREV, ATTN_DIM)), per_row((1, PREV, ATTN_DIM)),
                     per_row((1, CONV_WIDTH - 1, conv_dim))]
        args += list(cache)
    if cfg.sparse:
        assert nb * nt <= MAX_TILES
        in_specs += [const((cfg.tt, cfg.tt))]
        tok = jnp.arange(cfg.tt)
        args += [(tok[:, None] < tok[None, :]).astype(BF16)]
    aliases = {}
    if state is not None:
        for i, arr in enumerate(state):
            aliases[len(args)] = 3 + i
            in_specs.append(pl.BlockSpec(memory_space=pl.ANY))
            args.append(arr)
    keep = min(PREV, t_total)
    assert keep == cfg.tt
    out_shape = [jax.ShapeDtypeStruct((nb, t_total, d), F32),
                 jax.ShapeDtypeStruct((nb, t_total, d // 2), I32),
                 jax.ShapeDtypeStruct((nb, t_total, LANES), F32),
                 jax.ShapeDtypeStruct((b_all, keep, ATTN_DIM), F32),
                 jax.ShapeDtypeStruct((b_all, keep, ATTN_DIM), F32),
                 jax.ShapeDtypeStruct((b_all, CONV_WIDTH - 1, conv_dim), F32)]
    out_specs = [pl.BlockSpec((1, cfg.tt, d), lambda i, j: (i, j, 0)),
                 pl.BlockSpec((1, cfg.tt, d // 2), lambda i, j: (i, j, 0)),
                 pl.BlockSpec((1, cfg.tt, LANES), lambda i, j: (i, j, 0)),
                 per_row((1, keep, ATTN_DIM)), per_row((1, keep, ATTN_DIM)),
                 per_row((1, CONV_WIDTH - 1, conv_dim))]
    if cfg.sparse:
        out_shape += [jax.ShapeDtypeStruct((TOP_K, nb * t_total), F32),
                      jax.ShapeDtypeStruct((TOP_K, nb * t_total), F32),
                      jax.ShapeDtypeStruct((N_EXPERTS, MAX_TILES), F32)]
        out_specs += [pl.BlockSpec((TOP_K, cfg.tt), lambda i, j: (0, i * nt + j)),
                      pl.BlockSpec((TOP_K, cfg.tt), lambda i, j: (0, i * nt + j)),
                      pl.BlockSpec((N_EXPERTS, MAX_TILES), lambda i, j: (0, 0))]
    scratch = [pltpu.VMEM((cfg.tt, ATTN_DIM), BF16),
               pltpu.VMEM((PREV + cfg.tt, ATTN_DIM), BF16),
               pltpu.VMEM((PREV + cfg.tt, ATTN_DIM), BF16),
               pltpu.VMEM((cfg.tt, ATTN_DIM), BF16),
               pltpu.VMEM((CONV_PAD + cfg.tt, conv_dim), F32),
               pltpu.VMEM((cfg.tt, conv_dim), F32)]
    return pl.pallas_call(
        functools.partial(_mixer_kernel, cfg, 0 if state is None else len(state)),
        out_shape=out_shape,
        grid=(nb, nt),
        in_specs=in_specs,
        out_specs=out_specs,
        scratch_shapes=scratch,
        input_output_aliases=aliases,
        cost_estimate=_mixer_cost(cfg, nb * t_total, d, n_in, conv_dim),
        compiler_params=pltpu.CompilerParams(dimension_semantics=("arbitrary", "arbitrary"),
                                             vmem_limit_bytes=VMEM_LIMIT_BYTES),
        name="mixer_sample" if cfg.has_cache else "mixer_prompt",
    )(*args)


def _exact_parts(x, n):
    parts = []
    for _ in range(n):
        part = x.astype(BF16)
        parts.append(part)
        x = x - part.astype(F32)
    return parts


def _plan_kernel(tt, cnt_ref, eid_ref, rank_ref, tri_e_ref, tri_t_ref, pos_ref, blk_ref, nblk_ref, base_ref):
    step = pl.program_id(0)

    @pl.when(step == 0)
    def _():
        cnt = cnt_ref[...]
        total = jnp.sum(cnt, axis=1, keepdims=True)
        padded = jnp.floor((total + float(ROW_BLOCK - 1)) * (1.0 / ROW_BLOCK)) * float(ROW_BLOCK)
        padded = jnp.broadcast_to(padded, cnt.shape)
        start = functools.reduce(jnp.add, [_dot(tri_e_ref[...], part) for part in _exact_parts(padded, 3)])
        before = functools.reduce(jnp.add, [_dot(part, tri_t_ref[...]) for part in _exact_parts(cnt, 2)])
        base_ref[...] = start + before
        end = (start + padded)[:, 0:1]
        first_row = lax.broadcasted_iota(I32, blk_ref.shape, 1).astype(F32) * float(ROW_BLOCK)
        owner = jnp.sum(jnp.where(end <= first_row, 1.0, 0.0), axis=0, keepdims=True)
        blk_ref[...] = jnp.minimum(owner, float(N_EXPERTS - 1)).astype(I32)
        nblk = jnp.max(end, axis=0, keepdims=True) * (1.0 / ROW_BLOCK)
        nblk_ref[...] = jnp.broadcast_to(nblk, nblk_ref.shape).astype(I32)

    lane = lax.broadcasted_iota(I32, base_ref.shape, 1)
    col = jnp.sum(jnp.where(lane == step, base_ref[...], 0.0), axis=1, keepdims=True)
    ids = _expert_ids(tt)
    col_slabs = [jnp.broadcast_to(col[g * GROUP_SIZE:(g + 1) * GROUP_SIZE, :], (GROUP_SIZE, tt))
                 for g in range(N_GROUPS)]
    eid = eid_ref[...]
    rows = [_pick(ids, eid[k:k + 1, :], col_slabs) for k in range(TOP_K)]
    pos_ref[...] = (jnp.concatenate(rows, axis=0) + rank_ref[...]).astype(I32)


def _plan_call(cnt, eid, rank, tt, n_blocks):
    n = eid.shape[1]
    e = jnp.arange(N_EXPERTS)
    s = jnp.arange(MAX_TILES)
    nb_pad = -(-n_blocks // LANES) * LANES
    return pl.pallas_call(
        functools.partial(_plan_kernel, tt),
        out_shape=[jax.ShapeDtypeStruct((TOP_K, n), I32),
                   jax.ShapeDtypeStruct((1, nb_pad), I32),
                   jax.ShapeDtypeStruct((1, LANES), I32)],
        grid=(n // tt,),
        in_specs=[pl.BlockSpec((N_EXPERTS, MAX_TILES), lambda i: (0, 0)),
                  pl.BlockSpec((TOP_K, tt), lambda i: (0, i)),
                  pl.BlockSpec((TOP_K, tt), lambda i: (0, i)),
                  pl.BlockSpec((N_EXPERTS, N_EXPERTS), lambda i: (0, 0)),
                  pl.BlockSpec((MAX_TILES, MAX_TILES), lambda i: (0, 0))],
        out_specs=[pl.BlockSpec((TOP_K, tt), lambda i: (0, i)),
                   pl.BlockSpec((1, nb_pad), lambda i: (0, 0)),
                   pl.BlockSpec((1, LANES), lambda i: (0, 0))],
        scratch_shapes=[pltpu.VMEM((N_EXPERTS, MAX_TILES), F32)],
        compiler_params=pltpu.CompilerParams(dimension_semantics=("arbitrary",),
                                             vmem_limit_bytes=VMEM_LIMIT_BYTES),
        name="plan",
    )(cnt, eid, rank, (e[None, :] < e[:, None]).astype(BF16), (s[:, None] < s[None, :]).astype(BF16))


def _sc_mesh():
    return plsc.VectorSubcoreMesh(core_axis_name="c", subcore_axis_name="s")


def _dispatch_call(rows, pos, n_sorted):
    n, w = rows.shape
    per_worker = n // SC_WORKERS
    nsteps = per_worker // SC_ROWS
    assert per_worker * SC_WORKERS == n and nsteps * SC_ROWS == per_worker and nsteps % 2 == 0

    @functools.partial(
        pl.kernel, mesh=_sc_mesh(),
        out_type=jax.ShapeDtypeStruct((n_sorted, w), rows.dtype),
        scratch_types=[pltpu.VMEM((2, TOP_K, SC_ROWS), I32),
                       pltpu.VMEM((2, SC_ROWS, w), rows.dtype),
                       pltpu.SemaphoreType.DMA((2,)),
                       pltpu.SemaphoreType.DMA((2,))],
        cost_estimate=pl.CostEstimate(flops=0, transcendentals=0,
                                      bytes_accessed=(1 + TOP_K) * n * w * rows.dtype.itemsize + 4 * TOP_K * n),
    )
    def dispatch(x_hbm, pos_hbm, out_hbm, idx_v, rows_v, load_sem, scat_sem):
        base = (lax.axis_index("s") * SC_CORES + lax.axis_index("c")) * per_worker

        def load(i, b):
            return pltpu.make_async_copy(x_hbm.at[pl.ds(base + i * SC_ROWS, SC_ROWS)], rows_v.at[b], load_sem.at[b])

        def scatter(b, k):
            return pltpu.make_async_copy(rows_v.at[b], out_hbm.at[idx_v.at[b, k]], scat_sem.at[b])

        def load_start(i, b):
            for k in range(TOP_K):
                pltpu.sync_copy(pos_hbm.at[pl.ds(k * n + base + i * SC_ROWS, SC_ROWS)], idx_v.at[b, k])
            load(i, b).start()

        load_start(0, 0)

        @pl.loop(0, nsteps, step=2)
        def _(i):
            for b in range(2):
                ii = i + b

                @pl.when(ii >= 1)
                def _():
                    for k in range(TOP_K):
                        scatter(1 - b, k).wait()

                @pl.when(ii + 1 < nsteps)
                def _():
                    load_start(ii + 1, 1 - b)

                load(ii, b).wait()
                for k in range(TOP_K):
                    scatter(b, k).start()

        for k in range(TOP_K):
            scatter((nsteps - 1) % 2, k).wait()

    return dispatch(rows, pos)


def _collect_call(table, idx):
    n = idx.shape[0]
    w = table.shape[1]
    per_worker = n // SC_WORKERS
    nsteps = per_worker // SC_ROWS
    assert per_worker * SC_WORKERS == n and nsteps * SC_ROWS == per_worker and nsteps % 2 == 0

    @functools.partial(
        pl.kernel, mesh=_sc_mesh(),
        out_type=jax.ShapeDtypeStruct((n, w), table.dtype),
        scratch_types=[pltpu.VMEM((2, SC_ROWS), I32),
                       pltpu.VMEM((2, SC_ROWS, w), table.dtype),
                       pltpu.SemaphoreType.DMA((2,)),
                       pltpu.SemaphoreType.DMA((2,))],
        cost_estimate=pl.CostEstimate(flops=0, transcendentals=0,
                                      bytes_accessed=2 * n * w * table.dtype.itemsize + 4 * n),
    )
    def collect(table_hbm, idx_hbm, out_hbm, idx_v, rows_v, gather_sem, write_sem):
        base = (lax.axis_index("s") * SC_CORES + lax.axis_index("c")) * per_worker

        def gather(b):
            return pltpu.make_async_copy(table_hbm.at[idx_v.at[b]], rows_v.at[b], gather_sem.at[b])

        def write(i, b):
            return pltpu.make_async_copy(rows_v.at[b], out_hbm.at[pl.ds(base + i * SC_ROWS, SC_ROWS)], write_sem.at[b])

        def gather_start(i, b):
            pltpu.sync_copy(idx_hbm.at[pl.ds(base + i * SC_ROWS, SC_ROWS)], idx_v.at[b])
            gather(b).start()

        gather_start(0, 0)

        @pl.loop(0, nsteps, step=2)
        def _(i):
            for b in range(2):
                ii = i + b

                @pl.when(ii >= 1)
                def _():
                    write(ii - 1, 1 - b).wait()

                @pl.when(ii + 1 < nsteps)
                def _():
                    gather_start(ii + 1, 1 - b)

                gather(b).wait()
                write(ii, b).start()

        write(nsteps - 1, (nsteps - 1) % 2).wait()

    return collect(table, idx)


def _expert_ffn_kernel(blk_ref, nblk_ref, x_ref, wg_ref, wu_ref, wd_ref, o_ref, wg_s, wu_s, wd_s):
    j = pl.program_id(0)

    @pl.when(jnp.logical_or(j == 0, blk_ref[j] != blk_ref[jnp.maximum(j - 1, 0)]))
    def _():
        wg_s[...] = wg_ref[0].astype(BF16)
        wu_s[...] = wu_ref[0].astype(BF16)
        wd_s[...] = wd_ref[0].astype(BF16)

    @pl.when(j < nblk_ref[0])
    def _():
        a, b = _unpack(x_ref[...])
        a, b = a.astype(BF16), b.astype(BF16)
        hg = _dot_halves(a, b, wg_s)
        act = hg * _sigmoid_t(hg) * _dot_halves(a, b, wu_s)
        y = _dot(act.astype(BF16), wd_s[...])
        half = y.shape[1] // 2
        o_ref[...] = _pack(y[:, 0:half], y[:, half:2 * half])


def _expert_ffn_call(xs, blk, nblk, p):
    n_sorted, w = xs.shape
    d = 2 * w
    ff = p["w_e_gate"].shape[2]

    def rows(j, blk, nblk):
        return (jnp.minimum(j, nblk[0] - 1), 0)

    def expert(j, blk, nblk):
        return (blk[j], 0, 0)

    return pl.pallas_call(
        _expert_ffn_kernel,
        out_shape=jax.ShapeDtypeStruct((n_sorted, w), I32),
        grid_spec=pltpu.PrefetchScalarGridSpec(
            num_scalar_prefetch=2,
            grid=(n_sorted // ROW_BLOCK,),
            in_specs=[pl.BlockSpec((ROW_BLOCK, w), rows),
                      pl.BlockSpec((1, d, ff), expert),
                      pl.BlockSpec((1, d, ff), expert),
                      pl.BlockSpec((1, ff, d), expert)],
            out_specs=pl.BlockSpec((ROW_BLOCK, w), rows),
            scratch_shapes=[pltpu.VMEM((d, ff), BF16), pltpu.VMEM((d, ff), BF16), pltpu.VMEM((ff, d), BF16)]),
        cost_estimate=pl.CostEstimate(flops=6 * n_sorted * d * ff, transcendentals=n_sorted * ff,
                                      bytes_accessed=8 * n_sorted * w + 12 * N_EXPERTS * d * ff),
        compiler_params=pltpu.CompilerParams(dimension_semantics=("arbitrary",),
                                             vmem_limit_bytes=VMEM_LIMIT_BYTES),
        name="expert_ffn",
    )(blk, nblk, xs, p["w_e_gate"], p["w_e_up"], p["w_e_down"])


def _swiglu_halves(a, b, wg_ref, wu_ref):
    hg = _dot_halves(a, b, wg_ref)
    return hg * _sigmoid_t(hg) * _dot_halves(a, b, wu_ref)


def _moe_out_kernel(n_groups, starts, *refs):
    groups = [refs[4 * q:4 * q + 4] for q in range(n_groups)]
    g2_ref, wsg_ref, wsu_ref, wsd_ref, o_ref = refs[4 * n_groups:]

    def combine(x1_ref, h_ref, y_ref, w_ref):
        a, b = _unpack(h_ref[...])
        shared = _dot(_swiglu_halves(a.astype(BF16), b.astype(BF16), wsg_ref, wsu_ref).astype(BF16),
                      wsd_ref[...].astype(BF16))
        half = h_ref.shape[1]
        w = w_ref[...]
        acc_a, acc_b = shared[:, 0:half], shared[:, half:2 * half]
        for k in range(TOP_K):
            ya, yb = _unpack(y_ref[k])
            acc_a = acc_a + w[:, k:k + 1] * ya
            acc_b = acc_b + w[:, k:k + 1] * yb
        g2 = g2_ref[...]
        o_ref[:, 0:half] = x1_ref[:, 0:half] + g2[:, 0:half] * acc_a
        o_ref[:, half:2 * half] = x1_ref[:, half:2 * half] + g2[:, half:2 * half] * acc_b

    i = pl.program_id(0)
    for q in range(n_groups):
        pl.when(jnp.logical_and(i >= starts[q], i < starts[q + 1]))(functools.partial(combine, *groups[q]))


def _moe_out_call(groups, g2, p, tm):
    d = groups[0][0].shape[1]
    starts = [0]
    for group in groups:
        starts.append(starts[-1] + group[0].shape[0] // tm)
    n_all = starts[-1] * tm
    sff = p["w_s_gate"].shape[1]
    per_row = n_all // g2.shape[0]
    g2 = g2.reshape(g2.shape[0], 1, d)

    def const(shape):
        return pl.BlockSpec(shape, lambda i: (0,) * len(shape), pipeline_mode=pl.Buffered(1))

    in_specs, args = [], []
    for q, group in enumerate(groups):
        tile = lambda i, q=q: jnp.clip(i - starts[q], 0, starts[q + 1] - starts[q] - 1)
        in_specs += [pl.BlockSpec((tm, d), lambda i, tile=tile: (tile(i), 0)),
                     pl.BlockSpec((tm, d // 2), lambda i, tile=tile: (tile(i), 0)),
                     pl.BlockSpec((TOP_K, tm, d // 2), lambda i, tile=tile: (0, tile(i), 0)),
                     pl.BlockSpec((tm, LANES), lambda i, tile=tile: (tile(i), 0))]
        args += list(group)
    in_specs += [pl.BlockSpec((None, 1, d), lambda i: (i * tm // per_row, 0, 0)),
                 const((d, sff)), const((d, sff)), const((sff, d))]
    args += [g2, p["w_s_gate"], p["w_s_up"], p["w_s_down"]]
    return pl.pallas_call(
        functools.partial(_moe_out_kernel, len(groups), tuple(starts)),
        out_shape=jax.ShapeDtypeStruct((n_all, d), F32),
        grid=(n_all // tm,),
        in_specs=in_specs,
        out_specs=pl.BlockSpec((tm, d), lambda i: (i, 0)),
        cost_estimate=pl.CostEstimate(flops=n_all * (6 * d * sff + 2 * TOP_K * d), transcendentals=n_all * sff,
                                      bytes_accessed=n_all * (8 * d + 2 * d + 2 * TOP_K * d + 4 * LANES)),
        compiler_params=pltpu.CompilerParams(dimension_semantics=("arbitrary",),
                                             vmem_limit_bytes=VMEM_LIMIT_BYTES),
        name="moe_out",
    )(*args)


def _moe_dense_kernel(h_ref, gates_ref, x1_ref, g2_ref, wsg_ref, wsu_ref, wsd_ref, wg_ref, wu_ref, wd_ref,
                      o_ref, acc_ref):
    e = pl.program_id(1)
    a, b = _unpack(h_ref[...])
    a, b = a.astype(BF16), b.astype(BF16)

    @pl.when(e == 0)
    def _():
        acc_ref[...] = _dot(_swiglu_halves(a, b, wsg_ref, wsu_ref).astype(BF16), wsd_ref[...].astype(BF16))

    gates = gates_ref[...]
    lane = lax.broadcasted_iota(I32, gates.shape, 1)
    gate = jnp.sum(jnp.where(lane == e, gates, 0.0), axis=1, keepdims=True)
    act = _swiglu_halves(a, b, wg_ref.at[0], wu_ref.at[0]) * gate
    acc_ref[...] += _dot(act.astype(BF16), wd_ref[0].astype(BF16))

    @pl.when(e == pl.num_programs(1) - 1)
    def _():
        o_ref[...] = x1_ref[...] + g2_ref[...] * acc_ref[...]


def _moe_dense_call(h2, gates, x1, g2_rows, p):
    n, d = x1.shape
    ff = p["w_e_gate"].shape[2]
    sff = p["w_s_gate"].shape[1]

    def const(shape):
        return pl.BlockSpec(shape, lambda i, e: (0,) * len(shape), pipeline_mode=pl.Buffered(1))

    return pl.pallas_call(
        _moe_dense_kernel,
        out_shape=jax.ShapeDtypeStruct((n, d), F32),
        grid=(1, N_EXPERTS),
        in_specs=[const((n, d // 2)), const((n, LANES)), const((n, d)), const((n, d)),
                  const((d, sff)), const((d, sff)), const((sff, d)),
                  pl.BlockSpec((1, d, ff), lambda i, e: (e, 0, 0)),
                  pl.BlockSpec((1, d, ff), lambda i, e: (e, 0, 0)),
                  pl.BlockSpec((1, ff, d), lambda i, e: (e, 0, 0))],
        out_specs=pl.BlockSpec((n, d), lambda i, e: (0, 0)),
        scratch_shapes=[pltpu.VMEM((n, d), F32)],
        compiler_params=pltpu.CompilerParams(dimension_semantics=("arbitrary", "arbitrary"),
                                             vmem_limit_bytes=VMEM_LIMIT_BYTES),
        name="moe_dense",
    )(h2, gates, x1, g2_rows, p["w_s_gate"], p["w_s_up"], p["w_s_down"], p["w_e_gate"], p["w_e_up"], p["w_e_down"])


PROMPT_CFG = MixerCfg(tt=512, qb=256, kb=768, n_qblk=2, kstride=256, has_cache=False, sparse=True)
MOE_OUT_TILE = 256
PROMPT_SPLIT = (1, 1)


def _layer_params(l, w_in, q_norm_g, k_norm_g, w_attn_o, conv_w, conv_b, conv_ln_g, conv_ln_b, w_conv_o,
                  b_conv_o, b_gate, w_out, norm1_g, norm2_g, w_router, router_bias,
                  w_e_gate, w_e_up, w_e_down, w_s_gate, w_s_up, w_s_down):
    row = lambda a: a[l].reshape(1, -1)
    wr_t = w_router[l].T
    wr_hi = wr_t.astype(BF16)
    head = jnp.arange(ATTN_DIM) // HEAD_DIM
    return {
        "norm1_g": row(norm1_g), "norm2_g": row(norm2_g), "w_in": w_in[l].astype(BF16),
        "q_norm_g": jnp.tile(q_norm_g[l], N_HEADS).reshape(1, -1),
        "k_norm_g": jnp.tile(k_norm_g[l], N_HEADS).reshape(1, -1),
        "hsum": (head[:, None] == head[None, :]).astype(BF16),
        "w_attn_o": w_attn_o[l].astype(BF16), "conv_w": conv_w[l], "conv_b": row(conv_b),
        "conv_ln_g": row(conv_ln_g), "conv_ln_b": row(conv_ln_b), "w_conv_o": w_conv_o[l].astype(BF16),
        "b_conv_o": row(b_conv_o), "b_gate": row(b_gate), "w_out": w_out[l].astype(BF16),
        "wr_hi": wr_hi, "wr_lo": (wr_t - wr_hi.astype(F32)).astype(BF16),
        "router_bias": router_bias[l].reshape(-1, 1),
        "w_e_gate": w_e_gate[l], "w_e_up": w_e_up[l], "w_e_down": w_e_down[l],
        "w_s_gate": w_s_gate[l], "w_s_up": w_s_up[l], "w_s_down": w_s_down[l],
    }


def _sorted_rows(h2, eid, rank, cnt, tt):
    n = h2.shape[0]
    n_sorted = n * TOP_K + N_EXPERTS * ROW_BLOCK
    pos, blk, nblk = _plan_call(cnt, eid, rank, tt, n_sorted // ROW_BLOCK)
    pos = pos.reshape(TOP_K * n)
    return _dispatch_call(h2, pos, n_sorted), pos, blk.reshape(-1), nblk.reshape(-1)


def kernel(x_prompt, x_sample, c_prompt, c_sample, cache_k, cache_v, state_conv, w_mod, b_mod, norm1_g, w_in, q_norm_g, k_norm_g, rel_bias, w_attn_o, conv_w, conv_b, conv_ln_g, conv_ln_b, w_conv_o, b_conv_o, b_gate, w_out, norm2_g, w_router, router_bias, w_e_gate, w_e_up, w_e_down, w_s_gate, w_s_up, w_s_down):
    depth = w_mod.shape[0]
    bp, tp, d = x_prompt.shape
    bs, ts, _ = x_sample.shape
    assert cache_k.shape[2] == min(PREV, PAST_LEN) == PREV
    sample_cfg = MixerCfg(tt=ts, qb=ts, kb=PREV + ts, n_qblk=1, kstride=0, has_cache=True, sparse=False)

    yp, ys = x_prompt, x_sample
    outs = [[] for _ in range(6)]
    for l in range(depth):
        p = _layer_params(l, w_in, q_norm_g, k_norm_g, w_attn_o, conv_w, conv_b, conv_ln_g, conv_ln_b,
                          w_conv_o, b_conv_o, b_gate, w_out, norm1_g, norm2_g, w_router, router_bias,
                          w_e_gate, w_e_up, w_e_down, w_s_gate, w_s_up, w_s_down)
        mod = _mod_call(jnp.concatenate([c_prompt, c_sample], axis=0), w_mod[l], b_mod[l])
        mod = mod.reshape(bp + bs, 6, d)
        mod_p, mod_s = mod[:bp], mod[bp:]

        p["bias"] = _rel_bias_blocks(rel_bias[l], PROMPT_CFG, PREV, 0)
        big = max(bp * PROMPT_SPLIT[0] // sum(PROMPT_SPLIT), 1)
        group_rows = [big, bp - big] if bp > big else [bp]
        keep = min(PREV, tp)
        state = [jnp.zeros((bp, keep, ATTN_DIM), F32), jnp.zeros((bp, keep, ATTN_DIM), F32),
                 jnp.zeros((bp, CONV_WIDTH - 1, conv_w.shape[2]), F32)]
        mixed, moved = [], []
        for g, gb in enumerate(group_rows):
            x1, h2, w_tok, *state, eid, rank, cnt = _mixer_call(PROMPT_CFG, yp, mod_p, p, batch0=sum(group_rows[:g]),
                                                                nbatch=gb, state=state)
            x1, h2, w_tok = x1.reshape(gb * tp, d), h2.reshape(gb * tp, d // 2), w_tok.reshape(gb * tp, LANES)
            mixed.append((x1, h2, w_tok))
            moved.append(_sorted_rows(h2, eid, rank, cnt, PROMPT_CFG.tt))
            state, _ = lax.optimization_barrier((state, moved[-1][1]))
        kp, vp, cp = state
        groups, gate = [], cp
        for (x1, h2, w_tok), (xs, pos, blk, nblk) in zip(mixed, moved):
            xs, _ = lax.optimization_barrier((xs, gate))
            gate = _expert_ffn_call(xs, blk, nblk, p)
            yg = _collect_call(gate, pos).reshape(TOP_K, h2.shape[0], h2.shape[1])
            groups.append((x1, h2, yg, w_tok))
        yp = _moe_out_call(groups, mod_p[:, 5, :], p, MOE_OUT_TILE).reshape(bp, tp, d)
        outs[0].append(kp.reshape(bp, -1, N_HEADS, HEAD_DIM))
        outs[1].append(vp.reshape(bp, -1, N_HEADS, HEAD_DIM))
        outs[2].append(cp)

        p["bias"] = _rel_bias_blocks(rel_bias[l], sample_cfg, PAST_LEN, PAST_LEN - PREV)
        cache = (cache_k[l].reshape(bs, PREV, ATTN_DIM), cache_v[l].reshape(bs, PREV, ATTN_DIM), state_conv[l])
        x1, h2, gates, ks, vs, cs = _mixer_call(sample_cfg, ys, mod_s, p, cache)
        ys = _moe_dense_call(h2.reshape(bs * ts, d // 2), gates.reshape(bs * ts, LANES), x1.reshape(bs * ts, d),
                             jnp.repeat(mod_s[:, 5, :], ts, axis=0), p).reshape(bs, ts, d)
        outs[3].append(ks.reshape(bs, ts, N_HEADS, HEAD_DIM))
        outs[4].append(vs.reshape(bs, ts, N_HEADS, HEAD_DIM))
        outs[5].append(cs)
    return (yp, ys) + tuple(jnp.stack(o) for o in outs)
```

```python
import dataclasses
import functools

import jax
import jax.numpy as jnp
from jax import lax
from jax.experimental import pallas as pl
from jax.experimental.pallas import tpu as pltpu
from jax.experimental.pallas import tpu_sc as plsc

F32 = jnp.float32
BF16 = jnp.bfloat16
I32 = jnp.int32

CHUNK = 64
BAND_CHUNKS = 8
PREV = BAND_CHUNKS * CHUNK
PAST_LEN = 1024
N_HEADS = 8
HEAD_DIM = 64
ATTN_DIM = N_HEADS * HEAD_DIM
MAX_REL = 128
CONV_WIDTH = 31
CONV_PAD = 32
N_EXPERTS = 64
N_GROUPS = 8
GROUP_SIZE = N_EXPERTS // N_GROUPS
TOPK_GROUPS = 4
TOP_K = 8
ROUTE_SCALE = 2.5
EPS = 1e-6
LANES = 128
SUBLANES = 8
NEG = -1e30
LOG2_E = 1.4426950408889634

VMEM_LIMIT_BYTES = 60 * 1024 * 1024

SC_CORES = 2
SC_SUBCORES = 16
SC_WORKERS = SC_CORES * SC_SUBCORES
SC_ROWS = 64

MAX_TILES = LANES
ROW_BLOCK = 1024


def _sigmoid(x):
    return 1.0 / (1.0 + jnp.exp(-x))


def _sigmoid_t(x):
    return 0.5 * jnp.tanh(0.5 * x) + 0.5


def _split_bf16(x):
    hi = x.astype(BF16)
    lo = (x - hi.astype(F32)).astype(BF16)
    return hi, lo


def _dot(a, b):
    return jnp.dot(a, b, preferred_element_type=F32)


def _dot3(a, b):
    a_hi, a_lo = _split_bf16(a)
    b_hi, b_lo = _split_bf16(b)
    return _dot(a_hi, b_hi) + _dot(a_hi, b_lo) + _dot(a_lo, b_hi)


def _pack(a, b):
    ia = lax.bitcast_convert_type(a.astype(BF16).astype(F32), I32)
    ib = lax.bitcast_convert_type(b.astype(BF16).astype(F32), I32)
    return ia | lax.shift_right_logical(ib, 16)


def _unpack(p):
    a = lax.bitcast_convert_type(p & jnp.int32(-65536), F32)
    b = lax.bitcast_convert_type(lax.shift_left(p, 16), F32)
    return a, b


def _dot_halves(a, b, w_ref):
    half = a.shape[1]
    return _dot(a, w_ref[0:half, :].astype(BF16)) + _dot(b, w_ref[half:2 * half, :].astype(BF16))


def _mod_kernel(c_ref, w_ref, b_ref, o_ref):
    c = c_ref[...]
    o_ref[...] = _dot3(c * _sigmoid(c), w_ref[...]) + b_ref[...]


def _mod_call(c, w_mod, b_mod):
    n, d = c.shape
    dout = w_mod.shape[1]
    bn = 1536
    return pl.pallas_call(
        _mod_kernel,
        out_shape=jax.ShapeDtypeStruct((n, dout), F32),
        grid=(dout // bn,),
        in_specs=[pl.BlockSpec((n, d), lambda j: (0, 0)),
                  pl.BlockSpec((d, bn), lambda j: (0, j)),
                  pl.BlockSpec((1, bn), lambda j: (0, j))],
        out_specs=pl.BlockSpec((n, bn), lambda j: (0, j)),
        compiler_params=pltpu.CompilerParams(dimension_semantics=("arbitrary",),
                                             vmem_limit_bytes=VMEM_LIMIT_BYTES),
        name="mod",
    )(c, w_mod, b_mod.reshape(1, dout))


@dataclasses.dataclass(frozen=True)
class MixerCfg:
    tt: int
    qb: int
    kb: int
    n_qblk: int
    kstride: int
    has_cache: bool
    sparse: bool


def _rms(x, g):
    ms = jnp.mean(x * x, axis=-1, keepdims=True)
    return x * lax.rsqrt(ms + EPS) * g


def _head_rms(z, hsum, g):
    ss = _dot((z * z).astype(BF16), hsum)
    return z * lax.rsqrt(ss * (1.0 / HEAD_DIM) + EPS) * g


def _expert_ids(t):
    sub = lax.broadcasted_iota(I32, (GROUP_SIZE, t), 0).astype(F32)
    return [sub + float(g * GROUP_SIZE) for g in range(N_GROUPS)]


def _pick(ids, eid, slabs):
    acc = functools.reduce(jnp.add, [jnp.where(ids[g] == eid, slabs[g], 0.0) for g in range(N_GROUPS)])
    return jnp.sum(acc, axis=0, keepdims=True)


def _route(choice, s):
    t = choice.shape[1]
    sub = lax.broadcasted_iota(I32, (GROUP_SIZE, t), 0).astype(F32)
    slabs = [choice[g * GROUP_SIZE:(g + 1) * GROUP_SIZE, :] for g in range(N_GROUPS)]
    s_slabs = [s[g * GROUP_SIZE:(g + 1) * GROUP_SIZE, :] for g in range(N_GROUPS)]
    gscore = []
    for c in slabs:
        m1 = jnp.max(c, axis=0, keepdims=True)
        first = jnp.min(jnp.where(c == m1, sub, float(GROUP_SIZE)), axis=0, keepdims=True)
        m2 = jnp.max(jnp.where(sub == first, -jnp.inf, c), axis=0, keepdims=True)
        gscore.append(m1 + m2)
    gsel = [jnp.zeros((1, t), F32) for _ in range(N_GROUPS)]
    for _ in range(TOPK_GROUPS):
        m = functools.reduce(jnp.maximum, gscore)
        first = functools.reduce(jnp.minimum,
                                 [jnp.where(gscore[g] == m, float(g), float(N_GROUPS)) for g in range(N_GROUPS)])
        for g in range(N_GROUPS):
            hit = first == float(g)
            gsel[g] = jnp.where(hit, 1.0, gsel[g])
            gscore[g] = jnp.where(hit, -jnp.inf, gscore[g])
    masked = [jnp.where(gsel[g] > 0.5, slabs[g], -jnp.inf) for g in range(N_GROUPS)]
    ids = _expert_ids(t)
    eids, raw = [], []
    for _ in range(TOP_K):
        m = jnp.max(functools.reduce(jnp.maximum, masked), axis=0, keepdims=True)
        cand = [jnp.where(masked[g] == m, ids[g], float(N_EXPERTS)) for g in range(N_GROUPS)]
        first = jnp.min(functools.reduce(jnp.minimum, cand), axis=0, keepdims=True)
        eids.append(first)
        raw.append(_pick(ids, first, s_slabs))
        masked = [jnp.where(ids[g] == first, -jnp.inf, masked[g]) for g in range(N_GROUPS)]
    wsum = functools.reduce(jnp.add, raw)
    return eids, [r / wsum * ROUTE_SCALE for r in raw]


def _lane_dense_rows(rows, t):
    pad = jnp.zeros((LANES - len(rows), t), F32)
    return jnp.concatenate(rows + [pad], axis=0).T


def _mixer_kernel(cfg, n_state, *refs):
    (x_ref, mod_ref, n1g_ref, n2g_ref, win_ref, qg_ref, kg_ref, hsum_ref, bias_ref, wao_ref,
     cw_ref, cb_ref, lng_ref, lnb_ref, wco_ref, bco_ref, bg_ref, wout_ref,
     wrh_ref, wrl_ref, rb_ref) = refs[:21]
    refs = refs[21:]
    if cfg.has_cache:
        kc_ref, vc_ref, cs_ref = refs[:3]
        refs = refs[3:]
    if cfg.sparse:
        tri_ref = refs[0]
        refs = refs[1:]
    refs = refs[n_state:]
    x1_ref, h2_ref, tok_ref, kout_ref, vout_ref, cout_ref = refs[:6]
    refs = refs[6:]
    if cfg.sparse:
        eid_ref, rank_ref, cnt_ref = refs[:3]
        refs = refs[3:]
    qbuf, kbuf, vbuf, obuf, uext, cvbuf = refs

    tt = cfg.tt
    t = pl.program_id(1)

    if cfg.has_cache:
        kbuf[0:PREV, :] = kc_ref[0].astype(BF16)
        vbuf[0:PREV, :] = vc_ref[0].astype(BF16)
        uext[CONV_PAD - (CONV_WIDTH - 1):CONV_PAD, :] = cs_ref[0]
    else:
        @pl.when(t == 0)
        def _():
            kbuf[0:PREV, :] = jnp.zeros((PREV, ATTN_DIM), BF16)
            vbuf[0:PREV, :] = jnp.zeros((PREV, ATTN_DIM), BF16)
            uext[0:CONV_PAD, :] = jnp.zeros((CONV_PAD, uext.shape[1]), F32)

    mod = mod_ref[0]
    sh1, sc1, g1, sh2, sc2 = (mod[i:i + 1, :] for i in range(5))

    x = x_ref[0]
    hb = (_rms(x, n1g_ref[...]) * (1.0 + sc1) + sh1).astype(BF16)

    a0, a1, a2, a3, a4, a5 = (0, ATTN_DIM, 2 * ATTN_DIM, 3 * ATTN_DIM,
                              3 * ATTN_DIM + cw_ref.shape[1], 3 * ATTN_DIM + 2 * cw_ref.shape[1])
    d_model = x.shape[1]

    q = _head_rms(_dot(hb, win_ref[:, a0:a1]), hsum_ref[...], qg_ref[...])
    qbuf[...] = (q * (HEAD_DIM ** -0.5 * LOG2_E)).astype(BF16)
    k = _head_rms(_dot(hb, win_ref[:, a1:a2]), hsum_ref[...], kg_ref[...])
    kout_ref[0] = k
    kbuf[PREV:PREV + tt, :] = k.astype(BF16)
    v = _dot(hb, win_ref[:, a2:a3])
    vout_ref[0] = v
    vbuf[PREV:PREV + tt, :] = v.astype(BF16)

    u = _dot(hb, win_ref[:, a3:a4]) * _sigmoid_t(_dot(hb, win_ref[:, a4:a5]))
    uext[CONV_PAD:CONV_PAD + tt, :] = u
    cout_ref[0] = uext[CONV_PAD + tt - (CONV_WIDTH - 1):CONV_PAD + tt, :]

    rc = min(tt, 64)
    cc = min(u.shape[1], LANES)
    shifts = [j + CONV_PAD - (CONV_WIDTH - 1) for j in range(CONV_WIDTH)]

    def conv_unit(r0, c0):
        acc = jnp.broadcast_to(cb_ref[:, c0:c0 + cc], (rc, cc))
        for res in range(SUBLANES):
            group = [s for s in shifts if s % SUBLANES == res]
            if not group:
                continue
            lo, hi = min(group) - res, max(group) - res
            rows = hi - lo + rc + (SUBLANES if res else 0)
            slab = uext[pl.ds(r0 + lo, rows), c0:c0 + cc]
            if res:
                slab = pltpu.roll(slab, rows - res, 0)
            for s in group:
                j = s - shifts[0]
                a = s - res - lo
                acc = acc + cw_ref[j:j + 1, c0:c0 + cc] * slab[a:a + rc, :]
        cvbuf[pl.ds(r0, rc), c0:c0 + cc] = acc

    units = [(r0, c0) for r0 in range(0, tt, rc) for c0 in range(0, u.shape[1], cc)]
    gw = 256
    n_chunks = 2 * d_model // gw
    gates = []
    for ci in range(n_chunks):
        cols = slice(ci * gw, (ci + 1) * gw)
        gates.append(_sigmoid_t(_dot(hb, win_ref[:, a5 + ci * gw:a5 + (ci + 1) * gw]) + bg_ref[:, cols]))
        for unit in units[ci::n_chunks]:
            conv_unit(*unit)
    g_attn = jnp.concatenate(gates[:n_chunks // 2], axis=1)
    g_conv = jnp.concatenate(gates[n_chunks // 2:], axis=1)

    def attend(r0, k0):
        if not cfg.has_cache:
            col = lax.broadcasted_iota(I32, (1, cfg.kb), 1)
            valid = jnp.logical_or(col >= PREV - k0, t > 0)
        heads = []
        for h in range(N_HEADS):
            c0, c1 = h * HEAD_DIM, (h + 1) * HEAD_DIM
            s = lax.dot_general(qbuf[pl.ds(r0, cfg.qb), c0:c1], kbuf[pl.ds(k0, cfg.kb), c0:c1],
                                (((1,), (1,)), ((), ())), preferred_element_type=F32)
            s = s + bias_ref[h]
            if not cfg.has_cache:
                s = jnp.where(valid, s, NEG)
            p = jnp.exp2(s - jnp.max(s, axis=-1, keepdims=True))
            l = jnp.sum(p, axis=-1, keepdims=True)
            heads.append(_dot(p.astype(BF16), vbuf[pl.ds(k0, cfg.kb), c0:c1]) * (1.0 / l))
        obuf[pl.ds(r0, cfg.qb), :] = jnp.concatenate(heads, axis=1).astype(BF16)

    for j in range(cfg.n_qblk):
        attend(j * cfg.qb, j * cfg.kstride)
    y_attn = _dot(obuf[...], wao_ref[...])
    if not cfg.has_cache:
        kbuf[0:PREV, :] = kbuf[tt:tt + PREV, :]
        vbuf[0:PREV, :] = vbuf[tt:tt + PREV, :]

    if not cfg.has_cache:
        uext[0:CONV_PAD, :] = uext[tt:tt + CONV_PAD, :]
    cv = cvbuf[...]
    mu = jnp.mean(cv, axis=-1, keepdims=True)
    xc = cv - mu
    var = jnp.mean(xc * xc, axis=-1, keepdims=True)
    cv = xc * lax.rsqrt(var + EPS) * lng_ref[...] + lnb_ref[...]
    cv = cv * _sigmoid_t(cv)
    y_conv = _dot(cv.astype(BF16), wco_ref[...]) + bco_ref[...]

    m = g_attn * y_attn + g_conv * y_conv
    x1 = x + g1 * _dot(m.astype(BF16), wout_ref[...])
    x1_ref[0] = x1

    h2 = _rms(x1, n2g_ref[...]) * (1.0 + sc2) + sh2
    h2_ref[0] = _pack(h2[:, 0:d_model // 2], h2[:, d_model // 2:d_model])
    h2_hi, h2_lo = _split_bf16(h2)
    nt_dims = (((1,), (1,)), ((), ()))
    logits = (lax.dot_general(wrh_ref[...], h2_hi, nt_dims, preferred_element_type=F32)
              + lax.dot_general(wrl_ref[...], h2_hi, nt_dims, preferred_element_type=F32)
              + lax.dot_general(wrh_ref[...], h2_lo, nt_dims, preferred_element_type=F32))
    s = _sigmoid(logits)
    eids, weights = _route(s + rb_ref[...], s)
    ids = _expert_ids(tt)
    if cfg.sparse:
        sel = [functools.reduce(jnp.add, [jnp.where(ids[g] == e, 1.0, 0.0) for e in eids]) for g in range(N_GROUPS)]
        sel = jnp.concatenate(sel, axis=0)
        rank = _dot(sel.astype(BF16), tri_ref[...])
        rank_slabs = [rank[g * GROUP_SIZE:(g + 1) * GROUP_SIZE, :] for g in range(N_GROUPS)]
        tok_ref[0] = _lane_dense_rows(weights, tt)
        eid_ref[...] = jnp.concatenate(eids, axis=0)
        rank_ref[...] = jnp.concatenate([_pick(ids, e, rank_slabs) for e in eids], axis=0)
        step = pl.program_id(0) * pl.num_programs(1) + t

        @pl.when(step == 0)
        def _():
            cnt_ref[...] = jnp.zeros(cnt_ref.shape, F32)

        lane = lax.broadcasted_iota(I32, cnt_ref.shape, 1)
        cnt_ref[...] = jnp.where(lane == step, jnp.sum(sel, axis=1, keepdims=True), cnt_ref[...])
    else:
        gates = [functools.reduce(jnp.add, [jnp.where(ids[g] == e, w, 0.0) for e, w in zip(eids, weights)])
                 for g in range(N_GROUPS)]
        gates = jnp.concatenate(gates + [jnp.zeros((LANES - N_EXPERTS, tt), F32)], axis=0)
        tok_ref[0] = gates.T


def _toeplitz(v, rows, cols):
    w = rows + cols
    flat = jnp.tile(v, (1, rows))[:, :rows * (w - 1)]
    return flat.reshape(v.shape[0], rows, w - 1)[:, :, :cols]


def _rel_bias_blocks(table, cfg, q_pos0, k_pos0):
    q_pos = q_pos0 + jnp.arange(cfg.qb)
    k_pos = k_pos0 + jnp.arange(cfg.kb)
    qc = q_pos // CHUNK
    kc = k_pos // CHUNK
    allowed = (kc[None, :] <= qc[:, None]) & (kc[None, :] >= qc[:, None] - BAND_CHUNKS)
    w = cfg.qb + cfg.kb
    dj = jnp.arange(w)
    dj = jnp.where(dj < cfg.kb, dj, dj - w)
    rel = jnp.clip(q_pos0 - k_pos0 - dj, -MAX_REL, MAX_REL) + MAX_REL
    bias = _toeplitz(table[:, rel].astype(F32), cfg.qb, cfg.kb)
    return jnp.where(allowed[None], bias * LOG2_E, NEG)


def _mixer_cost(cfg, tokens, d, n_in, conv_dim):
    per_token = (2 * d * n_in + 2 * (ATTN_DIM + conv_dim + d) * d + 4 * ATTN_DIM * cfg.kb
                 + 4 * ATTN_DIM * ATTN_DIM + 6 * N_EXPERTS * d + 2 * CONV_WIDTH * conv_dim)
    return pl.CostEstimate(flops=tokens * per_token,
                           transcendentals=tokens * (N_HEADS * cfg.kb + 2 * d + 2 * conv_dim + N_EXPERTS),
                           bytes_accessed=tokens * (4 * d + 4 * d + 2 * d + 4 * LANES) + 2 * d * n_in)


def _mixer_call(cfg, x, mod, p, cache=None, batch0=0, nbatch=None, state=None):
    b_all, t_total, d = x.shape
    nb = b_all if nbatch is None else nbatch
    nt = t_total // cfg.tt
    conv_dim = p["conv_w"].shape[1]
    n_in = p["w_in"].shape[1]

    def const(shape):
        return pl.BlockSpec(shape, lambda i, j: (0,) * len(shape), pipeline_mode=pl.Buffered(1))

    def per_row(shape):
        return pl.BlockSpec(shape, lambda i, j: (i + batch0, 0, 0))

    in_specs = [
        pl.BlockSpec((1, cfg.tt, d), lambda i, j: (i + batch0, j, 0)),
        per_row((1, 6, d)),
        const((1, d)), const((1, d)), const((d, n_in)),
        const((1, ATTN_DIM)), const((1, ATTN_DIM)), const((ATTN_DIM, ATTN_DIM)),
        const((N_HEADS, cfg.qb, cfg.kb)), const((ATTN_DIM, d)),
        const((CONV_WIDTH, conv_dim)), const((1, conv_dim)), const((1, conv_dim)), const((1, conv_dim)),
        const((conv_dim, d)), const((1, d)), const((1, 2 * d)), const((d, d)),
        const((N_EXPERTS, d)), const((N_EXPERTS, d)), const((N_EXPERTS, 1)),
    ]
    args = [x, mod, p["norm1_g"], p["norm2_g"], p["w_in"], p["q_norm_g"], p["k_norm_g"], p["hsum"],
            p["bias"], p["w_attn_o"], p["conv_w"], p["conv_b"], p["conv_ln_g"], p["conv_ln_b"],
            p["w_conv_o"], p["b_conv_o"], p["b_gate"], p["w_out"], p["wr_hi"], p["wr_lo"], p["router_bias"]]
    if cfg.has_cache:
        in_specs += [per_row((1, PREV, ATTN_DIM)), per_row((1, PREV, ATTN_DIM)),
                     per_row((1, CONV_WIDTH - 1, conv_dim))]
        args += list(cache)
    if cfg.sparse:
        assert nb * nt <= MAX_TILES
        in_specs += [const((cfg.tt, cfg.tt))]
        tok = jnp.arange(cfg.tt)
        args += [(tok[:, None] < tok[None, :]).astype(BF16)]
    aliases = {}
    if state is not None:
        for i, arr in enumerate(state):
            aliases[len(args)] = 3 + i
            in_specs.append(pl.BlockSpec(memory_space=pl.ANY))
            args.append(arr)
    keep = min(PREV, t_total)
    assert keep == cfg.tt
    out_shape = [jax.ShapeDtypeStruct((nb, t_total, d), F32),
                 jax.ShapeDtypeStruct((nb, t_total, d // 2), I32),
                 jax.ShapeDtypeStruct((nb, t_total, LANES), F32),
                 jax.ShapeDtypeStruct((b_all, keep, ATTN_DIM), F32),
                 jax.ShapeDtypeStruct((b_all, keep, ATTN_DIM), F32),
                 jax.ShapeDtypeStruct((b_all, CONV_WIDTH - 1, conv_dim), F32)]
    out_specs = [pl.BlockSpec((1, cfg.tt, d), lambda i, j: (i, j, 0)),
                 pl.BlockSpec((1, cfg.tt, d // 2), lambda i, j: (i, j, 0)),
                 pl.BlockSpec((1, cfg.tt, LANES), lambda i, j: (i, j, 0)),
                 per_row((1, keep, ATTN_DIM)), per_row((1, keep, ATTN_DIM)),
                 per_row((1, CONV_WIDTH - 1, conv_dim))]
    if cfg.sparse:
        out_shape += [jax.ShapeDtypeStruct((TOP_K, nb * t_total), F32),
                      jax.ShapeDtypeStruct((TOP_K, nb * t_total), F32),
                      jax.ShapeDtypeStruct((N_EXPERTS, MAX_TILES), F32)]
        out_specs += [pl.BlockSpec((TOP_K, cfg.tt), lambda i, j: (0, i * nt + j)),
                      pl.BlockSpec((TOP_K, cfg.tt), lambda i, j: (0, i * nt + j)),
                      pl.BlockSpec((N_EXPERTS, MAX_TILES), lambda i, j: (0, 0))]
    scratch = [pltpu.VMEM((cfg.tt, ATTN_DIM), BF16),
               pltpu.VMEM((PREV + cfg.tt, ATTN_DIM), BF16),
               pltpu.VMEM((PREV + cfg.tt, ATTN_DIM), BF16),
               pltpu.VMEM((cfg.tt, ATTN_DIM), BF16),
               pltpu.VMEM((CONV_PAD + cfg.tt, conv_dim), F32),
               pltpu.VMEM((cfg.tt, conv_dim), F32)]
    return pl.pallas_call(
        functools.partial(_mixer_kernel, cfg, 0 if state is None else len(state)),
        out_shape=out_shape,
        grid=(nb, nt),
        in_specs=in_specs,
        out_specs=out_specs,
        scratch_shapes=scratch,
        input_output_aliases=aliases,
        cost_estimate=_mixer_cost(cfg, nb * t_total, d, n_in, conv_dim),
        compiler_params=pltpu.CompilerParams(dimension_semantics=("arbitrary", "arbitrary"),
                                             vmem_limit_bytes=VMEM_LIMIT_BYTES),
        name="mixer_sample" if cfg.has_cache else "mixer_prompt",
    )(*args)


def _exact_parts(x, n):
    parts = []
    for _ in range(n):
        part = x.astype(BF16)
        parts.append(part)
        x = x - part.astype(F32)
    return parts


def _plan_kernel(tt, cnt_ref, eid_ref, rank_ref, tri_e_ref, tri_t_ref, pos_ref, blk_ref, nblk_ref, base_ref):
    step = pl.program_id(0)

    @pl.when(step == 0)
    def _():
        cnt = cnt_ref[...]
        total = jnp.sum(cnt, axis=1, keepdims=True)
        padded = jnp.floor((total + float(ROW_BLOCK - 1)) * (1.0 / ROW_BLOCK)) * float(ROW_BLOCK)
        padded = jnp.broadcast_to(padded, cnt.shape)
        start = functools.reduce(jnp.add, [_dot(tri_e_ref[...], part) for part in _exact_parts(padded, 3)])
        before = functools.reduce(jnp.add, [_dot(part, tri_t_ref[...]) for part in _exact_parts(cnt, 2)])
        base_ref[...] = start + before
        end = (start + padded)[:, 0:1]
        first_row = lax.broadcasted_iota(I32, blk_ref.shape, 1).astype(F32) * float(ROW_BLOCK)
        owner = jnp.sum(jnp.where(end <= first_row, 1.0, 0.0), axis=0, keepdims=True)
        blk_ref[...] = jnp.minimum(owner, float(N_EXPERTS - 1)).astype(I32)
        nblk = jnp.max(end, axis=0, keepdims=True) * (1.0 / ROW_BLOCK)
        nblk_ref[...] = jnp.broadcast_to(nblk, nblk_ref.shape).astype(I32)

    lane = lax.broadcasted_iota(I32, base_ref.shape, 1)
    col = jnp.sum(jnp.where(lane == step, base_ref[...], 0.0), axis=1, keepdims=True)
    ids = _expert_ids(tt)
    col_slabs = [jnp.broadcast_to(col[g * GROUP_SIZE:(g + 1) * GROUP_SIZE, :], (GROUP_SIZE, tt))
                 for g in range(N_GROUPS)]
    eid = eid_ref[...]
    rows = [_pick(ids, eid[k:k + 1, :], col_slabs) for k in range(TOP_K)]
    pos_ref[...] = (jnp.concatenate(rows, axis=0) + rank_ref[...]).astype(I32)


def _plan_call(cnt, eid, rank, tt, n_blocks):
    n = eid.shape[1]
    e = jnp.arange(N_EXPERTS)
    s = jnp.arange(MAX_TILES)
    nb_pad = -(-n_blocks // LANES) * LANES
    return pl.pallas_call(
        functools.partial(_plan_kernel, tt),
        out_shape=[jax.ShapeDtypeStruct((TOP_K, n), I32),
                   jax.ShapeDtypeStruct((1, nb_pad), I32),
                   jax.ShapeDtypeStruct((1, LANES), I32)],
        grid=(n // tt,),
        in_specs=[pl.BlockSpec((N_EXPERTS, MAX_TILES), lambda i: (0, 0)),
                  pl.BlockSpec((TOP_K, tt), lambda i: (0, i)),
                  pl.BlockSpec((TOP_K, tt), lambda i: (0, i)),
                  pl.BlockSpec((N_EXPERTS, N_EXPERTS), lambda i: (0, 0)),
                  pl.BlockSpec((MAX_TILES, MAX_TILES), lambda i: (0, 0))],
        out_specs=[pl.BlockSpec((TOP_K, tt), lambda i: (0, i)),
                   pl.BlockSpec((1, nb_pad), lambda i: (0, 0)),
                   pl.BlockSpec((1, LANES), lambda i: (0, 0))],
        scratch_shapes=[pltpu.VMEM((N_EXPERTS, MAX_TILES), F32)],
        compiler_params=pltpu.CompilerParams(dimension_semantics=("arbitrary",),
                                             vmem_limit_bytes=VMEM_LIMIT_BYTES),
        name="plan",
    )(cnt, eid, rank, (e[None, :] < e[:, None]).astype(BF16), (s[:, None] < s[None, :]).astype(BF16))


def _sc_mesh():
    return plsc.VectorSubcoreMesh(core_axis_name="c", subcore_axis_name="s")


def _dispatch_call(rows, pos, n_sorted):
    n, w = rows.shape
    per_worker = n // SC_WORKERS
    nsteps = per_worker // SC_ROWS
    assert per_worker * SC_WORKERS == n and nsteps * SC_ROWS == per_worker and nsteps % 2 == 0

    @functools.partial(
        pl.kernel, mesh=_sc_mesh(),
        out_type=jax.ShapeDtypeStruct((n_sorted, w), rows.dtype),
        scratch_types=[pltpu.VMEM((2, TOP_K, SC_ROWS), I32),
                       pltpu.VMEM((2, SC_ROWS, w), rows.dtype),
                       pltpu.SemaphoreType.DMA((2,)),
                       pltpu.SemaphoreType.DMA((2,))],
        cost_estimate=pl.CostEstimate(flops=0, transcendentals=0,
                                      bytes_accessed=(1 + TOP_K) * n * w * rows.dtype.itemsize + 4 * TOP_K * n),
    )
    def dispatch(x_hbm, pos_hbm, out_hbm, idx_v, rows_v, load_sem, scat_sem):
        base = (lax.axis_index("s") * SC_CORES + lax.axis_index("c")) * per_worker

        def load(i, b):
            return pltpu.make_async_copy(x_hbm.at[pl.ds(base + i * SC_ROWS, SC_ROWS)], rows_v.at[b], load_sem.at[b])

        def scatter(b, k):
            return pltpu.make_async_copy(rows_v.at[b], out_hbm.at[idx_v.at[b, k]], scat_sem.at[b])

        def load_start(i, b):
            for k in range(TOP_K):
                pltpu.sync_copy(pos_hbm.at[pl.ds(k * n + base + i * SC_ROWS, SC_ROWS)], idx_v.at[b, k])
            load(i, b).start()

        load_start(0, 0)

        @pl.loop(0, nsteps, step=2)
        def _(i):
            for b in range(2):
                ii = i + b

                @pl.when(ii >= 1)
                def _():
                    for k in range(TOP_K):
                        scatter(1 - b, k).wait()

                @pl.when(ii + 1 < nsteps)
                def _():
                    load_start(ii + 1, 1 - b)

                load(ii, b).wait()
                for k in range(TOP_K):
                    scatter(b, k).start()

        for k in range(TOP_K):
            scatter((nsteps - 1) % 2, k).wait()

    return dispatch(rows, pos)


def _collect_call(table, idx):
    n = idx.shape[0]
    w = table.shape[1]
    per_worker = n // SC_WORKERS
    nsteps = per_worker // SC_ROWS
    assert per_worker * SC_WORKERS == n and nsteps * SC_ROWS == per_worker and nsteps % 2 == 0

    @functools.partial(
        pl.kernel, mesh=_sc_mesh(),
        out_type=jax.ShapeDtypeStruct((n, w), table.dtype),
        scratch_types=[pltpu.VMEM((2, SC_ROWS), I32),
                       pltpu.VMEM((2, SC_ROWS, w), table.dtype),
                       pltpu.SemaphoreType.DMA((2,)),
                       pltpu.SemaphoreType.DMA((2,))],
        cost_estimate=pl.CostEstimate(flops=0, transcendentals=0,
                                      bytes_accessed=2 * n * w * table.dtype.itemsize + 4 * n),
    )
    def collect(table_hbm, idx_hbm, out_hbm, idx_v, rows_v, gather_sem, write_sem):
        base = (lax.axis_index("s") * SC_CORES + lax.axis_index("c")) * per_worker

        def gather(b):
            return pltpu.make_async_copy(table_hbm.at[idx_v.at[b]], rows_v.at[b], gather_sem.at[b])

        def write(i, b):
            return pltpu.make_async_copy(rows_v.at[b], out_hbm.at[pl.ds(base + i * SC_ROWS, SC_ROWS)], write_sem.at[b])

        def gather_start(i, b):
            pltpu.sync_copy(idx_hbm.at[pl.ds(base + i * SC_ROWS, SC_ROWS)], idx_v.at[b])
            gather(b).start()

        gather_start(0, 0)

        @pl.loop(0, nsteps, step=2)
        def _(i):
            for b in range(2):
                ii = i + b

                @pl.when(ii >= 1)
                def _():
                    write(ii - 1, 1 - b).wait()

                @pl.when(ii + 1 < nsteps)
                def _():
                    gather_start(ii + 1, 1 - b)

                gather(b).wait()
                write(ii, b).start()

        write(nsteps - 1, (nsteps - 1) % 2).wait()

    return collect(table, idx)


def _expert_ffn_kernel(blk_ref, nblk_ref, x_ref, wg_ref, wu_ref, wd_ref, o_ref, wg_s, wu_s, wd_s):
    j = pl.program_id(0)

    @pl.when(jnp.logical_or(j == 0, blk_ref[j] != blk_ref[jnp.maximum(j - 1, 0)]))
    def _():
        wg_s[...] = wg_ref[0].astype(BF16)
        wu_s[...] = wu_ref[0].astype(BF16)
        wd_s[...] = wd_ref[0].astype(BF16)

    @pl.when(j < nblk_ref[0])
    def _():
        a, b = _unpack(x_ref[...])
        a, b = a.astype(BF16), b.astype(BF16)
        hg = _dot_halves(a, b, wg_s)
        act = hg * _sigmoid_t(hg) * _dot_halves(a, b, wu_s)
        y = _dot(act.astype(BF16), wd_s[...])
        half = y.shape[1] // 2
        o_ref[...] = _pack(y[:, 0:half], y[:, half:2 * half])


def _expert_ffn_call(xs, blk, nblk, p):
    n_sorted, w = xs.shape
    d = 2 * w
    ff = p["w_e_gate"].shape[2]

    def rows(j, blk, nblk):
        return (jnp.minimum(j, nblk[0] - 1), 0)

    def expert(j, blk, nblk):
        return (blk[j], 0, 0)

    return pl.pallas_call(
        _expert_ffn_kernel,
        out_shape=jax.ShapeDtypeStruct((n_sorted, w), I32),
        grid_spec=pltpu.PrefetchScalarGridSpec(
            num_scalar_prefetch=2,
            grid=(n_sorted // ROW_BLOCK,),
            in_specs=[pl.BlockSpec((ROW_BLOCK, w), rows),
                      pl.BlockSpec((1, d, ff), expert),
                      pl.BlockSpec((1, d, ff), expert),
                      pl.BlockSpec((1, ff, d), expert)],
            out_specs=pl.BlockSpec((ROW_BLOCK, w), rows),
            scratch_shapes=[pltpu.VMEM((d, ff), BF16), pltpu.VMEM((d, ff), BF16), pltpu.VMEM((ff, d), BF16)]),
        cost_estimate=pl.CostEstimate(flops=6 * n_sorted * d * ff, transcendentals=n_sorted * ff,
                                      bytes_accessed=8 * n_sorted * w + 12 * N_EXPERTS * d * ff),
        compiler_params=pltpu.CompilerParams(dimension_semantics=("arbitrary",),
                                             vmem_limit_bytes=VMEM_LIMIT_BYTES),
        name="expert_ffn",
    )(blk, nblk, xs, p["w_e_gate"], p["w_e_up"], p["w_e_down"])


def _swiglu_halves(a, b, wg_ref, wu_ref):
    hg = _dot_halves(a, b, wg_ref)
    return hg * _sigmoid_t(hg) * _dot_halves(a, b, wu_ref)


def _moe_out_kernel(n_groups, starts, *refs):
    groups = [refs[4 * q:4 * q + 4] for q in range(n_groups)]
    g2_ref, wsg_ref, wsu_ref, wsd_ref, o_ref = refs[4 * n_groups:]

    def combine(x1_ref, h_ref, y_ref, w_ref):
        a, b = _unpack(h_ref[...])
        shared = _dot(_swiglu_halves(a.astype(BF16), b.astype(BF16), wsg_ref, wsu_ref).astype(BF16),
                      wsd_ref[...].astype(BF16))
        half = h_ref.shape[1]
        w = w_ref[...]
        acc_a, acc_b = shared[:, 0:half], shared[:, half:2 * half]
        for k in range(TOP_K):
            ya, yb = _unpack(y_ref[k])
            acc_a = acc_a + w[:, k:k + 1] * ya
            acc_b = acc_b + w[:, k:k + 1] * yb
        g2 = g2_ref[...]
        o_ref[:, 0:half] = x1_ref[:, 0:half] + g2[:, 0:half] * acc_a
        o_ref[:, half:2 * half] = x1_ref[:, half:2 * half] + g2[:, half:2 * half] * acc_b

    i = pl.program_id(0)
    for q in range(n_groups):
        pl.when(jnp.logical_and(i >= starts[q], i < starts[q + 1]))(functools.partial(combine, *groups[q]))


def _moe_out_call(groups, g2, p, tm):
    d = groups[0][0].shape[1]
    starts = [0]
    for group in groups:
        starts.append(starts[-1] + group[0].shape[0] // tm)
    n_all = starts[-1] * tm
    sff = p["w_s_gate"].shape[1]
    per_row = n_all // g2.shape[0]
    g2 = g2.reshape(g2.shape[0], 1, d)

    def const(shape):
        return pl.BlockSpec(shape, lambda i: (0,) * len(shape), pipeline_mode=pl.Buffered(1))

    in_specs, args = [], []
    for q, group in enumerate(groups):
        tile = lambda i, q=q: jnp.clip(i - starts[q], 0, starts[q + 1] - starts[q] - 1)
        in_specs += [pl.BlockSpec((tm, d), lambda i, tile=tile: (tile(i), 0)),
                     pl.BlockSpec((tm, d // 2), lambda i, tile=tile: (tile(i), 0)),
                     pl.BlockSpec((TOP_K, tm, d // 2), lambda i, tile=tile: (0, tile(i), 0)),
                     pl.BlockSpec((tm, LANES), lambda i, tile=tile: (tile(i), 0))]
        args += list(group)
    in_specs += [pl.BlockSpec((None, 1, d), lambda i: (i * tm // per_row, 0, 0)),
                 const((d, sff)), const((d, sff)), const((sff, d))]
    args += [g2, p["w_s_gate"], p["w_s_up"], p["w_s_down"]]
    return pl.pallas_call(
        functools.partial(_moe_out_kernel, len(groups), tuple(starts)),
        out_shape=jax.ShapeDtypeStruct((n_all, d), F32),
        grid=(n_all // tm,),
        in_specs=in_specs,
        out_specs=pl.BlockSpec((tm, d), lambda i: (i, 0)),
        cost_estimate=pl.CostEstimate(flops=n_all * (6 * d * sff + 2 * TOP_K * d), transcendentals=n_all * sff,
                                      bytes_accessed=n_all * (8 * d + 2 * d + 2 * TOP_K * d + 4 * LANES)),
        compiler_params=pltpu.CompilerParams(dimension_semantics=("arbitrary",),
                                             vmem_limit_bytes=VMEM_LIMIT_BYTES),
        name="moe_out",
    )(*args)


def _moe_dense_kernel(h_ref, gates_ref, x1_ref, g2_ref, wsg_ref, wsu_ref, wsd_ref, wg_ref, wu_ref, wd_ref,
                      o_ref, acc_ref):
    e = pl.program_id(1)
    a, b = _unpack(h_ref[...])
    a, b = a.astype(BF16), b.astype(BF16)

    @pl.when(e == 0)
    def _():
        acc_ref[...] = _dot(_swiglu_halves(a, b, wsg_ref, wsu_ref).astype(BF16), wsd_ref[...].astype(BF16))

    gates = gates_ref[...]
    lane = lax.broadcasted_iota(I32, gates.shape, 1)
    gate = jnp.sum(jnp.where(lane == e, gates, 0.0), axis=1, keepdims=True)
    act = _swiglu_halves(a, b, wg_ref.at[0], wu_ref.at[0]) * gate
    acc_ref[...] += _dot(act.astype(BF16), wd_ref[0].astype(BF16))

    @pl.when(e == pl.num_programs(1) - 1)
    def _():
        o_ref[...] = x1_ref[...] + g2_ref[...] * acc_ref[...]


def _moe_dense_call(h2, gates, x1, g2_rows, p):
    n, d = x1.shape
    ff = p["w_e_gate"].shape[2]
    sff = p["w_s_gate"].shape[1]

    def const(shape):
        return pl.BlockSpec(shape, lambda i, e: (0,) * len(shape), pipeline_mode=pl.Buffered(1))

    return pl.pallas_call(
        _moe_dense_kernel,
        out_shape=jax.ShapeDtypeStruct((n, d), F32),
        grid=(1, N_EXPERTS),
        in_specs=[const((n, d // 2)), const((n, LANES)), const((n, d)), const((n, d)),
                  const((d, sff)), const((d, sff)), const((sff, d)),
                  pl.BlockSpec((1, d, ff), lambda i, e: (e, 0, 0)),
                  pl.BlockSpec((1, d, ff), lambda i, e: (e, 0, 0)),
                  pl.BlockSpec((1, ff, d), lambda i, e: (e, 0, 0))],
        out_specs=pl.BlockSpec((n, d), lambda i, e: (0, 0)),
        scratch_shapes=[pltpu.VMEM((n, d), F32)],
        compiler_params=pltpu.CompilerParams(dimension_semantics=("arbitrary", "arbitrary"),
                                             vmem_limit_bytes=VMEM_LIMIT_BYTES),
        name="moe_dense",
    )(h2, gates, x1, g2_rows, p["w_s_gate"], p["w_s_up"], p["w_s_down"], p["w_e_gate"], p["w_e_up"], p["w_e_down"])


PROMPT_CFG = MixerCfg(tt=512, qb=256, kb=768, n_qblk=2, kstride=256, has_cache=False, sparse=True)
MOE_OUT_TILE = 512
PROMPT_SPLIT = (1, 1)


def _layer_params(l, w_in, q_norm_g, k_norm_g, w_attn_o, conv_w, conv_b, conv_ln_g, conv_ln_b, w_conv_o,
                  b_conv_o, b_gate, w_out, norm1_g, norm2_g, w_router, router_bias,
                  w_e_gate, w_e_up, w_e_down, w_s_gate, w_s_up, w_s_down):
    row = lambda a: a[l].reshape(1, -1)
    wr_t = w_router[l].T
    wr_hi = wr_t.astype(BF16)
    head = jnp.arange(ATTN_DIM) // HEAD_DIM
    return {
        "norm1_g": row(norm1_g), "norm2_g": row(norm2_g), "w_in": w_in[l].astype(BF16),
        "q_norm_g": jnp.tile(q_norm_g[l], N_HEADS).reshape(1, -1),
        "k_norm_g": jnp.tile(k_norm_g[l], N_HEADS).reshape(1, -1),
        "hsum": (head[:, None] == head[None, :]).astype(BF16),
        "w_attn_o": w_attn_o[l].astype(BF16), "conv_w": conv_w[l], "conv_b": row(conv_b),
        "conv_ln_g": row(conv_ln_g), "conv_ln_b": row(conv_ln_b), "w_conv_o": w_conv_o[l].astype(BF16),
        "b_conv_o": row(b_conv_o), "b_gate": row(b_gate), "w_out": w_out[l].astype(BF16),
        "wr_hi": wr_hi, "wr_lo": (wr_t - wr_hi.astype(F32)).astype(BF16),
        "router_bias": router_bias[l].reshape(-1, 1),
        "w_e_gate": w_e_gate[l], "w_e_up": w_e_up[l], "w_e_down": w_e_down[l],
        "w_s_gate": w_s_gate[l], "w_s_up": w_s_up[l], "w_s_down": w_s_down[l],
    }


def _sorted_rows(h2, eid, rank, cnt, tt):
    n = h2.shape[0]
    n_sorted = n * TOP_K + N_EXPERTS * ROW_BLOCK
    pos, blk, nblk = _plan_call(cnt, eid, rank, tt, n_sorted // ROW_BLOCK)
    pos = pos.reshape(TOP_K * n)
    return _dispatch_call(h2, pos, n_sorted), pos, blk.reshape(-1), nblk.reshape(-1)


def kernel(x_prompt, x_sample, c_prompt, c_sample, cache_k, cache_v, state_conv, w_mod, b_mod, norm1_g, w_in, q_norm_g, k_norm_g, rel_bias, w_attn_o, conv_w, conv_b, conv_ln_g, conv_ln_b, w_conv_o, b_conv_o, b_gate, w_out, norm2_g, w_router, router_bias, w_e_gate, w_e_up, w_e_down, w_s_gate, w_s_up, w_s_down):
    depth = w_mod.shape[0]
    bp, tp, d = x_prompt.shape
    bs, ts, _ = x_sample.shape
    assert cache_k.shape[2] == min(PREV, PAST_LEN) == PREV
    sample_cfg = MixerCfg(tt=ts, qb=ts, kb=PREV + ts, n_qblk=1, kstride=0, has_cache=True, sparse=False)

    yp, ys = x_prompt, x_sample
    outs = [[] for _ in range(6)]
    for l in range(depth):
        p = _layer_params(l, w_in, q_norm_g, k_norm_g, w_attn_o, conv_w, conv_b, conv_ln_g, conv_ln_b,
                          w_conv_o, b_conv_o, b_gate, w_out, norm1_g, norm2_g, w_router, router_bias,
                          w_e_gate, w_e_up, w_e_down, w_s_gate, w_s_up, w_s_down)
        mod = _mod_call(jnp.concatenate([c_prompt, c_sample], axis=0), w_mod[l], b_mod[l])
        mod = mod.reshape(bp + bs, 6, d)
        mod_p, mod_s = mod[:bp], mod[bp:]

        p["bias"] = _rel_bias_blocks(rel_bias[l], PROMPT_CFG, PREV, 0)
        big = max(bp * PROMPT_SPLIT[0] // sum(PROMPT_SPLIT), 1)
        group_rows = [big, bp - big] if bp > big else [bp]
        keep = min(PREV, tp)
        state = [jnp.zeros((bp, keep, ATTN_DIM), F32), jnp.zeros((bp, keep, ATTN_DIM), F32),
                 jnp.zeros((bp, CONV_WIDTH - 1, conv_w.shape[2]), F32)]
        mixed, moved = [], []
        for g, gb in enumerate(group_rows):
            x1, h2, w_tok, *state, eid, rank, cnt = _mixer_call(PROMPT_CFG, yp, mod_p, p, batch0=sum(group_rows[:g]),
                                                                nbatch=gb, state=state)
            x1, h2, w_tok = x1.reshape(gb * tp, d), h2.reshape(gb * tp, d // 2), w_tok.reshape(gb * tp, LANES)
            mixed.append((x1, h2, w_tok))
            moved.append(_sorted_rows(h2, eid, rank, cnt, PROMPT_CFG.tt))
            state, _ = lax.optimization_barrier((state, moved[-1][1]))
        kp, vp, cp = state
        groups, gate = [], cp
        for (x1, h2, w_tok), (xs, pos, blk, nblk) in zip(mixed, moved):
            xs, _ = lax.optimization_barrier((xs, gate))
            gate = _expert_ffn_call(xs, blk, nblk, p)
            yg = _collect_call(gate, pos).reshape(TOP_K, h2.shape[0], h2.shape[1])
            groups.append((x1, h2, yg, w_tok))
        yp = _moe_out_call(groups, mod_p[:, 5, :], p, MOE_OUT_TILE).reshape(bp, tp, d)
        outs[0].append(kp.reshape(bp, -1, N_HEADS, HEAD_DIM))
        outs[1].append(vp.reshape(bp, -1, N_HEADS, HEAD_DIM))
        outs[2].append(cp)

        p["bias"] = _rel_bias_blocks(rel_bias[l], sample_cfg, PAST_LEN, PAST_LEN - PREV)
        cache = (cache_k[l].reshape(bs, PREV, ATTN_DIM), cache_v[l].reshape(bs, PREV, ATTN_DIM), state_conv[l])
        x1, h2, gates, ks, vs, cs = _mixer_call(sample_cfg, ys, mod_s, p, cache)
        ys = _moe_dense_call(h2.reshape(bs * ts, d // 2), gates.reshape(bs * ts, LANES), x1.reshape(bs * ts, d),
                             jnp.repeat(mod_s[:, 5, :], ts, axis=0), p).reshape(bs, ts, d)
        outs[3].append(ks.reshape(bs, ts, N_HEADS, HEAD_DIM))
        outs[4].append(vs.reshape(bs, ts, N_HEADS, HEAD_DIM))
        outs[5].append(cs)
    return (yp, ys) + tuple(jnp.stack(o) for o in outs)
```

```python
import dataclasses
import functools

import jax
import jax.numpy as jnp
from jax import lax
from jax.experimental import pallas as pl
from jax.experimental.pallas import tpu as pltpu
from jax.experimental.pallas import tpu_sc as plsc

F32 = jnp.float32
BF16 = jnp.bfloat16
I32 = jnp.int32

CHUNK = 64
BAND_CHUNKS = 8
PREV = BAND_CHUNKS * CHUNK
PAST_LEN = 1024
N_HEADS = 8
HEAD_DIM = 64
ATTN_DIM = N_HEADS * HEAD_DIM
MAX_REL = 128
CONV_WIDTH = 31
CONV_PAD = 32
N_EXPERTS = 64
N_GROUPS = 8
GROUP_SIZE = N_EXPERTS // N_GROUPS
TOPK_GROUPS = 4
TOP_K = 8
ROUTE_SCALE = 2.5
EPS = 1e-6
LANES = 128
SUBLANES = 8
NEG = -1e30
LOG2_E = 1.4426950408889634

VMEM_LIMIT_BYTES = 60 * 1024 * 1024

SC_CORES = 2
SC_SUBCORES = 16
SC_WORKERS = SC_CORES * SC_SUBCORES
SC_ROWS = 64

MAX_TILES = LANES
ROW_BLOCK = 1024
FFN_BLOCKS = 2


def _sigmoid(x):
    return 1.0 / (1.0 + jnp.exp(-x))


def _sigmoid_t(x):
    return 0.5 * jnp.tanh(0.5 * x) + 0.5


def _split_bf16(x):
    hi = x.astype(BF16)
    lo = (x - hi.astype(F32)).astype(BF16)
    return hi, lo


def _dot(a, b):
    return jnp.dot(a, b, preferred_element_type=F32)


def _dot3(a, b):
    a_hi, a_lo = _split_bf16(a)
    b_hi, b_lo = _split_bf16(b)
    return _dot(a_hi, b_hi) + _dot(a_hi, b_lo) + _dot(a_lo, b_hi)


def _pack(a, b):
    ia = lax.bitcast_convert_type(a.astype(BF16).astype(F32), I32)
    ib = lax.bitcast_convert_type(b.astype(BF16).astype(F32), I32)
    return ia | lax.shift_right_logical(ib, 16)


def _unpack(p):
    a = lax.bitcast_convert_type(p & jnp.int32(-65536), F32)
    b = lax.bitcast_convert_type(lax.shift_left(p, 16), F32)
    return a, b


def _dot_halves(a, b, w_ref):
    half = a.shape[1]
    return _dot(a, w_ref[0:half, :].astype(BF16)) + _dot(b, w_ref[half:2 * half, :].astype(BF16))


def _mod_kernel(c_ref, w_ref, b_ref, o_ref):
    c = c_ref[...]
    o_ref[...] = _dot3(c * _sigmoid(c), w_ref[...]) + b_ref[...]


def _mod_call(c, w_mod, b_mod):
    n, d = c.shape
    dout = w_mod.shape[1]
    bn = 1536
    return pl.pallas_call(
        _mod_kernel,
        out_shape=jax.ShapeDtypeStruct((n, dout), F32),
        grid=(dout // bn,),
        in_specs=[pl.BlockSpec((n, d), lambda j: (0, 0)),
                  pl.BlockSpec((d, bn), lambda j: (0, j)),
                  pl.BlockSpec((1, bn), lambda j: (0, j))],
        out_specs=pl.BlockSpec((n, bn), lambda j: (0, j)),
        compiler_params=pltpu.CompilerParams(dimension_semantics=("arbitrary",),
                                             vmem_limit_bytes=VMEM_LIMIT_BYTES),
        name="mod",
    )(c, w_mod, b_mod.reshape(1, dout))


@dataclasses.dataclass(frozen=True)
class MixerCfg:
    tt: int
    qb: int
    kb: int
    n_qblk: int
    kstride: int
    has_cache: bool
    sparse: bool


def _rms(x, g):
    ms = jnp.mean(x * x, axis=-1, keepdims=True)
    return x * lax.rsqrt(ms + EPS) * g


def _head_rms(z, hsum, g):
    ss = _dot((z * z).astype(BF16), hsum)
    return z * lax.rsqrt(ss * (1.0 / HEAD_DIM) + EPS) * g


def _expert_ids(t):
    sub = lax.broadcasted_iota(I32, (GROUP_SIZE, t), 0).astype(F32)
    return [sub + float(g * GROUP_SIZE) for g in range(N_GROUPS)]


def _pick(ids, eid, slabs):
    acc = functools.reduce(jnp.add, [jnp.where(ids[g] == eid, slabs[g], 0.0) for g in range(N_GROUPS)])
    return jnp.sum(acc, axis=0, keepdims=True)


def _route(choice, s):
    t = choice.shape[1]
    sub = lax.broadcasted_iota(I32, (GROUP_SIZE, t), 0).astype(F32)
    slabs = [choice[g * GROUP_SIZE:(g + 1) * GROUP_SIZE, :] for g in range(N_GROUPS)]
    s_slabs = [s[g * GROUP_SIZE:(g + 1) * GROUP_SIZE, :] for g in range(N_GROUPS)]
    gscore = []
    for c in slabs:
        m1 = jnp.max(c, axis=0, keepdims=True)
        first = jnp.min(jnp.where(c == m1, sub, float(GROUP_SIZE)), axis=0, keepdims=True)
        m2 = jnp.max(jnp.where(sub == first, -jnp.inf, c), axis=0, keepdims=True)
        gscore.append(m1 + m2)
    gsel = [jnp.zeros((1, t), F32) for _ in range(N_GROUPS)]
    for _ in range(TOPK_GROUPS):
        m = functools.reduce(jnp.maximum, gscore)
        first = functools.reduce(jnp.minimum,
                                 [jnp.where(gscore[g] == m, float(g), float(N_GROUPS)) for g in range(N_GROUPS)])
        for g in range(N_GROUPS):
            hit = first == float(g)
            gsel[g] = jnp.where(hit, 1.0, gsel[g])
            gscore[g] = jnp.where(hit, -jnp.inf, gscore[g])
    masked = [jnp.where(gsel[g] > 0.5, slabs[g], -jnp.inf) for g in range(N_GROUPS)]
    ids = _expert_ids(t)
    eids, raw = [], []
    for _ in range(TOP_K):
        m = jnp.max(functools.reduce(jnp.maximum, masked), axis=0, keepdims=True)
        cand = [jnp.where(masked[g] == m, ids[g], float(N_EXPERTS)) for g in range(N_GROUPS)]
        first = jnp.min(functools.reduce(jnp.minimum, cand), axis=0, keepdims=True)
        eids.append(first)
        raw.append(_pick(ids, first, s_slabs))
        masked = [jnp.where(ids[g] == first, -jnp.inf, masked[g]) for g in range(N_GROUPS)]
    wsum = functools.reduce(jnp.add, raw)
    return eids, [r / wsum * ROUTE_SCALE for r in raw]


def _lane_dense_rows(rows, t):
    pad = jnp.zeros((LANES - len(rows), t), F32)
    return jnp.concatenate(rows + [pad], axis=0).T


def _mixer_kernel(cfg, n_state, *refs):
    (x_ref, mod_ref, n1g_ref, n2g_ref, win_ref, qg_ref, kg_ref, hsum_ref, bias_ref, wao_ref,
     cw_ref, cb_ref, lng_ref, lnb_ref, wco_ref, bco_ref, bg_ref, wout_ref,
     wrh_ref, wrl_ref, rb_ref) = refs[:21]
    refs = refs[21:]
    if cfg.has_cache:
        kc_ref, vc_ref, cs_ref = refs[:3]
        refs = refs[3:]
    if cfg.sparse:
        tri_ref = refs[0]
        refs = refs[1:]
    refs = refs[n_state:]
    x1_ref, h2_ref, tok_ref, kout_ref, vout_ref, cout_ref = refs[:6]
    refs = refs[6:]
    if cfg.sparse:
        eid_ref, rank_ref, cnt_ref = refs[:3]
        refs = refs[3:]
    qbuf, kbuf, vbuf, obuf, uext, cvbuf = refs

    tt = cfg.tt
    t = pl.program_id(1)

    if cfg.has_cache:
        kbuf[0:PREV, :] = kc_ref[0].astype(BF16)
        vbuf[0:PREV, :] = vc_ref[0].astype(BF16)
        uext[CONV_PAD - (CONV_WIDTH - 1):CONV_PAD, :] = cs_ref[0]
    else:
        @pl.when(t == 0)
        def _():
            kbuf[0:PREV, :] = jnp.zeros((PREV, ATTN_DIM), BF16)
            vbuf[0:PREV, :] = jnp.zeros((PREV, ATTN_DIM), BF16)
            uext[0:CONV_PAD, :] = jnp.zeros((CONV_PAD, uext.shape[1]), F32)

    mod = mod_ref[0]
    sh1, sc1, g1, sh2, sc2 = (mod[i:i + 1, :] for i in range(5))

    x = x_ref[0]
    hb = (_rms(x, n1g_ref[...]) * (1.0 + sc1) + sh1).astype(BF16)

    a0, a1, a2, a3, a4, a5 = (0, ATTN_DIM, 2 * ATTN_DIM, 3 * ATTN_DIM,
                              3 * ATTN_DIM + cw_ref.shape[1], 3 * ATTN_DIM + 2 * cw_ref.shape[1])
    d_model = x.shape[1]

    q = _head_rms(_dot(hb, win_ref[:, a0:a1]), hsum_ref[...], qg_ref[...])
    qbuf[...] = (q * (HEAD_DIM ** -0.5 * LOG2_E)).astype(BF16)
    k = _head_rms(_dot(hb, win_ref[:, a1:a2]), hsum_ref[...], kg_ref[...])
    kout_ref[0] = k
    kbuf[PREV:PREV + tt, :] = k.astype(BF16)
    v = _dot(hb, win_ref[:, a2:a3])
    vout_ref[0] = v
    vbuf[PREV:PREV + tt, :] = v.astype(BF16)

    u = _dot(hb, win_ref[:, a3:a4]) * _sigmoid_t(_dot(hb, win_ref[:, a4:a5]))
    uext[CONV_PAD:CONV_PAD + tt, :] = u
    cout_ref[0] = uext[CONV_PAD + tt - (CONV_WIDTH - 1):CONV_PAD + tt, :]

    rc = min(tt, 64)
    cc = min(u.shape[1], LANES)
    shifts = [j + CONV_PAD - (CONV_WIDTH - 1) for j in range(CONV_WIDTH)]

    def conv_unit(r0, c0):
        acc = jnp.broadcast_to(cb_ref[:, c0:c0 + cc], (rc, cc))
        for res in range(SUBLANES):
            group = [s for s in shifts if s % SUBLANES == res]
            if not group:
                continue
            lo, hi = min(group) - res, max(group) - res
            rows = hi - lo + rc + (SUBLANES if res else 0)
            slab = uext[pl.ds(r0 + lo, rows), c0:c0 + cc]
            if res:
                slab = pltpu.roll(slab, rows - res, 0)
            for s in group:
                j = s - shifts[0]
                a = s - res - lo
                acc = acc + cw_ref[j:j + 1, c0:c0 + cc] * slab[a:a + rc, :]
        cvbuf[pl.ds(r0, rc), c0:c0 + cc] = acc

    units = [(r0, c0) for r0 in range(0, tt, rc) for c0 in range(0, u.shape[1], cc)]
    gw = 256
    n_chunks = 2 * d_model // gw
    gates = []
    for ci in range(n_chunks):
        cols = slice(ci * gw, (ci + 1) * gw)
        gates.append(_sigmoid_t(_dot(hb, win_ref[:, a5 + ci * gw:a5 + (ci + 1) * gw]) + bg_ref[:, cols]))
        for unit in units[ci::n_chunks]:
            conv_unit(*unit)
    g_attn = jnp.concatenate(gates[:n_chunks // 2], axis=1)
    g_conv = jnp.concatenate(gates[n_chunks // 2:], axis=1)

    def attend(r0, k0):
        if not cfg.has_cache:
            col = lax.broadcasted_iota(I32, (1, cfg.kb), 1)
            valid = jnp.logical_or(col >= PREV - k0, t > 0)
        heads = []
        for h in range(N_HEADS):
            c0, c1 = h * HEAD_DIM, (h + 1) * HEAD_DIM
            s = lax.dot_general(qbuf[pl.ds(r0, cfg.qb), c0:c1], kbuf[pl.ds(k0, cfg.kb), c0:c1],
                                (((1,), (1,)), ((), ())), preferred_element_type=F32)
            s = s + bias_ref[h]
            if not cfg.has_cache:
                s = jnp.where(valid, s, NEG)
            p = jnp.exp2(s - jnp.max(s, axis=-1, keepdims=True))
            l = jnp.sum(p, axis=-1, keepdims=True)
            heads.append(_dot(p.astype(BF16), vbuf[pl.ds(k0, cfg.kb), c0:c1]) * (1.0 / l))
        obuf[pl.ds(r0, cfg.qb), :] = jnp.concatenate(heads, axis=1).astype(BF16)

    for j in range(cfg.n_qblk):
        attend(j * cfg.qb, j * cfg.kstride)
    y_attn = _dot(obuf[...], wao_ref[...])
    if not cfg.has_cache:
        kbuf[0:PREV, :] = kbuf[tt:tt + PREV, :]
        vbuf[0:PREV, :] = vbuf[tt:tt + PREV, :]

    if not cfg.has_cache:
        uext[0:CONV_PAD, :] = uext[tt:tt + CONV_PAD, :]
    cv = cvbuf[...]
    mu = jnp.mean(cv, axis=-1, keepdims=True)
    xc = cv - mu
    var = jnp.mean(xc * xc, axis=-1, keepdims=True)
    cv = xc * lax.rsqrt(var + EPS) * lng_ref[...] + lnb_ref[...]
    cv = cv * _sigmoid_t(cv)
    y_conv = _dot(cv.astype(BF16), wco_ref[...]) + bco_ref[...]

    m = g_attn * y_attn + g_conv * y_conv
    x1 = x + g1 * _dot(m.astype(BF16), wout_ref[...])
    x1_ref[0] = x1

    h2 = _rms(x1, n2g_ref[...]) * (1.0 + sc2) + sh2
    h2_ref[0] = _pack(h2[:, 0:d_model // 2], h2[:, d_model // 2:d_model])
    h2_hi, h2_lo = _split_bf16(h2)
    nt_dims = (((1,), (1,)), ((), ()))
    logits = (lax.dot_general(wrh_ref[...], h2_hi, nt_dims, preferred_element_type=F32)
              + lax.dot_general(wrl_ref[...], h2_hi, nt_dims, preferred_element_type=F32)
              + lax.dot_general(wrh_ref[...], h2_lo, nt_dims, preferred_element_type=F32))
    s = _sigmoid(logits)
    eids, weights = _route(s + rb_ref[...], s)
    ids = _expert_ids(tt)
    if cfg.sparse:
        sel = [functools.reduce(jnp.add, [jnp.where(ids[g] == e, 1.0, 0.0) for e in eids]) for g in range(N_GROUPS)]
        sel = jnp.concatenate(sel, axis=0)
        rank = _dot(sel.astype(BF16), tri_ref[...])
        rank_slabs = [rank[g * GROUP_SIZE:(g + 1) * GROUP_SIZE, :] for g in range(N_GROUPS)]
        tok_ref[0] = _lane_dense_rows(weights, tt)
        eid_ref[...] = jnp.concatenate(eids, axis=0)
        rank_ref[...] = jnp.concatenate([_pick(ids, e, rank_slabs) for e in eids], axis=0)
        step = pl.program_id(0) * pl.num_programs(1) + t

        @pl.when(step == 0)
        def _():
            cnt_ref[...] = jnp.zeros(cnt_ref.shape, F32)

        lane = lax.broadcasted_iota(I32, cnt_ref.shape, 1)
        cnt_ref[...] = jnp.where(lane == step, jnp.sum(sel, axis=1, keepdims=True), cnt_ref[...])
    else:
        gates = [functools.reduce(jnp.add, [jnp.where(ids[g] == e, w, 0.0) for e, w in zip(eids, weights)])
                 for g in range(N_GROUPS)]
        gates = jnp.concatenate(gates + [jnp.zeros((LANES - N_EXPERTS, tt), F32)], axis=0)
        tok_ref[0] = gates.T


def _toeplitz(v, rows, cols):
    w = rows + cols
    flat = jnp.tile(v, (1, rows))[:, :rows * (w - 1)]
    return flat.reshape(v.shape[0], rows, w - 1)[:, :, :cols]


def _rel_bias_blocks(table, cfg, q_pos0, k_pos0):
    q_pos = q_pos0 + jnp.arange(cfg.qb)
    k_pos = k_pos0 + jnp.arange(cfg.kb)
    qc = q_pos // CHUNK
    kc = k_pos // CHUNK
    allowed = (kc[None, :] <= qc[:, None]) & (kc[None, :] >= qc[:, None] - BAND_CHUNKS)
    w = cfg.qb + cfg.kb
    dj = jnp.arange(w)
    dj = jnp.where(dj < cfg.kb, dj, dj - w)
    rel = jnp.clip(q_pos0 - k_pos0 - dj, -MAX_REL, MAX_REL) + MAX_REL
    bias = _toeplitz(table[:, rel].astype(F32), cfg.qb, cfg.kb)
    return jnp.where(allowed[None], bias * LOG2_E, NEG)


def _mixer_cost(cfg, tokens, d, n_in, conv_dim):
    per_token = (2 * d * n_in + 2 * (ATTN_DIM + conv_dim + d) * d + 4 * ATTN_DIM * cfg.kb
                 + 4 * ATTN_DIM * ATTN_DIM + 6 * N_EXPERTS * d + 2 * CONV_WIDTH * conv_dim)
    return pl.CostEstimate(flops=tokens * per_token,
                           transcendentals=tokens * (N_HEADS * cfg.kb + 2 * d + 2 * conv_dim + N_EXPERTS),
                           bytes_accessed=tokens * (4 * d + 4 * d + 2 * d + 4 * LANES) + 2 * d * n_in)


def _mixer_call(cfg, x, mod, p, cache=None, batch0=0, nbatch=None, state=None):
    b_all, t_total, d = x.shape
    nb = b_all if nbatch is None else nbatch
    nt = t_total // cfg.tt
    conv_dim = p["conv_w"].shape[1]
    n_in = p["w_in"].shape[1]

    def const(shape):
        return pl.BlockSpec(shape, lambda i, j: (0,) * len(shape), pipeline_mode=pl.Buffered(1))

    def per_row(shape):
        return pl.BlockSpec(shape, lambda i, j: (i + batch0, 0, 0))

    in_specs = [
        pl.BlockSpec((1, cfg.tt, d), lambda i, j: (i + batch0, j, 0)),
        per_row((1, 6, d)),
        const((1, d)), const((1, d)), const((d, n_in)),
        const((1, ATTN_DIM)), const((1, ATTN_DIM)), const((ATTN_DIM, ATTN_DIM)),
        const((N_HEADS, cfg.qb, cfg.kb)), const((ATTN_DIM, d)),
        const((CONV_WIDTH, conv_dim)), const((1, conv_dim)), const((1, conv_dim)), const((1, conv_dim)),
        const((conv_dim, d)), const((1, d)), const((1, 2 * d)), const((d, d)),
        const((N_EXPERTS, d)), const((N_EXPERTS, d)), const((N_EXPERTS, 1)),
    ]
    args = [x, mod, p["norm1_g"], p["norm2_g"], p["w_in"], p["q_norm_g"], p["k_norm_g"], p["hsum"],
            p["bias"], p["w_attn_o"], p["conv_w"], p["conv_b"], p["conv_ln_g"], p["conv_ln_b"],
            p["w_conv_o"], p["b_conv_o"], p["b_gate"], p["w_out"], p["wr_hi"], p["wr_lo"], p["router_bias"]]
    if cfg.has_cache:
        in_specs += [per_row((1, PREV, ATTN_DIM)), per_row((1, PREV, ATTN_DIM)),
                     per_row((1, CONV_WIDTH - 1, conv_dim))]
        args += list(cache)
    if cfg.sparse:
        assert nb * nt <= MAX_TILES
        in_specs += [const((cfg.tt, cfg.tt))]
        tok = jnp.arange(cfg.tt)
        args += [(tok[:, None] < tok[None, :]).astype(BF16)]
    aliases = {}
    if state is not None:
        for i, arr in enumerate(state):
            aliases[len(args)] = 3 + i
            in_specs.append(pl.BlockSpec(memory_space=pl.ANY))
            args.append(arr)
    keep = min(PREV, t_total)
    assert keep == cfg.tt
    out_shape = [jax.ShapeDtypeStruct((nb, t_total, d), F32),
                 jax.ShapeDtypeStruct((nb, t_total, d // 2), I32),
                 jax.ShapeDtypeStruct((nb, t_total, LANES), F32),
                 jax.ShapeDtypeStruct((b_all, keep, ATTN_DIM), F32),
                 jax.ShapeDtypeStruct((b_all, keep, ATTN_DIM), F32),
                 jax.ShapeDtypeStruct((b_all, CONV_WIDTH - 1, conv_dim), F32)]
    out_specs = [pl.BlockSpec((1, cfg.tt, d), lambda i, j: (i, j, 0)),
                 pl.BlockSpec((1, cfg.tt, d // 2), lambda i, j: (i, j, 0)),
                 pl.BlockSpec((1, cfg.tt, LANES), lambda i, j: (i, j, 0)),
                 per_row((1, keep, ATTN_DIM)), per_row((1, keep, ATTN_DIM)),
                 per_row((1, CONV_WIDTH - 1, conv_dim))]
    if cfg.sparse:
        out_shape += [jax.ShapeDtypeStruct((TOP_K, nb * t_total), F32),
                      jax.ShapeDtypeStruct((TOP_K, nb * t_total), F32),
                      jax.ShapeDtypeStruct((N_EXPERTS, MAX_TILES), F32)]
        out_specs += [pl.BlockSpec((TOP_K, cfg.tt), lambda i, j: (0, i * nt + j)),
                      pl.BlockSpec((TOP_K, cfg.tt), lambda i, j: (0, i * nt + j)),
                      pl.BlockSpec((N_EXPERTS, MAX_TILES), lambda i, j: (0, 0))]
    scratch = [pltpu.VMEM((cfg.tt, ATTN_DIM), BF16),
               pltpu.VMEM((PREV + cfg.tt, ATTN_DIM), BF16),
               pltpu.VMEM((PREV + cfg.tt, ATTN_DIM), BF16),
               pltpu.VMEM((cfg.tt, ATTN_DIM), BF16),
               pltpu.VMEM((CONV_PAD + cfg.tt, conv_dim), F32),
               pltpu.VMEM((cfg.tt, conv_dim), F32)]
    return pl.pallas_call(
        functools.partial(_mixer_kernel, cfg, 0 if state is None else len(state)),
        out_shape=out_shape,
        grid=(nb, nt),
        in_specs=in_specs,
        out_specs=out_specs,
        scratch_shapes=scratch,
        input_output_aliases=aliases,
        cost_estimate=_mixer_cost(cfg, nb * t_total, d, n_in, conv_dim),
        compiler_params=pltpu.CompilerParams(dimension_semantics=("arbitrary", "arbitrary"),
                                             vmem_limit_bytes=VMEM_LIMIT_BYTES),
        name="mixer_sample" if cfg.has_cache else "mixer_prompt",
    )(*args)


def _exact_parts(x, n):
    parts = []
    for _ in range(n):
        part = x.astype(BF16)
        parts.append(part)
        x = x - part.astype(F32)
    return parts


def _plan_kernel(tt, cnt_ref, eid_ref, rank_ref, tri_e_ref, tri_t_ref, pos_ref, blk_ref, nblk_ref, base_ref):
    step = pl.program_id(0)

    @pl.when(step == 0)
    def _():
        cnt = cnt_ref[...]
        total = jnp.sum(cnt, axis=1, keepdims=True)
        padded = jnp.floor((total + float(ROW_BLOCK - 1)) * (1.0 / ROW_BLOCK)) * float(ROW_BLOCK)
        padded = jnp.broadcast_to(padded, cnt.shape)
        start = functools.reduce(jnp.add, [_dot(tri_e_ref[...], part) for part in _exact_parts(padded, 3)])
        before = functools.reduce(jnp.add, [_dot(part, tri_t_ref[...]) for part in _exact_parts(cnt, 2)])
        base_ref[...] = start + before
        end = (start + padded)[:, 0:1]
        first_row = lax.broadcasted_iota(I32, blk_ref.shape, 1).astype(F32) * float(ROW_BLOCK)
        owner = jnp.sum(jnp.where(end <= first_row, 1.0, 0.0), axis=0, keepdims=True)
        blk_ref[...] = jnp.minimum(owner, float(N_EXPERTS - 1)).astype(I32)
        nblk = jnp.max(end, axis=0, keepdims=True) * (1.0 / ROW_BLOCK)
        nblk_ref[...] = jnp.broadcast_to(nblk, nblk_ref.shape).astype(I32)

    lane = lax.broadcasted_iota(I32, base_ref.shape, 1)
    col = jnp.sum(jnp.where(lane == step, base_ref[...], 0.0), axis=1, keepdims=True)
    ids = _expert_ids(tt)
    col_slabs = [jnp.broadcast_to(col[g * GROUP_SIZE:(g + 1) * GROUP_SIZE, :], (GROUP_SIZE, tt))
                 for g in range(N_GROUPS)]
    eid = eid_ref[...]
    rows = [_pick(ids, eid[k:k + 1, :], col_slabs) for k in range(TOP_K)]
    pos_ref[...] = (jnp.concatenate(rows, axis=0) + rank_ref[...]).astype(I32)


def _plan_call(cnt, eid, rank, tt, n_blocks):
    n = eid.shape[1]
    e = jnp.arange(N_EXPERTS)
    s = jnp.arange(MAX_TILES)
    nb_pad = -(-n_blocks // LANES) * LANES
    return pl.pallas_call(
        functools.partial(_plan_kernel, tt),
        out_shape=[jax.ShapeDtypeStruct((TOP_K, n), I32),
                   jax.ShapeDtypeStruct((1, nb_pad), I32),
                   jax.ShapeDtypeStruct((1, LANES), I32)],
        grid=(n // tt,),
        in_specs=[pl.BlockSpec((N_EXPERTS, MAX_TILES), lambda i: (0, 0)),
                  pl.BlockSpec((TOP_K, tt), lambda i: (0, i)),
                  pl.BlockSpec((TOP_K, tt), lambda i: (0, i)),
                  pl.BlockSpec((N_EXPERTS, N_EXPERTS), lambda i: (0, 0)),
                  pl.BlockSpec((MAX_TILES, MAX_TILES), lambda i: (0, 0))],
        out_specs=[pl.BlockSpec((TOP_K, tt), lambda i: (0, i)),
                   pl.BlockSpec((1, nb_pad), lambda i: (0, 0)),
                   pl.BlockSpec((1, LANES), lambda i: (0, 0))],
        scratch_shapes=[pltpu.VMEM((N_EXPERTS, MAX_TILES), F32)],
        compiler_params=pltpu.CompilerParams(dimension_semantics=("arbitrary",),
                                             vmem_limit_bytes=VMEM_LIMIT_BYTES),
        name="plan",
    )(cnt, eid, rank, (e[None, :] < e[:, None]).astype(BF16), (s[:, None] < s[None, :]).astype(BF16))


def _sc_mesh():
    return plsc.VectorSubcoreMesh(core_axis_name="c", subcore_axis_name="s")


def _dispatch_call(rows, pos, n_sorted):
    n, w = rows.shape
    per_worker = n // SC_WORKERS
    nsteps = per_worker // SC_ROWS
    assert per_worker * SC_WORKERS == n and nsteps * SC_ROWS == per_worker and nsteps % 2 == 0

    @functools.partial(
        pl.kernel, mesh=_sc_mesh(),
        out_type=jax.ShapeDtypeStruct((n_sorted, w), rows.dtype),
        scratch_types=[pltpu.VMEM((2, TOP_K, SC_ROWS), I32),
                       pltpu.VMEM((2, SC_ROWS, w), rows.dtype),
                       pltpu.SemaphoreType.DMA((2,)),
                       pltpu.SemaphoreType.DMA((2,))],
        cost_estimate=pl.CostEstimate(flops=0, transcendentals=0,
                                      bytes_accessed=(1 + TOP_K) * n * w * rows.dtype.itemsize + 4 * TOP_K * n),
    )
    def dispatch(x_hbm, pos_hbm, out_hbm, idx_v, rows_v, load_sem, scat_sem):
        base = (lax.axis_index("s") * SC_CORES + lax.axis_index("c")) * per_worker

        def load(i, b):
            return pltpu.make_async_copy(x_hbm.at[pl.ds(base + i * SC_ROWS, SC_ROWS)], rows_v.at[b], load_sem.at[b])

        def scatter(b, k):
            return pltpu.make_async_copy(rows_v.at[b], out_hbm.at[idx_v.at[b, k]], scat_sem.at[b])

        def load_start(i, b):
            for k in range(TOP_K):
                pltpu.sync_copy(pos_hbm.at[pl.ds(k * n + base + i * SC_ROWS, SC_ROWS)], idx_v.at[b, k])
            load(i, b).start()

        load_start(0, 0)

        @pl.loop(0, nsteps, step=2)
        def _(i):
            for b in range(2):
                ii = i + b

                @pl.when(ii >= 1)
                def _():
                    for k in range(TOP_K):
                        scatter(1 - b, k).wait()

                @pl.when(ii + 1 < nsteps)
                def _():
                    load_start(ii + 1, 1 - b)

                load(ii, b).wait()
                for k in range(TOP_K):
                    scatter(b, k).start()

        for k in range(TOP_K):
            scatter((nsteps - 1) % 2, k).wait()

    return dispatch(rows, pos)


def _collect_call(table, idx):
    n = idx.shape[0]
    w = table.shape[1]
    per_worker = n // SC_WORKERS
    nsteps = per_worker // SC_ROWS
    assert per_worker * SC_WORKERS == n and nsteps * SC_ROWS == per_worker and nsteps % 2 == 0

    @functools.partial(
        pl.kernel, mesh=_sc_mesh(),
        out_type=jax.ShapeDtypeStruct((n, w), table.dtype),
        scratch_types=[pltpu.VMEM((2, SC_ROWS), I32),
                       pltpu.VMEM((2, SC_ROWS, w), table.dtype),
                       pltpu.SemaphoreType.DMA((2,)),
                       pltpu.SemaphoreType.DMA((2,))],
        cost_estimate=pl.CostEstimate(flops=0, transcendentals=0,
                                      bytes_accessed=2 * n * w * table.dtype.itemsize + 4 * n),
    )
    def collect(table_hbm, idx_hbm, out_hbm, idx_v, rows_v, gather_sem, write_sem):
        base = (lax.axis_index("s") * SC_CORES + lax.axis_index("c")) * per_worker

        def gather(b):
            return pltpu.make_async_copy(table_hbm.at[idx_v.at[b]], rows_v.at[b], gather_sem.at[b])

        def write(i, b):
            return pltpu.make_async_copy(rows_v.at[b], out_hbm.at[pl.ds(base + i * SC_ROWS, SC_ROWS)], write_sem.at[b])

        def gather_start(i, b):
            pltpu.sync_copy(idx_hbm.at[pl.ds(base + i * SC_ROWS, SC_ROWS)], idx_v.at[b])
            gather(b).start()

        gather_start(0, 0)

        @pl.loop(0, nsteps, step=2)
        def _(i):
            for b in range(2):
                ii = i + b

                @pl.when(ii >= 1)
                def _():
                    write(ii - 1, 1 - b).wait()

                @pl.when(ii + 1 < nsteps)
                def _():
                    gather_start(ii + 1, 1 - b)

                gather(b).wait()
                write(ii, b).start()

        write(nsteps - 1, (nsteps - 1) % 2).wait()

    return collect(table, idx)


def _expert_ffn_kernel(blk_ref, nblk_ref, x_ref, *refs):
    w_refs, o_ref, w_bf16 = refs[:3 * FFN_BLOCKS], refs[3 * FFN_BLOCKS], refs[3 * FFN_BLOCKS + 1:]
    j = pl.program_id(0)
    for h in range(FFN_BLOCKS):
        wg_ref, wu_ref, wd_ref = w_refs[3 * h:3 * h + 3]
        wg_s, wu_s, wd_s = w_bf16[3 * h:3 * h + 3]
        block = FFN_BLOCKS * j + h

        @pl.when(jnp.logical_or(j == 0, blk_ref[block] != blk_ref[jnp.maximum(block - FFN_BLOCKS, 0)]))
        def _():
            wg_s[...] = wg_ref[0].astype(BF16)
            wu_s[...] = wu_ref[0].astype(BF16)
            wd_s[...] = wd_ref[0].astype(BF16)

        @pl.when(block < nblk_ref[0])
        def _():
            rows = pl.ds(h * ROW_BLOCK, ROW_BLOCK)
            a, b = _unpack(x_ref[rows, :])
            a, b = a.astype(BF16), b.astype(BF16)
            hg = _dot_halves(a, b, wg_s)
            act = hg * _sigmoid_t(hg) * _dot_halves(a, b, wu_s)
            y = _dot(act.astype(BF16), wd_s[...])
            half = y.shape[1] // 2
            o_ref[rows, :] = _pack(y[:, 0:half], y[:, half:2 * half])


def _expert_ffn_call(xs, blk, nblk, p):
    n_sorted, w = xs.shape
    d = 2 * w
    ff = p["w_e_gate"].shape[2]
    step_rows = FFN_BLOCKS * ROW_BLOCK

    def rows(j, blk, nblk):
        return (jnp.minimum(j, (nblk[0] - 1) // FFN_BLOCKS), 0)

    def expert(h):
        return lambda j, blk, nblk: (blk[FFN_BLOCKS * j + h], 0, 0)

    w_specs, w_args, w_scratch = [], [], []
    for h in range(FFN_BLOCKS):
        w_specs += [pl.BlockSpec((1, d, ff), expert(h)), pl.BlockSpec((1, d, ff), expert(h)),
                    pl.BlockSpec((1, ff, d), expert(h))]
        w_args += [p["w_e_gate"], p["w_e_up"], p["w_e_down"]]
        w_scratch += [pltpu.VMEM((d, ff), BF16), pltpu.VMEM((d, ff), BF16), pltpu.VMEM((ff, d), BF16)]
    return pl.pallas_call(
        _expert_ffn_kernel,
        out_shape=jax.ShapeDtypeStruct((n_sorted, w), I32),
        grid_spec=pltpu.PrefetchScalarGridSpec(
            num_scalar_prefetch=2,
            grid=(n_sorted // step_rows,),
            in_specs=[pl.BlockSpec((step_rows, w), rows)] + w_specs,
            out_specs=pl.BlockSpec((step_rows, w), rows),
            scratch_shapes=w_scratch),
        cost_estimate=pl.CostEstimate(flops=6 * n_sorted * d * ff, transcendentals=n_sorted * ff,
                                      bytes_accessed=8 * n_sorted * w + 12 * N_EXPERTS * d * ff),
        compiler_params=pltpu.CompilerParams(dimension_semantics=("arbitrary",),
                                             vmem_limit_bytes=VMEM_LIMIT_BYTES),
        name="expert_ffn",
    )(blk, nblk, xs, *w_args)


def _swiglu_halves(a, b, wg_ref, wu_ref):
    hg = _dot_halves(a, b, wg_ref)
    return hg * _sigmoid_t(hg) * _dot_halves(a, b, wu_ref)


def _moe_out_kernel(n_groups, starts, *refs):
    groups = [refs[4 * q:4 * q + 4] for q in range(n_groups)]
    g2_ref, wsg_ref, wsu_ref, wsd_ref, o_ref = refs[4 * n_groups:]

    def combine(x1_ref, h_ref, y_ref, w_ref):
        a, b = _unpack(h_ref[...])
        shared = _dot(_swiglu_halves(a.astype(BF16), b.astype(BF16), wsg_ref, wsu_ref).astype(BF16),
                      wsd_ref[...].astype(BF16))
        half = h_ref.shape[1]
        w = w_ref[...]
        acc_a, acc_b = shared[:, 0:half], shared[:, half:2 * half]
        for k in range(TOP_K):
            ya, yb = _unpack(y_ref[k])
            acc_a = acc_a + w[:, k:k + 1] * ya
            acc_b = acc_b + w[:, k:k + 1] * yb
        g2 = g2_ref[...]
        o_ref[:, 0:half] = x1_ref[:, 0:half] + g2[:, 0:half] * acc_a
        o_ref[:, half:2 * half] = x1_ref[:, half:2 * half] + g2[:, half:2 * half] * acc_b

    i = pl.program_id(0)
    for q in range(n_groups):
        pl.when(jnp.logical_and(i >= starts[q], i < starts[q + 1]))(functools.partial(combine, *groups[q]))


def _moe_out_call(groups, g2, p, tm):
    d = groups[0][0].shape[1]
    starts = [0]
    for group in groups:
        starts.append(starts[-1] + group[0].shape[0] // tm)
    n_all = starts[-1] * tm
    sff = p["w_s_gate"].shape[1]
    per_row = n_all // g2.shape[0]
    g2 = g2.reshape(g2.shape[0], 1, d)

    def const(shape):
        return pl.BlockSpec(shape, lambda i: (0,) * len(shape), pipeline_mode=pl.Buffered(1))

    in_specs, args = [], []
    for q, group in enumerate(groups):
        tile = lambda i, q=q: jnp.clip(i - starts[q], 0, starts[q + 1] - starts[q] - 1)
        in_specs += [pl.BlockSpec((tm, d), lambda i, tile=tile: (tile(i), 0)),
                     pl.BlockSpec((tm, d // 2), lambda i, tile=tile: (tile(i), 0)),
                     pl.BlockSpec((TOP_K, tm, d // 2), lambda i, tile=tile: (0, tile(i), 0)),
                     pl.BlockSpec((tm, LANES), lambda i, tile=tile: (tile(i), 0))]
        args += list(group)
    in_specs += [pl.BlockSpec((None, 1, d), lambda i: (i * tm // per_row, 0, 0)),
                 const((d, sff)), const((d, sff)), const((sff, d))]
    args += [g2, p["w_s_gate"], p["w_s_up"], p["w_s_down"]]
    return pl.pallas_call(
        functools.partial(_moe_out_kernel, len(groups), tuple(starts)),
        out_shape=jax.ShapeDtypeStruct((n_all, d), F32),
        grid=(n_all // tm,),
        in_specs=in_specs,
        out_specs=pl.BlockSpec((tm, d), lambda i: (i, 0)),
        cost_estimate=pl.CostEstimate(flops=n_all * (6 * d * sff + 2 * TOP_K * d), transcendentals=n_all * sff,
                                      bytes_accessed=n_all * (8 * d + 2 * d + 2 * TOP_K * d + 4 * LANES)),
        compiler_params=pltpu.CompilerParams(dimension_semantics=("arbitrary",),
                                             vmem_limit_bytes=VMEM_LIMIT_BYTES),
        name="moe_out",
    )(*args)


def _moe_dense_kernel(h_ref, gates_ref, x1_ref, g2_ref, wsg_ref, wsu_ref, wsd_ref, wg_ref, wu_ref, wd_ref,
                      o_ref, acc_ref):
    e = pl.program_id(1)
    a, b = _unpack(h_ref[...])
    a, b = a.astype(BF16), b.astype(BF16)

    @pl.when(e == 0)
    def _():
        acc_ref[...] = _dot(_swiglu_halves(a, b, wsg_ref, wsu_ref).astype(BF16), wsd_ref[...].astype(BF16))

    gates = gates_ref[...]
    lane = lax.broadcasted_iota(I32, gates.shape, 1)
    gate = jnp.sum(jnp.where(lane == e, gates, 0.0), axis=1, keepdims=True)
    act = _swiglu_halves(a, b, wg_ref.at[0], wu_ref.at[0]) * gate
    acc_ref[...] += _dot(act.astype(BF16), wd_ref[0].astype(BF16))

    @pl.when(e == pl.num_programs(1) - 1)
    def _():
        o_ref[...] = x1_ref[...] + g2_ref[...] * acc_ref[...]


def _moe_dense_call(h2, gates, x1, g2_rows, p):
    n, d = x1.shape
    ff = p["w_e_gate"].shape[2]
    sff = p["w_s_gate"].shape[1]

    def const(shape):
        return pl.BlockSpec(shape, lambda i, e: (0,) * len(shape), pipeline_mode=pl.Buffered(1))

    return pl.pallas_call(
        _moe_dense_kernel,
        out_shape=jax.ShapeDtypeStruct((n, d), F32),
        grid=(1, N_EXPERTS),
        in_specs=[const((n, d // 2)), const((n, LANES)), const((n, d)), const((n, d)),
                  const((d, sff)), const((d, sff)), const((sff, d)),
                  pl.BlockSpec((1, d, ff), lambda i, e: (e, 0, 0)),
                  pl.BlockSpec((1, d, ff), lambda i, e: (e, 0, 0)),
                  pl.BlockSpec((1, ff, d), lambda i, e: (e, 0, 0))],
        out_specs=pl.BlockSpec((n, d), lambda i, e: (0, 0)),
        scratch_shapes=[pltpu.VMEM((n, d), F32)],
        compiler_params=pltpu.CompilerParams(dimension_semantics=("arbitrary", "arbitrary"),
                                             vmem_limit_bytes=VMEM_LIMIT_BYTES),
        name="moe_dense",
    )(h2, gates, x1, g2_rows, p["w_s_gate"], p["w_s_up"], p["w_s_down"], p["w_e_gate"], p["w_e_up"], p["w_e_down"])


PROMPT_CFG = MixerCfg(tt=512, qb=256, kb=768, n_qblk=2, kstride=256, has_cache=False, sparse=True)
MOE_OUT_TILE = 512
PROMPT_SPLIT = (1, 1)


def _layer_params(l, w_in, q_norm_g, k_norm_g, w_attn_o, conv_w, conv_b, conv_ln_g, conv_ln_b, w_conv_o,
                  b_conv_o, b_gate, w_out, norm1_g, norm2_g, w_router, router_bias,
                  w_e_gate, w_e_up, w_e_down, w_s_gate, w_s_up, w_s_down):
    row = lambda a: a[l].reshape(1, -1)
    wr_t = w_router[l].T
    wr_hi = wr_t.astype(BF16)
    head = jnp.arange(ATTN_DIM) // HEAD_DIM
    return {
        "norm1_g": row(norm1_g), "norm2_g": row(norm2_g), "w_in": w_in[l].astype(BF16),
        "q_norm_g": jnp.tile(q_norm_g[l], N_HEADS).reshape(1, -1),
        "k_norm_g": jnp.tile(k_norm_g[l], N_HEADS).reshape(1, -1),
        "hsum": (head[:, None] == head[None, :]).astype(BF16),
        "w_attn_o": w_attn_o[l].astype(BF16), "conv_w": conv_w[l], "conv_b": row(conv_b),
        "conv_ln_g": row(conv_ln_g), "conv_ln_b": row(conv_ln_b), "w_conv_o": w_conv_o[l].astype(BF16),
        "b_conv_o": row(b_conv_o), "b_gate": row(b_gate), "w_out": w_out[l].astype(BF16),
        "wr_hi": wr_hi, "wr_lo": (wr_t - wr_hi.astype(F32)).astype(BF16),
        "router_bias": router_bias[l].reshape(-1, 1),
        "w_e_gate": w_e_gate[l], "w_e_up": w_e_up[l], "w_e_down": w_e_down[l],
        "w_s_gate": w_s_gate[l], "w_s_up": w_s_up[l], "w_s_down": w_s_down[l],
    }


def _sorted_rows(h2, eid, rank, cnt, tt):
    n = h2.shape[0]
    n_sorted = n * TOP_K + N_EXPERTS * ROW_BLOCK
    assert n_sorted % (FFN_BLOCKS * ROW_BLOCK) == 0
    pos, blk, nblk = _plan_call(cnt, eid, rank, tt, n_sorted // ROW_BLOCK)
    pos = pos.reshape(TOP_K * n)
    return _dispatch_call(h2, pos, n_sorted), pos, blk.reshape(-1), nblk.reshape(-1)


def kernel(x_prompt, x_sample, c_prompt, c_sample, cache_k, cache_v, state_conv, w_mod, b_mod, norm1_g, w_in, q_norm_g, k_norm_g, rel_bias, w_attn_o, conv_w, conv_b, conv_ln_g, conv_ln_b, w_conv_o, b_conv_o, b_gate, w_out, norm2_g, w_router, router_bias, w_e_gate, w_e_up, w_e_down, w_s_gate, w_s_up, w_s_down):
    depth = w_mod.shape[0]
    bp, tp, d = x_prompt.shape
    bs, ts, _ = x_sample.shape
    assert cache_k.shape[2] == min(PREV, PAST_LEN) == PREV
    sample_cfg = MixerCfg(tt=ts, qb=ts, kb=PREV + ts, n_qblk=1, kstride=0, has_cache=True, sparse=False)

    yp, ys = x_prompt, x_sample
    outs = [[] for _ in range(6)]
    for l in range(depth):
        p = _layer_params(l, w_in, q_norm_g, k_norm_g, w_attn_o, conv_w, conv_b, conv_ln_g, conv_ln_b,
                          w_conv_o, b_conv_o, b_gate, w_out, norm1_g, norm2_g, w_router, router_bias,
                          w_e_gate, w_e_up, w_e_down, w_s_gate, w_s_up, w_s_down)
        mod = _mod_call(jnp.concatenate([c_prompt, c_sample], axis=0), w_mod[l], b_mod[l])
        mod = mod.reshape(bp + bs, 6, d)
        mod_p, mod_s = mod[:bp], mod[bp:]

        p["bias"] = _rel_bias_blocks(rel_bias[l], PROMPT_CFG, PREV, 0)
        big = max(bp * PROMPT_SPLIT[0] // sum(PROMPT_SPLIT), 1)
        group_rows = [big, bp - big] if bp > big else [bp]
        keep = min(PREV, tp)
        state = [jnp.zeros((bp, keep, ATTN_DIM), F32), jnp.zeros((bp, keep, ATTN_DIM), F32),
                 jnp.zeros((bp, CONV_WIDTH - 1, conv_w.shape[2]), F32)]
        mixed, moved = [], []
        for g, gb in enumerate(group_rows):
            x1, h2, w_tok, *state, eid, rank, cnt = _mixer_call(PROMPT_CFG, yp, mod_p, p, batch0=sum(group_rows[:g]),
                                                                nbatch=gb, state=state)
            x1, h2, w_tok = x1.reshape(gb * tp, d), h2.reshape(gb * tp, d // 2), w_tok.reshape(gb * tp, LANES)
            mixed.append((x1, h2, w_tok))
            moved.append(_sorted_rows(h2, eid, rank, cnt, PROMPT_CFG.tt))
            state, _ = lax.optimization_barrier((state, moved[-1][1]))
        kp, vp, cp = state
        groups, gate = [], cp
        for (x1, h2, w_tok), (xs, pos, blk, nblk) in zip(mixed, moved):
            xs, _ = lax.optimization_barrier((xs, gate))
            gate = _expert_ffn_call(xs, blk, nblk, p)
            yg = _collect_call(gate, pos).reshape(TOP_K, h2.shape[0], h2.shape[1])
            groups.append((x1, h2, yg, w_tok))
        yp = _moe_out_call(groups, mod_p[:, 5, :], p, MOE_OUT_TILE).reshape(bp, tp, d)
        outs[0].append(kp.reshape(bp, -1, N_HEADS, HEAD_DIM))
        outs[1].append(vp.reshape(bp, -1, N_HEADS, HEAD_DIM))
        outs[2].append(cp)

        p["bias"] = _rel_bias_blocks(rel_bias[l], sample_cfg, PAST_LEN, PAST_LEN - PREV)
        cache = (cache_k[l].reshape(bs, PREV, ATTN_DIM), cache_v[l].reshape(bs, PREV, ATTN_DIM), state_conv[l])
        x1, h2, gates, ks, vs, cs = _mixer_call(sample_cfg, ys, mod_s, p, cache)
        ys = _moe_dense_call(h2.reshape(bs * ts, d // 2), gates.reshape(bs * ts, LANES), x1.reshape(bs * ts, d),
                             jnp.repeat(mod_s[:, 5, :], ts, axis=0), p).reshape(bs, ts, d)
        outs[3].append(ks.reshape(bs, ts, N_HEADS, HEAD_DIM))
        outs[4].append(vs.reshape(bs, ts, N_HEADS, HEAD_DIM))
        outs[5].append(cs)
    return (yp, ys) + tuple(jnp.stack(o) for o in outs)
```

```python
import dataclasses
import functools

import jax
import jax.numpy as jnp
from jax import lax
from jax.experimental import pallas as pl
from jax.experimental.pallas import tpu as pltpu
from jax.experimental.pallas import tpu_sc as plsc

F32 = jnp.float32
BF16 = jnp.bfloat16
I32 = jnp.int32

CHUNK = 64
BAND_CHUNKS = 8
PREV = BAND_CHUNKS * CHUNK
PAST_LEN = 1024
N_HEADS = 8
HEAD_DIM = 64
ATTN_DIM = N_HEADS * HEAD_DIM
MAX_REL = 128
CONV_WIDTH = 31
CONV_PAD = 32
N_EXPERTS = 64
N_GROUPS = 8
GROUP_SIZE = N_EXPERTS // N_GROUPS
TOPK_GROUPS = 4
TOP_K = 8
ROUTE_SCALE = 2.5
EPS = 1e-6
LANES = 128
SUBLANES = 8
NEG = -1e30
LOG2_E = 1.4426950408889634

VMEM_LIMIT_BYTES = 60 * 1024 * 1024

SC_CORES = 2
SC_SUBCORES = 16
SC_WORKERS = SC_CORES * SC_SUBCORES
SC_ROWS = 64

MAX_TILES = LANES
ROW_BLOCK = 512
FFN_BLOCKS = 2


def _sigmoid(x):
    return 1.0 / (1.0 + jnp.exp(-x))


def _sigmoid_t(x):
    return 0.5 * jnp.tanh(0.5 * x) + 0.5


def _split_bf16(x):
    hi = x.astype(BF16)
    lo = (x - hi.astype(F32)).astype(BF16)
    return hi, lo


def _dot(a, b):
    return jnp.dot(a, b, preferred_element_type=F32)


def _dot3(a, b):
    a_hi, a_lo = _split_bf16(a)
    b_hi, b_lo = _split_bf16(b)
    return _dot(a_hi, b_hi) + _dot(a_hi, b_lo) + _dot(a_lo, b_hi)


def _pack(a, b):
    ia = lax.bitcast_convert_type(a.astype(BF16).astype(F32), I32)
    ib = lax.bitcast_convert_type(b.astype(BF16).astype(F32), I32)
    return ia | lax.shift_right_logical(ib, 16)


def _unpack(p):
    a = lax.bitcast_convert_type(p & jnp.int32(-65536), F32)
    b = lax.bitcast_convert_type(lax.shift_left(p, 16), F32)
    return a, b


def _dot_halves(a, b, w_ref):
    half = a.shape[1]
    return _dot(a, w_ref[0:half, :].astype(BF16)) + _dot(b, w_ref[half:2 * half, :].astype(BF16))


def _mod_kernel(c_ref, w_ref, b_ref, o_ref):
    c = c_ref[...]
    o_ref[...] = _dot3(c * _sigmoid(c), w_ref[...]) + b_ref[...]


def _mod_call(c, w_mod, b_mod):
    n, d = c.shape
    dout = w_mod.shape[1]
    bn = 1536
    return pl.pallas_call(
        _mod_kernel,
        out_shape=jax.ShapeDtypeStruct((n, dout), F32),
        grid=(dout // bn,),
        in_specs=[pl.BlockSpec((n, d), lambda j: (0, 0)),
                  pl.BlockSpec((d, bn), lambda j: (0, j)),
                  pl.BlockSpec((1, bn), lambda j: (0, j))],
        out_specs=pl.BlockSpec((n, bn), lambda j: (0, j)),
        compiler_params=pltpu.CompilerParams(dimension_semantics=("arbitrary",),
                                             vmem_limit_bytes=VMEM_LIMIT_BYTES),
        name="mod",
    )(c, w_mod, b_mod.reshape(1, dout))


@dataclasses.dataclass(frozen=True)
class MixerCfg:
    tt: int
    qb: int
    kb: int
    n_qblk: int
    kstride: int
    has_cache: bool
    sparse: bool


def _rms(x, g):
    ms = jnp.mean(x * x, axis=-1, keepdims=True)
    return x * lax.rsqrt(ms + EPS) * g


def _head_rms(z, hsum, g):
    ss = _dot((z * z).astype(BF16), hsum)
    return z * lax.rsqrt(ss * (1.0 / HEAD_DIM) + EPS) * g


def _expert_ids(t):
    sub = lax.broadcasted_iota(I32, (GROUP_SIZE, t), 0).astype(F32)
    return [sub + float(g * GROUP_SIZE) for g in range(N_GROUPS)]


def _pick(ids, eid, slabs):
    acc = functools.reduce(jnp.add, [jnp.where(ids[g] == eid, slabs[g], 0.0) for g in range(N_GROUPS)])
    return jnp.sum(acc, axis=0, keepdims=True)


def _route(choice, s):
    t = choice.shape[1]
    sub = lax.broadcasted_iota(I32, (GROUP_SIZE, t), 0).astype(F32)
    slabs = [choice[g * GROUP_SIZE:(g + 1) * GROUP_SIZE, :] for g in range(N_GROUPS)]
    s_slabs = [s[g * GROUP_SIZE:(g + 1) * GROUP_SIZE, :] for g in range(N_GROUPS)]
    gscore = []
    for c in slabs:
        m1 = jnp.max(c, axis=0, keepdims=True)
        first = jnp.min(jnp.where(c == m1, sub, float(GROUP_SIZE)), axis=0, keepdims=True)
        m2 = jnp.max(jnp.where(sub == first, -jnp.inf, c), axis=0, keepdims=True)
        gscore.append(m1 + m2)
    gsel = [jnp.zeros((1, t), F32) for _ in range(N_GROUPS)]
    for _ in range(TOPK_GROUPS):
        m = functools.reduce(jnp.maximum, gscore)
        first = functools.reduce(jnp.minimum,
                                 [jnp.where(gscore[g] == m, float(g), float(N_GROUPS)) for g in range(N_GROUPS)])
        for g in range(N_GROUPS):
            hit = first == float(g)
            gsel[g] = jnp.where(hit, 1.0, gsel[g])
            gscore[g] = jnp.where(hit, -jnp.inf, gscore[g])
    masked = [jnp.where(gsel[g] > 0.5, slabs[g], -jnp.inf) for g in range(N_GROUPS)]
    ids = _expert_ids(t)
    eids, raw = [], []
    for _ in range(TOP_K):
        m = jnp.max(functools.reduce(jnp.maximum, masked), axis=0, keepdims=True)
        cand = [jnp.where(masked[g] == m, ids[g], float(N_EXPERTS)) for g in range(N_GROUPS)]
        first = jnp.min(functools.reduce(jnp.minimum, cand), axis=0, keepdims=True)
        eids.append(first)
        raw.append(_pick(ids, first, s_slabs))
        masked = [jnp.where(ids[g] == first, -jnp.inf, masked[g]) for g in range(N_GROUPS)]
    wsum = functools.reduce(jnp.add, raw)
    return eids, [r / wsum * ROUTE_SCALE for r in raw]


def _lane_dense_rows(rows, t):
    pad = jnp.zeros((LANES - len(rows), t), F32)
    return jnp.concatenate(rows + [pad], axis=0).T


def _mixer_kernel(cfg, n_state, *refs):
    (x_ref, mod_ref, n1g_ref, n2g_ref, win_ref, qg_ref, kg_ref, hsum_ref, bias_ref, wao_ref,
     cw_ref, cb_ref, lng_ref, lnb_ref, wco_ref, bco_ref, bg_ref, wout_ref,
     wrh_ref, wrl_ref, rb_ref) = refs[:21]
    refs = refs[21:]
    if cfg.has_cache:
        kc_ref, vc_ref, cs_ref = refs[:3]
        refs = refs[3:]
    if cfg.sparse:
        tri_ref = refs[0]
        refs = refs[1:]
    refs = refs[n_state:]
    x1_ref, h2_ref, tok_ref, kout_ref, vout_ref, cout_ref = refs[:6]
    refs = refs[6:]
    if cfg.sparse:
        eid_ref, rank_ref, cnt_ref = refs[:3]
        refs = refs[3:]
    qbuf, kbuf, vbuf, obuf, uext, cvbuf = refs

    tt = cfg.tt
    t = pl.program_id(1)

    if cfg.has_cache:
        kbuf[0:PREV, :] = kc_ref[0].astype(BF16)
        vbuf[0:PREV, :] = vc_ref[0].astype(BF16)
        uext[CONV_PAD - (CONV_WIDTH - 1):CONV_PAD, :] = cs_ref[0]
    else:
        @pl.when(t == 0)
        def _():
            kbuf[0:PREV, :] = jnp.zeros((PREV, ATTN_DIM), BF16)
            vbuf[0:PREV, :] = jnp.zeros((PREV, ATTN_DIM), BF16)
            uext[0:CONV_PAD, :] = jnp.zeros((CONV_PAD, uext.shape[1]), F32)

    mod = mod_ref[0]
    sh1, sc1, g1, sh2, sc2 = (mod[i:i + 1, :] for i in range(5))

    x = x_ref[0]
    hb = (_rms(x, n1g_ref[...]) * (1.0 + sc1) + sh1).astype(BF16)

    a0, a1, a2, a3, a4, a5 = (0, ATTN_DIM, 2 * ATTN_DIM, 3 * ATTN_DIM,
                              3 * ATTN_DIM + cw_ref.shape[1], 3 * ATTN_DIM + 2 * cw_ref.shape[1])
    d_model = x.shape[1]

    q = _head_rms(_dot(hb, win_ref[:, a0:a1]), hsum_ref[...], qg_ref[...])
    qbuf[...] = (q * (HEAD_DIM ** -0.5 * LOG2_E)).astype(BF16)
    k = _head_rms(_dot(hb, win_ref[:, a1:a2]), hsum_ref[...], kg_ref[...])
    kout_ref[0] = k
    kbuf[PREV:PREV + tt, :] = k.astype(BF16)
    v = _dot(hb, win_ref[:, a2:a3])
    vout_ref[0] = v
    vbuf[PREV:PREV + tt, :] = v.astype(BF16)

    u = _dot(hb, win_ref[:, a3:a4]) * _sigmoid_t(_dot(hb, win_ref[:, a4:a5]))
    uext[CONV_PAD:CONV_PAD + tt, :] = u
    cout_ref[0] = uext[CONV_PAD + tt - (CONV_WIDTH - 1):CONV_PAD + tt, :]

    rc = min(tt, 64)
    cc = min(u.shape[1], LANES)
    shifts = [j + CONV_PAD - (CONV_WIDTH - 1) for j in range(CONV_WIDTH)]

    def conv_unit(r0, c0):
        acc = jnp.broadcast_to(cb_ref[:, c0:c0 + cc], (rc, cc))
        for res in range(SUBLANES):
            group = [s for s in shifts if s % SUBLANES == res]
            if not group:
                continue
            lo, hi = min(group) - res, max(group) - res
            rows = hi - lo + rc + (SUBLANES if res else 0)
            slab = uext[pl.ds(r0 + lo, rows), c0:c0 + cc]
            if res:
                slab = pltpu.roll(slab, rows - res, 0)
            for s in group:
                j = s - shifts[0]
                a = s - res - lo
                acc = acc + cw_ref[j:j + 1, c0:c0 + cc] * slab[a:a + rc, :]
        cvbuf[pl.ds(r0, rc), c0:c0 + cc] = acc

    units = [(r0, c0) for r0 in range(0, tt, rc) for c0 in range(0, u.shape[1], cc)]
    gw = 256
    n_chunks = 2 * d_model // gw
    gates = []
    for ci in range(n_chunks):
        cols = slice(ci * gw, (ci + 1) * gw)
        gates.append(_sigmoid_t(_dot(hb, win_ref[:, a5 + ci * gw:a5 + (ci + 1) * gw]) + bg_ref[:, cols]))
        for unit in units[ci::n_chunks]:
            conv_unit(*unit)
    g_attn = jnp.concatenate(gates[:n_chunks // 2], axis=1)
    g_conv = jnp.concatenate(gates[n_chunks // 2:], axis=1)

    def attend(r0, k0):
        if not cfg.has_cache:
            col = lax.broadcasted_iota(I32, (1, cfg.kb), 1)
            valid = jnp.logical_or(col >= PREV - k0, t > 0)
        heads = []
        for h in range(N_HEADS):
            c0, c1 = h * HEAD_DIM, (h + 1) * HEAD_DIM
            s = lax.dot_general(qbuf[pl.ds(r0, cfg.qb), c0:c1], kbuf[pl.ds(k0, cfg.kb), c0:c1],
                                (((1,), (1,)), ((), ())), preferred_element_type=F32)
            s = s + bias_ref[h]
            if not cfg.has_cache:
                s = jnp.where(valid, s, NEG)
            p = jnp.exp2(s - jnp.max(s, axis=-1, keepdims=True))
            l = jnp.sum(p, axis=-1, keepdims=True)
            heads.append(_dot(p.astype(BF16), vbuf[pl.ds(k0, cfg.kb), c0:c1]) * (1.0 / l))
        obuf[pl.ds(r0, cfg.qb), :] = jnp.concatenate(heads, axis=1).astype(BF16)

    for j in range(cfg.n_qblk):
        attend(j * cfg.qb, j * cfg.kstride)
    y_attn = _dot(obuf[...], wao_ref[...])
    if not cfg.has_cache:
        kbuf[0:PREV, :] = kbuf[tt:tt + PREV, :]
        vbuf[0:PREV, :] = vbuf[tt:tt + PREV, :]

    if not cfg.has_cache:
        uext[0:CONV_PAD, :] = uext[tt:tt + CONV_PAD, :]
    cv = cvbuf[...]
    mu = jnp.mean(cv, axis=-1, keepdims=True)
    xc = cv - mu
    var = jnp.mean(xc * xc, axis=-1, keepdims=True)
    cv = xc * lax.rsqrt(var + EPS) * lng_ref[...] + lnb_ref[...]
    cv = cv * _sigmoid_t(cv)
    y_conv = _dot(cv.astype(BF16), wco_ref[...]) + bco_ref[...]

    m = g_attn * y_attn + g_conv * y_conv
    x1 = x + g1 * _dot(m.astype(BF16), wout_ref[...])
    x1_ref[0] = x1

    h2 = _rms(x1, n2g_ref[...]) * (1.0 + sc2) + sh2
    h2_ref[0] = _pack(h2[:, 0:d_model // 2], h2[:, d_model // 2:d_model])
    h2_hi, h2_lo = _split_bf16(h2)
    nt_dims = (((1,), (1,)), ((), ()))
    logits = (lax.dot_general(wrh_ref[...], h2_hi, nt_dims, preferred_element_type=F32)
              + lax.dot_general(wrl_ref[...], h2_hi, nt_dims, preferred_element_type=F32)
              + lax.dot_general(wrh_ref[...], h2_lo, nt_dims, preferred_element_type=F32))
    s = _sigmoid(logits)
    eids, weights = _route(s + rb_ref[...], s)
    ids = _expert_ids(tt)
    if cfg.sparse:
        sel = [functools.reduce(jnp.add, [jnp.where(ids[g] == e, 1.0, 0.0) for e in eids]) for g in range(N_GROUPS)]
        sel = jnp.concatenate(sel, axis=0)
        rank = _dot(sel.astype(BF16), tri_ref[...])
        rank_slabs = [rank[g * GROUP_SIZE:(g + 1) * GROUP_SIZE, :] for g in range(N_GROUPS)]
        tok_ref[0] = _lane_dense_rows(weights, tt)
        eid_ref[...] = jnp.concatenate(eids, axis=0)
        rank_ref[...] = jnp.concatenate([_pick(ids, e, rank_slabs) for e in eids], axis=0)
        step = pl.program_id(0) * pl.num_programs(1) + t

        @pl.when(step == 0)
        def _():
            cnt_ref[...] = jnp.zeros(cnt_ref.shape, F32)

        lane = lax.broadcasted_iota(I32, cnt_ref.shape, 1)
        cnt_ref[...] = jnp.where(lane == step, jnp.sum(sel, axis=1, keepdims=True), cnt_ref[...])
    else:
        gates = [functools.reduce(jnp.add, [jnp.where(ids[g] == e, w, 0.0) for e, w in zip(eids, weights)])
                 for g in range(N_GROUPS)]
        gates = jnp.concatenate(gates + [jnp.zeros((LANES - N_EXPERTS, tt), F32)], axis=0)
        tok_ref[0] = gates.T


def _toeplitz(v, rows, cols):
    w = rows + cols
    flat = jnp.tile(v, (1, rows))[:, :rows * (w - 1)]
    return flat.reshape(v.shape[0], rows, w - 1)[:, :, :cols]


def _rel_bias_blocks(table, cfg, q_pos0, k_pos0):
    q_pos = q_pos0 + jnp.arange(cfg.qb)
    k_pos = k_pos0 + jnp.arange(cfg.kb)
    qc = q_pos // CHUNK
    kc = k_pos // CHUNK
    allowed = (kc[None, :] <= qc[:, None]) & (kc[None, :] >= qc[:, None] - BAND_CHUNKS)
    w = cfg.qb + cfg.kb
    dj = jnp.arange(w)
    dj = jnp.where(dj < cfg.kb, dj, dj - w)
    rel = jnp.clip(q_pos0 - k_pos0 - dj, -MAX_REL, MAX_REL) + MAX_REL
    bias = _toeplitz(table[:, rel].astype(F32), cfg.qb, cfg.kb)
    return jnp.where(allowed[None], bias * LOG2_E, NEG)


def _mixer_cost(cfg, tokens, d, n_in, conv_dim):
    per_token = (2 * d * n_in + 2 * (ATTN_DIM + conv_dim + d) * d + 4 * ATTN_DIM * cfg.kb
                 + 4 * ATTN_DIM * ATTN_DIM + 6 * N_EXPERTS * d + 2 * CONV_WIDTH * conv_dim)
    return pl.CostEstimate(flops=tokens * per_token,
                           transcendentals=tokens * (N_HEADS * cfg.kb + 2 * d + 2 * conv_dim + N_EXPERTS),
                           bytes_accessed=tokens * (4 * d + 4 * d + 2 * d + 4 * LANES) + 2 * d * n_in)


def _mixer_call(cfg, x, mod, p, cache=None, batch0=0, nbatch=None, state=None):
    b_all, t_total, d = x.shape
    nb = b_all if nbatch is None else nbatch
    nt = t_total // cfg.tt
    conv_dim = p["conv_w"].shape[1]
    n_in = p["w_in"].shape[1]

    def const(shape):
        return pl.BlockSpec(shape, lambda i, j: (0,) * len(shape), pipeline_mode=pl.Buffered(1))

    def per_row(shape):
        return pl.BlockSpec(shape, lambda i, j: (i + batch0, 0, 0))

    in_specs = [
        pl.BlockSpec((1, cfg.tt, d), lambda i, j: (i + batch0, j, 0)),
        per_row((1, 6, d)),
        const((1, d)), const((1, d)), const((d, n_in)),
        const((1, ATTN_DIM)), const((1, ATTN_DIM)), const((ATTN_DIM, ATTN_DIM)),
        const((N_HEADS, cfg.qb, cfg.kb)), const((ATTN_DIM, d)),
        const((CONV_WIDTH, conv_dim)), const((1, conv_dim)), const((1, conv_dim)), const((1, conv_dim)),
        const((conv_dim, d)), const((1, d)), const((1, 2 * d)), const((d, d)),
        const((N_EXPERTS, d)), const((N_EXPERTS, d)), const((N_EXPERTS, 1)),
    ]
    args = [x, mod, p["norm1_g"], p["norm2_g"], p["w_in"], p["q_norm_g"], p["k_norm_g"], p["hsum"],
            p["bias"], p["w_attn_o"], p["conv_w"], p["conv_b"], p["conv_ln_g"], p["conv_ln_b"],
            p["w_conv_o"], p["b_conv_o"], p["b_gate"], p["w_out"], p["wr_hi"], p["wr_lo"], p["router_bias"]]
    if cfg.has_cache:
        in_specs += [per_row((1, PREV, ATTN_DIM)), per_row((1, PREV, ATTN_DIM)),
                     per_row((1, CONV_WIDTH - 1, conv_dim))]
        args += list(cache)
    if cfg.sparse:
        assert nb * nt <= MAX_TILES
        in_specs += [const((cfg.tt, cfg.tt))]
        tok = jnp.arange(cfg.tt)
        args += [(tok[:, None] < tok[None, :]).astype(BF16)]
    aliases = {}
    if state is not None:
        for i, arr in enumerate(state):
            aliases[len(args)] = 3 + i
            in_specs.append(pl.BlockSpec(memory_space=pl.ANY))
            args.append(arr)
    keep = min(PREV, t_total)
    assert keep == cfg.tt
    out_shape = [jax.ShapeDtypeStruct((nb, t_total, d), F32),
                 jax.ShapeDtypeStruct((nb, t_total, d // 2), I32),
                 jax.ShapeDtypeStruct((nb, t_total, LANES), F32),
                 jax.ShapeDtypeStruct((b_all, keep, ATTN_DIM), F32),
                 jax.ShapeDtypeStruct((b_all, keep, ATTN_DIM), F32),
                 jax.ShapeDtypeStruct((b_all, CONV_WIDTH - 1, conv_dim), F32)]
    out_specs = [pl.BlockSpec((1, cfg.tt, d), lambda i, j: (i, j, 0)),
                 pl.BlockSpec((1, cfg.tt, d // 2), lambda i, j: (i, j, 0)),
                 pl.BlockSpec((1, cfg.tt, LANES), lambda i, j: (i, j, 0)),
                 per_row((1, keep, ATTN_DIM)), per_row((1, keep, ATTN_DIM)),
                 per_row((1, CONV_WIDTH - 1, conv_dim))]
    if cfg.sparse:
        out_shape += [jax.ShapeDtypeStruct((TOP_K, nb * t_total), F32),
                      jax.ShapeDtypeStruct((TOP_K, nb * t_total), F32),
                      jax.ShapeDtypeStruct((N_EXPERTS, MAX_TILES), F32)]
        out_specs += [pl.BlockSpec((TOP_K, cfg.tt), lambda i, j: (0, i * nt + j)),
                      pl.BlockSpec((TOP_K, cfg.tt), lambda i, j: (0, i * nt + j)),
                      pl.BlockSpec((N_EXPERTS, MAX_TILES), lambda i, j: (0, 0))]
    scratch = [pltpu.VMEM((cfg.tt, ATTN_DIM), BF16),
               pltpu.VMEM((PREV + cfg.tt, ATTN_DIM), BF16),
               pltpu.VMEM((PREV + cfg.tt, ATTN_DIM), BF16),
               pltpu.VMEM((cfg.tt, ATTN_DIM), BF16),
               pltpu.VMEM((CONV_PAD + cfg.tt, conv_dim), F32),
               pltpu.VMEM((cfg.tt, conv_dim), F32)]
    return pl.pallas_call(
        functools.partial(_mixer_kernel, cfg, 0 if state is None else len(state)),
        out_shape=out_shape,
        grid=(nb, nt),
        in_specs=in_specs,
        out_specs=out_specs,
        scratch_shapes=scratch,
        input_output_aliases=aliases,
        cost_estimate=_mixer_cost(cfg, nb * t_total, d, n_in, conv_dim),
        compiler_params=pltpu.CompilerParams(dimension_semantics=("arbitrary", "arbitrary"),
                                             vmem_limit_bytes=VMEM_LIMIT_BYTES),
        name="mixer_sample" if cfg.has_cache else "mixer_prompt",
    )(*args)


def _exact_parts(x, n):
    parts = []
    for _ in range(n):
        part = x.astype(BF16)
        parts.append(part)
        x = x - part.astype(F32)
    return parts


def _plan_kernel(tt, cnt_ref, eid_ref, rank_ref, tri_e_ref, tri_t_ref, pos_ref, blk_ref, nblk_ref, base_ref):
    step = pl.program_id(0)

    @pl.when(step == 0)
    def _():
        cnt = cnt_ref[...]
        total = jnp.sum(cnt, axis=1, keepdims=True)
        padded = jnp.floor((total + float(ROW_BLOCK - 1)) * (1.0 / ROW_BLOCK)) * float(ROW_BLOCK)
        padded = jnp.broadcast_to(padded, cnt.shape)
        start = functools.reduce(jnp.add, [_dot(tri_e_ref[...], part) for part in _exact_parts(padded, 3)])
        before = functools.reduce(jnp.add, [_dot(part, tri_t_ref[...]) for part in _exact_parts(cnt, 2)])
        base_ref[...] = start + before
        end = (start + padded)[:, 0:1]
        first_row = lax.broadcasted_iota(I32, blk_ref.shape, 1).astype(F32) * float(ROW_BLOCK)
        owner = jnp.sum(jnp.where(end <= first_row, 1.0, 0.0), axis=0, keepdims=True)
        blk_ref[...] = jnp.minimum(owner, float(N_EXPERTS - 1)).astype(I32)
        nblk = jnp.max(end, axis=0, keepdims=True) * (1.0 / ROW_BLOCK)
        nblk_ref[...] = jnp.broadcast_to(nblk, nblk_ref.shape).astype(I32)

    lane = lax.broadcasted_iota(I32, base_ref.shape, 1)
    col = jnp.sum(jnp.where(lane == step, base_ref[...], 0.0), axis=1, keepdims=True)
    ids = _expert_ids(tt)
    col_slabs = [jnp.broadcast_to(col[g * GROUP_SIZE:(g + 1) * GROUP_SIZE, :], (GROUP_SIZE, tt))
                 for g in range(N_GROUPS)]
    eid = eid_ref[...]
    rows = [_pick(ids, eid[k:k + 1, :], col_slabs) for k in range(TOP_K)]
    pos_ref[...] = (jnp.concatenate(rows, axis=0) + rank_ref[...]).astype(I32)


def _plan_call(cnt, eid, rank, tt, n_blocks):
    n = eid.shape[1]
    e = jnp.arange(N_EXPERTS)
    s = jnp.arange(MAX_TILES)
    nb_pad = -(-n_blocks // LANES) * LANES
    return pl.pallas_call(
        functools.partial(_plan_kernel, tt),
        out_shape=[jax.ShapeDtypeStruct((TOP_K, n), I32),
                   jax.ShapeDtypeStruct((1, nb_pad), I32),
                   jax.ShapeDtypeStruct((1, LANES), I32)],
        grid=(n // tt,),
        in_specs=[pl.BlockSpec((N_EXPERTS, MAX_TILES), lambda i: (0, 0)),
                  pl.BlockSpec((TOP_K, tt), lambda i: (0, i)),
                  pl.BlockSpec((TOP_K, tt), lambda i: (0, i)),
                  pl.BlockSpec((N_EXPERTS, N_EXPERTS), lambda i: (0, 0)),
                  pl.BlockSpec((MAX_TILES, MAX_TILES), lambda i: (0, 0))],
        out_specs=[pl.BlockSpec((TOP_K, tt), lambda i: (0, i)),
                   pl.BlockSpec((1, nb_pad), lambda i: (0, 0)),
                   pl.BlockSpec((1, LANES), lambda i: (0, 0))],
        scratch_shapes=[pltpu.VMEM((N_EXPERTS, MAX_TILES), F32)],
        compiler_params=pltpu.CompilerParams(dimension_semantics=("arbitrary",),
                                             vmem_limit_bytes=VMEM_LIMIT_BYTES),
        name="plan",
    )(cnt, eid, rank, (e[None, :] < e[:, None]).astype(BF16), (s[:, None] < s[None, :]).astype(BF16))


def _sc_mesh():
    return plsc.VectorSubcoreMesh(core_axis_name="c", subcore_axis_name="s")


def _dispatch_call(rows, pos, n_sorted):
    n, w = rows.shape
    per_worker = n // SC_WORKERS
    nsteps = per_worker // SC_ROWS
    assert per_worker * SC_WORKERS == n and nsteps * SC_ROWS == per_worker and nsteps % 2 == 0

    @functools.partial(
        pl.kernel, mesh=_sc_mesh(),
        out_type=jax.ShapeDtypeStruct((n_sorted, w), rows.dtype),
        scratch_types=[pltpu.VMEM((2, TOP_K, SC_ROWS), I32),
                       pltpu.VMEM((2, SC_ROWS, w), rows.dtype),
                       pltpu.SemaphoreType.DMA((2,)),
                       pltpu.SemaphoreType.DMA((2,))],
        cost_estimate=pl.CostEstimate(flops=0, transcendentals=0,
                                      bytes_accessed=(1 + TOP_K) * n * w * rows.dtype.itemsize + 4 * TOP_K * n),
    )
    def dispatch(x_hbm, pos_hbm, out_hbm, idx_v, rows_v, load_sem, scat_sem):
        base = (lax.axis_index("s") * SC_CORES + lax.axis_index("c")) * per_worker

        def load(i, b):
            return pltpu.make_async_copy(x_hbm.at[pl.ds(base + i * SC_ROWS, SC_ROWS)], rows_v.at[b], load_sem.at[b])

        def scatter(b, k):
            return pltpu.make_async_copy(rows_v.at[b], out_hbm.at[idx_v.at[b, k]], scat_sem.at[b])

        def load_start(i, b):
            for k in range(TOP_K):
                pltpu.sync_copy(pos_hbm.at[pl.ds(k * n + base + i * SC_ROWS, SC_ROWS)], idx_v.at[b, k])
            load(i, b).start()

        load_start(0, 0)

        @pl.loop(0, nsteps, step=2)
        def _(i):
            for b in range(2):
                ii = i + b

                @pl.when(ii >= 1)
                def _():
                    for k in range(TOP_K):
                        scatter(1 - b, k).wait()

                @pl.when(ii + 1 < nsteps)
                def _():
                    load_start(ii + 1, 1 - b)

                load(ii, b).wait()
                for k in range(TOP_K):
                    scatter(b, k).start()

        for k in range(TOP_K):
            scatter((nsteps - 1) % 2, k).wait()

    return dispatch(rows, pos)


def _collect_call(table, idx):
    n = idx.shape[0]
    w = table.shape[1]
    per_worker = n // SC_WORKERS
    nsteps = per_worker // SC_ROWS
    assert per_worker * SC_WORKERS == n and nsteps * SC_ROWS == per_worker and nsteps % 2 == 0

    @functools.partial(
        pl.kernel, mesh=_sc_mesh(),
        out_type=jax.ShapeDtypeStruct((n, w), table.dtype),
        scratch_types=[pltpu.VMEM((2, SC_ROWS), I32),
                       pltpu.VMEM((2, SC_ROWS, w), table.dtype),
                       pltpu.SemaphoreType.DMA((2,)),
                       pltpu.SemaphoreType.DMA((2,))],
        cost_estimate=pl.CostEstimate(flops=0, transcendentals=0,
                                      bytes_accessed=2 * n * w * table.dtype.itemsize + 4 * n),
    )
    def collect(table_hbm, idx_hbm, out_hbm, idx_v, rows_v, gather_sem, write_sem):
        base = (lax.axis_index("s") * SC_CORES + lax.axis_index("c")) * per_worker

        def gather(b):
            return pltpu.make_async_copy(table_hbm.at[idx_v.at[b]], rows_v.at[b], gather_sem.at[b])

        def write(i, b):
            return pltpu.make_async_copy(rows_v.at[b], out_hbm.at[pl.ds(base + i * SC_ROWS, SC_ROWS)], write_sem.at[b])

        def gather_start(i, b):
            pltpu.sync_copy(idx_hbm.at[pl.ds(base + i * SC_ROWS, SC_ROWS)], idx_v.at[b])
            gather(b).start()

        gather_start(0, 0)

        @pl.loop(0, nsteps, step=2)
        def _(i):
            for b in range(2):
                ii = i + b

                @pl.when(ii >= 1)
                def _():
                    write(ii - 1, 1 - b).wait()

                @pl.when(ii + 1 < nsteps)
                def _():
                    gather_start(ii + 1, 1 - b)

                gather(b).wait()
                write(ii, b).start()

        write(nsteps - 1, (nsteps - 1) % 2).wait()

    return collect(table, idx)


def _expert_ffn_kernel(blk_ref, nblk_ref, x_ref, *refs):
    w_refs, o_ref, w_bf16 = refs[:3 * FFN_BLOCKS], refs[3 * FFN_BLOCKS], refs[3 * FFN_BLOCKS + 1:]
    j = pl.program_id(0)
    for h in range(FFN_BLOCKS):
        wg_ref, wu_ref, wd_ref = w_refs[3 * h:3 * h + 3]
        wg_s, wu_s, wd_s = w_bf16[3 * h:3 * h + 3]
        block = FFN_BLOCKS * j + h

        @pl.when(jnp.logical_or(j == 0, blk_ref[block] != blk_ref[jnp.maximum(block - FFN_BLOCKS, 0)]))
        def _():
            wg_s[...] = wg_ref[0].astype(BF16)
            wu_s[...] = wu_ref[0].astype(BF16)
            wd_s[...] = wd_ref[0].astype(BF16)

        @pl.when(block < nblk_ref[0])
        def _():
            rows = pl.ds(h * ROW_BLOCK, ROW_BLOCK)
            a, b = _unpack(x_ref[rows, :])
            a, b = a.astype(BF16), b.astype(BF16)
            hg = _dot_halves(a, b, wg_s)
            act = hg * _sigmoid_t(hg) * _dot_halves(a, b, wu_s)
            y = _dot(act.astype(BF16), wd_s[...])
            half = y.shape[1] // 2
            o_ref[rows, :] = _pack(y[:, 0:half], y[:, half:2 * half])


def _expert_ffn_call(xs, blk, nblk, p):
    n_sorted, w = xs.shape
    d = 2 * w
    ff = p["w_e_gate"].shape[2]
    step_rows = FFN_BLOCKS * ROW_BLOCK

    def rows(j, blk, nblk):
        return (jnp.minimum(j, (nblk[0] - 1) // FFN_BLOCKS), 0)

    def expert(h):
        return lambda j, blk, nblk: (blk[FFN_BLOCKS * j + h], 0, 0)

    w_specs, w_args, w_scratch = [], [], []
    for h in range(FFN_BLOCKS):
        w_specs += [pl.BlockSpec((1, d, ff), expert(h)), pl.BlockSpec((1, d, ff), expert(h)),
                    pl.BlockSpec((1, ff, d), expert(h))]
        w_args += [p["w_e_gate"], p["w_e_up"], p["w_e_down"]]
        w_scratch += [pltpu.VMEM((d, ff), BF16), pltpu.VMEM((d, ff), BF16), pltpu.VMEM((ff, d), BF16)]
    return pl.pallas_call(
        _expert_ffn_kernel,
        out_shape=jax.ShapeDtypeStruct((n_sorted, w), I32),
        grid_spec=pltpu.PrefetchScalarGridSpec(
            num_scalar_prefetch=2,
            grid=(n_sorted // step_rows,),
            in_specs=[pl.BlockSpec((step_rows, w), rows)] + w_specs,
            out_specs=pl.BlockSpec((step_rows, w), rows),
            scratch_shapes=w_scratch),
        cost_estimate=pl.CostEstimate(flops=6 * n_sorted * d * ff, transcendentals=n_sorted * ff,
                                      bytes_accessed=8 * n_sorted * w + 12 * N_EXPERTS * d * ff),
        compiler_params=pltpu.CompilerParams(dimension_semantics=("arbitrary",),
                                             vmem_limit_bytes=VMEM_LIMIT_BYTES),
        name="expert_ffn",
    )(blk, nblk, xs, *w_args)


def _swiglu_halves(a, b, wg_ref, wu_ref):
    hg = _dot_halves(a, b, wg_ref)
    return hg * _sigmoid_t(hg) * _dot_halves(a, b, wu_ref)


def _moe_out_kernel(n_groups, starts, *refs):
    groups = [refs[4 * q:4 * q + 4] for q in range(n_groups)]
    g2_ref, wsg_ref, wsu_ref, wsd_ref, o_ref = refs[4 * n_groups:]

    def combine(x1_ref, h_ref, y_ref, w_ref):
        a, b = _unpack(h_ref[...])
        shared = _dot(_swiglu_halves(a.astype(BF16), b.astype(BF16), wsg_ref, wsu_ref).astype(BF16),
                      wsd_ref[...].astype(BF16))
        half = h_ref.shape[1]
        w = w_ref[...]
        acc_a, acc_b = shared[:, 0:half], shared[:, half:2 * half]
        for k in range(TOP_K):
            ya, yb = _unpack(y_ref[k])
            acc_a = acc_a + w[:, k:k + 1] * ya
            acc_b = acc_b + w[:, k:k + 1] * yb
        g2 = g2_ref[...]
        o_ref[:, 0:half] = x1_ref[:, 0:half] + g2[:, 0:half] * acc_a
        o_ref[:, half:2 * half] = x1_ref[:, half:2 * half] + g2[:, half:2 * half] * acc_b

    i = pl.program_id(0)
    for q in range(n_groups):
        pl.when(jnp.logical_and(i >= starts[q], i < starts[q + 1]))(functools.partial(combine, *groups[q]))


def _moe_out_call(groups, g2, p, tm):
    d = groups[0][0].shape[1]
    starts = [0]
    for group in groups:
        starts.append(starts[-1] + group[0].shape[0] // tm)
    n_all = starts[-1] * tm
    sff = p["w_s_gate"].shape[1]
    per_row = n_all // g2.shape[0]
    g2 = g2.reshape(g2.shape[0], 1, d)

    def const(shape):
        return pl.BlockSpec(shape, lambda i: (0,) * len(shape), pipeline_mode=pl.Buffered(1))

    in_specs, args = [], []
    for q, group in enumerate(groups):
        tile = lambda i, q=q: jnp.clip(i - starts[q], 0, starts[q + 1] - starts[q] - 1)
        in_specs += [pl.BlockSpec((tm, d), lambda i, tile=tile: (tile(i), 0)),
                     pl.BlockSpec((tm, d // 2), lambda i, tile=tile: (tile(i), 0)),
                     pl.BlockSpec((TOP_K, tm, d // 2), lambda i, tile=tile: (0, tile(i), 0)),
                     pl.BlockSpec((tm, LANES), lambda i, tile=tile: (tile(i), 0))]
        args += list(group)
    in_specs += [pl.BlockSpec((None, 1, d), lambda i: (i * tm // per_row, 0, 0)),
                 const((d, sff)), const((d, sff)), const((sff, d))]
    args += [g2, p["w_s_gate"], p["w_s_up"], p["w_s_down"]]
    return pl.pallas_call(
        functools.partial(_moe_out_kernel, len(groups), tuple(starts)),
        out_shape=jax.ShapeDtypeStruct((n_all, d), F32),
        grid=(n_all // tm,),
        in_specs=in_specs,
        out_specs=pl.BlockSpec((tm, d), lambda i: (i, 0)),
        cost_estimate=pl.CostEstimate(flops=n_all * (6 * d * sff + 2 * TOP_K * d), transcendentals=n_all * sff,
                                      bytes_accessed=n_all * (8 * d + 2 * d + 2 * TOP_K * d + 4 * LANES)),
        compiler_params=pltpu.CompilerParams(dimension_semantics=("arbitrary",),
                                             vmem_limit_bytes=VMEM_LIMIT_BYTES),
        name="moe_out",
    )(*args)


def _moe_dense_kernel(h_ref, gates_ref, x1_ref, g2_ref, wsg_ref, wsu_ref, wsd_ref, wg_ref, wu_ref, wd_ref,
                      o_ref, acc_ref):
    e = pl.program_id(1)
    a, b = _unpack(h_ref[...])
    a, b = a.astype(BF16), b.astype(BF16)

    @pl.when(e == 0)
    def _():
        acc_ref[...] = _dot(_swiglu_halves(a, b, wsg_ref, wsu_ref).astype(BF16), wsd_ref[...].astype(BF16))

    gates = gates_ref[...]
    lane = lax.broadcasted_iota(I32, gates.shape, 1)
    gate = jnp.sum(jnp.where(lane == e, gates, 0.0), axis=1, keepdims=True)
    act = _swiglu_halves(a, b, wg_ref.at[0], wu_ref.at[0]) * gate
    acc_ref[...] += _dot(act.astype(BF16), wd_ref[0].astype(BF16))

    @pl.when(e == pl.num_programs(1) - 1)
    def _():
        o_ref[...] = x1_ref[...] + g2_ref[...] * acc_ref[...]


def _moe_dense_call(h2, gates, x1, g2_rows, p):
    n, d = x1.shape
    ff = p["w_e_gate"].shape[2]
    sff = p["w_s_gate"].shape[1]

    def const(shape):
        return pl.BlockSpec(shape, lambda i, e: (0,) * len(shape), pipeline_mode=pl.Buffered(1))

    return pl.pallas_call(
        _moe_dense_kernel,
        out_shape=jax.ShapeDtypeStruct((n, d), F32),
        grid=(1, N_EXPERTS),
        in_specs=[const((n, d // 2)), const((n, LANES)), const((n, d)), const((n, d)),
                  const((d, sff)), const((d, sff)), const((sff, d)),
                  pl.BlockSpec((1, d, ff), lambda i, e: (e, 0, 0)),
                  pl.BlockSpec((1, d, ff), lambda i, e: (e, 0, 0)),
                  pl.BlockSpec((1, ff, d), lambda i, e: (e, 0, 0))],
        out_specs=pl.BlockSpec((n, d), lambda i, e: (0, 0)),
        scratch_shapes=[pltpu.VMEM((n, d), F32)],
        compiler_params=pltpu.CompilerParams(dimension_semantics=("arbitrary", "arbitrary"),
                                             vmem_limit_bytes=VMEM_LIMIT_BYTES),
        name="moe_dense",
    )(h2, gates, x1, g2_rows, p["w_s_gate"], p["w_s_up"], p["w_s_down"], p["w_e_gate"], p["w_e_up"], p["w_e_down"])


PROMPT_CFG = MixerCfg(tt=512, qb=256, kb=768, n_qblk=2, kstride=256, has_cache=False, sparse=True)
MOE_OUT_TILE = 512
PROMPT_SPLIT = (1, 1)


def _layer_params(l, w_in, q_norm_g, k_norm_g, w_attn_o, conv_w, conv_b, conv_ln_g, conv_ln_b, w_conv_o,
                  b_conv_o, b_gate, w_out, norm1_g, norm2_g, w_router, router_bias,
                  w_e_gate, w_e_up, w_e_down, w_s_gate, w_s_up, w_s_down):
    row = lambda a: a[l].reshape(1, -1)
    wr_t = w_router[l].T
    wr_hi = wr_t.astype(BF16)
    head = jnp.arange(ATTN_DIM) // HEAD_DIM
    return {
        "norm1_g": row(norm1_g), "norm2_g": row(norm2_g), "w_in": w_in[l].astype(BF16),
        "q_norm_g": jnp.tile(q_norm_g[l], N_HEADS).reshape(1, -1),
        "k_norm_g": jnp.tile(k_norm_g[l], N_HEADS).reshape(1, -1),
        "hsum": (head[:, None] == head[None, :]).astype(BF16),
        "w_attn_o": w_attn_o[l].astype(BF16), "conv_w": conv_w[l], "conv_b": row(conv_b),
        "conv_ln_g": row(conv_ln_g), "conv_ln_b": row(conv_ln_b), "w_conv_o": w_conv_o[l].astype(BF16),
        "b_conv_o": row(b_conv_o), "b_gate": row(b_gate), "w_out": w_out[l].astype(BF16),
        "wr_hi": wr_hi, "wr_lo": (wr_t - wr_hi.astype(F32)).astype(BF16),
        "router_bias": router_bias[l].reshape(-1, 1),
        "w_e_gate": w_e_gate[l], "w_e_up": w_e_up[l], "w_e_down": w_e_down[l],
        "w_s_gate": w_s_gate[l], "w_s_up": w_s_up[l], "w_s_down": w_s_down[l],
    }


def _sorted_rows(h2, eid, rank, cnt, tt):
    n = h2.shape[0]
    n_sorted = n * TOP_K + N_EXPERTS * ROW_BLOCK
    assert n_sorted % (FFN_BLOCKS * ROW_BLOCK) == 0
    pos, blk, nblk = _plan_call(cnt, eid, rank, tt, n_sorted // ROW_BLOCK)
    pos = pos.reshape(TOP_K * n)
    return _dispatch_call(h2, pos, n_sorted), pos, blk.reshape(-1), nblk.reshape(-1)


def kernel(x_prompt, x_sample, c_prompt, c_sample, cache_k, cache_v, state_conv, w_mod, b_mod, norm1_g, w_in, q_norm_g, k_norm_g, rel_bias, w_attn_o, conv_w, conv_b, conv_ln_g, conv_ln_b, w_conv_o, b_conv_o, b_gate, w_out, norm2_g, w_router, router_bias, w_e_gate, w_e_up, w_e_down, w_s_gate, w_s_up, w_s_down):
    depth = w_mod.shape[0]
    bp, tp, d = x_prompt.shape
    bs, ts, _ = x_sample.shape
    assert cache_k.shape[2] == min(PREV, PAST_LEN) == PREV
    sample_cfg = MixerCfg(tt=ts, qb=ts, kb=PREV + ts, n_qblk=1, kstride=0, has_cache=True, sparse=False)

    yp, ys = x_prompt, x_sample
    outs = [[] for _ in range(6)]
    for l in range(depth):
        p = _layer_params(l, w_in, q_norm_g, k_norm_g, w_attn_o, conv_w, conv_b, conv_ln_g, conv_ln_b,
                          w_conv_o, b_conv_o, b_gate, w_out, norm1_g, norm2_g, w_router, router_bias,
                          w_e_gate, w_e_up, w_e_down, w_s_gate, w_s_up, w_s_down)
        mod = _mod_call(jnp.concatenate([c_prompt, c_sample], axis=0), w_mod[l], b_mod[l])
        mod = mod.reshape(bp + bs, 6, d)
        mod_p, mod_s = mod[:bp], mod[bp:]

        p["bias"] = _rel_bias_blocks(rel_bias[l], PROMPT_CFG, PREV, 0)
        big = max(bp * PROMPT_SPLIT[0] // sum(PROMPT_SPLIT), 1)
        group_rows = [big, bp - big] if bp > big else [bp]
        keep = min(PREV, tp)
        state = [jnp.zeros((bp, keep, ATTN_DIM), F32), jnp.zeros((bp, keep, ATTN_DIM), F32),
                 jnp.zeros((bp, CONV_WIDTH - 1, conv_w.shape[2]), F32)]
        mixed, moved = [], []
        for g, gb in enumerate(group_rows):
            x1, h2, w_tok, *state, eid, rank, cnt = _mixer_call(PROMPT_CFG, yp, mod_p, p, batch0=sum(group_rows[:g]),
                                                                nbatch=gb, state=state)
            x1, h2, w_tok = x1.reshape(gb * tp, d), h2.reshape(gb * tp, d // 2), w_tok.reshape(gb * tp, LANES)
            mixed.append((x1, h2, w_tok))
            moved.append(_sorted_rows(h2, eid, rank, cnt, PROMPT_CFG.tt))
            state, _ = lax.optimization_barrier((state, moved[-1][1]))
        kp, vp, cp = state
        groups, gate = [], cp
        for (x1, h2, w_tok), (xs, pos, blk, nblk) in zip(mixed, moved):
            xs, _ = lax.optimization_barrier((xs, gate))
            gate = _expert_ffn_call(xs, blk, nblk, p)
            yg = _collect_call(gate, pos).reshape(TOP_K, h2.shape[0], h2.shape[1])
            groups.append((x1, h2, yg, w_tok))
        yp = _moe_out_call(groups, mod_p[:, 5, :], p, MOE_OUT_TILE).reshape(bp, tp, d)
        outs[0].append(kp.reshape(bp, -1, N_HEADS, HEAD_DIM))
        outs[1].append(vp.reshape(bp, -1, N_HEADS, HEAD_DIM))
        outs[2].append(cp)

        p["bias"] = _rel_bias_blocks(rel_bias[l], sample_cfg, PAST_LEN, PAST_LEN - PREV)
        cache = (cache_k[l].reshape(bs, PREV, ATTN_DIM), cache_v[l].reshape(bs, PREV, ATTN_DIM), state_conv[l])
        x1, h2, gates, ks, vs, cs = _mixer_call(sample_cfg, ys, mod_s, p, cache)
        ys = _moe_dense_call(h2.reshape(bs * ts, d // 2), gates.reshape(bs * ts, LANES), x1.reshape(bs * ts, d),
                             jnp.repeat(mod_s[:, 5, :], ts, axis=0), p).reshape(bs, ts, d)
        outs[3].append(ks.reshape(bs, ts, N_HEADS, HEAD_DIM))
        outs[4].append(vs.reshape(bs, ts, N_HEADS, HEAD_DIM))
        outs[5].append(cs)
    return (yp, ys) + tuple(jnp.stack(o) for o in outs)
```

```python
import dataclasses
import functools

import jax
import jax.numpy as jnp
from jax import lax
from jax.experimental import pallas as pl
from jax.experimental.pallas import tpu as pltpu
from jax.experimental.pallas import tpu_sc as plsc

F32 = jnp.float32
BF16 = jnp.bfloat16
I32 = jnp.int32

CHUNK = 64
BAND_CHUNKS = 8
PREV = BAND_CHUNKS * CHUNK
PAST_LEN = 1024
N_HEADS = 8
HEAD_DIM = 64
ATTN_DIM = N_HEADS * HEAD_DIM
MAX_REL = 128
CONV_WIDTH = 31
CONV_PAD = 32
N_EXPERTS = 64
N_GROUPS = 8
GROUP_SIZE = N_EXPERTS // N_GROUPS
TOPK_GROUPS = 4
TOP_K = 8
ROUTE_SCALE = 2.5
EPS = 1e-6
LANES = 128
SUBLANES = 8
NEG = -1e30
LOG2_E = 1.4426950408889634

VMEM_LIMIT_BYTES = 60 * 1024 * 1024

SC_CORES = 2
SC_SUBCORES = 16
SC_WORKERS = SC_CORES * SC_SUBCORES
SC_ROWS = 64

MAX_TILES = LANES
ROW_BLOCK = 1024
FFN_ROW_BUFFERS = 3


def _sigmoid(x):
    return 1.0 / (1.0 + jnp.exp(-x))


def _sigmoid_t(x):
    return 0.5 * jnp.tanh(0.5 * x) + 0.5


def _split_bf16(x):
    hi = x.astype(BF16)
    lo = (x - hi.astype(F32)).astype(BF16)
    return hi, lo


def _dot(a, b):
    return jnp.dot(a, b, preferred_element_type=F32)


def _dot3(a, b):
    a_hi, a_lo = _split_bf16(a)
    b_hi, b_lo = _split_bf16(b)
    return _dot(a_hi, b_hi) + _dot(a_hi, b_lo) + _dot(a_lo, b_hi)


def _pack(a, b):
    ia = lax.bitcast_convert_type(a.astype(BF16).astype(F32), I32)
    ib = lax.bitcast_convert_type(b.astype(BF16).astype(F32), I32)
    return ia | lax.shift_right_logical(ib, 16)


def _unpack(p):
    a = lax.bitcast_convert_type(p & jnp.int32(-65536), F32)
    b = lax.bitcast_convert_type(lax.shift_left(p, 16), F32)
    return a, b


def _dot_halves(a, b, w_ref):
    half = a.shape[1]
    return _dot(a, w_ref[0:half, :].astype(BF16)) + _dot(b, w_ref[half:2 * half, :].astype(BF16))


def _mod_kernel(c_ref, w_ref, b_ref, o_ref):
    c = c_ref[...]
    o_ref[...] = _dot3(c * _sigmoid(c), w_ref[...]) + b_ref[...]


def _mod_call(c, w_mod, b_mod):
    n, d = c.shape
    dout = w_mod.shape[1]
    bn = 1536
    return pl.pallas_call(
        _mod_kernel,
        out_shape=jax.ShapeDtypeStruct((n, dout), F32),
        grid=(dout // bn,),
        in_specs=[pl.BlockSpec((n, d), lambda j: (0, 0)),
                  pl.BlockSpec((d, bn), lambda j: (0, j)),
                  pl.BlockSpec((1, bn), lambda j: (0, j))],
        out_specs=pl.BlockSpec((n, bn), lambda j: (0, j)),
        compiler_params=pltpu.CompilerParams(dimension_semantics=("arbitrary",),
                                             vmem_limit_bytes=VMEM_LIMIT_BYTES),
        name="mod",
    )(c, w_mod, b_mod.reshape(1, dout))


@dataclasses.dataclass(frozen=True)
class MixerCfg:
    tt: int
    qb: int
    kb: int
    n_qblk: int
    kstride: int
    has_cache: bool
    sparse: bool


def _rms(x, g):
    ms = jnp.mean(x * x, axis=-1, keepdims=True)
    return x * lax.rsqrt(ms + EPS) * g


def _head_rms(z, hsum, g):
    ss = _dot((z * z).astype(BF16), hsum)
    return z * lax.rsqrt(ss * (1.0 / HEAD_DIM) + EPS) * g


def _expert_ids(t):
    sub = lax.broadcasted_iota(I32, (GROUP_SIZE, t), 0).astype(F32)
    return [sub + float(g * GROUP_SIZE) for g in range(N_GROUPS)]


def _pick(ids, eid, slabs):
    acc = functools.reduce(jnp.add, [jnp.where(ids[g] == eid, slabs[g], 0.0) for g in range(N_GROUPS)])
    return jnp.sum(acc, axis=0, keepdims=True)


def _route(choice, s):
    t = choice.shape[1]
    sub = lax.broadcasted_iota(I32, (GROUP_SIZE, t), 0).astype(F32)
    slabs = [choice[g * GROUP_SIZE:(g + 1) * GROUP_SIZE, :] for g in range(N_GROUPS)]
    s_slabs = [s[g * GROUP_SIZE:(g + 1) * GROUP_SIZE, :] for g in range(N_GROUPS)]
    gscore = []
    for c in slabs:
        m1 = jnp.max(c, axis=0, keepdims=True)
        first = jnp.min(jnp.where(c == m1, sub, float(GROUP_SIZE)), axis=0, keepdims=True)
        m2 = jnp.max(jnp.where(sub == first, -jnp.inf, c), axis=0, keepdims=True)
        gscore.append(m1 + m2)
    gsel = [jnp.zeros((1, t), F32) for _ in range(N_GROUPS)]
    for _ in range(TOPK_GROUPS):
        m = functools.reduce(jnp.maximum, gscore)
        first = functools.reduce(jnp.minimum,
                                 [jnp.where(gscore[g] == m, float(g), float(N_GROUPS)) for g in range(N_GROUPS)])
        for g in range(N_GROUPS):
            hit = first == float(g)
            gsel[g] = jnp.where(hit, 1.0, gsel[g])
            gscore[g] = jnp.where(hit, -jnp.inf, gscore[g])
    masked = [jnp.where(gsel[g] > 0.5, slabs[g], -jnp.inf) for g in range(N_GROUPS)]
    ids = _expert_ids(t)
    eids, raw = [], []
    for _ in range(TOP_K):
        m = jnp.max(functools.reduce(jnp.maximum, masked), axis=0, keepdims=True)
        cand = [jnp.where(masked[g] == m, ids[g], float(N_EXPERTS)) for g in range(N_GROUPS)]
        first = jnp.min(functools.reduce(jnp.minimum, cand), axis=0, keepdims=True)
        eids.append(first)
        raw.append(_pick(ids, first, s_slabs))
        masked = [jnp.where(ids[g] == first, -jnp.inf, masked[g]) for g in range(N_GROUPS)]
    wsum = functools.reduce(jnp.add, raw)
    return eids, [r / wsum * ROUTE_SCALE for r in raw]


def _lane_dense_rows(rows, t):
    pad = jnp.zeros((LANES - len(rows), t), F32)
    return jnp.concatenate(rows + [pad], axis=0).T


def _mixer_kernel(cfg, n_state, *refs):
    (x_ref, mod_ref, n1g_ref, n2g_ref, win_ref, qg_ref, kg_ref, hsum_ref, bias_ref, wao_ref,
     cw_ref, cb_ref, lng_ref, lnb_ref, wco_ref, bco_ref, bg_ref, wout_ref,
     wrh_ref, wrl_ref, rb_ref) = refs[:21]
    refs = refs[21:]
    if cfg.has_cache:
        kc_ref, vc_ref, cs_ref = refs[:3]
        refs = refs[3:]
    if cfg.sparse:
        tri_ref = refs[0]
        refs = refs[1:]
    refs = refs[n_state:]
    x1_ref, h2_ref, tok_ref, kout_ref, vout_ref, cout_ref = refs[:6]
    refs = refs[6:]
    if cfg.sparse:
        eid_ref, rank_ref, cnt_ref = refs[:3]
        refs = refs[3:]
    qbuf, kbuf, vbuf, obuf, uext, cvbuf = refs

    tt = cfg.tt
    t = pl.program_id(1)

    if cfg.has_cache:
        kbuf[0:PREV, :] = kc_ref[0].astype(BF16)
        vbuf[0:PREV, :] = vc_ref[0].astype(BF16)
        uext[CONV_PAD - (CONV_WIDTH - 1):CONV_PAD, :] = cs_ref[0]
    else:
        @pl.when(t == 0)
        def _():
            kbuf[0:PREV, :] = jnp.zeros((PREV, ATTN_DIM), BF16)
            vbuf[0:PREV, :] = jnp.zeros((PREV, ATTN_DIM), BF16)
            uext[0:CONV_PAD, :] = jnp.zeros((CONV_PAD, uext.shape[1]), F32)

    mod = mod_ref[0]
    sh1, sc1, g1, sh2, sc2 = (mod[i:i + 1, :] for i in range(5))

    x = x_ref[0]
    hb = (_rms(x, n1g_ref[...]) * (1.0 + sc1) + sh1).astype(BF16)

    a0, a1, a2, a3, a4, a5 = (0, ATTN_DIM, 2 * ATTN_DIM, 3 * ATTN_DIM,
                              3 * ATTN_DIM + cw_ref.shape[1], 3 * ATTN_DIM + 2 * cw_ref.shape[1])
    d_model = x.shape[1]

    q = _head_rms(_dot(hb, win_ref[:, a0:a1]), hsum_ref[...], qg_ref[...])
    qbuf[...] = (q * (HEAD_DIM ** -0.5 * LOG2_E)).astype(BF16)
    k = _head_rms(_dot(hb, win_ref[:, a1:a2]), hsum_ref[...], kg_ref[...])
    kout_ref[0] = k
    kbuf[PREV:PREV + tt, :] = k.astype(BF16)
    v = _dot(hb, win_ref[:, a2:a3])
    vout_ref[0] = v
    vbuf[PREV:PREV + tt, :] = v.astype(BF16)

    u = _dot(hb, win_ref[:, a3:a4]) * _sigmoid_t(_dot(hb, win_ref[:, a4:a5]))
    uext[CONV_PAD:CONV_PAD + tt, :] = u
    cout_ref[0] = uext[CONV_PAD + tt - (CONV_WIDTH - 1):CONV_PAD + tt, :]

    rc = min(tt, 64)
    cc = min(u.shape[1], LANES)
    shifts = [j + CONV_PAD - (CONV_WIDTH - 1) for j in range(CONV_WIDTH)]

    def conv_unit(r0, c0):
        acc = jnp.broadcast_to(cb_ref[:, c0:c0 + cc], (rc, cc))
        for res in range(SUBLANES):
            group = [s for s in shifts if s % SUBLANES == res]
            if not group:
                continue
            lo, hi = min(group) - res, max(group) - res
            rows = hi - lo + rc + (SUBLANES if res else 0)
            slab = uext[pl.ds(r0 + lo, rows), c0:c0 + cc]
            if res:
                slab = pltpu.roll(slab, rows - res, 0)
            for s in group:
                j = s - shifts[0]
                a = s - res - lo
                acc = acc + cw_ref[j:j + 1, c0:c0 + cc] * slab[a:a + rc, :]
        cvbuf[pl.ds(r0, rc), c0:c0 + cc] = acc

    units = [(r0, c0) for r0 in range(0, tt, rc) for c0 in range(0, u.shape[1], cc)]
    gw = 256
    n_chunks = 2 * d_model // gw
    gates = []
    for ci in range(n_chunks):
        cols = slice(ci * gw, (ci + 1) * gw)
        gates.append(_sigmoid_t(_dot(hb, win_ref[:, a5 + ci * gw:a5 + (ci + 1) * gw]) + bg_ref[:, cols]))
        for unit in units[ci::n_chunks]:
            conv_unit(*unit)
    g_attn = jnp.concatenate(gates[:n_chunks // 2], axis=1)
    g_conv = jnp.concatenate(gates[n_chunks // 2:], axis=1)

    def attend(r0, k0):
        if not cfg.has_cache:
            col = lax.broadcasted_iota(I32, (1, cfg.kb), 1)
            valid = jnp.logical_or(col >= PREV - k0, t > 0)
        heads = []
        for h in range(N_HEADS):
            c0, c1 = h * HEAD_DIM, (h + 1) * HEAD_DIM
            s = lax.dot_general(qbuf[pl.ds(r0, cfg.qb), c0:c1], kbuf[pl.ds(k0, cfg.kb), c0:c1],
                                (((1,), (1,)), ((), ())), preferred_element_type=F32)
            s = s + bias_ref[h]
            if not cfg.has_cache:
                s = jnp.where(valid, s, NEG)
            p = jnp.exp2(s - jnp.max(s, axis=-1, keepdims=True))
            l = jnp.sum(p, axis=-1, keepdims=True)
            heads.append(_dot(p.astype(BF16), vbuf[pl.ds(k0, cfg.kb), c0:c1]) * (1.0 / l))
        obuf[pl.ds(r0, cfg.qb), :] = jnp.concatenate(heads, axis=1).astype(BF16)

    for j in range(cfg.n_qblk):
        attend(j * cfg.qb, j * cfg.kstride)
    y_attn = _dot(obuf[...], wao_ref[...])
    if not cfg.has_cache:
        kbuf[0:PREV, :] = kbuf[tt:tt + PREV, :]
        vbuf[0:PREV, :] = vbuf[tt:tt + PREV, :]

    if not cfg.has_cache:
        uext[0:CONV_PAD, :] = uext[tt:tt + CONV_PAD, :]
    cv = cvbuf[...]
    mu = jnp.mean(cv, axis=-1, keepdims=True)
    xc = cv - mu
    var = jnp.mean(xc * xc, axis=-1, keepdims=True)
    cv = xc * lax.rsqrt(var + EPS) * lng_ref[...] + lnb_ref[...]
    cv = cv * _sigmoid_t(cv)
    y_conv = _dot(cv.astype(BF16), wco_ref[...]) + bco_ref[...]

    m = g_attn * y_attn + g_conv * y_conv
    x1 = x + g1 * _dot(m.astype(BF16), wout_ref[...])
    x1_ref[0] = x1

    h2 = _rms(x1, n2g_ref[...]) * (1.0 + sc2) + sh2
    h2_ref[0] = _pack(h2[:, 0:d_model // 2], h2[:, d_model // 2:d_model])
    h2_hi, h2_lo = _split_bf16(h2)
    nt_dims = (((1,), (1,)), ((), ()))
    logits = (lax.dot_general(wrh_ref[...], h2_hi, nt_dims, preferred_element_type=F32)
              + lax.dot_general(wrl_ref[...], h2_hi, nt_dims, preferred_element_type=F32)
              + lax.dot_general(wrh_ref[...], h2_lo, nt_dims, preferred_element_type=F32))
    s = _sigmoid(logits)
    eids, weights = _route(s + rb_ref[...], s)
    ids = _expert_ids(tt)
    if cfg.sparse:
        sel = [functools.reduce(jnp.add, [jnp.where(ids[g] == e, 1.0, 0.0) for e in eids]) for g in range(N_GROUPS)]
        sel = jnp.concatenate(sel, axis=0)
        rank = _dot(sel.astype(BF16), tri_ref[...])
        rank_slabs = [rank[g * GROUP_SIZE:(g + 1) * GROUP_SIZE, :] for g in range(N_GROUPS)]
        tok_ref[0] = _lane_dense_rows(weights, tt)
        eid_ref[...] = jnp.concatenate(eids, axis=0)
        rank_ref[...] = jnp.concatenate([_pick(ids, e, rank_slabs) for e in eids], axis=0)
        step = pl.program_id(0) * pl.num_programs(1) + t

        @pl.when(step == 0)
        def _():
            cnt_ref[...] = jnp.zeros(cnt_ref.shape, F32)

        lane = lax.broadcasted_iota(I32, cnt_ref.shape, 1)
        cnt_ref[...] = jnp.where(lane == step, jnp.sum(sel, axis=1, keepdims=True), cnt_ref[...])
    else:
        gates = [functools.reduce(jnp.add, [jnp.where(ids[g] == e, w, 0.0) for e, w in zip(eids, weights)])
                 for g in range(N_GROUPS)]
        gates = jnp.concatenate(gates + [jnp.zeros((LANES - N_EXPERTS, tt), F32)], axis=0)
        tok_ref[0] = gates.T


def _toeplitz(v, rows, cols):
    w = rows + cols
    flat = jnp.tile(v, (1, rows))[:, :rows * (w - 1)]
    return flat.reshape(v.shape[0], rows, w - 1)[:, :, :cols]


def _rel_bias_blocks(table, cfg, q_pos0, k_pos0):
    q_pos = q_pos0 + jnp.arange(cfg.qb)
    k_pos = k_pos0 + jnp.arange(cfg.kb)
    qc = q_pos // CHUNK
    kc = k_pos // CHUNK
    allowed = (kc[None, :] <= qc[:, None]) & (kc[None, :] >= qc[:, None] - BAND_CHUNKS)
    w = cfg.qb + cfg.kb
    dj = jnp.arange(w)
    dj = jnp.where(dj < cfg.kb, dj, dj - w)
    rel = jnp.clip(q_pos0 - k_pos0 - dj, -MAX_REL, MAX_REL) + MAX_REL
    bias = _toeplitz(table[:, rel].astype(F32), cfg.qb, cfg.kb)
    return jnp.where(allowed[None], bias * LOG2_E, NEG)


def _mixer_cost(cfg, tokens, d, n_in, conv_dim):
    per_token = (2 * d * n_in + 2 * (ATTN_DIM + conv_dim + d) * d + 4 * ATTN_DIM * cfg.kb
                 + 4 * ATTN_DIM * ATTN_DIM + 6 * N_EXPERTS * d + 2 * CONV_WIDTH * conv_dim)
    return pl.CostEstimate(flops=tokens * per_token,
                           transcendentals=tokens * (N_HEADS * cfg.kb + 2 * d + 2 * conv_dim + N_EXPERTS),
                           bytes_accessed=tokens * (4 * d + 4 * d + 2 * d + 4 * LANES) + 2 * d * n_in)


def _mixer_call(cfg, x, mod, p, cache=None, batch0=0, nbatch=None, state=None):
    b_all, t_total, d = x.shape
    nb = b_all if nbatch is None else nbatch
    nt = t_total // cfg.tt
    conv_dim = p["conv_w"].shape[1]
    n_in = p["w_in"].shape[1]

    def const(shape):
        return pl.BlockSpec(shape, lambda i, j: (0,) * len(shape), pipeline_mode=pl.Buffered(1))

    def per_row(shape):
        return pl.BlockSpec(shape, lambda i, j: (i + batch0, 0, 0))

    in_specs = [
        pl.BlockSpec((1, cfg.tt, d), lambda i, j: (i + batch0, j, 0)),
        per_row((1, 6, d)),
        const((1, d)), const((1, d)), const((d, n_in)),
        const((1, ATTN_DIM)), const((1, ATTN_DIM)), const((ATTN_DIM, ATTN_DIM)),
        const((N_HEADS, cfg.qb, cfg.kb)), const((ATTN_DIM, d)),
        const((CONV_WIDTH, conv_dim)), const((1, conv_dim)), const((1, conv_dim)), const((1, conv_dim)),
        const((conv_dim, d)), const((1, d)), const((1, 2 * d)), const((d, d)),
        const((N_EXPERTS, d)), const((N_EXPERTS, d)), const((N_EXPERTS, 1)),
    ]
    args = [x, mod, p["norm1_g"], p["norm2_g"], p["w_in"], p["q_norm_g"], p["k_norm_g"], p["hsum"],
            p["bias"], p["w_attn_o"], p["conv_w"], p["conv_b"], p["conv_ln_g"], p["conv_ln_b"],
            p["w_conv_o"], p["b_conv_o"], p["b_gate"], p["w_out"], p["wr_hi"], p["wr_lo"], p["router_bias"]]
    if cfg.has_cache:
        in_specs += [per_row((1, PREV, ATTN_DIM)), per_row((1, PREV, ATTN_DIM)),
                     per_row((1, CONV_WIDTH - 1, conv_dim))]
        args += list(cache)
    if cfg.sparse:
        assert nb * nt <= MAX_TILES
        in_specs += [const((cfg.tt, cfg.tt))]
        tok = jnp.arange(cfg.tt)
        args += [(tok[:, None] < tok[None, :]).astype(BF16)]
    aliases = {}
    if state is not None:
        for i, arr in enumerate(state):
            aliases[len(args)] = 3 + i
            in_specs.append(pl.BlockSpec(memory_space=pl.ANY))
            args.append(arr)
    keep = min(PREV, t_total)
    assert keep == cfg.tt
    out_shape = [jax.ShapeDtypeStruct((nb, t_total, d), F32),
                 jax.ShapeDtypeStruct((nb, t_total, d // 2), I32),
                 jax.ShapeDtypeStruct((nb, t_total, LANES), F32),
                 jax.ShapeDtypeStruct((b_all, keep, ATTN_DIM), F32),
                 jax.ShapeDtypeStruct((b_all, keep, ATTN_DIM), F32),
                 jax.ShapeDtypeStruct((b_all, CONV_WIDTH - 1, conv_dim), F32)]
    out_specs = [pl.BlockSpec((1, cfg.tt, d), lambda i, j: (i, j, 0)),
                 pl.BlockSpec((1, cfg.tt, d // 2), lambda i, j: (i, j, 0)),
                 pl.BlockSpec((1, cfg.tt, LANES), lambda i, j: (i, j, 0)),
                 per_row((1, keep, ATTN_DIM)), per_row((1, keep, ATTN_DIM)),
                 per_row((1, CONV_WIDTH - 1, conv_dim))]
    if cfg.sparse:
        out_shape += [jax.ShapeDtypeStruct((TOP_K, nb * t_total), F32),
                      jax.ShapeDtypeStruct((TOP_K, nb * t_total), F32),
                      jax.ShapeDtypeStruct((N_EXPERTS, MAX_TILES), F32)]
        out_specs += [pl.BlockSpec((TOP_K, cfg.tt), lambda i, j: (0, i * nt + j)),
                      pl.BlockSpec((TOP_K, cfg.tt), lambda i, j: (0, i * nt + j)),
                      pl.BlockSpec((N_EXPERTS, MAX_TILES), lambda i, j: (0, 0))]
    scratch = [pltpu.VMEM((cfg.tt, ATTN_DIM), BF16),
               pltpu.VMEM((PREV + cfg.tt, ATTN_DIM), BF16),
               pltpu.VMEM((PREV + cfg.tt, ATTN_DIM), BF16),
               pltpu.VMEM((cfg.tt, ATTN_DIM), BF16),
               pltpu.VMEM((CONV_PAD + cfg.tt, conv_dim), F32),
               pltpu.VMEM((cfg.tt, conv_dim), F32)]
    return pl.pallas_call(
        functools.partial(_mixer_kernel, cfg, 0 if state is None else len(state)),
        out_shape=out_shape,
        grid=(nb, nt),
        in_specs=in_specs,
        out_specs=out_specs,
        scratch_shapes=scratch,
        input_output_aliases=aliases,
        cost_estimate=_mixer_cost(cfg, nb * t_total, d, n_in, conv_dim),
        compiler_params=pltpu.CompilerParams(dimension_semantics=("arbitrary", "arbitrary"),
                                             vmem_limit_bytes=VMEM_LIMIT_BYTES),
        name="mixer_sample" if cfg.has_cache else "mixer_prompt",
    )(*args)


def _exact_parts(x, n):
    parts = []
    for _ in range(n):
        part = x.astype(BF16)
        parts.append(part)
        x = x - part.astype(F32)
    return parts


def _plan_kernel(tt, cnt_ref, eid_ref, rank_ref, tri_e_ref, tri_t_ref, pos_ref, blk_ref, nblk_ref, base_ref):
    step = pl.program_id(0)

    @pl.when(step == 0)
    def _():
        cnt = cnt_ref[...]
        total = jnp.sum(cnt, axis=1, keepdims=True)
        padded = jnp.floor((total + float(ROW_BLOCK - 1)) * (1.0 / ROW_BLOCK)) * float(ROW_BLOCK)
        padded = jnp.broadcast_to(padded, cnt.shape)
        start = functools.reduce(jnp.add, [_dot(tri_e_ref[...], part) for part in _exact_parts(padded, 3)])
        before = functools.reduce(jnp.add, [_dot(part, tri_t_ref[...]) for part in _exact_parts(cnt, 2)])
        base_ref[...] = start + before
        end = (start + padded)[:, 0:1]
        first_row = lax.broadcasted_iota(I32, blk_ref.shape, 1).astype(F32) * float(ROW_BLOCK)
        owner = jnp.sum(jnp.where(end <= first_row, 1.0, 0.0), axis=0, keepdims=True)
        blk_ref[...] = jnp.minimum(owner, float(N_EXPERTS - 1)).astype(I32)
        nblk = jnp.max(end, axis=0, keepdims=True) * (1.0 / ROW_BLOCK)
        nblk_ref[...] = jnp.broadcast_to(nblk, nblk_ref.shape).astype(I32)

    lane = lax.broadcasted_iota(I32, base_ref.shape, 1)
    col = jnp.sum(jnp.where(lane == step, base_ref[...], 0.0), axis=1, keepdims=True)
    ids = _expert_ids(tt)
    col_slabs = [jnp.broadcast_to(col[g * GROUP_SIZE:(g + 1) * GROUP_SIZE, :], (GROUP_SIZE, tt))
                 for g in range(N_GROUPS)]
    eid = eid_ref[...]
    rows = [_pick(ids, eid[k:k + 1, :], col_slabs) for k in range(TOP_K)]
    pos_ref[...] = (jnp.concatenate(rows, axis=0) + rank_ref[...]).astype(I32)


def _plan_call(cnt, eid, rank, tt, n_blocks):
    n = eid.shape[1]
    e = jnp.arange(N_EXPERTS)
    s = jnp.arange(MAX_TILES)
    nb_pad = -(-n_blocks // LANES) * LANES
    return pl.pallas_call(
        functools.partial(_plan_kernel, tt),
        out_shape=[jax.ShapeDtypeStruct((TOP_K, n), I32),
                   jax.ShapeDtypeStruct((1, nb_pad), I32),
                   jax.ShapeDtypeStruct((1, LANES), I32)],
        grid=(n // tt,),
        in_specs=[pl.BlockSpec((N_EXPERTS, MAX_TILES), lambda i: (0, 0)),
                  pl.BlockSpec((TOP_K, tt), lambda i: (0, i)),
                  pl.BlockSpec((TOP_K, tt), lambda i: (0, i)),
                  pl.BlockSpec((N_EXPERTS, N_EXPERTS), lambda i: (0, 0)),
                  pl.BlockSpec((MAX_TILES, MAX_TILES), lambda i: (0, 0))],
        out_specs=[pl.BlockSpec((TOP_K, tt), lambda i: (0, i)),
                   pl.BlockSpec((1, nb_pad), lambda i: (0, 0)),
                   pl.BlockSpec((1, LANES), lambda i: (0, 0))],
        scratch_shapes=[pltpu.VMEM((N_EXPERTS, MAX_TILES), F32)],
        compiler_params=pltpu.CompilerParams(dimension_semantics=("arbitrary",),
                                             vmem_limit_bytes=VMEM_LIMIT_BYTES),
        name="plan",
    )(cnt, eid, rank, (e[None, :] < e[:, None]).astype(BF16), (s[:, None] < s[None, :]).astype(BF16))


def _sc_mesh():
    return plsc.VectorSubcoreMesh(core_axis_name="c", subcore_axis_name="s")


def _dispatch_call(rows, pos, n_sorted):
    n, w = rows.shape
    per_worker = n // SC_WORKERS
    nsteps = per_worker // SC_ROWS
    assert per_worker * SC_WORKERS == n and nsteps * SC_ROWS == per_worker and nsteps % 2 == 0

    @functools.partial(
        pl.kernel, mesh=_sc_mesh(),
        out_type=jax.ShapeDtypeStruct((n_sorted, w), rows.dtype),
        scratch_types=[pltpu.VMEM((2, TOP_K, SC_ROWS), I32),
                       pltpu.VMEM((2, SC_ROWS, w), rows.dtype),
                       pltpu.SemaphoreType.DMA((2,)),
                       pltpu.SemaphoreType.DMA((2,))],
        cost_estimate=pl.CostEstimate(flops=0, transcendentals=0,
                                      bytes_accessed=(1 + TOP_K) * n * w * rows.dtype.itemsize + 4 * TOP_K * n),
    )
    def dispatch(x_hbm, pos_hbm, out_hbm, idx_v, rows_v, load_sem, scat_sem):
        base = (lax.axis_index("s") * SC_CORES + lax.axis_index("c")) * per_worker

        def load(i, b):
            return pltpu.make_async_copy(x_hbm.at[pl.ds(base + i * SC_ROWS, SC_ROWS)], rows_v.at[b], load_sem.at[b])

        def scatter(b, k):
            return pltpu.make_async_copy(rows_v.at[b], out_hbm.at[idx_v.at[b, k]], scat_sem.at[b])

        def load_start(i, b):
            for k in range(TOP_K):
                pltpu.sync_copy(pos_hbm.at[pl.ds(k * n + base + i * SC_ROWS, SC_ROWS)], idx_v.at[b, k])
            load(i, b).start()

        load_start(0, 0)

        @pl.loop(0, nsteps, step=2)
        def _(i):
            for b in range(2):
                ii = i + b

                @pl.when(ii >= 1)
                def _():
                    for k in range(TOP_K):
                        scatter(1 - b, k).wait()

                @pl.when(ii + 1 < nsteps)
                def _():
                    load_start(ii + 1, 1 - b)

                load(ii, b).wait()
                for k in range(TOP_K):
                    scatter(b, k).start()

        for k in range(TOP_K):
            scatter((nsteps - 1) % 2, k).wait()

    return dispatch(rows, pos)


def _collect_call(table, idx):
    n = idx.shape[0]
    w = table.shape[1]
    per_worker = n // SC_WORKERS
    nsteps = per_worker // SC_ROWS
    assert per_worker * SC_WORKERS == n and nsteps * SC_ROWS == per_worker and nsteps % 2 == 0

    @functools.partial(
        pl.kernel, mesh=_sc_mesh(),
        out_type=jax.ShapeDtypeStruct((n, w), table.dtype),
        scratch_types=[pltpu.VMEM((2, SC_ROWS), I32),
                       pltpu.VMEM((2, SC_ROWS, w), table.dtype),
                       pltpu.SemaphoreType.DMA((2,)),
                       pltpu.SemaphoreType.DMA((2,))],
        cost_estimate=pl.CostEstimate(flops=0, transcendentals=0,
                                      bytes_accessed=2 * n * w * table.dtype.itemsize + 4 * n),
    )
    def collect(table_hbm, idx_hbm, out_hbm, idx_v, rows_v, gather_sem, write_sem):
        base = (lax.axis_index("s") * SC_CORES + lax.axis_index("c")) * per_worker

        def gather(b):
            return pltpu.make_async_copy(table_hbm.at[idx_v.at[b]], rows_v.at[b], gather_sem.at[b])

        def write(i, b):
            return pltpu.make_async_copy(rows_v.at[b], out_hbm.at[pl.ds(base + i * SC_ROWS, SC_ROWS)], write_sem.at[b])

        def gather_start(i, b):
            pltpu.sync_copy(idx_hbm.at[pl.ds(base + i * SC_ROWS, SC_ROWS)], idx_v.at[b])
            gather(b).start()

        gather_start(0, 0)

        @pl.loop(0, nsteps, step=2)
        def _(i):
            for b in range(2):
                ii = i + b

                @pl.when(ii >= 1)
                def _():
                    write(ii - 1, 1 - b).wait()

                @pl.when(ii + 1 < nsteps)
                def _():
                    gather_start(ii + 1, 1 - b)

                gather(b).wait()
                write(ii, b).start()

        write(nsteps - 1, (nsteps - 1) % 2).wait()

    return collect(table, idx)


def _expert_ffn_kernel(blk_ref, nblk_ref, x_hbm, wg_ref, wu_ref, wd_ref, o_ref, wg_s, wu_s, wd_s, x_buf, x_sem):
    j = pl.program_id(0)
    nblk = nblk_ref[0]

    def fetch(block):
        slot = block % FFN_ROW_BUFFERS
        return pltpu.make_async_copy(x_hbm.at[pl.ds(pl.multiple_of(block * ROW_BLOCK, ROW_BLOCK), ROW_BLOCK), :],
                                     x_buf.at[slot], x_sem.at[slot])

    @pl.when(j == 0)
    def _():
        for ahead in range(FFN_ROW_BUFFERS - 1):
            pl.when(ahead < nblk)(lambda ahead=ahead: fetch(ahead).start())

    @pl.when(j + FFN_ROW_BUFFERS - 1 < nblk)
    def _():
        fetch(j + FFN_ROW_BUFFERS - 1).start()

    @pl.when(jnp.logical_or(j == 0, blk_ref[j] != blk_ref[jnp.maximum(j - 1, 0)]))
    def _():
        wg_s[...] = wg_ref[0].astype(BF16)
        wu_s[...] = wu_ref[0].astype(BF16)
        wd_s[...] = wd_ref[0].astype(BF16)

    @pl.when(j < nblk)
    def _():
        fetch(j).wait()
        a, b = _unpack(x_buf[j % FFN_ROW_BUFFERS])
        a, b = a.astype(BF16), b.astype(BF16)
        hg = _dot_halves(a, b, wg_s)
        act = hg * _sigmoid_t(hg) * _dot_halves(a, b, wu_s)
        y = _dot(act.astype(BF16), wd_s[...])
        half = y.shape[1] // 2
        o_ref[...] = _pack(y[:, 0:half], y[:, half:2 * half])


def _expert_ffn_call(xs, blk, nblk, p):
    n_sorted, w = xs.shape
    d = 2 * w
    ff = p["w_e_gate"].shape[2]

    def rows(j, blk, nblk):
        return (jnp.minimum(j, nblk[0] - 1), 0)

    def expert(j, blk, nblk):
        return (blk[j], 0, 0)

    return pl.pallas_call(
        _expert_ffn_kernel,
        out_shape=jax.ShapeDtypeStruct((n_sorted, w), I32),
        grid_spec=pltpu.PrefetchScalarGridSpec(
            num_scalar_prefetch=2,
            grid=(n_sorted // ROW_BLOCK,),
            in_specs=[pl.BlockSpec(memory_space=pl.ANY),
                      pl.BlockSpec((1, d, ff), expert),
                      pl.BlockSpec((1, d, ff), expert),
                      pl.BlockSpec((1, ff, d), expert)],
            out_specs=pl.BlockSpec((ROW_BLOCK, w), rows),
            scratch_shapes=[pltpu.VMEM((d, ff), BF16), pltpu.VMEM((d, ff), BF16), pltpu.VMEM((ff, d), BF16),
                            pltpu.VMEM((FFN_ROW_BUFFERS, ROW_BLOCK, w), I32),
                            pltpu.SemaphoreType.DMA((FFN_ROW_BUFFERS,))]),
        cost_estimate=pl.CostEstimate(flops=6 * n_sorted * d * ff, transcendentals=n_sorted * ff,
                                      bytes_accessed=8 * n_sorted * w + 12 * N_EXPERTS * d * ff),
        compiler_params=pltpu.CompilerParams(dimension_semantics=("arbitrary",),
                                             vmem_limit_bytes=VMEM_LIMIT_BYTES),
        name="expert_ffn",
    )(blk, nblk, xs, p["w_e_gate"], p["w_e_up"], p["w_e_down"])


def _swiglu_halves(a, b, wg_ref, wu_ref):
    hg = _dot_halves(a, b, wg_ref)
    return hg * _sigmoid_t(hg) * _dot_halves(a, b, wu_ref)


def _moe_out_kernel(n_groups, starts, *refs):
    groups = [refs[4 * q:4 * q + 4] for q in range(n_groups)]
    g2_ref, wsg_ref, wsu_ref, wsd_ref, o_ref = refs[4 * n_groups:]

    def combine(x1_ref, h_ref, y_ref, w_ref):
        a, b = _unpack(h_ref[...])
        shared = _dot(_swiglu_halves(a.astype(BF16), b.astype(BF16), wsg_ref, wsu_ref).astype(BF16),
                      wsd_ref[...].astype(BF16))
        half = h_ref.shape[1]
        w = w_ref[...]
        acc_a, acc_b = shared[:, 0:half], shared[:, half:2 * half]
        for k in range(TOP_K):
            ya, yb = _unpack(y_ref[k])
            acc_a = acc_a + w[:, k:k + 1] * ya
            acc_b = acc_b + w[:, k:k + 1] * yb
        g2 = g2_ref[...]
        o_ref[:, 0:half] = x1_ref[:, 0:half] + g2[:, 0:half] * acc_a
        o_ref[:, half:2 * half] = x1_ref[:, half:2 * half] + g2[:, half:2 * half] * acc_b

    i = pl.program_id(0)
    for q in range(n_groups):
        pl.when(jnp.logical_and(i >= starts[q], i < starts[q + 1]))(functools.partial(combine, *groups[q]))


def _moe_out_call(groups, g2, p, tm):
    d = groups[0][0].shape[1]
    starts = [0]
    for group in groups:
        starts.append(starts[-1] + group[0].shape[0] // tm)
    n_all = starts[-1] * tm
    sff = p["w_s_gate"].shape[1]
    per_row = n_all // g2.shape[0]
    g2 = g2.reshape(g2.shape[0], 1, d)

    def const(shape):
        return pl.BlockSpec(shape, lambda i: (0,) * len(shape), pipeline_mode=pl.Buffered(1))

    in_specs, args = [], []
    for q, group in enumerate(groups):
        tile = lambda i, q=q: jnp.clip(i - starts[q], 0, starts[q + 1] - starts[q] - 1)
        in_specs += [pl.BlockSpec((tm, d), lambda i, tile=tile: (tile(i), 0)),
                     pl.BlockSpec((tm, d // 2), lambda i, tile=tile: (tile(i), 0)),
                     pl.BlockSpec((TOP_K, tm, d // 2), lambda i, tile=tile: (0, tile(i), 0)),
                     pl.BlockSpec((tm, LANES), lambda i, tile=tile: (tile(i), 0))]
        args += list(group)
    in_specs += [pl.BlockSpec((None, 1, d), lambda i: (i * tm // per_row, 0, 0)),
                 const((d, sff)), const((d, sff)), const((sff, d))]
    args += [g2, p["w_s_gate"], p["w_s_up"], p["w_s_down"]]
    return pl.pallas_call(
        functools.partial(_moe_out_kernel, len(groups), tuple(starts)),
        out_shape=jax.ShapeDtypeStruct((n_all, d), F32),
        grid=(n_all // tm,),
        in_specs=in_specs,
        out_specs=pl.BlockSpec((tm, d), lambda i: (i, 0)),
        cost_estimate=pl.CostEstimate(flops=n_all * (6 * d * sff + 2 * TOP_K * d), transcendentals=n_all * sff,
                                      bytes_accessed=n_all * (8 * d + 2 * d + 2 * TOP_K * d + 4 * LANES)),
        compiler_params=pltpu.CompilerParams(dimension_semantics=("arbitrary",),
                                             vmem_limit_bytes=VMEM_LIMIT_BYTES),
        name="moe_out",
    )(*args)


def _moe_dense_kernel(h_ref, gates_ref, x1_ref, g2_ref, wsg_ref, wsu_ref, wsd_ref, wg_ref, wu_ref, wd_ref,
                      o_ref, acc_ref):
    e = pl.program_id(1)
    a, b = _unpack(h_ref[...])
    a, b = a.astype(BF16), b.astype(BF16)

    @pl.when(e == 0)
    def _():
        acc_ref[...] = _dot(_swiglu_halves(a, b, wsg_ref, wsu_ref).astype(BF16), wsd_ref[...].astype(BF16))

    gates = gates_ref[...]
    lane = lax.broadcasted_iota(I32, gates.shape, 1)
    gate = jnp.sum(jnp.where(lane == e, gates, 0.0), axis=1, keepdims=True)
    act = _swiglu_halves(a, b, wg_ref.at[0], wu_ref.at[0]) * gate
    acc_ref[...] += _dot(act.astype(BF16), wd_ref[0].astype(BF16))

    @pl.when(e == pl.num_programs(1) - 1)
    def _():
        o_ref[...] = x1_ref[...] + g2_ref[...] * acc_ref[...]


def _moe_dense_call(h2, gates, x1, g2_rows, p):
    n, d = x1.shape
    ff = p["w_e_gate"].shape[2]
    sff = p["w_s_gate"].shape[1]

    def const(shape):
        return pl.BlockSpec(shape, lambda i, e: (0,) * len(shape), pipeline_mode=pl.Buffered(1))

    return pl.pallas_call(
        _moe_dense_kernel,
        out_shape=jax.ShapeDtypeStruct((n, d), F32),
        grid=(1, N_EXPERTS),
        in_specs=[const((n, d // 2)), const((n, LANES)), const((n, d)), const((n, d)),
                  const((d, sff)), const((d, sff)), const((sff, d)),
                  pl.BlockSpec((1, d, ff), lambda i, e: (e, 0, 0)),
                  pl.BlockSpec((1, d, ff), lambda i, e: (e, 0, 0)),
                  pl.BlockSpec((1, ff, d), lambda i, e: (e, 0, 0))],
        out_specs=pl.BlockSpec((n, d), lambda i, e: (0, 0)),
        scratch_shapes=[pltpu.VMEM((n, d), F32)],
        compiler_params=pltpu.CompilerParams(dimension_semantics=("arbitrary", "arbitrary"),
                                             vmem_limit_bytes=VMEM_LIMIT_BYTES),
        name="moe_dense",
    )(h2, gates, x1, g2_rows, p["w_s_gate"], p["w_s_up"], p["w_s_down"], p["w_e_gate"], p["w_e_up"], p["w_e_down"])


PROMPT_CFG = MixerCfg(tt=512, qb=256, kb=768, n_qblk=2, kstride=256, has_cache=False, sparse=True)
MOE_OUT_TILE = 512
PROMPT_SPLIT = (1, 1)


def _layer_params(l, w_in, q_norm_g, k_norm_g, w_attn_o, conv_w, conv_b, conv_ln_g, conv_ln_b, w_conv_o,
                  b_conv_o, b_gate, w_out, norm1_g, norm2_g, w_router, router_bias,
                  w_e_gate, w_e_up, w_e_down, w_s_gate, w_s_up, w_s_down):
    row = lambda a: a[l].reshape(1, -1)
    wr_t = w_router[l].T
    wr_hi = wr_t.astype(BF16)
    head = jnp.arange(ATTN_DIM) // HEAD_DIM
    return {
        "norm1_g": row(norm1_g), "norm2_g": row(norm2_g), "w_in": w_in[l].astype(BF16),
        "q_norm_g": jnp.tile(q_norm_g[l], N_HEADS).reshape(1, -1),
        "k_norm_g": jnp.tile(k_norm_g[l], N_HEADS).reshape(1, -1),
        "hsum": (head[:, None] == head[None, :]).astype(BF16),
        "w_attn_o": w_attn_o[l].astype(BF16), "conv_w": conv_w[l], "conv_b": row(conv_b),
        "conv_ln_g": row(conv_ln_g), "conv_ln_b": row(conv_ln_b), "w_conv_o": w_conv_o[l].astype(BF16),
        "b_conv_o": row(b_conv_o), "b_gate": row(b_gate), "w_out": w_out[l].astype(BF16),
        "wr_hi": wr_hi, "wr_lo": (wr_t - wr_hi.astype(F32)).astype(BF16),
        "router_bias": router_bias[l].reshape(-1, 1),
        "w_e_gate": w_e_gate[l], "w_e_up": w_e_up[l], "w_e_down": w_e_down[l],
        "w_s_gate": w_s_gate[l], "w_s_up": w_s_up[l], "w_s_down": w_s_down[l],
    }


def _sorted_rows(h2, eid, rank, cnt, tt):
    n = h2.shape[0]
    n_sorted = n * TOP_K + N_EXPERTS * ROW_BLOCK
    pos, blk, nblk = _plan_call(cnt, eid, rank, tt, n_sorted // ROW_BLOCK)
    pos = pos.reshape(TOP_K * n)
    return _dispatch_call(h2, pos, n_sorted), pos, blk.reshape(-1), nblk.reshape(-1)


def kernel(x_prompt, x_sample, c_prompt, c_sample, cache_k, cache_v, state_conv, w_mod, b_mod, norm1_g, w_in, q_norm_g, k_norm_g, rel_bias, w_attn_o, conv_w, conv_b, conv_ln_g, conv_ln_b, w_conv_o, b_conv_o, b_gate, w_out, norm2_g, w_router, router_bias, w_e_gate, w_e_up, w_e_down, w_s_gate, w_s_up, w_s_down):
    depth = w_mod.shape[0]
    bp, tp, d = x_prompt.shape
    bs, ts, _ = x_sample.shape
    assert cache_k.shape[2] == min(PREV, PAST_LEN) == PREV
    sample_cfg = MixerCfg(tt=ts, qb=ts, kb=PREV + ts, n_qblk=1, kstride=0, has_cache=True, sparse=False)

    yp, ys = x_prompt, x_sample
    outs = [[] for _ in range(6)]
    for l in range(depth):
        p = _layer_params(l, w_in, q_norm_g, k_norm_g, w_attn_o, conv_w, conv_b, conv_ln_g, conv_ln_b,
                          w_conv_o, b_conv_o, b_gate, w_out, norm1_g, norm2_g, w_router, router_bias,
                          w_e_gate, w_e_up, w_e_down, w_s_gate, w_s_up, w_s_down)
        mod = _mod_call(jnp.concatenate([c_prompt, c_sample], axis=0), w_mod[l], b_mod[l])
        mod = mod.reshape(bp + bs, 6, d)
        mod_p, mod_s = mod[:bp], mod[bp:]

        p["bias"] = _rel_bias_blocks(rel_bias[l], PROMPT_CFG, PREV, 0)
        big = max(bp * PROMPT_SPLIT[0] // sum(PROMPT_SPLIT), 1)
        group_rows = [big, bp - big] if bp > big else [bp]
        keep = min(PREV, tp)
        state = [jnp.zeros((bp, keep, ATTN_DIM), F32), jnp.zeros((bp, keep, ATTN_DIM), F32),
                 jnp.zeros((bp, CONV_WIDTH - 1, conv_w.shape[2]), F32)]
        mixed, moved = [], []
        for g, gb in enumerate(group_rows):
            x1, h2, w_tok, *state, eid, rank, cnt = _mixer_call(PROMPT_CFG, yp, mod_p, p, batch0=sum(group_rows[:g]),
                                                                nbatch=gb, state=state)
            x1, h2, w_tok = x1.reshape(gb * tp, d), h2.reshape(gb * tp, d // 2), w_tok.reshape(gb * tp, LANES)
            mixed.append((x1, h2, w_tok))
            moved.append(_sorted_rows(h2, eid, rank, cnt, PROMPT_CFG.tt))
            state, _ = lax.optimization_barrier((state, moved[-1][1]))
        kp, vp, cp = state
        groups, gate = [], cp
        for (x1, h2, w_tok), (xs, pos, blk, nblk) in zip(mixed, moved):
            xs, _ = lax.optimization_barrier((xs, gate))
            gate = _expert_ffn_call(xs, blk, nblk, p)
            yg = _collect_call(gate, pos).reshape(TOP_K, h2.shape[0], h2.shape[1])
            groups.append((x1, h2, yg, w_tok))
        yp = _moe_out_call(groups, mod_p[:, 5, :], p, MOE_OUT_TILE).reshape(bp, tp, d)
        outs[0].append(kp.reshape(bp, -1, N_HEADS, HEAD_DIM))
        outs[1].append(vp.reshape(bp, -1, N_HEADS, HEAD_DIM))
        outs[2].append(cp)

        p["bias"] = _rel_bias_blocks(rel_bias[l], sample_cfg, PAST_LEN, PAST_LEN - PREV)
        cache = (cache_k[l].reshape(bs, PREV, ATTN_DIM), cache_v[l].reshape(bs, PREV, ATTN_DIM), state_conv[l])
        x1, h2, gates, ks, vs, cs = _mixer_call(sample_cfg, ys, mod_s, p, cache)
        ys = _moe_dense_call(h2.reshape(bs * ts, d // 2), gates.reshape(bs * ts, LANES), x1.reshape(bs * ts, d),
                             jnp.repeat(mod_s[:, 5, :], ts, axis=0), p).reshape(bs, ts, d)
        outs[3].append(ks.reshape(bs, ts, N_HEADS, HEAD_DIM))
        outs[4].append(vs.reshape(bs, ts, N_HEADS, HEAD_DIM))
        outs[5].append(cs)
    return (yp, ys) + tuple(jnp.stack(o) for o in outs)
```
